```python
import math, functools
import jax, jax.numpy as jnp
from jax import lax
import numpy as np

D_MODEL = 4096
BATCH = 1
SEQ = 8192
DEPTH = 1
DEC_BATCH = 32
DEC_SEQ = 4
PAST_LEN = 8192
PAGE_SIZE = 128

A_HEADS = 16
A_KV_HEADS = 4
A_HEAD_DIM = 128
IDX_HEADS = 32
IDX_DIM = 64
TOPK_MAX = 256
Q_BLOCK = 128
B_HEADS = 8
B_HEAD_DIM = 128
B_CHUNK = 64
MEM_TOKENS = 256
M_HEADS = 4
M_HEAD_DIM = 256
PEER_HEADS = 8
PEER_NKEYS = 128
PEER_EXPERTS = PEER_NKEYS * PEER_NKEYS
PEER_DKEY = 256
PEER_TOPK = 16
PEER_TOKEN_BLOCK = 64

ROPE_THETA = 10000.0
LN_EPS = 1e-5
RMS_EPS = 1e-6
ALPHA = (2 * DEPTH) ** 0.25
BETA = (8 * DEPTH) ** -0.25

A_Q_W = A_HEADS * A_HEAD_DIM
A_KV_W = A_KV_HEADS * A_HEAD_DIM
IDX_Q_W = IDX_HEADS * IDX_DIM
B_W = B_HEADS * B_HEAD_DIM
M_W = M_HEADS * M_HEAD_DIM
IN_SPLITS = (A_Q_W, A_KV_W, A_KV_W, IDX_Q_W, IDX_DIM, IDX_HEADS, B_W, B_W, B_W, B_W, M_W, D_MODEL, D_MODEL, D_MODEL)
IN_COLS = A_Q_W + 2 * A_KV_W + IDX_Q_W + IDX_DIM + IDX_HEADS + 4 * B_W + M_W + 3 * D_MODEL

kernel_name = 'dsa_hgrn2_memory_peer_hybrid_step'


def rope(x, pos):
    half = x.shape[-1] // 2
    inv_freq = ROPE_THETA ** (-jnp.arange(half, dtype=jnp.float32) / half)
    ang = pos.astype(jnp.float32)[:, None] * inv_freq[None, :]
    cos = jnp.cos(ang)[:, None, :]
    sin = jnp.sin(ang)[:, None, :]
    xf = x.astype(jnp.float32)
    x1, x2 = xf[..., :half], xf[..., half:]
    return jnp.concatenate([x1 * cos - x2 * sin, x1 * sin + x2 * cos], axis=-1).astype(x.dtype)


def layer_norm(x, g, b):
    xf = x.astype(jnp.float32)
    mu = jnp.mean(xf, axis=-1, keepdims=True)
    var = jnp.mean(jnp.square(xf - mu), axis=-1, keepdims=True)
    return ((xf - mu) * lax.rsqrt(var + LN_EPS) * g + b).astype(x.dtype)


def index_scores(qi, wi, ki):
    dots = jnp.einsum('bqhd,bsd->bqhs', qi, ki).astype(jnp.float32) * (IDX_DIM ** -0.5)
    return jnp.einsum('bqh,bqhs->bqs', wi.astype(jnp.float32) * (IDX_HEADS ** -0.5), jax.nn.relu(dots))


def sparse_attend(q, kg, vg, valid):
    b, nq, h, dh = q.shape
    g = kg.shape[3]
    qg = q.reshape(b, nq, g, h // g, dh)
    s = jnp.einsum('bqgrd,bqkgd->bqgrk', qg, kg).astype(jnp.float32) * (dh ** -0.5)
    s = jnp.where(valid[:, :, None, None, :], s, -jnp.inf)
    p = jax.nn.softmax(s, axis=-1).astype(vg.dtype)
    return jnp.einsum('bqgrk,bqkgd->bqgrd', p, vg).reshape(b, nq, h * dh)


def dsa_prompt(q, k, v, qi, wi, ki):
    b, t = q.shape[:2]
    topk = min(TOPK_MAX, t // 4)
    qb = math.gcd(t, Q_BLOCK)
    nblk = t // qb
    key_pos = jnp.arange(t)
    b_idx = jnp.arange(b)[:, None, None]

    def to_blocks(a):
        return jnp.moveaxis(a.reshape((b, nblk, qb) + a.shape[2:]), 1, 0)

    def block(args):
        q_blk, qi_blk, wi_blk, start = args
        q_pos = start + jnp.arange(qb)
        sc = index_scores(qi_blk, wi_blk, ki)
        sc = jnp.where((key_pos[None, :] <= q_pos[:, None])[None], sc, -jnp.inf)
        _, sel = lax.top_k(sc, topk)
        valid = sel <= q_pos[None, :, None]
        return sparse_attend(q_blk, k[b_idx, sel], v[b_idx, sel], valid)

    starts = jnp.arange(nblk) * qb
    out = lax.map(block, (to_blocks(q), to_blocks(qi), to_blocks(wi), starts))
    return jnp.moveaxis(out, 0, 1).reshape(b, t, -1)


def dsa_sample(q, k, v, qi, wi, ki, cache_k, cache_v, cache_idx_k, page_table):
    b, t = q.shape[:2]
    past_len = page_table.shape[1] * PAGE_SIZE
    n_keys = past_len + t
    topk = min(TOPK_MAX, n_keys // 4)
    b_idx = jnp.arange(b)[:, None, None]
    ki_past = cache_idx_k[page_table].reshape(b, past_len, IDX_DIM)
    ki_all = jnp.concatenate([ki_past, ki.astype(ki_past.dtype)], axis=1)
    q_pos = past_len + jnp.arange(t)
    sc = index_scores(qi, wi, ki_all)
    sc = jnp.where((jnp.arange(n_keys)[None, :] <= q_pos[:, None])[None], sc, -jnp.inf)
    _, sel = lax.top_k(sc, topk)
    valid = sel <= q_pos[None, :, None]
    in_past = (sel < past_len)[..., None, None]
    ps = jnp.minimum(sel, past_len - 1)
    phys = page_table[b_idx, ps // PAGE_SIZE]
    off = ps % PAGE_SIZE
    ns = jnp.clip(sel - past_len, 0, t - 1)
    kg = jnp.where(in_past, cache_k[phys, off], k[b_idx, ns].astype(cache_k.dtype))
    vg = jnp.where(in_past, cache_v[phys, off], v[b_idx, ns].astype(cache_v.dtype))
    return sparse_attend(q, kg, vg, valid)


def hgrn2_recurrence(q, f_logit, i, lb, s0):
    b, t, h, _ = q.shape
    f = lb + (1.0 - lb) * jax.nn.sigmoid(f_logit.astype(jnp.float32))
    k = 1.0 - f
    log_f = jnp.log(f)
    c = math.gcd(t, B_CHUNK)
    n = t // c

    def chunks(a):
        return jnp.moveaxis(a.astype(jnp.float32).reshape(b, n, c, h, a.shape[-1]), 1, 0)

    causal = jnp.tril(jnp.ones((c, c), dtype=bool))[None, :, :, None, None]

    def step(S, inp):
        qc, kc, ic, lfc = inp
        cum = jnp.cumsum(lfc, axis=1)
        o_inter = jnp.einsum('bthk,bhkv->bthv', qc * jnp.exp(cum), S)
        decay = jnp.exp(jnp.where(causal, cum[:, :, None] - cum[:, None, :], -jnp.inf))
        scores = jnp.einsum('bthk,btshk->btsh', qc, decay * kc[:, None])
        o_intra = jnp.einsum('btsh,bshv->bthv', scores, ic)
        last = cum[:, -1]
        S = jnp.exp(last)[..., None] * S + jnp.einsum('bshk,bshv->bhkv', kc * jnp.exp(last[:, None] - cum), ic)
        return S, o_inter + o_intra

    s_t, o = lax.scan(step, s0.astype(jnp.float32), (chunks(q), chunks(k), chunks(i), chunks(log_f)))
    return jnp.moveaxis(o, 0, 1).reshape(b, t, h, -1), s_t


def group_rmsnorm(o, g):
    return o * lax.rsqrt(jnp.mean(o * o, axis=-1, keepdims=True) + RMS_EPS) * g


def memory_kv(mem, w_mem_kv):
    b, m, _ = mem.shape
    mk, mv = jnp.split(mem @ w_mem_kv, 2, axis=-1)
    return mk.reshape(b, m, M_HEADS, M_HEAD_DIM), mv.reshape(b, m, M_HEADS, M_HEAD_DIM)


def memory_attention(q, mk, mv):
    s = jnp.einsum('bthd,bmhd->bhtm', q, mk.astype(q.dtype)).astype(jnp.float32) * (M_HEAD_DIM ** -0.5)
    p = jax.nn.softmax(s, axis=-1).astype(q.dtype)
    return jnp.einsum('bhtm,bmhd->bthd', p, mv.astype(q.dtype)).reshape(q.shape[0], q.shape[1], -1)


def peer_ffn(x, w_pq, sub_k1, sub_k2, u_tab, v_tab):
    lead = x.shape[:-1]
    d = x.shape[-1]
    xt = x.reshape(-1, d)
    n = xt.shape[0]
    q = (xt @ w_pq).reshape(n, PEER_HEADS, 2, PEER_DKEY // 2)
    s1 = jnp.einsum('nhd,kd->nhk', q[:, :, 0], sub_k1).astype(jnp.float32)
    s2 = jnp.einsum('nhd,kd->nhk', q[:, :, 1], sub_k2).astype(jnp.float32)
    v1, i1 = lax.top_k(s1, PEER_TOPK)
    v2, i2 = lax.top_k(s2, PEER_TOPK)
    cand = (v1[..., :, None] + v2[..., None, :]).reshape(n, PEER_HEADS, PEER_TOPK * PEER_TOPK)
    cand_idx = (i1[..., :, None] * PEER_NKEYS + i2[..., None, :]).reshape(n, PEER_HEADS, PEER_TOPK * PEER_TOPK)
    best, pos = lax.top_k(cand, PEER_TOPK)
    experts = jnp.take_along_axis(cand_idx, pos, axis=-1).reshape(n, PEER_HEADS * PEER_TOPK)
    gates = jax.nn.softmax(best, axis=-1).reshape(n, PEER_HEADS * PEER_TOPK)
    tb = min(PEER_TOKEN_BLOCK, n)
    nb = -(-n // tb)
    pad = nb * tb - n
    xp = jnp.pad(xt, ((0, pad), (0, 0))).reshape(nb, tb, d)
    ep = jnp.pad(experts, ((0, pad), (0, 0))).reshape(nb, tb, -1)
    gp = jnp.pad(gates, ((0, pad), (0, 0))).reshape(nb, tb, -1)

    def block(args):
        xb, eb, gb = args
        act = jax.nn.gelu(jnp.einsum('nd,ned->ne', xb, u_tab[eb]), approximate=False)
        return jnp.einsum('ne,ned->nd', gb.astype(xb.dtype) * act, v_tab[eb])

    out = lax.map(block, (xp, ep, gp))
    return out.reshape(nb * tb, d)[:n].reshape(lead + (d,))


def trunk_layer(x, pos, attend, s0, mem_k, mem_v, w_in, lb, norm_g, p_a, p_b, p_m, w_out,
                ln1_g, ln1_b, w_pq, sub_k1, sub_k2, peer_u, peer_v, ln2_g, ln2_b):
    b, t, _ = x.shape
    offs = np.cumsum(IN_SPLITS)[:-1].tolist()
    aq, ak, av, iq, ik, iw, bq, bf, bi, bg, mq, ga, gb, gm = jnp.split(x @ w_in, offs, axis=-1)
    aq = rope(aq.reshape(b, t, A_HEADS, A_HEAD_DIM), pos)
    ak = rope(ak.reshape(b, t, A_KV_HEADS, A_HEAD_DIM), pos)
    av = av.reshape(b, t, A_KV_HEADS, A_HEAD_DIM)
    iq = rope(iq.reshape(b, t, IDX_HEADS, IDX_DIM), pos)
    ik = rope(ik[:, :, None, :], pos)[:, :, 0]
    o_a = attend(aq, ak, av, iq, iw, ik)
    o_b, s_new = hgrn2_recurrence(bq.reshape(b, t, B_HEADS, B_HEAD_DIM), bf.reshape(b, t, B_HEADS, B_HEAD_DIM),
                                  bi.reshape(b, t, B_HEADS, B_HEAD_DIM), lb, s0)
    o_b = group_rmsnorm(o_b, norm_g) * jax.nn.silu(bg.reshape(b, t, B_HEADS, B_HEAD_DIM).astype(jnp.float32))
    o_b = o_b.reshape(b, t, B_W).astype(x.dtype)
    o_m = memory_attention(mq.reshape(b, t, M_HEADS, M_HEAD_DIM), mem_k, mem_v)
    merged = (jax.nn.sigmoid(ga) * (o_a @ p_a) + jax.nn.sigmoid(gb) * (o_b @ p_b)
              + jax.nn.sigmoid(gm) * (o_m @ p_m))
    h = layer_norm(ALPHA * x + merged @ w_out, ln1_g, ln1_b)
    y = layer_norm(ALPHA * h + peer_ffn(h, w_pq, sub_k1, sub_k2, peer_u, peer_v), ln2_g, ln2_b)
    return y, ak, av, ik, s_new


def setup_inputs(seed: int = 0) -> dict:
    key = jax.random.key(seed)
    ks = jax.random.split(key, 32)
    n_pages = PAST_LEN // PAGE_SIZE
    n_used = DEC_BATCH * n_pages
    n_pool = n_used + max(1, n_used // 4)

    def nrm(k, shape, scale):
        return jax.random.normal(k, shape, jnp.float32) * scale

    perm = jax.random.permutation(ks[0], n_pool)
    page_table = perm[:n_used].reshape(DEC_BATCH, n_pages).astype(jnp.int32)
    return {
        'x_prompt': nrm(ks[1], (BATCH, SEQ, D_MODEL), 1.0),
        'x_sample': nrm(ks[2], (DEC_BATCH, DEC_SEQ, D_MODEL), 1.0),
        'cache_k': nrm(ks[3], (DEPTH, n_pool, PAGE_SIZE, A_KV_HEADS, A_HEAD_DIM), 1.0),
        'cache_v': nrm(ks[4], (DEPTH, n_pool, PAGE_SIZE, A_KV_HEADS, A_HEAD_DIM), 1.0),
        'cache_idx_k': nrm(ks[5], (DEPTH, n_pool, PAGE_SIZE, IDX_DIM), 1.0),
        'cache_mem_k': nrm(ks[6], (DEPTH, DEC_BATCH, MEM_TOKENS, M_HEADS, M_HEAD_DIM), 1.0),
        'cache_mem_v': nrm(ks[7], (DEPTH, DEC_BATCH, MEM_TOKENS, M_HEADS, M_HEAD_DIM), 1.0),
        'state_hgrn': nrm(ks[8], (DEPTH, DEC_BATCH, B_HEADS, B_HEAD_DIM, B_HEAD_DIM), 0.3),
        'page_table': page_table,
        'mem_prompt': nrm(ks[9], (BATCH, MEM_TOKENS, D_MODEL), 1.0),
        'w_in': nrm(ks[10], (DEPTH, D_MODEL, IN_COLS), D_MODEL ** -0.5),
        'w_mem_kv': nrm(ks[11], (DEPTH, D_MODEL, 2 * M_W), D_MODEL ** -0.5),
        'hgrn_lb': nrm(ks[12], (DEPTH + 1, B_W), 0.5),
        'hgrn_norm_g': 1.0 + nrm(ks[13], (DEPTH, B_HEAD_DIM), 0.02),
        'p_a': nrm(ks[14], (DEPTH, A_Q_W, D_MODEL), A_Q_W ** -0.5),
        'p_b': nrm(ks[15], (DEPTH, B_W, D_MODEL), B_W ** -0.5),
        'p_m': nrm(ks[16], (DEPTH, M_W, D_MODEL), M_W ** -0.5),
        'w_out': nrm(ks[17], (DEPTH, D_MODEL, D_MODEL), BETA * D_MODEL ** -0.5),
        'ln1_g': 1.0 + nrm(ks[18], (DEPTH, D_MODEL), 0.02),
        'ln1_b': nrm(ks[19], (DEPTH, D_MODEL), 0.02),
        'w_pq': nrm(ks[20], (DEPTH, D_MODEL, PEER_HEADS * PEER_DKEY), D_MODEL ** -0.5),
        'peer_sub_k1': nrm(ks[21], (DEPTH, PEER_NKEYS, PEER_DKEY // 2), (PEER_DKEY // 2) ** -0.5),
        'peer_sub_k2': nrm(ks[22], (DEPTH, PEER_NKEYS, PEER_DKEY // 2), (PEER_DKEY // 2) ** -0.5),
        'peer_u': nrm(ks[23], (DEPTH, PEER_EXPERTS, D_MODEL), D_MODEL ** -0.5),
        'peer_v': nrm(ks[24], (DEPTH, PEER_EXPERTS, D_MODEL), BETA),
        'ln2_g': 1.0 + nrm(ks[25], (DEPTH, D_MODEL), 0.02),
        'ln2_b': nrm(ks[26], (DEPTH, D_MODEL), 0.02),
    }


def reference(x_prompt, x_sample, cache_k, cache_v, cache_idx_k, cache_mem_k, cache_mem_v, state_hgrn,
              page_table, mem_prompt, w_in, w_mem_kv, hgrn_lb, hgrn_norm_g, p_a, p_b, p_m, w_out,
              ln1_g, ln1_b, w_pq, peer_sub_k1, peer_sub_k2, peer_u, peer_v, ln2_g, ln2_b):
    b, t = x_prompt.shape[:2]
    ts = x_sample.shape[1]
    past_len = page_table.shape[1] * PAGE_SIZE
    pos_p = jnp.arange(t, dtype=jnp.int32)
    pos_s = past_len + jnp.arange(ts, dtype=jnp.int32)
    lower_bounds = jnp.cumsum(jax.nn.softmax(hgrn_lb.astype(jnp.float32), axis=0), axis=0)
    h_p, h_s = x_prompt, x_sample
    kp, vp, ikp, mkp, mvp, sp = [], [], [], [], [], []
    k_s, v_s, ik_s, s_s = [], [], [], []
    for l in range(DEPTH):
        lw = (w_in[l], lower_bounds[l].reshape(B_HEADS, B_HEAD_DIM), hgrn_norm_g[l], p_a[l], p_b[l], p_m[l],
              w_out[l], ln1_g[l], ln1_b[l], w_pq[l], peer_sub_k1[l], peer_sub_k2[l], peer_u[l], peer_v[l],
              ln2_g[l], ln2_b[l])
        mk_l, mv_l = memory_kv(mem_prompt, w_mem_kv[l])
        s0 = jnp.zeros((b, B_HEADS, B_HEAD_DIM, B_HEAD_DIM), jnp.float32)
        h_p, k_l, v_l, ik_l, s_l = trunk_layer(h_p, pos_p, dsa_prompt, s0, mk_l, mv_l, *lw)
        kp.append(k_l); vp.append(v_l); ikp.append(ik_l); mkp.append(mk_l); mvp.append(mv_l); sp.append(s_l)
        attend_s = functools.partial(dsa_sample, cache_k=cache_k[l], cache_v=cache_v[l],
                                     cache_idx_k=cache_idx_k[l], page_table=page_table)
        h_s, k_l, v_l, ik_l, s_l = trunk_layer(h_s, pos_s, attend_s, state_hgrn[l], cache_mem_k[l],
                                               cache_mem_v[l], *lw)
        k_s.append(k_l); v_s.append(v_l); ik_s.append(ik_l); s_s.append(s_l)
    return (h_p, h_s, jnp.stack(kp), jnp.stack(vp), jnp.stack(ikp), jnp.stack(mkp), jnp.stack(mvp),
            jnp.stack(sp), jnp.stack(k_s), jnp.stack(v_s), jnp.stack(ik_s), jnp.stack(s_s))
```

```python
import functools
import math

import jax
import jax.numpy as jnp
import numpy as np
from jax import lax
from jax.experimental import pallas as pl
from jax.experimental.pallas import tpu as pltpu

F32 = jnp.float32
BF16 = jnp.bfloat16
I32 = jnp.int32

A_HEADS, A_KV_HEADS, A_HEAD_DIM = 16, 4, 128
IDX_HEADS, IDX_DIM = 32, 64
TOPK_MAX = 256
B_HEADS, B_HEAD_DIM = 8, 128
M_HEADS, M_HEAD_DIM = 4, 256
PEER_HEADS, PEER_NKEYS, PEER_DKEY, PEER_TOPK = 8, 128, 256, 16
PAGE_SIZE = 128
ROPE_THETA = 10000.0
LN_EPS = 1e-5
RMS_EPS = 1e-6

A_Q_W = A_HEADS * A_HEAD_DIM
A_KV_W = A_KV_HEADS * A_HEAD_DIM
IDX_Q_W = IDX_HEADS * IDX_DIM
B_W = B_HEADS * B_HEAD_DIM
M_W = M_HEADS * M_HEAD_DIM

D_MODEL = 4096

_SEGS = (("aq", A_Q_W), ("iq", IDX_Q_W), ("ak", A_KV_W), ("av", A_KV_W), ("bq", B_W), ("bf", B_W),
         ("bi", B_W), ("bg", B_W), ("mq", M_W), ("ga", D_MODEL), ("gb", D_MODEL), ("gm", D_MODEL),
         ("ikw", 128))
COLS = {}
_off = 0
for _name, _w in _SEGS:
    COLS[_name] = _off
    _off += _w
Z_TILE = 512
Z_WIDTH = -(-_off // Z_TILE) * Z_TILE

LANES = 128
Q_BLK = 128
KEY_CHUNK = 512
HGRN_SUB = 16
EXP_CLAMP = 80.0
INT_MIN = -2 ** 31
NEG_BIG = -1e30


def _cp(sem, vmem_mb):
    return pltpu.CompilerParams(dimension_semantics=sem, vmem_limit_bytes=vmem_mb * 2 ** 20)


def _whole_vmem():
    return pl.BlockSpec(memory_space=pltpu.VMEM)


def _mm_kernel(a_ref, b_ref, o_ref):
    o_ref[...] = jnp.dot(a_ref[...], b_ref[...], preferred_element_type=F32).astype(o_ref.dtype)


def matmul(a, b, tm, tn, out_dtype=F32, name="matmul"):
    m, k = a.shape
    n = b.shape[1]
    return pl.pallas_call(
        _mm_kernel,
        grid=(m // tm, n // tn),
        in_specs=[pl.BlockSpec((tm, k), lambda i, j: (i, 0)),
                  pl.BlockSpec((k, tn), lambda i, j: (0, j))],
        out_specs=pl.BlockSpec((tm, tn), lambda i, j: (i, j)),
        out_shape=jax.ShapeDtypeStruct((m, n), out_dtype),
        compiler_params=_cp(("parallel", "parallel"), 48),
        name=name,
    )(a, b)


def _rope_kernel(q_ref, iq_ref, k_ref, ikw_ref, c128_ref, s128_ref, c64_ref, s64_ref,
                 qhm_ref, iqhm_ref, ko_ref, iko_ref):
    c128, s128 = c128_ref[...], s128_ref[...]
    c64, s64 = c64_ref[...], s64_ref[...]
    lane = lax.broadcasted_iota(I32, (Q_BLK, LANES), 1)
    first_half = (lane % IDX_DIM) < (IDX_DIM // 2)

    def rope128(x):
        return x * c128 + pltpu.roll(x, A_HEAD_DIM // 2, axis=1) * s128

    def rope64(x):
        partner = jnp.where(first_half, pltpu.roll(x, LANES - IDX_DIM // 2, axis=1),
                            pltpu.roll(x, IDX_DIM // 2, axis=1))
        return x * c64 + partner * s64

    for h in range(A_HEADS):
        qhm_ref[0, h] = rope128(q_ref[:, h * LANES:(h + 1) * LANES]).astype(qhm_ref.dtype)
    for h in range(A_KV_HEADS):
        ko_ref[:, h * LANES:(h + 1) * LANES] = rope128(k_ref[:, h * LANES:(h + 1) * LANES])
    for p in range(IDX_HEADS // 2):
        r = rope64(iq_ref[:, p * LANES:(p + 1) * LANES]).astype(iqhm_ref.dtype)
        iqhm_ref[0, 2 * p] = r[:, :IDX_DIM]
        iqhm_ref[0, 2 * p + 1] = r[:, IDX_DIM:]
    iko_ref[...] = rope64(ikw_ref[...])[:, :IDX_DIM]


def rope_all(z, col, tabs):
    nt = z.shape[0]
    nb = nt // Q_BLK
    c128, s128, c64, s64 = tabs
    tab_spec = pl.BlockSpec((Q_BLK, LANES), lambda i: (i, 0))
    return pl.pallas_call(
        _rope_kernel,
        grid=(nb,),
        in_specs=[pl.BlockSpec((Q_BLK, A_Q_W), lambda i: (i, col["aq"] // A_Q_W)),
                  pl.BlockSpec((Q_BLK, IDX_Q_W), lambda i: (i, col["iq"] // IDX_Q_W)),
                  pl.BlockSpec((Q_BLK, A_KV_W), lambda i: (i, col["ak"] // A_KV_W)),
                  pl.BlockSpec((Q_BLK, LANES), lambda i: (i, col["ikw"] // LANES)),
                  tab_spec, tab_spec, tab_spec, tab_spec],
        out_specs=[pl.BlockSpec((1, A_HEADS, Q_BLK, A_HEAD_DIM), lambda i: (i, 0, 0, 0)),
                   pl.BlockSpec((1, IDX_HEADS, Q_BLK, IDX_DIM), lambda i: (i, 0, 0, 0)),
                   pl.BlockSpec((Q_BLK, A_KV_W), lambda i: (i, 0)),
                   pl.BlockSpec((Q_BLK, IDX_DIM), lambda i: (i, 0))],
        out_shape=[jax.ShapeDtypeStruct((nb, A_HEADS, Q_BLK, A_HEAD_DIM), BF16),
                   jax.ShapeDtypeStruct((nb, IDX_HEADS, Q_BLK, IDX_DIM), BF16),
                   jax.ShapeDtypeStruct((nt, A_KV_W), F32),
                   jax.ShapeDtypeStruct((nt, IDX_DIM), F32)],
        compiler_params=_cp(("parallel",), 32),
        name="rope",
    )(z, z, z, z, c128, s128, c64, s64)


def rope_tables(pos):
    def tab(half, reps):
        inv_freq = ROPE_THETA ** (-jnp.arange(half, dtype=F32) / half)
        ang = pos.astype(F32)[:, None] * inv_freq[None, :]
        c, s = jnp.cos(ang), jnp.sin(ang)
        return jnp.tile(jnp.concatenate([c, c], 1), (1, reps)), jnp.tile(jnp.concatenate([-s, s], 1), (1, reps))
    c128, s128 = tab(A_HEAD_DIM // 2, 1)
    c64, s64 = tab(IDX_DIM // 2, 2)
    return c128, s128, c64, s64


def _sort_key(x):
    i = pltpu.bitcast(x, I32)
    return i ^ ((i >> 31) & 0x7FFFFFFF)


def _kth_largest_key(count_ge, k, rows):
    def body(it, cur):
        bit = jnp.left_shift(jnp.int32(1), 31 - it)
        cand = cur | bit
        ok = count_ge(cand ^ INT_MIN) >= k
        return jnp.where(ok, cand, cur)
    cur = lax.fori_loop(0, 32, body, jnp.zeros((rows, 1), I32))
    return jnp.maximum(cur ^ INT_MIN, INT_MIN + 1)


def _flash_step(s, mask, v, m_ref, l_ref, acc_ref, g):
    hh, rr, ss = s.shape
    s = jnp.where(mask[None], s, -jnp.inf)
    m_old = m_ref[g]
    m_new = jnp.maximum(m_old, jnp.max(s, axis=-1, keepdims=True))
    p = jnp.exp(s - m_new)
    alpha = jnp.exp(m_old - m_new)
    l_ref[g] = alpha * l_ref[g] + jnp.sum(p, axis=-1, keepdims=True)
    pv = jnp.dot(p.reshape(hh * rr, ss).astype(BF16), v, preferred_element_type=F32)
    acc_ref[g] = alpha * acc_ref[g] + pv.reshape(hh, rr, -1)
    m_ref[g] = m_new


def _dsa_prompt_kernel(iq_ref, w_ref, q_ref, ik_ref, k_ref, v_ref, o_ref,
                       key_ref, wb_ref, m_ref, l_ref, acc_ref, *, topk, w_lane0):
    i = pl.program_id(0)
    nch = (i * Q_BLK + Q_BLK + KEY_CHUNK - 1) // KEY_CHUNK
    hpg = A_HEADS // A_KV_HEADS
    heads_per_dot = 8

    wscale = (IDX_DIM ** -0.5) * (IDX_HEADS ** -0.5)
    wt = w_ref[...] * wscale
    for h in range(IDX_HEADS):
        wb_ref[h] = jnp.broadcast_to(wt[:, w_lane0 + h:w_lane0 + h + 1], (Q_BLK, LANES))

    q_pos = i * Q_BLK + lax.broadcasted_iota(I32, (Q_BLK, KEY_CHUNK), 0)
    k_off = lax.broadcasted_iota(I32, (Q_BLK, KEY_CHUNK), 1)

    def score_chunk(c, carry):
        k0 = pl.multiple_of(c * KEY_CHUNK, KEY_CHUNK)
        ikc = ik_ref[pl.ds(k0, KEY_CHUNK), :]
        ntile = KEY_CHUNK // LANES
        acc = [jnp.zeros((Q_BLK, LANES), F32)] * ntile
        for hg in range(IDX_HEADS // heads_per_dot):
            lhs = iq_ref[0, hg * heads_per_dot * Q_BLK:(hg + 1) * heads_per_dot * Q_BLK, :]
            d = lax.dot_general(lhs, ikc, (((1,), (1,)), ((), ())), preferred_element_type=F32)
            for hl in range(heads_per_dot):
                wb = wb_ref[hg * heads_per_dot + hl]
                r = jnp.maximum(d[hl * Q_BLK:(hl + 1) * Q_BLK], 0.0)
                acc = [acc[j] + wb * r[:, j * LANES:(j + 1) * LANES] for j in range(ntile)]
        sc = jnp.concatenate(acc, axis=1)
        keys = jnp.where(k0 + k_off <= q_pos, _sort_key(sc), INT_MIN)
        key_ref[c] = keys
        return carry

    lax.fori_loop(0, nch, score_chunk, 0)

    def count_ge(t):
        def body(c, cnt):
            ge = jnp.where(key_ref[c] >= t, 1, 0)
            for j in range(KEY_CHUNK // LANES):
                cnt = cnt + ge[:, j * LANES:(j + 1) * LANES]
            return cnt
        cnt = lax.fori_loop(0, nch, body, jnp.zeros((Q_BLK, LANES), I32))
        return jnp.sum(cnt, axis=1, keepdims=True)

    thr = _kth_largest_key(count_ge, topk, Q_BLK)

    m_ref[...] = jnp.full(m_ref.shape, NEG_BIG, F32)
    l_ref[...] = jnp.zeros(l_ref.shape, F32)
    acc_ref[...] = jnp.zeros(acc_ref.shape, F32)
    scale = A_HEAD_DIM ** -0.5

    def attend_chunk(c, carry):
        k0 = pl.multiple_of(c * KEY_CHUNK, KEY_CHUNK)
        mask = key_ref[c] >= thr
        for g in range(A_KV_HEADS):
            qg = q_ref[0, g * hpg * Q_BLK:(g + 1) * hpg * Q_BLK, :]
            kc = k_ref[pl.ds(k0, KEY_CHUNK), g * LANES:(g + 1) * LANES]
            vc = v_ref[pl.ds(k0, KEY_CHUNK), g * LANES:(g + 1) * LANES]
            s = lax.dot_general(qg, kc, (((1,), (1,)), ((), ())), preferred_element_type=F32) * scale
            _flash_step(s.reshape(hpg, Q_BLK, KEY_CHUNK), mask, vc, m_ref, l_ref, acc_ref, g)
        return carry

    lax.fori_loop(0, nch, attend_chunk, 0)
    for g in range(A_KV_HEADS):
        o = acc_ref[g] / l_ref[g]
        o_ref[0, g * hpg * Q_BLK:(g + 1) * hpg * Q_BLK, :] = o.reshape(hpg * Q_BLK, A_HEAD_DIM).astype(o_ref.dtype)


def dsa_prompt(iq_hm, z, col, q_hm, ik, k, v, t):
    nb = t // Q_BLK
    topk = min(TOPK_MAX, t // 4)
    nch_max = (t + KEY_CHUNK - 1) // KEY_CHUNK
    hpg = A_HEADS // A_KV_HEADS
    kern = functools.partial(_dsa_prompt_kernel, topk=topk, w_lane0=IDX_DIM)
    return pl.pallas_call(
        kern,
        grid=(nb,),
        in_specs=[pl.BlockSpec((1, IDX_HEADS * Q_BLK, IDX_DIM), lambda i: (i, 0, 0)),
                  pl.BlockSpec((Q_BLK, LANES), lambda i: (i, col["ikw"] // LANES)),
                  pl.BlockSpec((1, A_HEADS * Q_BLK, A_HEAD_DIM), lambda i: (i, 0, 0)),
                  _whole_vmem(), _whole_vmem(), _whole_vmem()],
        out_specs=pl.BlockSpec((1, A_HEADS * Q_BLK, A_HEAD_DIM), lambda i: (i, 0, 0)),
        out_shape=jax.ShapeDtypeStruct((nb, A_HEADS * Q_BLK, A_HEAD_DIM), BF16),
        scratch_shapes=[pltpu.VMEM((nch_max, Q_BLK, KEY_CHUNK), I32),
                        pltpu.VMEM((IDX_HEADS, Q_BLK, LANES), F32),
                        pltpu.VMEM((A_KV_HEADS, hpg, Q_BLK, 1), F32),
                        pltpu.VMEM((A_KV_HEADS, hpg, Q_BLK, 1), F32),
                        pltpu.VMEM((A_KV_HEADS, hpg, Q_BLK, A_HEAD_DIM), F32)],
        compiler_params=_cp(("arbitrary",), 56),
        name="dsa_prompt",
    )(iq_hm, z, q_hm, ik, k, v)


S_ROWS = 8
IDX_PAGES = 8
ATT_PAGES = 4


def _dsa_sample_index_kernel(pt_ref, iq_ref, w_ref, ikn_ref, *rest, past_len, t_new, topk):
    pages = rest[:IDX_PAGES]
    key_ref, thr_ref = rest[IDX_PAGES], rest[IDX_PAGES + 1]
    j = pl.program_id(1)
    n_past_tiles = past_len // PAGE_SIZE
    wscale = (IDX_DIM ** -0.5) * (IDX_HEADS ** -0.5)
    wt = w_ref[0] * wscale
    wb = [jnp.broadcast_to(wt[:, h:h + 1], (S_ROWS, LANES)) for h in range(IDX_HEADS)]
    iq = iq_ref[0]

    def tile_scores(ik_tile):
        d = lax.dot_general(iq, ik_tile, (((1,), (1,)), ((), ())), preferred_element_type=F32)
        acc = jnp.zeros((S_ROWS, LANES), F32)
        for h in range(IDX_HEADS):
            acc = acc + wb[h] * jnp.maximum(d[h * S_ROWS:(h + 1) * S_ROWS], 0.0)
        return acc

    for r in range(IDX_PAGES):
        key_ref[0, j * IDX_PAGES + r] = _sort_key(tile_scores(pages[r][0].astype(BF16)))

    @pl.when(j == pl.num_programs(1) - 1)
    def _():
        row = lax.broadcasted_iota(I32, (S_ROWS, LANES), 0)
        lane = lax.broadcasted_iota(I32, (S_ROWS, LANES), 1)
        valid = (lane < t_new) & (lane <= row)
        key_ref[0, n_past_tiles] = jnp.where(valid, _sort_key(tile_scores(ikn_ref[0])), INT_MIN)

        def count_ge(t):
            def body(c, cnt):
                return cnt + jnp.where(key_ref[0, c] >= t, 1, 0)
            cnt = lax.fori_loop(0, n_past_tiles + 1, body, jnp.zeros((S_ROWS, LANES), I32))
            return jnp.sum(cnt, axis=1, keepdims=True)

        thr = _kth_largest_key(count_ge, topk, S_ROWS)
        thr_ref[0] = jnp.broadcast_to(thr, (S_ROWS, LANES))


def dsa_sample_index(page_table, iq_s, iw_s, ik_new, cache_ik, t_new):
    b, n_pages = page_table.shape
    past_len = n_pages * PAGE_SIZE
    topk = min(TOPK_MAX, (past_len + t_new) // 4)
    steps = n_pages // IDX_PAGES
    kern = functools.partial(_dsa_sample_index_kernel, past_len=past_len, t_new=t_new, topk=topk)

    def page_spec(r):
        return pl.BlockSpec((1, PAGE_SIZE, IDX_DIM), lambda bi, j, pt: (pt[bi, j * IDX_PAGES + r], 0, 0))

    grid_spec = pltpu.PrefetchScalarGridSpec(
        num_scalar_prefetch=1,
        grid=(b, steps),
        in_specs=[pl.BlockSpec((1, IDX_HEADS * S_ROWS, IDX_DIM), lambda bi, j, pt: (bi, 0, 0)),
                  pl.BlockSpec((1, S_ROWS, IDX_HEADS), lambda bi, j, pt: (bi, 0, 0)),
                  pl.BlockSpec((1, PAGE_SIZE, IDX_DIM), lambda bi, j, pt: (bi, 0, 0))]
                 + [page_spec(r) for r in range(IDX_PAGES)],
        out_specs=[pl.BlockSpec((1, n_pages + 1, S_ROWS, LANES), lambda bi, j, pt: (bi, 0, 0, 0)),
                   pl.BlockSpec((1, S_ROWS, LANES), lambda bi, j, pt: (bi, 0, 0))],
    )
    return pl.pallas_call(
        kern,
        grid_spec=grid_spec,
        out_shape=[jax.ShapeDtypeStruct((b, n_pages + 1, S_ROWS, LANES), I32),
                   jax.ShapeDtypeStruct((b, S_ROWS, LANES), I32)],
        compiler_params=_cp(("parallel", "arbitrary"), 32),
        name="dsa_sample_index",
    )(page_table, iq_s, iw_s, ik_new, *([cache_ik] * IDX_PAGES))


def _dsa_sample_attend_kernel(pt_ref, key_ref, thr_ref, q_ref, kn_ref, vn_ref, *rest, n_pages):
    kpages = rest[:ATT_PAGES]
    vpages = rest[ATT_PAGES:2 * ATT_PAGES]
    o_ref, m_ref, l_ref, acc_ref = rest[2 * ATT_PAGES:]
    j = pl.program_id(1)
    hpg = A_HEADS // A_KV_HEADS
    scale = A_HEAD_DIM ** -0.5
    thr = thr_ref[0]

    @pl.when(j == 0)
    def _():
        m_ref[...] = jnp.full(m_ref.shape, NEG_BIG, F32)
        l_ref[...] = jnp.zeros(l_ref.shape, F32)
        acc_ref[...] = jnp.zeros(acc_ref.shape, F32)

    def attend(mask, kc, vc):
        for g in range(A_KV_HEADS):
            qg = q_ref[0, g * hpg * S_ROWS:(g + 1) * hpg * S_ROWS, :]
            kg = kc[:, g * LANES:(g + 1) * LANES]
            vg = vc[:, g * LANES:(g + 1) * LANES]
            s = lax.dot_general(qg, kg, (((1,), (1,)), ((), ())), preferred_element_type=F32) * scale
            _flash_step(s.reshape(hpg, S_ROWS, s.shape[-1]), mask, vg, m_ref, l_ref, acc_ref, g)

    mask = jnp.concatenate([key_ref[0, j * ATT_PAGES + r] >= thr for r in range(ATT_PAGES)], axis=1)
    kc = jnp.concatenate([kp[0].astype(BF16) for kp in kpages], axis=0)
    vc = jnp.concatenate([vp[0].astype(BF16) for vp in vpages], axis=0)
    attend(mask, kc, vc)

    @pl.when(j == pl.num_programs(1) - 1)
    def _():
        attend(key_ref[0, n_pages] >= thr, kn_ref[0], vn_ref[0])
        for g in range(A_KV_HEADS):
            o = acc_ref[g] / l_ref[g]
            o_ref[0, g * hpg * S_ROWS:(g + 1) * hpg * S_ROWS, :] = (
                o.reshape(hpg * S_ROWS, A_HEAD_DIM).astype(o_ref.dtype))


def dsa_sample_attend(page_table, keys, thr, q_s, k_new, v_new, cache_k, cache_v):
    b, n_pages = page_table.shape
    steps = n_pages // ATT_PAGES
    hpg = A_HEADS // A_KV_HEADS
    kern = functools.partial(_dsa_sample_attend_kernel, n_pages=n_pages)

    def page_spec(r):
        return pl.BlockSpec((1, PAGE_SIZE, A_KV_W), lambda bi, j, pt: (pt[bi, j * ATT_PAGES + r], 0, 0))

    grid_spec = pltpu.PrefetchScalarGridSpec(
        num_scalar_prefetch=1,
        grid=(b, steps),
        in_specs=[pl.BlockSpec((1, n_pages + 1, S_ROWS, LANES), lambda bi, j, pt: (bi, 0, 0, 0)),
                  pl.BlockSpec((1, S_ROWS, LANES), lambda bi, j, pt: (bi, 0, 0)),
                  pl.BlockSpec((1, A_HEADS * S_ROWS, A_HEAD_DIM), lambda bi, j, pt: (bi, 0, 0)),
                  pl.BlockSpec((1, PAGE_SIZE, A_KV_W), lambda bi, j, pt: (bi, 0, 0)),
                  pl.BlockSpec((1, PAGE_SIZE, A_KV_W), lambda bi, j, pt: (bi, 0, 0))]
                 + [page_spec(r) for r in range(ATT_PAGES)] * 2,
        out_specs=pl.BlockSpec((1, A_HEADS * S_ROWS, A_HEAD_DIM), lambda bi, j, pt: (bi, 0, 0)),
        scratch_shapes=[pltpu.VMEM((A_KV_HEADS, hpg, S_ROWS, 1), F32),
                        pltpu.VMEM((A_KV_HEADS, hpg, S_ROWS, 1), F32),
                        pltpu.VMEM((A_KV_HEADS, hpg, S_ROWS, A_HEAD_DIM), F32)],
    )
    return pl.pallas_call(
        kern,
        grid_spec=grid_spec,
        out_shape=jax.ShapeDtypeStruct((b, A_HEADS * S_ROWS, A_HEAD_DIM), BF16),
        compiler_params=_cp(("parallel", "arbitrary"), 32),
        name="dsa_sample_attend",
    )(page_table, keys, thr, q_s, k_new, v_new, *([cache_k] * ATT_PAGES), *([cache_v] * ATT_PAGES))


def _cumsum_rows(x):
    n = x.shape[0]
    row = lax.broadcasted_iota(I32, x.shape, 0)
    d = 1
    while d < n:
        x = x + jnp.where(row >= d, pltpu.roll(x, d, axis=0), 0.0)
        d *= 2
    return x


def _hgrn_kernel(q_ref, f_ref, i_ref, g_ref, lb_ref, ng_ref, s0_ref, o_ref, so_ref, st_ref,
                 *, chunk, sub, t_valid):
    n = pl.program_id(1)
    nsub = chunk // sub

    @pl.when(n == 0)
    def _():
        for h in range(B_HEADS):
            st_ref[h] = s0_ref[0, h].T

    lbx = lb_ref[...]
    lbe = jnp.exp(lbx - jnp.max(lbx, axis=0, keepdims=True))
    lb_all = lbe[0:1] / jnp.sum(lbe, axis=0, keepdims=True)
    row = lax.broadcasted_iota(I32, (chunk, B_HEAD_DIM), 0)
    valid = row < t_valid
    tt = lax.broadcasted_iota(I32, (chunk, nsub * chunk), 0)
    cc = lax.broadcasted_iota(I32, (chunk, nsub * chunk), 1)
    pair_ok = ((cc // chunk) == (tt // sub)) & ((cc % chunk) <= tt)
    ng = ng_ref[...]

    for h in range(B_HEADS):
        sl = slice(h * B_HEAD_DIM, (h + 1) * B_HEAD_DIM)
        lb = lb_all[:, sl]
        f = lb + (1.0 - lb) * jax.nn.sigmoid(f_ref[:, sl])
        logf = jnp.where(valid, jnp.log(f), 0.0)
        kk = jnp.where(valid, 1.0 - f, 0.0)
        q = q_ref[:, sl]
        iv = i_ref[:, sl]
        cum = _cumsum_rows(logf)
        last = cum[chunk - 1:chunk]
        st = st_ref[h]

        o = lax.dot_general((q * jnp.exp(cum)).astype(BF16), st.astype(BF16),
                            (((1,), (1,)), ((), ())), preferred_element_type=F32)

        refs = [jnp.zeros((1, B_HEAD_DIM), F32)] + [cum[s * sub - 1:s * sub] for s in range(1, nsub)]
        ref_row = refs[0]
        for s in range(1, nsub):
            ref_row = jnp.where(row >= s * sub, refs[s], ref_row)
        qt = (q * jnp.exp(cum - ref_row)).astype(BF16)
        kcat = jnp.concatenate(
            [(kk * jnp.exp(jnp.minimum(r - cum, EXP_CLAMP))).astype(BF16) for r in refs], axis=0)
        a = lax.dot_general(qt, kcat, (((1,), (1,)), ((), ())), preferred_element_type=F32)
        p = jnp.where(pair_ok, a, 0.0).astype(BF16)
        icat = jnp.concatenate([iv.astype(BF16)] * nsub, axis=0)
        o = o + jnp.dot(p, icat, preferred_element_type=F32)

        kd = (kk * jnp.exp(last - cum)).astype(BF16)
        upd = lax.dot_general(iv.astype(BF16), kd, (((0,), (0,)), ((), ())), preferred_element_type=F32)
        st_ref[h] = st * jnp.exp(last) + upd

        on = o * lax.rsqrt(jnp.mean(o * o, axis=-1, keepdims=True) + RMS_EPS) * ng
        gate = g_ref[:, sl]
        o_ref[:, sl] = (on * (gate * jax.nn.sigmoid(gate))).astype(o_ref.dtype)

    @pl.when(n == pl.num_programs(1) - 1)
    def _():
        for h in range(B_HEADS):
            so_ref[0, h] = st_ref[h].T


def hgrn(zsrc, colblk, lb, ng, s0, batch, t_pad, chunk, sub, t_valid):
    nch = t_pad // chunk
    kern = functools.partial(_hgrn_kernel, chunk=chunk, sub=sub, t_valid=t_valid)

    def zspec(cb):
        return pl.BlockSpec((chunk, B_W), lambda b, n: (b * nch + n, cb))

    return pl.pallas_call(
        kern,
        grid=(batch, nch),
        in_specs=[zspec(colblk[0]), zspec(colblk[1]), zspec(colblk[2]), zspec(colblk[3]),
                  pl.BlockSpec(lb.shape, lambda b, n: (0, 0)),
                  pl.BlockSpec((1, B_HEAD_DIM), lambda b, n: (0, 0)),
                  pl.BlockSpec((1, B_HEADS, B_HEAD_DIM, B_HEAD_DIM), lambda b, n: (b, 0, 0, 0))],
        out_specs=[pl.BlockSpec((chunk, B_W), lambda b, n: (b * nch + n, 0)),
                   pl.BlockSpec((1, B_HEADS, B_HEAD_DIM, B_HEAD_DIM), lambda b, n: (b, 0, 0, 0))],
        out_shape=[jax.ShapeDtypeStruct((batch * t_pad, B_W), BF16),
                   jax.ShapeDtypeStruct((batch, B_HEADS, B_HEAD_DIM, B_HEAD_DIM), F32)],
        scratch_shapes=[pltpu.VMEM((B_HEADS, B_HEAD_DIM, B_HEAD_DIM), F32)],
        compiler_params=_cp(("parallel", "arbitrary"), 32),
        name="hgrn",
    )(zsrc, zsrc, zsrc, zsrc, lb, ng, s0)


def _mem_attn_kernel(q_ref, mk_ref, mv_ref, o_ref):
    scale = M_HEAD_DIM ** -0.5
    for h in range(M_HEADS):
        sl = slice(h * M_HEAD_DIM, (h + 1) * M_HEAD_DIM)
        q = q_ref[0, :, sl].astype(BF16)
        mk = mk_ref[0, :, sl].astype(BF16)
        mv = mv_ref[0, :, sl].astype(BF16)
        s = lax.dot_general(q, mk, (((1,), (1,)), ((), ())), preferred_element_type=F32) * scale
        p = jnp.exp(s - jnp.max(s, axis=-1, keepdims=True))
        p = p / jnp.sum(p, axis=-1, keepdims=True)
        o_ref[0, :, sl] = jnp.dot(p.astype(BF16), mv, preferred_element_type=F32).astype(o_ref.dtype)


def mem_attention(q3, qcolblk, mk3, mkcolblk, mv3, mvcolblk, batch, t, tq):
    m = mk3.shape[1]
    return pl.pallas_call(
        _mem_attn_kernel,
        grid=(batch, t // tq),
        in_specs=[pl.BlockSpec((1, tq, M_W), lambda b, i: (b, i, qcolblk)),
                  pl.BlockSpec((1, m, M_W), lambda b, i: (b, 0, mkcolblk)),
                  pl.BlockSpec((1, m, M_W), lambda b, i: (b, 0, mvcolblk))],
        out_specs=pl.BlockSpec((1, tq, M_W), lambda b, i: (b, i, 0)),
        out_shape=jax.ShapeDtypeStruct((batch, t, M_W), BF16),
        compiler_params=_cp(("parallel", "parallel"), 32),
        name="mem_attention",
    )(q3, mk3, mv3)


def _merge_kernel(oa_ref, ob_ref, om_ref, pa_ref, pb_ref, pm_ref, ga_ref, gb_ref, gm_ref, o_ref):
    acc = jax.nn.sigmoid(ga_ref[...]) * jnp.dot(oa_ref[...], pa_ref[...], preferred_element_type=F32)
    acc += jax.nn.sigmoid(gb_ref[...]) * jnp.dot(ob_ref[...], pb_ref[...], preferred_element_type=F32)
    acc += jax.nn.sigmoid(gm_ref[...]) * jnp.dot(om_ref[...], pm_ref[...], preferred_element_type=F32)
    o_ref[...] = acc.astype(o_ref.dtype)


def merge(oa, ob, om, pa, pb, pm, z, col, tm, tn):
    nt = oa.shape[0]
    d = pa.shape[1]

    def act(w):
        return pl.BlockSpec((tm, w), lambda i, j: (i, 0))

    def wgt(w):
        return pl.BlockSpec((w, tn), lambda i, j: (0, j))

    def gate(name):
        return pl.BlockSpec((tm, tn), lambda i, j: (i, col[name] // tn + j))

    return pl.pallas_call(
        _merge_kernel,
        grid=(nt // tm, d // tn),
        in_specs=[act(oa.shape[1]), act(ob.shape[1]), act(om.shape[1]),
                  wgt(pa.shape[0]), wgt(pb.shape[0]), wgt(pm.shape[0]),
                  gate("ga"), gate("gb"), gate("gm")],
        out_specs=pl.BlockSpec((tm, tn), lambda i, j: (i, j)),
        out_shape=jax.ShapeDtypeStruct((nt, d), BF16),
        compiler_params=_cp(("parallel", "parallel"), 48),
        name="merge",
    )(oa, ob, om, pa, pb, pm, z, z, z)


def _resid_mm_kernel(a_ref, b_ref, x_ref, o_ref, *, alpha):
    o_ref[...] = alpha * x_ref[...] + jnp.dot(a_ref[...], b_ref[...], preferred_element_type=F32)


def resid_matmul(a, b, x, alpha, tm, tn):
    m, k = a.shape
    n = b.shape[1]
    return pl.pallas_call(
        functools.partial(_resid_mm_kernel, alpha=alpha),
        grid=(m // tm, n // tn),
        in_specs=[pl.BlockSpec((tm, k), lambda i, j: (i, 0)),
                  pl.BlockSpec((k, tn), lambda i, j: (0, j)),
                  pl.BlockSpec((tm, tn), lambda i, j: (i, j))],
        out_specs=pl.BlockSpec((tm, tn), lambda i, j: (i, j)),
        out_shape=jax.ShapeDtypeStruct((m, n), F32),
        compiler_params=_cp(("parallel", "parallel"), 48),
        name="resid_matmul",
    )(a, b, x)


def _ln(x, g, b):
    mu = jnp.mean(x, axis=-1, keepdims=True)
    xc = x - mu
    var = jnp.mean(xc * xc, axis=-1, keepdims=True)
    return xc * lax.rsqrt(var + LN_EPS) * g + b


def _ln_kernel(x_ref, g_ref, b_ref, o_ref, ob_ref):
    y = _ln(x_ref[...], g_ref[...], b_ref[...])
    o_ref[...] = y
    ob_ref[...] = y.astype(ob_ref.dtype)


def layer_norm_dual(x, g, b, tm):
    n, d = x.shape
    row = pl.BlockSpec((tm, d), lambda i: (i, 0))
    vec = pl.BlockSpec((1, d), lambda i: (0, 0))
    return pl.pallas_call(
        _ln_kernel,
        grid=(n // tm,),
        in_specs=[row, vec, vec],
        out_specs=[row, row],
        out_shape=[jax.ShapeDtypeStruct((n, d), F32), jax.ShapeDtypeStruct((n, d), BF16)],
        compiler_params=_cp(("parallel",), 48),
        name="layer_norm1",
    )(x, g, b)


def _ln_resid_kernel(h_ref, p_ref, g_ref, b_ref, o_ref, *, alpha):
    o_ref[...] = _ln(alpha * h_ref[...] + p_ref[...], g_ref[...], b_ref[...])


def layer_norm_resid(h, p, g, b, alpha, tm):
    n, d = h.shape
    row = pl.BlockSpec((tm, d), lambda i: (i, 0))
    vec = pl.BlockSpec((1, d), lambda i: (0, 0))
    return pl.pallas_call(
        functools.partial(_ln_resid_kernel, alpha=alpha),
        grid=(n // tm,),
        in_specs=[row, row, vec, vec],
        out_specs=row,
        out_shape=jax.ShapeDtypeStruct((n, d), F32),
        compiler_params=_cp(("parallel",), 48),
        name="layer_norm2",
    )(h, p, g, b)


def _top_ranked(x, n_top):
    kdim = x.shape[0]
    idx = lax.broadcasted_iota(I32, x.shape, 0).astype(F32)
    rank = jnp.full(x.shape, float(n_top), F32)
    vals = []
    for a in range(n_top):
        m = jnp.max(x, axis=0, keepdims=True)
        first = jnp.min(jnp.where(x == m, idx, float(kdim)), axis=0, keepdims=True)
        sel = idx == first
        rank = jnp.where(sel, float(a), rank)
        x = jnp.where(sel, -jnp.inf, x)
        vals.append(m)
    return rank, vals


def _peer_route_kernel(pq_ref, k1_ref, k2_ref, rank2_ref, cnt1_ref, e1_ref, e2_ref):
    half = PEER_DKEY // 2
    k1 = k1_ref[...].astype(BF16)
    k2 = k2_ref[...].astype(BF16)
    nt = (((1,), (1,)), ((), ()))
    for h in range(PEER_HEADS):
        q1 = pq_ref[:, h * PEER_DKEY:h * PEER_DKEY + half].astype(BF16)
        q2 = pq_ref[:, h * PEER_DKEY + half:(h + 1) * PEER_DKEY].astype(BF16)
        s1 = lax.dot_general(k1, q1, nt, preferred_element_type=F32)
        s2 = lax.dot_general(k2, q2, nt, preferred_element_type=F32)
        rank1, v1 = _top_ranked(s1, PEER_TOPK)
        rank2, v2 = _top_ranked(s2, PEER_TOPK)
        v2m = jnp.concatenate(v2, axis=0)
        cand = jnp.concatenate([v1[a] + v2m for a in range(PEER_TOPK)], axis=0)
        crank, _ = _top_ranked(cand, PEER_TOPK)
        chosen = jnp.where(crank < float(PEER_TOPK), 1.0, 0.0)
        cmax = v1[0] + v2[0]
        zsum = jnp.sum(chosen * jnp.exp(cand - cmax), axis=0, keepdims=True)
        cnt1 = jnp.zeros_like(s1)
        for a in range(PEER_TOPK):
            m_a = jnp.sum(chosen[a * PEER_TOPK:(a + 1) * PEER_TOPK], axis=0, keepdims=True)
            cnt1 = jnp.where(rank1 == float(a), m_a, cnt1)
        rank2_ref[h] = rank2
        cnt1_ref[h] = cnt1
        e1_ref[h] = jnp.exp(s1 - v1[0])
        e2_ref[h] = jnp.exp(s2 - v2[0]) / zsum


def peer_route(pq, k1, k2, tn):
    n = pq.shape[0]
    out = jax.ShapeDtypeStruct((PEER_HEADS, PEER_NKEYS, n), F32)
    ospec = pl.BlockSpec((PEER_HEADS, PEER_NKEYS, tn), lambda i: (0, 0, i))
    kspec = pl.BlockSpec(k1.shape, lambda i: (0, 0))
    return pl.pallas_call(
        _peer_route_kernel,
        grid=(n // tn,),
        in_specs=[pl.BlockSpec((tn, pq.shape[1]), lambda i: (i, 0)), kspec, kspec],
        out_specs=[ospec] * 4,
        out_shape=[out] * 4,
        compiler_params=_cp(("parallel",), 32),
        name="peer_route",
    )(pq, k1, k2)


PEER_EROWS = 4


def _gelu_exact(x):
    return 0.5 * x * (1.0 + lax.erf(x * (2.0 ** -0.5)))


def _peer_dense_kernel(x_ref, u_ref, v_ref, rank2_ref, cnt1_ref, e1_ref, e2_ref, o_ref):
    r = pl.program_id(1)

    @pl.when(r == 0)
    def _():
        o_ref[...] = jnp.zeros(o_ref.shape, o_ref.dtype)

    at = lax.dot_general(u_ref[...], x_ref[...], (((1,), (1,)), ((), ())), preferred_element_type=F32)
    act = _gelu_exact(at)
    gts = []
    for rr in range(PEER_EROWS):
        i1 = r * PEER_EROWS + rr
        g = jnp.zeros((PEER_NKEYS, x_ref.shape[0]), F32)
        for h in range(PEER_HEADS):
            cnt = cnt1_ref[h, pl.ds(i1, 1), :]
            e1 = e1_ref[h, pl.ds(i1, 1), :]
            g = g + jnp.where(rank2_ref[h] < cnt, e2_ref[h] * e1, 0.0)
        gts.append(g)
    zt = (jnp.concatenate(gts, axis=0) * act).astype(BF16)
    o_ref[...] += lax.dot_general(zt, v_ref[...], (((0,), (0,)), ((), ())), preferred_element_type=F32)


def peer_dense(xb, u, v, rank2, cnt1, e1, e2, tn):
    n, d = xb.shape
    ne = u.shape[0]
    te = PEER_EROWS * PEER_NKEYS
    aux = pl.BlockSpec((PEER_HEADS, PEER_NKEYS, tn), lambda j, r: (0, 0, j))
    tab = pl.BlockSpec((te, d), lambda j, r: (r, 0))
    return pl.pallas_call(
        _peer_dense_kernel,
        grid=(n // tn, ne // te),
        in_specs=[pl.BlockSpec((tn, d), lambda j, r: (j, 0)), tab, tab, aux, aux, aux, aux],
        out_specs=pl.BlockSpec((tn, d), lambda j, r: (j, 0)),
        out_shape=jax.ShapeDtypeStruct((n, d), F32),
        compiler_params=_cp(("parallel", "arbitrary"), 60),
        name="peer_dense",
    )(xb, u, v, rank2, cnt1, e1, e2)


TOK_TILE = 512


def _pad_rows(a, n, axis=0):
    pad = [(0, 0)] * a.ndim
    pad[axis] = (0, n - a.shape[axis])
    return jnp.pad(a, pad)


def _pack_w_in(w):
    splits = (A_Q_W, A_KV_W, A_KV_W, IDX_Q_W, IDX_DIM, IDX_HEADS, B_W, B_W, B_W, B_W, M_W,
              D_MODEL, D_MODEL, D_MODEL)
    offs = np.concatenate([[0], np.cumsum(splits)])
    aq, ak, av, iq, ik, iw, bq, bf, bi, bg, mq, ga, gb, gm = [w[:, offs[i]:offs[i + 1]] for i in range(len(splits))]
    used = COLS["ikw"] + IDX_DIM + IDX_HEADS
    tail = jnp.zeros((w.shape[0], Z_WIDTH - used), w.dtype)
    return jnp.concatenate([aq, iq, ak, av, bq, bf, bi, bg, mq, ga, gb, gm, ik, iw, tail], axis=1).astype(BF16)


def kernel(x_prompt, x_sample, cache_k, cache_v, cache_idx_k, cache_mem_k, cache_mem_v, state_hgrn, page_table,
           mem_prompt, w_in, w_mem_kv, hgrn_lb, hgrn_norm_g, p_a, p_b, p_m, w_out, ln1_g, ln1_b, w_pq,
           peer_sub_k1, peer_sub_k2, peer_u, peer_v, ln2_g, ln2_b):
    depth = w_in.shape[0]
    assert depth == 1, "single trunk layer"
    bp, t, d = x_prompt.shape
    bs, ts, _ = x_sample.shape
    assert bp == 1 and d == D_MODEL and t % KEY_CHUNK == 0 and bs * ts == Q_BLK and ts <= S_ROWS
    n_p, n_s = bp * t, bs * ts
    n_real = n_p + n_s
    nt = -(-n_real // TOK_TILE) * TOK_TILE
    n_pool = cache_k.shape[1]
    past_len = page_table.shape[1] * PAGE_SIZE
    alpha = (2 * depth) ** 0.25
    nb_p = n_p // Q_BLK

    x = _pad_rows(jnp.concatenate([x_prompt.reshape(n_p, d), x_sample.reshape(n_s, d)], axis=0), nt)
    z = matmul(x.astype(BF16), _pack_w_in(w_in[0]), TOK_TILE, Z_TILE, name="proj_in")

    def zcols(name, width, lo, hi):
        return z[lo:hi, COLS[name]:COLS[name] + width]

    pos = jnp.concatenate([jnp.arange(t, dtype=I32), past_len + jnp.tile(jnp.arange(ts, dtype=I32), bs),
                           jnp.zeros((nt - n_real,), I32)])
    q_hm, iq_hm, k_rope, ik_rope = rope_all(z, COLS, rope_tables(pos))
    nb = q_hm.shape[0]
    v_all = zcols("av", A_KV_W, 0, n_real)

    o_hm = dsa_prompt(iq_hm.reshape(nb, IDX_HEADS * Q_BLK, IDX_DIM), z, COLS,
                      q_hm.reshape(nb, A_HEADS * Q_BLK, A_HEAD_DIM),
                      ik_rope[:n_p].astype(BF16), k_rope[:n_p].astype(BF16), v_all[:n_p].astype(BF16), t)
    oa_p = o_hm.reshape(nb_p, A_HEADS, Q_BLK, A_HEAD_DIM).transpose(0, 2, 1, 3).reshape(n_p, A_Q_W)

    def per_seq(a, rows):
        return _pad_rows(a.reshape((bs, ts) + a.shape[1:]), rows, axis=1)

    q_s = per_seq(q_hm[nb_p].transpose(1, 0, 2), S_ROWS).transpose(0, 2, 1, 3).reshape(bs, A_HEADS * S_ROWS, A_HEAD_DIM)
    iq_s = per_seq(iq_hm[nb_p].transpose(1, 0, 2), S_ROWS).transpose(0, 2, 1, 3).reshape(bs, IDX_HEADS * S_ROWS, IDX_DIM)
    iw_s = per_seq(z[n_p:n_real, COLS["ikw"] + IDX_DIM:COLS["ikw"] + IDX_DIM + IDX_HEADS], S_ROWS)
    ik_new = per_seq(ik_rope[n_p:n_real], PAGE_SIZE).astype(BF16)
    k_new = per_seq(k_rope[n_p:n_real], PAGE_SIZE).astype(BF16)
    v_new = per_seq(v_all[n_p:n_real], PAGE_SIZE).astype(BF16)
    keys, thr = dsa_sample_index(page_table, iq_s, iw_s, ik_new, cache_idx_k[0], ts)
    os_hm = dsa_sample_attend(page_table, keys, thr, q_s, k_new, v_new,
                              cache_k[0].reshape(n_pool, PAGE_SIZE, A_KV_W), cache_v[0].reshape(n_pool, PAGE_SIZE, A_KV_W))
    oa_s = os_hm.reshape(bs, A_HEADS, S_ROWS, A_HEAD_DIM)[:, :, :ts].transpose(0, 2, 1, 3).reshape(n_s, A_Q_W)
    o_a = _pad_rows(jnp.concatenate([oa_p, oa_s], axis=0), nt)

    b_blk = [COLS[c] // B_W for c in ("bq", "bf", "bi", "bg")]
    ng = hgrn_norm_g[0][None]
    ob_p, st_p = hgrn(z, b_blk, hgrn_lb, ng, jnp.zeros((bp, B_HEADS, B_HEAD_DIM, B_HEAD_DIM), F32),
                      bp, t, 64, HGRN_SUB, 64)
    zs_b = per_seq(z[n_p:n_real, COLS["bq"]:COLS["bq"] + 4 * B_W], S_ROWS).reshape(bs * S_ROWS, 4 * B_W)
    ob_s, st_s = hgrn(zs_b, [0, 1, 2, 3], hgrn_lb, ng, state_hgrn[0], bs, S_ROWS, S_ROWS, S_ROWS, ts)
    ob_s = ob_s.reshape(bs, S_ROWS, B_W)[:, :ts].reshape(n_s, B_W)
    o_b = _pad_rows(jnp.concatenate([ob_p[:n_p], ob_s], axis=0), nt)

    mem_kv = matmul(mem_prompt[0].astype(BF16), w_mem_kv[0].astype(BF16), mem_prompt.shape[1], 512, name="mem_kv")
    om_p = mem_attention(z[None], COLS["mq"] // M_W, mem_kv[None], 0, mem_kv[None], 1, bp, t, 512)[0]
    zs_m = per_seq(zcols("mq", M_W, n_p, n_real), S_ROWS)
    n_mem = cache_mem_k.shape[2]
    om_s = mem_attention(zs_m, 0, cache_mem_k[0].reshape(bs, n_mem, M_W), 0,
                         cache_mem_v[0].reshape(bs, n_mem, M_W), 0, bs, S_ROWS, S_ROWS)
    o_m = _pad_rows(jnp.concatenate([om_p, om_s[:, :ts].reshape(n_s, M_W)], axis=0), nt)

    merged = merge(o_a, o_b, o_m, p_a[0].astype(BF16), p_b[0].astype(BF16), p_m[0].astype(BF16), z, COLS,
                   TOK_TILE, 512)
    h_pre = resid_matmul(merged, w_out[0].astype(BF16), x, alpha, TOK_TILE, 512)
    h, hb = layer_norm_dual(h_pre, ln1_g[0][None], ln1_b[0][None], 256)

    pq = matmul(hb, w_pq[0].astype(BF16), TOK_TILE, 512, name="peer_query")
    rank2, cnt1, e1, e2 = peer_route(pq, peer_sub_k1[0], peer_sub_k2[0], 256)
    p_out = peer_dense(hb, peer_u[0].astype(BF16), peer_v[0].astype(BF16), rank2, cnt1, e1, e2, TOK_TILE)
    y = layer_norm_resid(h, p_out, ln2_g[0][None], ln2_b[0][None], alpha, 256)

    kv_p = (depth, bp, t, A_KV_HEADS, A_HEAD_DIM)
    kv_s = (depth, bs, ts, A_KV_HEADS, A_HEAD_DIM)
    mem_shape = (depth, bp, mem_prompt.shape[1], M_HEADS, M_HEAD_DIM)
    return (y[:n_p].reshape(bp, t, d), y[n_p:n_real].reshape(bs, ts, d),
            k_rope[:n_p].reshape(kv_p), v_all[:n_p].reshape(kv_p), ik_rope[:n_p].reshape(depth, bp, t, IDX_DIM),
            mem_kv[:, :M_W].reshape(mem_shape), mem_kv[:, M_W:].reshape(mem_shape), st_p[None],
            k_rope[n_p:n_real].reshape(kv_s), v_all[n_p:n_real].reshape(kv_s),
            ik_rope[n_p:n_real].reshape(depth, bs, ts, IDX_DIM), st_s[None])
```

```python
import functools
import math

import jax
import jax.numpy as jnp
import numpy as np
from jax import lax
from jax.experimental import pallas as pl
from jax.experimental.pallas import tpu as pltpu

F32 = jnp.float32
BF16 = jnp.bfloat16
I32 = jnp.int32

A_HEADS, A_KV_HEADS, A_HEAD_DIM = 16, 4, 128
IDX_HEADS, IDX_DIM = 32, 64
TOPK_MAX = 256
B_HEADS, B_HEAD_DIM = 8, 128
M_HEADS, M_HEAD_DIM = 4, 256
PEER_HEADS, PEER_NKEYS, PEER_DKEY, PEER_TOPK = 8, 128, 256, 16
PAGE_SIZE = 128
ROPE_THETA = 10000.0
LN_EPS = 1e-5
RMS_EPS = 1e-6

A_Q_W = A_HEADS * A_HEAD_DIM
A_KV_W = A_KV_HEADS * A_HEAD_DIM
IDX_Q_W = IDX_HEADS * IDX_DIM
B_W = B_HEADS * B_HEAD_DIM
M_W = M_HEADS * M_HEAD_DIM

D_MODEL = 4096

_SEGS = (("aq", A_Q_W), ("iq", IDX_Q_W), ("ak", A_KV_W), ("av", A_KV_W), ("bq", B_W), ("bf", B_W),
         ("bi", B_W), ("bg", B_W), ("mq", M_W), ("ga", D_MODEL), ("gb", D_MODEL), ("gm", D_MODEL),
         ("ikw", 128))
COLS = {}
_off = 0
for _name, _w in _SEGS:
    COLS[_name] = _off
    _off += _w
Z_TILE = 512
Z_WIDTH = -(-_off // Z_TILE) * Z_TILE
Z_TILE_WIDE = 1536
assert Z_WIDTH % Z_TILE_WIDE == 0

LANES = 128
Q_BLK = 128
KEY_CHUNK = 512
HGRN_SUB = 16
EXP_CLAMP = 80.0
Q_LOG2_SCALE = (A_HEAD_DIM ** -0.5) * math.log2(math.e)
INT_MIN = -2 ** 31
NEG_BIG = -1e30


def _cp(sem, vmem_mb):
    return pltpu.CompilerParams(dimension_semantics=sem, vmem_limit_bytes=vmem_mb * 2 ** 20)


def _whole_vmem():
    return pl.BlockSpec(memory_space=pltpu.VMEM)


def _mm_kernel(a_ref, b_ref, o_ref):
    o_ref[...] = jnp.dot(a_ref[...], b_ref[...], preferred_element_type=F32).astype(o_ref.dtype)


def matmul(a, b, tm, tn, out_dtype=F32, name="matmul", weights_outer=False):
    m, k = a.shape
    n = b.shape[1]
    if weights_outer:
        grid = (n // tn, m // tm)
        row, colt = (lambda j, i: (i, 0)), (lambda j, i: (0, j))
        out = lambda j, i: (i, j)
    else:
        grid = (m // tm, n // tn)
        row, colt = (lambda i, j: (i, 0)), (lambda i, j: (0, j))
        out = lambda i, j: (i, j)
    return pl.pallas_call(
        _mm_kernel,
        grid=grid,
        in_specs=[pl.BlockSpec((tm, k), row), pl.BlockSpec((k, tn), colt)],
        out_specs=pl.BlockSpec((tm, tn), out),
        out_shape=jax.ShapeDtypeStruct((m, n), out_dtype),
        compiler_params=_cp(("parallel", "parallel"), 48),
        name=name,
    )(a, b)


def _rope_kernel(q_ref, iq_ref, k_ref, v_ref, ikw_ref, c128_ref, s128_ref, c64_ref, s64_ref,
                 qhm_ref, iqhm_ref, k4_ref, kb_ref, v4_ref, vb_ref, iko_ref, ikb_ref):
    c128, s128 = c128_ref[...], s128_ref[...]
    c64, s64 = c64_ref[...], s64_ref[...]
    lane = lax.broadcasted_iota(I32, (Q_BLK, LANES), 1)
    first_half = (lane % IDX_DIM) < (IDX_DIM // 2)

    def rope128(x):
        return x * c128 + pltpu.roll(x, A_HEAD_DIM // 2, axis=1) * s128

    def rope64(x):
        partner = jnp.where(first_half, pltpu.roll(x, LANES - IDX_DIM // 2, axis=1),
                            pltpu.roll(x, IDX_DIM // 2, axis=1))
        return x * c64 + partner * s64

    for h in range(A_HEADS):
        qhm_ref[0, h] = (rope128(q_ref[:, h * LANES:(h + 1) * LANES]) * Q_LOG2_SCALE).astype(qhm_ref.dtype)
    for h in range(A_KV_HEADS):
        rows = pl.ds(h, Q_BLK, stride=A_KV_HEADS)
        kr = rope128(k_ref[:, h * LANES:(h + 1) * LANES])
        k4_ref[rows, :] = kr
        kb_ref[:, h * LANES:(h + 1) * LANES] = kr.astype(kb_ref.dtype)
        vh = v_ref[:, h * LANES:(h + 1) * LANES]
        v4_ref[rows, :] = vh
        vb_ref[:, h * LANES:(h + 1) * LANES] = vh.astype(vb_ref.dtype)
    for p in range(IDX_HEADS // 2):
        r = rope64(iq_ref[:, p * LANES:(p + 1) * LANES]).astype(iqhm_ref.dtype)
        iqhm_ref[0, 2 * p] = r[:, :IDX_DIM]
        iqhm_ref[0, 2 * p + 1] = r[:, IDX_DIM:]
    ikr = rope64(ikw_ref[...])[:, :IDX_DIM]
    iko_ref[...] = ikr
    ikb_ref[...] = ikr.astype(ikb_ref.dtype)


def rope_all(z, col, tabs):
    nt = z.shape[0]
    nb = nt // Q_BLK
    c128, s128, c64, s64 = tabs
    tab_spec = pl.BlockSpec((Q_BLK, LANES), lambda i: (i, 0))
    kv4_spec = pl.BlockSpec((Q_BLK * A_KV_HEADS, A_HEAD_DIM), lambda i: (i, 0))
    kvb_spec = pl.BlockSpec((Q_BLK, A_KV_W), lambda i: (i, 0))
    ik_spec = pl.BlockSpec((Q_BLK, IDX_DIM), lambda i: (i, 0))
    kv4 = jax.ShapeDtypeStruct((nt * A_KV_HEADS, A_HEAD_DIM), F32)
    kvb = jax.ShapeDtypeStruct((nt, A_KV_W), BF16)
    return pl.pallas_call(
        _rope_kernel,
        grid=(nb,),
        in_specs=[pl.BlockSpec((Q_BLK, A_Q_W), lambda i: (i, col["aq"] // A_Q_W)),
                  pl.BlockSpec((Q_BLK, IDX_Q_W), lambda i: (i, col["iq"] // IDX_Q_W)),
                  pl.BlockSpec((Q_BLK, A_KV_W), lambda i: (i, col["ak"] // A_KV_W)),
                  pl.BlockSpec((Q_BLK, A_KV_W), lambda i: (i, col["av"] // A_KV_W)),
                  pl.BlockSpec((Q_BLK, LANES), lambda i: (i, col["ikw"] // LANES)),
                  tab_spec, tab_spec, tab_spec, tab_spec],
        out_specs=[pl.BlockSpec((1, A_HEADS, Q_BLK, A_HEAD_DIM), lambda i: (i, 0, 0, 0)),
                   pl.BlockSpec((1, IDX_HEADS, Q_BLK, IDX_DIM), lambda i: (i, 0, 0, 0)),
                   kv4_spec, kvb_spec, kv4_spec, kvb_spec, ik_spec, ik_spec],
        out_shape=[jax.ShapeDtypeStruct((nb, A_HEADS, Q_BLK, A_HEAD_DIM), BF16),
                   jax.ShapeDtypeStruct((nb, IDX_HEADS, Q_BLK, IDX_DIM), BF16),
                   kv4, kvb, kv4, kvb,
                   jax.ShapeDtypeStruct((nt, IDX_DIM), F32), jax.ShapeDtypeStruct((nt, IDX_DIM), BF16)],
        compiler_params=_cp(("parallel",), 32),
        name="rope",
    )(z, z, z, z, z, c128, s128, c64, s64)


def rope_tables(pos):
    def tab(half, reps):
        inv_freq = ROPE_THETA ** (-jnp.arange(half, dtype=F32) / half)
        ang = pos.astype(F32)[:, None] * inv_freq[None, :]
        c, s = jnp.cos(ang), jnp.sin(ang)
        return jnp.tile(jnp.concatenate([c, c], 1), (1, reps)), jnp.tile(jnp.concatenate([-s, s], 1), (1, reps))
    c128, s128 = tab(A_HEAD_DIM // 2, 1)
    c64, s64 = tab(IDX_DIM // 2, 2)
    return c128, s128, c64, s64


def _sort_key(x):
    i = pltpu.bitcast(x, I32)
    return i ^ ((i >> 31) & 0x7FFFFFFF)


def _kth_largest_key(count_ge, k, rows):
    def body(it, cur):
        bit = jnp.left_shift(jnp.int32(1), 31 - it)
        cand = cur | bit
        ok = count_ge(cand ^ INT_MIN) >= k
        return jnp.where(ok, cand, cur)
    cur = lax.fori_loop(0, 32, body, jnp.zeros((rows, 1), I32))
    return jnp.maximum(cur ^ INT_MIN, INT_MIN + 1)


def _flash_step(s, mask, v, m_ref, l_ref, acc_ref, g):
    hh, rr, ss = s.shape
    s = jnp.where(mask[None], s, -jnp.inf)
    m_old = m_ref[g]
    m_new = jnp.maximum(m_old, jnp.max(s, axis=-1, keepdims=True))
    p = jnp.exp(s - m_new)
    alpha = jnp.exp(m_old - m_new)
    l_ref[g] = alpha * l_ref[g] + jnp.sum(p, axis=-1, keepdims=True)
    pv = jnp.dot(p.reshape(hh * rr, ss).astype(BF16), v, preferred_element_type=F32)
    acc_ref[g] = alpha * acc_ref[g] + pv.reshape(hh, rr, -1)
    m_ref[g] = m_new


def _flash_step2(s, mask, vext, m_ref, l_ref, acc_ref, g):
    hh, rr, ss = s.shape
    d = vext.shape[1] // 2
    s = jnp.where(mask[None], s, -jnp.inf).reshape(hh * rr, ss)
    m_old = m_ref[g]
    m_new = jnp.maximum(m_old, jnp.max(s, axis=-1, keepdims=True))
    p = jnp.exp2(s - jnp.concatenate([m_new] * (ss // LANES), axis=1))
    alpha = jnp.exp2(m_old - m_new)
    pv = jnp.dot(p.astype(BF16), vext, preferred_element_type=F32)
    acc_ref[g] = alpha * acc_ref[g] + pv[:, :d]
    l_ref[g] = alpha * l_ref[g] + pv[:, d:]
    m_ref[g] = m_new


def _dsa_prompt_kernel(iq_ref, w_ref, q_ref, ik_ref, k_ref, v_ref, o_ref,
                       key_ref, wb_ref, m_ref, l_ref, acc_ref, *, topk, w_lane0):
    i = pl.program_id(0)
    nch = (i * Q_BLK + Q_BLK + KEY_CHUNK - 1) // KEY_CHUNK
    hpg = A_HEADS // A_KV_HEADS
    heads_per_dot = 8

    wscale = (IDX_DIM ** -0.5) * (IDX_HEADS ** -0.5)
    wt = w_ref[...] * wscale
    for h in range(IDX_HEADS):
        wb_ref[h] = jnp.broadcast_to(wt[:, w_lane0 + h:w_lane0 + h + 1], (Q_BLK, LANES))

    q_pos = i * Q_BLK + lax.broadcasted_iota(I32, (Q_BLK, KEY_CHUNK), 0)
    k_off = lax.broadcasted_iota(I32, (Q_BLK, KEY_CHUNK), 1)

    def score_chunk(c, carry):
        k0 = pl.multiple_of(c * KEY_CHUNK, KEY_CHUNK)
        ikc = ik_ref[pl.ds(k0, KEY_CHUNK), :]
        ntile = KEY_CHUNK // LANES
        acc = [jnp.zeros((Q_BLK, LANES), F32)] * ntile
        for hg in range(IDX_HEADS // heads_per_dot):
            lhs = iq_ref[0, hg * heads_per_dot * Q_BLK:(hg + 1) * heads_per_dot * Q_BLK, :]
            d = lax.dot_general(lhs, ikc, (((1,), (1,)), ((), ())), preferred_element_type=F32)
            for hl in range(heads_per_dot):
                wb = wb_ref[hg * heads_per_dot + hl]
                r = jnp.maximum(d[hl * Q_BLK:(hl + 1) * Q_BLK], 0.0)
                acc = [acc[j] + wb * r[:, j * LANES:(j + 1) * LANES] for j in range(ntile)]
        sc = jnp.concatenate(acc, axis=1)
        keys = jnp.where(k0 + k_off <= q_pos, _sort_key(sc), INT_MIN)
        key_ref[c] = keys
        return carry

    lax.fori_loop(0, nch, score_chunk, 0)

    def count_ge(t):
        def body(c, cnt):
            ge = jnp.where(key_ref[c] >= t, 1, 0)
            for j in range(KEY_CHUNK // LANES):
                cnt = cnt + ge[:, j * LANES:(j + 1) * LANES]
            return cnt
        cnt = lax.fori_loop(0, nch, body, jnp.zeros((Q_BLK, LANES), I32))
        return jnp.sum(cnt, axis=1, keepdims=True)

    thr = _kth_largest_key(count_ge, topk, Q_BLK)

    m_ref[...] = jnp.full(m_ref.shape, NEG_BIG, F32)
    l_ref[...] = jnp.zeros(l_ref.shape, F32)
    acc_ref[...] = jnp.zeros(acc_ref.shape, F32)
    ones = jnp.ones((KEY_CHUNK, A_HEAD_DIM), BF16)

    def attend_chunk(c, carry):
        k0 = pl.multiple_of(c * KEY_CHUNK, KEY_CHUNK)
        mask = key_ref[c] >= thr
        for g in range(A_KV_HEADS):
            qg = q_ref[0, g * hpg * Q_BLK:(g + 1) * hpg * Q_BLK, :]
            kc = k_ref[pl.ds(k0, KEY_CHUNK), g * LANES:(g + 1) * LANES]
            vc = v_ref[pl.ds(k0, KEY_CHUNK), g * LANES:(g + 1) * LANES]
            s = lax.dot_general(qg, kc, (((1,), (1,)), ((), ())), preferred_element_type=F32)
            _flash_step2(s.reshape(hpg, Q_BLK, KEY_CHUNK), mask, jnp.concatenate([vc, ones], axis=1),
                         m_ref, l_ref, acc_ref, g)
        return carry

    lax.fori_loop(0, nch, attend_chunk, 0)
    for g in range(A_KV_HEADS):
        o = (acc_ref[g] / l_ref[g]).astype(o_ref.dtype)
        for hl in range(hpg):
            h = g * hpg + hl
            o_ref[:, h * A_HEAD_DIM:(h + 1) * A_HEAD_DIM] = o[hl * Q_BLK:(hl + 1) * Q_BLK]


def dsa_prompt(iq_hm, z, col, q_hm, ik, k, v, t):
    nb = t // Q_BLK
    topk = min(TOPK_MAX, t // 4)
    nch_max = (t + KEY_CHUNK - 1) // KEY_CHUNK
    hpg = A_HEADS // A_KV_HEADS
    kern = functools.partial(_dsa_prompt_kernel, topk=topk, w_lane0=IDX_DIM)
    return pl.pallas_call(
        kern,
        grid=(nb,),
        in_specs=[pl.BlockSpec((1, IDX_HEADS * Q_BLK, IDX_DIM), lambda i: (i, 0, 0)),
                  pl.BlockSpec((Q_BLK, LANES), lambda i: (i, col["ikw"] // LANES)),
                  pl.BlockSpec((1, A_HEADS * Q_BLK, A_HEAD_DIM), lambda i: (i, 0, 0)),
                  _whole_vmem(), _whole_vmem(), _whole_vmem()],
        out_specs=pl.BlockSpec((Q_BLK, A_Q_W), lambda i: (i, 0)),
        out_shape=jax.ShapeDtypeStruct((t, A_Q_W), BF16),
        scratch_shapes=[pltpu.VMEM((nch_max, Q_BLK, KEY_CHUNK), I32),
                        pltpu.VMEM((IDX_HEADS, Q_BLK, LANES), F32),
                        pltpu.VMEM((A_KV_HEADS, hpg * Q_BLK, LANES), F32),
                        pltpu.VMEM((A_KV_HEADS, hpg * Q_BLK, LANES), F32),
                        pltpu.VMEM((A_KV_HEADS, hpg * Q_BLK, A_HEAD_DIM), F32)],
        compiler_params=_cp(("arbitrary",), 56),
        name="dsa_prompt",
    )(iq_hm, z, q_hm, ik, k, v)


S_ROWS = 8
IDX_PAGES = 8
ATT_PAGES = 4


def _dsa_sample_index_kernel(pt_ref, iq_ref, w_ref, ikn_ref, *rest, past_len, t_new, topk):
    pages = rest[:IDX_PAGES]
    key_ref, thr_ref = rest[IDX_PAGES], rest[IDX_PAGES + 1]
    j = pl.program_id(1)
    n_past_tiles = past_len // PAGE_SIZE
    wscale = (IDX_DIM ** -0.5) * (IDX_HEADS ** -0.5)
    wt = w_ref[0] * wscale
    wb = [jnp.broadcast_to(wt[:, h:h + 1], (S_ROWS, LANES)) for h in range(IDX_HEADS)]
    iq = iq_ref[0]

    def tile_scores(ik_tile):
        d = lax.dot_general(iq, ik_tile, (((1,), (1,)), ((), ())), preferred_element_type=F32)
        acc = jnp.zeros((S_ROWS, LANES), F32)
        for h in range(IDX_HEADS):
            acc = acc + wb[h] * jnp.maximum(d[h * S_ROWS:(h + 1) * S_ROWS], 0.0)
        return acc

    for r in range(IDX_PAGES):
        key_ref[0, j * IDX_PAGES + r] = _sort_key(tile_scores(pages[r][0].astype(BF16)))

    @pl.when(j == pl.num_programs(1) - 1)
    def _():
        row = lax.broadcasted_iota(I32, (S_ROWS, LANES), 0)
        lane = lax.broadcasted_iota(I32, (S_ROWS, LANES), 1)
        valid = (lane < t_new) & (lane <= row)
        key_ref[0, n_past_tiles] = jnp.where(valid, _sort_key(tile_scores(ikn_ref[0])), INT_MIN)

        def count_ge(t):
            def body(c, cnt):
                return cnt + jnp.where(key_ref[0, c] >= t, 1, 0)
            cnt = lax.fori_loop(0, n_past_tiles + 1, body, jnp.zeros((S_ROWS, LANES), I32))
            return jnp.sum(cnt, axis=1, keepdims=True)

        thr = _kth_largest_key(count_ge, topk, S_ROWS)
        thr_ref[0] = jnp.broadcast_to(thr, (S_ROWS, LANES))


def dsa_sample_index(page_table, iq_s, iw_s, ik_new, cache_ik, t_new):
    b, n_pages = page_table.shape
    past_len = n_pages * PAGE_SIZE
    topk = min(TOPK_MAX, (past_len + t_new) // 4)
    steps = n_pages // IDX_PAGES
    kern = functools.partial(_dsa_sample_index_kernel, past_len=past_len, t_new=t_new, topk=topk)

    def page_spec(r):
        return pl.BlockSpec((1, PAGE_SIZE, IDX_DIM), lambda bi, j, pt: (pt[bi, j * IDX_PAGES + r], 0, 0))

    grid_spec = pltpu.PrefetchScalarGridSpec(
        num_scalar_prefetch=1,
        grid=(b, steps),
        in_specs=[pl.BlockSpec((1, IDX_HEADS * S_ROWS, IDX_DIM), lambda bi, j, pt: (bi, 0, 0)),
                  pl.BlockSpec((1, S_ROWS, IDX_HEADS), lambda bi, j, pt: (bi, 0, 0)),
                  pl.BlockSpec((1, PAGE_SIZE, IDX_DIM), lambda bi, j, pt: (bi, 0, 0))]
                 + [page_spec(r) for r in range(IDX_PAGES)],
        out_specs=[pl.BlockSpec((1, n_pages + 1, S_ROWS, LANES), lambda bi, j, pt: (bi, 0, 0, 0)),
                   pl.BlockSpec((1, S_ROWS, LANES), lambda bi, j, pt: (bi, 0, 0))],
    )
    return pl.pallas_call(
        kern,
        grid_spec=grid_spec,
        out_shape=[jax.ShapeDtypeStruct((b, n_pages + 1, S_ROWS, LANES), I32),
                   jax.ShapeDtypeStruct((b, S_ROWS, LANES), I32)],
        compiler_params=_cp(("parallel", "arbitrary"), 32),
        name="dsa_sample_index",
    )(page_table, iq_s, iw_s, ik_new, *([cache_ik] * IDX_PAGES))


def _dsa_sample_attend_kernel(pt_ref, key_ref, thr_ref, q_ref, kn_ref, vn_ref, *rest, n_pages):
    kpages = rest[:ATT_PAGES]
    vpages = rest[ATT_PAGES:2 * ATT_PAGES]
    o_ref, m_ref, l_ref, acc_ref = rest[2 * ATT_PAGES:]
    j = pl.program_id(1)
    hpg = A_HEADS // A_KV_HEADS
    scale = A_HEAD_DIM ** -0.5
    thr = thr_ref[0]

    @pl.when(j == 0)
    def _():
        m_ref[...] = jnp.full(m_ref.shape, NEG_BIG, F32)
        l_ref[...] = jnp.zeros(l_ref.shape, F32)
        acc_ref[...] = jnp.zeros(acc_ref.shape, F32)

    def attend(mask, kc, vc):
        for g in range(A_KV_HEADS):
            qg = q_ref[0, g * hpg * S_ROWS:(g + 1) * hpg * S_ROWS, :]
            kg = kc[:, g * LANES:(g + 1) * LANES]
            vg = vc[:, g * LANES:(g + 1) * LANES]
            s = lax.dot_general(qg, kg, (((1,), (1,)), ((), ())), preferred_element_type=F32) * scale
            _flash_step(s.reshape(hpg, S_ROWS, s.shape[-1]), mask, vg, m_ref, l_ref, acc_ref, g)

    mask = jnp.concatenate([key_ref[0, j * ATT_PAGES + r] >= thr for r in range(ATT_PAGES)], axis=1)
    kc = jnp.concatenate([kp[0].astype(BF16) for kp in kpages], axis=0)
    vc = jnp.concatenate([vp[0].astype(BF16) for vp in vpages], axis=0)
    attend(mask, kc, vc)

    @pl.when(j == pl.num_programs(1) - 1)
    def _():
        attend(key_ref[0, n_pages] >= thr, kn_ref[0], vn_ref[0])
        for g in range(A_KV_HEADS):
            o = acc_ref[g] / l_ref[g]
            o_ref[0, g * hpg * S_ROWS:(g + 1) * hpg * S_ROWS, :] = (
                o.reshape(hpg * S_ROWS, A_HEAD_DIM).astype(o_ref.dtype))


def dsa_sample_attend(page_table, keys, thr, q_s, k_new, v_new, cache_k, cache_v):
    b, n_pages = page_table.shape
    steps = n_pages // ATT_PAGES
    hpg = A_HEADS // A_KV_HEADS
    kern = functools.partial(_dsa_sample_attend_kernel, n_pages=n_pages)

    def page_spec(r):
        return pl.BlockSpec((1, PAGE_SIZE, A_KV_W), lambda bi, j, pt: (pt[bi, j * ATT_PAGES + r], 0, 0))

    grid_spec = pltpu.PrefetchScalarGridSpec(
        num_scalar_prefetch=1,
        grid=(b, steps),
        in_specs=[pl.BlockSpec((1, n_pages + 1, S_ROWS, LANES), lambda bi, j, pt: (bi, 0, 0, 0)),
                  pl.BlockSpec((1, S_ROWS, LANES), lambda bi, j, pt: (bi, 0, 0)),
                  pl.BlockSpec((1, A_HEADS * S_ROWS, A_HEAD_DIM), lambda bi, j, pt: (bi, 0, 0)),
                  pl.BlockSpec((1, PAGE_SIZE, A_KV_W), lambda bi, j, pt: (bi, 0, 0)),
                  pl.BlockSpec((1, PAGE_SIZE, A_KV_W), lambda bi, j, pt: (bi, 0, 0))]
                 + [page_spec(r) for r in range(ATT_PAGES)] * 2,
        out_specs=pl.BlockSpec((1, A_HEADS * S_ROWS, A_HEAD_DIM), lambda bi, j, pt: (bi, 0, 0)),
        scratch_shapes=[pltpu.VMEM((A_KV_HEADS, hpg, S_ROWS, 1), F32),
                        pltpu.VMEM((A_KV_HEADS, hpg, S_ROWS, 1), F32),
                        pltpu.VMEM((A_KV_HEADS, hpg, S_ROWS, A_HEAD_DIM), F32)],
    )
    return pl.pallas_call(
        kern,
        grid_spec=grid_spec,
        out_shape=jax.ShapeDtypeStruct((b, A_HEADS * S_ROWS, A_HEAD_DIM), BF16),
        compiler_params=_cp(("parallel", "arbitrary"), 32),
        name="dsa_sample_attend",
    )(page_table, keys, thr, q_s, k_new, v_new, *([cache_k] * ATT_PAGES), *([cache_v] * ATT_PAGES))


ATT_PAGES2 = 16


def _dsa_sample_scores_kernel(pt_ref, iq_ref, w_ref, ikn_ref, *rest, n_pages, t_new):
    pages = rest[:n_pages]
    key_ref = rest[n_pages]
    wscale = (IDX_DIM ** -0.5) * (IDX_HEADS ** -0.5)
    wt = w_ref[0] * wscale
    wb = [jnp.broadcast_to(wt[:, h:h + 1], (S_ROWS, LANES)) for h in range(IDX_HEADS)]
    iq = iq_ref[0]

    def tile_scores(ik_tile):
        d = lax.dot_general(iq, ik_tile, (((1,), (1,)), ((), ())), preferred_element_type=F32)
        acc = jnp.zeros((S_ROWS, LANES), F32)
        for h in range(IDX_HEADS):
            acc = acc + wb[h] * jnp.maximum(d[h * S_ROWS:(h + 1) * S_ROWS], 0.0)
        return acc

    for r in range(n_pages):
        key_ref[0, r] = _sort_key(tile_scores(pages[r][0].astype(BF16)))
    row = lax.broadcasted_iota(I32, (S_ROWS, LANES), 0)
    lane = lax.broadcasted_iota(I32, (S_ROWS, LANES), 1)
    valid = jnp.where(lane < t_new, lane, S_ROWS) <= row
    key_ref[0, n_pages] = jnp.where(valid, _sort_key(tile_scores(ikn_ref[0])), INT_MIN)


def _dsa_sample_thr_kernel(key_ref, thr_ref, *, topk):
    b, n_tiles = key_ref.shape[0], key_ref.shape[1]

    def count_ge(t):
        def body(c, cnt):
            return cnt + jnp.where(key_ref[:, c] >= t, 1, 0)
        cnt = lax.fori_loop(0, n_tiles, body, jnp.zeros((b, S_ROWS, LANES), I32))
        return jnp.sum(cnt, axis=2, keepdims=True)

    def body(it, cur):
        cand = cur | jnp.left_shift(jnp.int32(1), 31 - it)
        return jnp.where(count_ge(cand ^ INT_MIN) >= topk, cand, cur)

    cur = lax.fori_loop(0, 32, body, jnp.zeros((b, S_ROWS, 1), I32))
    thr_ref[...] = jnp.broadcast_to(jnp.maximum(cur ^ INT_MIN, INT_MIN + 1), thr_ref.shape)


def dsa_sample_index2(page_table, iq_s, iw_s, ik_new, cache_ik, t_new):
    b, n_pages = page_table.shape
    topk = min(TOPK_MAX, (n_pages * PAGE_SIZE + t_new) // 4)

    def page_spec(r):
        return pl.BlockSpec((1, PAGE_SIZE, IDX_DIM), lambda bi, pt: (pt[bi, r], 0, 0))

    grid_spec = pltpu.PrefetchScalarGridSpec(
        num_scalar_prefetch=1,
        grid=(b,),
        in_specs=[pl.BlockSpec((1, IDX_HEADS * S_ROWS, IDX_DIM), lambda bi, pt: (bi, 0, 0)),
                  pl.BlockSpec((1, S_ROWS, IDX_HEADS), lambda bi, pt: (bi, 0, 0)),
                  pl.BlockSpec((1, PAGE_SIZE, IDX_DIM), lambda bi, pt: (bi, 0, 0))]
                 + [page_spec(r) for r in range(n_pages)],
        out_specs=pl.BlockSpec((1, n_pages + 1, S_ROWS, LANES), lambda bi, pt: (bi, 0, 0, 0)),
    )
    keys = pl.pallas_call(
        functools.partial(_dsa_sample_scores_kernel, n_pages=n_pages, t_new=t_new),
        grid_spec=grid_spec,
        out_shape=jax.ShapeDtypeStruct((b, n_pages + 1, S_ROWS, LANES), I32),
        compiler_params=_cp(("parallel",), 32),
        name="dsa_sample_scores",
    )(page_table, iq_s, iw_s, ik_new, *([cache_ik] * n_pages))
    thr = pl.pallas_call(
        functools.partial(_dsa_sample_thr_kernel, topk=topk),
        in_specs=[_whole_vmem()],
        out_specs=_whole_vmem(),
        out_shape=jax.ShapeDtypeStruct((b, S_ROWS, LANES), I32),
        compiler_params=pltpu.CompilerParams(vmem_limit_bytes=32 * 2 ** 20),
        name="dsa_sample_threshold",
    )(keys)
    return keys, thr


def _dsa_sample_attend2_kernel(pt_ref, key_ref, thr_ref, q_ref, kn_ref, vn_ref, *rest, n_pages):
    kpages = rest[:ATT_PAGES2]
    vpages = rest[ATT_PAGES2:2 * ATT_PAGES2]
    o_ref, m_ref, l_ref, acc_ref = rest[2 * ATT_PAGES2:]
    j = pl.program_id(1)
    hpg = A_HEADS // A_KV_HEADS
    thr = thr_ref[0]

    @pl.when(j == 0)
    def _():
        m_ref[...] = jnp.full(m_ref.shape, NEG_BIG, F32)
        l_ref[...] = jnp.zeros(l_ref.shape, F32)
        acc_ref[...] = jnp.zeros(acc_ref.shape, F32)

    def attend(mask, g, kg, vg):
        qg = q_ref[0, g * hpg * S_ROWS:(g + 1) * hpg * S_ROWS, :]
        s = lax.dot_general(qg, kg, (((1,), (1,)), ((), ())), preferred_element_type=F32)
        vext = jnp.concatenate([vg, jnp.ones(vg.shape, BF16)], axis=1)
        _flash_step2(s.reshape(hpg, S_ROWS, s.shape[-1]), mask, vext, m_ref, l_ref, acc_ref, g)

    mask = jnp.concatenate([key_ref[0, j * ATT_PAGES2 + r] >= thr for r in range(ATT_PAGES2)], axis=1)
    for g in range(A_KV_HEADS):
        rows = pl.ds(g, PAGE_SIZE, stride=A_KV_HEADS)
        kg = jnp.concatenate([kp[rows, :].astype(BF16) for kp in kpages], axis=0)
        vg = jnp.concatenate([vp[rows, :].astype(BF16) for vp in vpages], axis=0)
        attend(mask, g, kg, vg)

    @pl.when(j == pl.num_programs(1) - 1)
    def _():
        mask_new = key_ref[0, n_pages] >= thr
        for g in range(A_KV_HEADS):
            attend(mask_new, g, kn_ref[0, :, g * LANES:(g + 1) * LANES], vn_ref[0, :, g * LANES:(g + 1) * LANES])
        for g in range(A_KV_HEADS):
            o_ref[0, g * hpg * S_ROWS:(g + 1) * hpg * S_ROWS, :] = (acc_ref[g] / l_ref[g]).astype(o_ref.dtype)


def dsa_sample_attend2(page_table, keys, thr, q_s, k_new, v_new, cache_k, cache_v):
    b, n_pages = page_table.shape
    steps = n_pages // ATT_PAGES2
    hpg = A_HEADS // A_KV_HEADS
    page_rows = PAGE_SIZE * A_KV_HEADS

    def page_spec(r):
        return pl.BlockSpec((page_rows, A_HEAD_DIM), lambda bi, j, pt: (pt[bi, j * ATT_PAGES2 + r], 0))

    grid_spec = pltpu.PrefetchScalarGridSpec(
        num_scalar_prefetch=1,
        grid=(b, steps),
        in_specs=[pl.BlockSpec((1, n_pages + 1, S_ROWS, LANES), lambda bi, j, pt: (bi, 0, 0, 0)),
                  pl.BlockSpec((1, S_ROWS, LANES), lambda bi, j, pt: (bi, 0, 0)),
                  pl.BlockSpec((1, A_HEADS * S_ROWS, A_HEAD_DIM), lambda bi, j, pt: (bi, 0, 0)),
                  pl.BlockSpec((1, PAGE_SIZE, A_KV_W), lambda bi, j, pt: (bi, 0, 0)),
                  pl.BlockSpec((1, PAGE_SIZE, A_KV_W), lambda bi, j, pt: (bi, 0, 0))]
                 + [page_spec(r) for r in range(ATT_PAGES2)] * 2,
        out_specs=pl.BlockSpec((1, A_HEADS * S_ROWS, A_HEAD_DIM), lambda bi, j, pt: (bi, 0, 0)),
        scratch_shapes=[pltpu.VMEM((A_KV_HEADS, hpg * S_ROWS, LANES), F32),
                        pltpu.VMEM((A_KV_HEADS, hpg * S_ROWS, LANES), F32),
                        pltpu.VMEM((A_KV_HEADS, hpg * S_ROWS, A_HEAD_DIM), F32)],
    )
    return pl.pallas_call(
        functools.partial(_dsa_sample_attend2_kernel, n_pages=n_pages),
        grid_spec=grid_spec,
        out_shape=jax.ShapeDtypeStruct((b, A_HEADS * S_ROWS, A_HEAD_DIM), BF16),
        compiler_params=_cp(("parallel", "arbitrary"), 48),
        name="dsa_sample_attend",
    )(page_table, keys, thr, q_s, k_new, v_new, *([cache_k] * ATT_PAGES2), *([cache_v] * ATT_PAGES2))


def _cumsum_rows(x):
    n = x.shape[0]
    row = lax.broadcasted_iota(I32, x.shape, 0)
    d = 1
    while d < n:
        x = x + jnp.where(row >= d, pltpu.roll(x, d, axis=0), 0.0)
        d *= 2
    return x


def _hgrn_kernel(q_ref, f_ref, i_ref, g_ref, lb_ref, ng_ref, s0_ref, o_ref, so_ref, st_ref,
                 *, chunk, sub, t_valid):
    n = pl.program_id(1)
    nsub = chunk // sub

    @pl.when(n == 0)
    def _():
        for h in range(B_HEADS):
            st_ref[h] = s0_ref[0, h].T

    lbx = lb_ref[...]
    lbe = jnp.exp(lbx - jnp.max(lbx, axis=0, keepdims=True))
    lb_all = lbe[0:1] / jnp.sum(lbe, axis=0, keepdims=True)
    row = lax.broadcasted_iota(I32, (chunk, B_HEAD_DIM), 0)
    valid = row < t_valid
    tt = lax.broadcasted_iota(I32, (chunk, nsub * chunk), 0)
    cc = lax.broadcasted_iota(I32, (chunk, nsub * chunk), 1)
    pair_ok = ((cc // chunk) == (tt // sub)) & ((cc % chunk) <= tt)
    ng = ng_ref[...]

    for h in range(B_HEADS):
        sl = slice(h * B_HEAD_DIM, (h + 1) * B_HEAD_DIM)
        lb = lb_all[:, sl]
        f = lb + (1.0 - lb) * jax.nn.sigmoid(f_ref[:, sl])
        logf = jnp.where(valid, jnp.log(f), 0.0)
        kk = jnp.where(valid, 1.0 - f, 0.0)
        q = q_ref[:, sl]
        iv = i_ref[:, sl]
        cum = _cumsum_rows(logf)
        last = cum[chunk - 1:chunk]
        st = st_ref[h]

        o = lax.dot_general((q * jnp.exp(cum)).astype(BF16), st.astype(BF16),
                            (((1,), (1,)), ((), ())), preferred_element_type=F32)

        refs = [jnp.zeros((1, B_HEAD_DIM), F32)] + [cum[s * sub - 1:s * sub] for s in range(1, nsub)]
        ref_row = refs[0]
        for s in range(1, nsub):
            ref_row = jnp.where(row >= s * sub, refs[s], ref_row)
        qt = (q * jnp.exp(cum - ref_row)).astype(BF16)
        kcat = jnp.concatenate(
            [(kk * jnp.exp(jnp.minimum(r - cum, EXP_CLAMP))).astype(BF16) for r in refs], axis=0)
        a = lax.dot_general(qt, kcat, (((1,), (1,)), ((), ())), preferred_element_type=F32)
        p = jnp.where(pair_ok, a, 0.0).astype(BF16)
        icat = jnp.concatenate([iv.astype(BF16)] * nsub, axis=0)
        o = o + jnp.dot(p, icat, preferred_element_type=F32)

        kd = (kk * jnp.exp(last - cum)).astype(BF16)
        upd = lax.dot_general(iv.astype(BF16), kd, (((0,), (0,)), ((), ())), preferred_element_type=F32)
        st_ref[h] = st * jnp.exp(last) + upd

        on = o * lax.rsqrt(jnp.mean(o * o, axis=-1, keepdims=True) + RMS_EPS) * ng
        gate = g_ref[:, sl]
        o_ref[:, sl] = (on * (gate * jax.nn.sigmoid(gate))).astype(o_ref.dtype)

    @pl.when(n == pl.num_programs(1) - 1)
    def _():
        for h in range(B_HEADS):
            so_ref[0, h] = st_ref[h].T


def hgrn(zsrc, colblk, lb, ng, s0, batch, t_pad, chunk, sub, t_valid):
    nch = t_pad // chunk
    kern = functools.partial(_hgrn_kernel, chunk=chunk, sub=sub, t_valid=t_valid)

    def zspec(cb):
        return pl.BlockSpec((chunk, B_W), lambda b, n: (b * nch + n, cb))

    return pl.pallas_call(
        kern,
        grid=(batch, nch),
        in_specs=[zspec(colblk[0]), zspec(colblk[1]), zspec(colblk[2]), zspec(colblk[3]),
                  pl.BlockSpec(lb.shape, lambda b, n: (0, 0)),
                  pl.BlockSpec((1, B_HEAD_DIM), lambda b, n: (0, 0)),
                  pl.BlockSpec((1, B_HEADS, B_HEAD_DIM, B_HEAD_DIM), lambda b, n: (b, 0, 0, 0))],
        out_specs=[pl.BlockSpec((chunk, B_W), lambda b, n: (b * nch + n, 0)),
                   pl.BlockSpec((1, B_HEADS, B_HEAD_DIM, B_HEAD_DIM), lambda b, n: (b, 0, 0, 0))],
        out_shape=[jax.ShapeDtypeStruct((batch * t_pad, B_W), BF16),
                   jax.ShapeDtypeStruct((batch, B_HEADS, B_HEAD_DIM, B_HEAD_DIM), F32)],
        scratch_shapes=[pltpu.VMEM((B_HEADS, B_HEAD_DIM, B_HEAD_DIM), F32)],
        compiler_params=_cp(("parallel", "arbitrary"), 32),
        name="hgrn",
    )(zsrc, zsrc, zsrc, zsrc, lb, ng, s0)


def _mem_attn_kernel(q_ref, mk_ref, mv_ref, o_ref):
    scale = M_HEAD_DIM ** -0.5
    for h in range(M_HEADS):
        sl = slice(h * M_HEAD_DIM, (h + 1) * M_HEAD_DIM)
        q = q_ref[0, :, sl].astype(BF16)
        mk = mk_ref[0, :, sl].astype(BF16)
        mv = mv_ref[0, :, sl].astype(BF16)
        s = lax.dot_general(q, mk, (((1,), (1,)), ((), ())), preferred_element_type=F32) * scale
        p = jnp.exp(s - jnp.max(s, axis=-1, keepdims=True))
        p = p / jnp.sum(p, axis=-1, keepdims=True)
        o_ref[0, :, sl] = jnp.dot(p.astype(BF16), mv, preferred_element_type=F32).astype(o_ref.dtype)


def mem_attention(q3, qcolblk, mk3, mkcolblk, mv3, mvcolblk, batch, t, tq):
    m = mk3.shape[1]
    return pl.pallas_call(
        _mem_attn_kernel,
        grid=(batch, t // tq),
        in_specs=[pl.BlockSpec((1, tq, M_W), lambda b, i: (b, i, qcolblk)),
                  pl.BlockSpec((1, m, M_W), lambda b, i: (b, 0, mkcolblk)),
                  pl.BlockSpec((1, m, M_W), lambda b, i: (b, 0, mvcolblk))],
        out_specs=pl.BlockSpec((1, tq, M_W), lambda b, i: (b, i, 0)),
        out_shape=jax.ShapeDtypeStruct((batch, t, M_W), BF16),
        compiler_params=_cp(("parallel", "parallel"), 32),
        name="mem_attention",
    )(q3, mk3, mv3)


def _merge_kernel(oa_ref, ob_ref, om_ref, pa_ref, pb_ref, pm_ref, ga_ref, gb_ref, gm_ref, o_ref):
    acc = jax.nn.sigmoid(ga_ref[...]) * jnp.dot(oa_ref[...], pa_ref[...], preferred_element_type=F32)
    acc += jax.nn.sigmoid(gb_ref[...]) * jnp.dot(ob_ref[...], pb_ref[...], preferred_element_type=F32)
    acc += jax.nn.sigmoid(gm_ref[...]) * jnp.dot(om_ref[...], pm_ref[...], preferred_element_type=F32)
    o_ref[...] = acc.astype(o_ref.dtype)


def merge(oa, ob, om, pa, pb, pm, z, col, tm, tn):
    nt = oa.shape[0]
    d = pa.shape[1]

    def act(w):
        return pl.BlockSpec((tm, w), lambda i, j: (i, 0))

    def wgt(w):
        return pl.BlockSpec((w, tn), lambda i, j: (0, j))

    def gate(name):
        return pl.BlockSpec((tm, tn), lambda i, j: (i, col[name] // tn + j))

    return pl.pallas_call(
        _merge_kernel,
        grid=(nt // tm, d // tn),
        in_specs=[act(oa.shape[1]), act(ob.shape[1]), act(om.shape[1]),
                  wgt(pa.shape[0]), wgt(pb.shape[0]), wgt(pm.shape[0]),
                  gate("ga"), gate("gb"), gate("gm")],
        out_specs=pl.BlockSpec((tm, tn), lambda i, j: (i, j)),
        out_shape=jax.ShapeDtypeStruct((nt, d), BF16),
        compiler_params=_cp(("parallel", "parallel"), 48),
        name="merge",
    )(oa, ob, om, pa, pb, pm, z, z, z)


def _resid_mm_kernel(a_ref, b_ref, x_ref, o_ref, *, alpha):
    o_ref[...] = alpha * x_ref[...] + jnp.dot(a_ref[...], b_ref[...], preferred_element_type=F32)


def resid_matmul(a, b, x, alpha, tm, tn):
    m, k = a.shape
    n = b.shape[1]
    return pl.pallas_call(
        functools.partial(_resid_mm_kernel, alpha=alpha),
        grid=(m // tm, n // tn),
        in_specs=[pl.BlockSpec((tm, k), lambda i, j: (i, 0)),
                  pl.BlockSpec((k, tn), lambda i, j: (0, j)),
                  pl.BlockSpec((tm, tn), lambda i, j: (i, j))],
        out_specs=pl.BlockSpec((tm, tn), lambda i, j: (i, j)),
        out_shape=jax.ShapeDtypeStruct((m, n), F32),
        compiler_params=_cp(("parallel", "parallel"), 48),
        name="resid_matmul",
    )(a, b, x)


def _ln(x, g, b):
    mu = jnp.mean(x, axis=-1, keepdims=True)
    xc = x - mu
    var = jnp.mean(xc * xc, axis=-1, keepdims=True)
    return xc * lax.rsqrt(var + LN_EPS) * g + b


def _ln_kernel(x_ref, g_ref, b_ref, o_ref, ob_ref):
    y = _ln(x_ref[...], g_ref[...], b_ref[...])
    o_ref[...] = y
    ob_ref[...] = y.astype(ob_ref.dtype)


def layer_norm_dual(x, g, b, tm):
    n, d = x.shape
    row = pl.BlockSpec((tm, d), lambda i: (i, 0))
    vec = pl.BlockSpec((1, d), lambda i: (0, 0))
    return pl.pallas_call(
        _ln_kernel,
        grid=(n // tm,),
        in_specs=[row, vec, vec],
        out_specs=[row, row],
        out_shape=[jax.ShapeDtypeStruct((n, d), F32), jax.ShapeDtypeStruct((n, d), BF16)],
        compiler_params=_cp(("parallel",), 48),
        name="layer_norm1",
    )(x, g, b)


def _ln_resid_kernel(h_ref, p_ref, g_ref, b_ref, o_ref, *, alpha):
    o_ref[...] = _ln(alpha * h_ref[...] + p_ref[...], g_ref[...], b_ref[...])


def layer_norm_resid(h, p, g, b, alpha, tm):
    n, d = h.shape
    row = pl.BlockSpec((tm, d), lambda i: (i, 0))
    vec = pl.BlockSpec((1, d), lambda i: (0, 0))
    return pl.pallas_call(
        functools.partial(_ln_resid_kernel, alpha=alpha),
        grid=(n // tm,),
        in_specs=[row, row, vec, vec],
        out_specs=row,
        out_shape=jax.ShapeDtypeStruct((n, d), F32),
        compiler_params=_cp(("parallel",), 48),
        name="layer_norm2",
    )(h, p, g, b)


def _top_ranked(x, n_top):
    kdim = x.shape[0]
    idx = lax.broadcasted_iota(I32, x.shape, 0).astype(F32)
    rank = jnp.full(x.shape, float(n_top), F32)
    vals = []
    for a in range(n_top):
        m = jnp.max(x, axis=0, keepdims=True)
        first = jnp.min(jnp.where(x == m, idx, float(kdim)), axis=0, keepdims=True)
        sel = idx == first
        rank = jnp.where(sel, float(a), rank)
        x = jnp.where(sel, -jnp.inf, x)
        vals.append(m)
    return rank, vals


def _peer_route_kernel(pq_ref, k1_ref, k2_ref, rank2_ref, cnt1_ref, e1_ref, e2_ref):
    half = PEER_DKEY // 2
    k1 = k1_ref[...].astype(BF16)
    k2 = k2_ref[...].astype(BF16)
    nt = (((1,), (1,)), ((), ()))
    for h in range(PEER_HEADS):
        q1 = pq_ref[:, h * PEER_DKEY:h * PEER_DKEY + half].astype(BF16)
        q2 = pq_ref[:, h * PEER_DKEY + half:(h + 1) * PEER_DKEY].astype(BF16)
        s1 = lax.dot_general(k1, q1, nt, preferred_element_type=F32)
        s2 = lax.dot_general(k2, q2, nt, preferred_element_type=F32)
        rank1, v1 = _top_ranked(s1, PEER_TOPK)
        rank2, v2 = _top_ranked(s2, PEER_TOPK)
        v2m = jnp.concatenate(v2, axis=0)
        cand = jnp.concatenate([v1[a] + v2m for a in range(PEER_TOPK)], axis=0)
        crank, _ = _top_ranked(cand, PEER_TOPK)
        chosen = jnp.where(crank < float(PEER_TOPK), 1.0, 0.0)
        cmax = v1[0] + v2[0]
        zsum = jnp.sum(chosen * jnp.exp(cand - cmax), axis=0, keepdims=True)
        cnt1 = jnp.zeros_like(s1)
        for a in range(PEER_TOPK):
            m_a = jnp.sum(chosen[a * PEER_TOPK:(a + 1) * PEER_TOPK], axis=0, keepdims=True)
            cnt1 = jnp.where(rank1 == float(a), m_a, cnt1)
        rank2_ref[h] = rank2.astype(rank2_ref.dtype)
        cnt1_ref[h] = cnt1
        e1_ref[h] = jnp.exp(s1 - v1[0])
        e2_ref[h] = (jnp.exp(s2 - v2[0]) / zsum).astype(e2_ref.dtype)


def peer_route(pq, k1, k2, tn):
    n = pq.shape[0]
    shape = (PEER_HEADS, PEER_NKEYS, n)
    ospec = pl.BlockSpec((PEER_HEADS, PEER_NKEYS, tn), lambda i: (0, 0, i))
    kspec = pl.BlockSpec(k1.shape, lambda i: (0, 0))
    return pl.pallas_call(
        _peer_route_kernel,
        grid=(n // tn,),
        in_specs=[pl.BlockSpec((tn, pq.shape[1]), lambda i: (i, 0)), kspec, kspec],
        out_specs=[ospec] * 4,
        out_shape=[jax.ShapeDtypeStruct(shape, BF16), jax.ShapeDtypeStruct(shape, F32),
                   jax.ShapeDtypeStruct(shape, F32), jax.ShapeDtypeStruct(shape, BF16)],
        compiler_params=_cp(("parallel",), 32),
        name="peer_route",
    )(pq, k1, k2)


PEER_EROWS = 4


def _gelu_exact(x):
    return 0.5 * x * (1.0 + lax.erf(x * (2.0 ** -0.5)))


def _peer_dense_kernel(x_ref, u_ref, v_ref, rank2_ref, cnt1_ref, e1_ref, e2_ref, o_ref, z_ref, *, n_tiles):
    r = pl.program_id(1)

    @pl.when(r == 0)
    def _():
        o_ref[...] = jnp.zeros(o_ref.shape, o_ref.dtype)
        z_ref[...] = jnp.zeros(z_ref.shape, z_ref.dtype)

    o_ref[...] += lax.dot_general(z_ref[...], v_ref[...], (((0,), (0,)), ((), ())), preferred_element_type=F32)

    at = lax.dot_general(u_ref[...], x_ref[...], (((1,), (1,)), ((), ())), preferred_element_type=F32)
    act = _gelu_exact(at).astype(BF16)
    tile = jnp.minimum(r, n_tiles - 1)
    for rr in range(PEER_EROWS):
        i1 = tile * PEER_EROWS + rr
        g = jnp.zeros((PEER_NKEYS, x_ref.shape[0]), BF16)
        for h in range(PEER_HEADS):
            cnt = cnt1_ref[h, pl.ds(i1, 1), :].astype(BF16)
            e1 = e1_ref[h, pl.ds(i1, 1), :].astype(BF16)
            g = g + jnp.where(rank2_ref[h] < cnt, e2_ref[h] * e1, jnp.zeros((), BF16))
        z_ref[rr * PEER_NKEYS:(rr + 1) * PEER_NKEYS, :] = g * act[rr * PEER_NKEYS:(rr + 1) * PEER_NKEYS]


def peer_dense(xb, u, v, rank2, cnt1, e1, e2, tn):
    n, d = xb.shape
    ne = u.shape[0]
    te = PEER_EROWS * PEER_NKEYS
    n_tiles = ne // te
    aux = pl.BlockSpec((PEER_HEADS, PEER_NKEYS, tn), lambda j, r: (0, 0, j))
    return pl.pallas_call(
        functools.partial(_peer_dense_kernel, n_tiles=n_tiles),
        grid=(n // tn, n_tiles + 1),
        in_specs=[pl.BlockSpec((tn, d), lambda j, r: (j, 0)),
                  pl.BlockSpec((te, d), lambda j, r: (jnp.minimum(r, n_tiles - 1), 0)),
                  pl.BlockSpec((te, d), lambda j, r: (jnp.maximum(r - 1, 0), 0)),
                  aux, aux, aux, aux],
        out_specs=pl.BlockSpec((tn, d), lambda j, r: (j, 0)),
        out_shape=jax.ShapeDtypeStruct((n, d), F32),
        scratch_shapes=[pltpu.VMEM((te, tn), BF16)],
        compiler_params=_cp(("parallel", "arbitrary"), 60),
        name="peer_dense",
    )(xb, u, v, rank2, cnt1, e1, e2)


TOK_TILE = 512


def _pad_rows(a, n, axis=0):
    pad = [(0, 0)] * a.ndim
    pad[axis] = (0, n - a.shape[axis])
    return jnp.pad(a, pad)


def _pack_w_in(w):
    splits = (A_Q_W, A_KV_W, A_KV_W, IDX_Q_W, IDX_DIM, IDX_HEADS, B_W, B_W, B_W, B_W, M_W,
              D_MODEL, D_MODEL, D_MODEL)
    offs = np.concatenate([[0], np.cumsum(splits)])
    aq, ak, av, iq, ik, iw, bq, bf, bi, bg, mq, ga, gb, gm = [w[:, offs[i]:offs[i + 1]] for i in range(len(splits))]
    used = COLS["ikw"] + IDX_DIM + IDX_HEADS
    tail = jnp.zeros((w.shape[0], Z_WIDTH - used), w.dtype)
    return jnp.concatenate([aq, iq, ak, av, bq, bf, bi, bg, mq, ga, gb, gm, ik, iw, tail], axis=1).astype(BF16)


def kernel(x_prompt, x_sample, cache_k, cache_v, cache_idx_k, cache_mem_k, cache_mem_v, state_hgrn, page_table,
           mem_prompt, w_in, w_mem_kv, hgrn_lb, hgrn_norm_g, p_a, p_b, p_m, w_out, ln1_g, ln1_b, w_pq,
           peer_sub_k1, peer_sub_k2, peer_u, peer_v, ln2_g, ln2_b):
    depth = w_in.shape[0]
    assert depth == 1, "single trunk layer"
    bp, t, d = x_prompt.shape
    bs, ts, _ = x_sample.shape
    assert bp == 1 and d == D_MODEL and t % KEY_CHUNK == 0 and bs * ts == Q_BLK and ts <= S_ROWS
    n_p, n_s = bp * t, bs * ts
    n_real = n_p + n_s
    nt = -(-n_real // TOK_TILE) * TOK_TILE
    n_pool = cache_k.shape[1]
    past_len = page_table.shape[1] * PAGE_SIZE
    alpha = (2 * depth) ** 0.25
    nb_p = n_p // Q_BLK

    x = _pad_rows(jnp.concatenate([x_prompt.reshape(n_p, d), x_sample.reshape(n_s, d)], axis=0), nt)
    z = matmul(x.astype(BF16), _pack_w_in(w_in[0]), TOK_TILE, Z_TILE_WIDE, name="proj_in", weights_outer=True)

    def zcols(name, width, lo, hi):
        return z[lo:hi, COLS[name]:COLS[name] + width]

    pos = jnp.concatenate([jnp.arange(t, dtype=I32), past_len + jnp.tile(jnp.arange(ts, dtype=I32), bs),
                           jnp.zeros((nt - n_real,), I32)])
    q_hm, iq_hm, k4, kb, v4, vb, ik_rope, ikb = rope_all(z, COLS, rope_tables(pos))
    nb = q_hm.shape[0]

    oa_p = dsa_prompt(iq_hm.reshape(nb, IDX_HEADS * Q_BLK, IDX_DIM), z, COLS,
                      q_hm.reshape(nb, A_HEADS * Q_BLK, A_HEAD_DIM), ikb[:n_p], kb[:n_p], vb[:n_p], t)

    def per_seq(a, rows):
        return _pad_rows(a.reshape((bs, ts) + a.shape[1:]), rows, axis=1)

    q_s = per_seq(q_hm[nb_p].transpose(1, 0, 2), S_ROWS).transpose(0, 2, 1, 3).reshape(bs, A_HEADS * S_ROWS, A_HEAD_DIM)
    iq_s = per_seq(iq_hm[nb_p].transpose(1, 0, 2), S_ROWS).transpose(0, 2, 1, 3).reshape(bs, IDX_HEADS * S_ROWS, IDX_DIM)
    iw_s = per_seq(z[n_p:n_real, COLS["ikw"] + IDX_DIM:COLS["ikw"] + IDX_DIM + IDX_HEADS], S_ROWS)
    ik_new = per_seq(ikb[n_p:n_real], PAGE_SIZE)
    k_new = per_seq(kb[n_p:n_real], PAGE_SIZE)
    v_new = per_seq(vb[n_p:n_real], PAGE_SIZE)
    keys, thr = dsa_sample_index2(page_table, iq_s, iw_s, ik_new, cache_idx_k[0], ts)
    cache_rows = n_pool * PAGE_SIZE * A_KV_HEADS
    os_hm = dsa_sample_attend2(page_table, keys, thr, q_s, k_new, v_new,
                               cache_k[0].reshape(cache_rows, A_HEAD_DIM), cache_v[0].reshape(cache_rows, A_HEAD_DIM))
    oa_s = os_hm.reshape(bs, A_HEADS, S_ROWS, A_HEAD_DIM)[:, :, :ts].transpose(0, 2, 1, 3).reshape(n_s, A_Q_W)
    o_a = _pad_rows(jnp.concatenate([oa_p, oa_s], axis=0), nt)

    b_blk = [COLS[c] // B_W for c in ("bq", "bf", "bi", "bg")]
    ng = hgrn_norm_g[0][None]
    ob_p, st_p = hgrn(z, b_blk, hgrn_lb, ng, jnp.zeros((bp, B_HEADS, B_HEAD_DIM, B_HEAD_DIM), F32),
                      bp, t, 64, HGRN_SUB, 64)
    zs_b = per_seq(z[n_p:n_real, COLS["bq"]:COLS["bq"] + 4 * B_W], S_ROWS).reshape(bs * S_ROWS, 4 * B_W)
    ob_s, st_s = hgrn(zs_b, [0, 1, 2, 3], hgrn_lb, ng, state_hgrn[0], bs, S_ROWS, S_ROWS, S_ROWS, ts)
    ob_s = ob_s.reshape(bs, S_ROWS, B_W)[:, :ts].reshape(n_s, B_W)
    o_b = _pad_rows(jnp.concatenate([ob_p[:n_p], ob_s], axis=0), nt)

    mem_kv = matmul(mem_prompt[0].astype(BF16), w_mem_kv[0].astype(BF16), mem_prompt.shape[1], 512, name="mem_kv")
    om_p = mem_attention(z[None], COLS["mq"] // M_W, mem_kv[None], 0, mem_kv[None], 1, bp, t, 512)[0]
    zs_m = per_seq(zcols("mq", M_W, n_p, n_real), S_ROWS)
    n_mem = cache_mem_k.shape[2]
    om_s = mem_attention(zs_m, 0, cache_mem_k[0].reshape(bs, n_mem, M_W), 0,
                         cache_mem_v[0].reshape(bs, n_mem, M_W), 0, bs, S_ROWS, S_ROWS)
    o_m = _pad_rows(jnp.concatenate([om_p, om_s[:, :ts].reshape(n_s, M_W)], axis=0), nt)

    merged = merge(o_a, o_b, o_m, p_a[0].astype(BF16), p_b[0].astype(BF16), p_m[0].astype(BF16), z, COLS,
                   TOK_TILE, 512)
    h_pre = resid_matmul(merged, w_out[0].astype(BF16), x, alpha, TOK_TILE, 512)
    h, hb = layer_norm_dual(h_pre, ln1_g[0][None], ln1_b[0][None], 256)

    pq = matmul(hb, w_pq[0].astype(BF16), TOK_TILE, 512, name="peer_query")
    rank2, cnt1, e1, e2 = peer_route(pq, peer_sub_k1[0], peer_sub_k2[0], 256)
    p_out = peer_dense(hb, peer_u[0].astype(BF16), peer_v[0].astype(BF16), rank2, cnt1, e1, e2, TOK_TILE)
    y = layer_norm_resid(h, p_out, ln2_g[0][None], ln2_b[0][None], alpha, 256)

    kv_p = (depth, bp, t, A_KV_HEADS, A_HEAD_DIM)
    kv_s = (depth, bs, ts, A_KV_HEADS, A_HEAD_DIM)
    mem_shape = (depth, bp, mem_prompt.shape[1], M_HEADS, M_HEAD_DIM)
    g4 = A_KV_HEADS
    return (y[:n_p].reshape(bp, t, d), y[n_p:n_real].reshape(bs, ts, d),
            k4[:n_p * g4].reshape(kv_p), v4[:n_p * g4].reshape(kv_p), ik_rope[:n_p].reshape(depth, bp, t, IDX_DIM),
            mem_kv[:, :M_W].reshape(mem_shape), mem_kv[:, M_W:].reshape(mem_shape), st_p[None],
            k4[n_p * g4:n_real * g4].reshape(kv_s), v4[n_p * g4:n_real * g4].reshape(kv_s),
            ik_rope[n_p:n_real].reshape(depth, bs, ts, IDX_DIM), st_s[None])
```

```python
import functools
import math

import jax
import jax.numpy as jnp
import numpy as np
from jax import lax
from jax.experimental import pallas as pl
from jax.experimental.pallas import tpu as pltpu

F32 = jnp.float32
BF16 = jnp.bfloat16
I32 = jnp.int32

A_HEADS, A_KV_HEADS, A_HEAD_DIM = 16, 4, 128
IDX_HEADS, IDX_DIM = 32, 64
TOPK_MAX = 256
B_HEADS, B_HEAD_DIM = 8, 128
M_HEADS, M_HEAD_DIM = 4, 256
PEER_HEADS, PEER_NKEYS, PEER_DKEY, PEER_TOPK = 8, 128, 256, 16
PAGE_SIZE = 128
ROPE_THETA = 10000.0
LN_EPS = 1e-5
RMS_EPS = 1e-6

A_Q_W = A_HEADS * A_HEAD_DIM
A_KV_W = A_KV_HEADS * A_HEAD_DIM
IDX_Q_W = IDX_HEADS * IDX_DIM
B_W = B_HEADS * B_HEAD_DIM
M_W = M_HEADS * M_HEAD_DIM

D_MODEL = 4096

_SEGS = (("aq", A_Q_W), ("iq", IDX_Q_W), ("ak", A_KV_W), ("av", A_KV_W), ("bq", B_W), ("bf", B_W),
         ("bi", B_W), ("bg", B_W), ("mq", M_W), ("ga", D_MODEL), ("gb", D_MODEL), ("gm", D_MODEL),
         ("ikw", 128))
COLS = {}
_off = 0
for _name, _w in _SEGS:
    COLS[_name] = _off
    _off += _w
Z_TILE = 512
Z_WIDTH = -(-_off // Z_TILE) * Z_TILE
Z_TILE_WIDE = 1536
assert Z_WIDTH % Z_TILE_WIDE == 0

LANES = 128
Q_BLK = 128
KEY_CHUNK = 512
HGRN_SUB = 16
EXP_CLAMP = 80.0
Q_LOG2_SCALE = (A_HEAD_DIM ** -0.5) * math.log2(math.e)
INT_MIN = -2 ** 31
NEG_BIG = -1e30


def _cp(sem, vmem_mb):
    return pltpu.CompilerParams(dimension_semantics=sem, vmem_limit_bytes=vmem_mb * 2 ** 20)


def _whole_vmem():
    return pl.BlockSpec(memory_space=pltpu.VMEM)


def _mm_kernel(a_ref, b_ref, o_ref):
    o_ref[...] = jnp.dot(a_ref[...], b_ref[...], preferred_element_type=F32).astype(o_ref.dtype)


def matmul(a, b, tm, tn, out_dtype=F32, name="matmul", weights_outer=False):
    m, k = a.shape
    n = b.shape[1]
    if weights_outer:
        grid = (n // tn, m // tm)
        row, colt = (lambda j, i: (i, 0)), (lambda j, i: (0, j))
        out = lambda j, i: (i, j)
    else:
        grid = (m // tm, n // tn)
        row, colt = (lambda i, j: (i, 0)), (lambda i, j: (0, j))
        out = lambda i, j: (i, j)
    return pl.pallas_call(
        _mm_kernel,
        grid=grid,
        in_specs=[pl.BlockSpec((tm, k), row), pl.BlockSpec((k, tn), colt)],
        out_specs=pl.BlockSpec((tm, tn), out),
        out_shape=jax.ShapeDtypeStruct((m, n), out_dtype),
        compiler_params=_cp(("parallel", "parallel"), 48),
        name=name,
    )(a, b)


def _rope_kernel(q_ref, iq_ref, k_ref, v_ref, ikw_ref, c128_ref, s128_ref, c64_ref, s64_ref,
                 qhm_ref, iqhm_ref, k4_ref, kb_ref, v4_ref, vb_ref, iko_ref, ikb_ref):
    c128, s128 = c128_ref[...], s128_ref[...]
    c64, s64 = c64_ref[...], s64_ref[...]
    lane = lax.broadcasted_iota(I32, (Q_BLK, LANES), 1)
    first_half = (lane % IDX_DIM) < (IDX_DIM // 2)

    def rope128(x):
        return x * c128 + pltpu.roll(x, A_HEAD_DIM // 2, axis=1) * s128

    def rope64(x):
        partner = jnp.where(first_half, pltpu.roll(x, LANES - IDX_DIM // 2, axis=1),
                            pltpu.roll(x, IDX_DIM // 2, axis=1))
        return x * c64 + partner * s64

    for h in range(A_HEADS):
        qhm_ref[0, h] = (rope128(q_ref[:, h * LANES:(h + 1) * LANES]) * Q_LOG2_SCALE).astype(qhm_ref.dtype)
    for h in range(A_KV_HEADS):
        rows = pl.ds(h, Q_BLK, stride=A_KV_HEADS)
        kr = rope128(k_ref[:, h * LANES:(h + 1) * LANES])
        k4_ref[rows, :] = kr
        kb_ref[:, h * LANES:(h + 1) * LANES] = kr.astype(kb_ref.dtype)
        vh = v_ref[:, h * LANES:(h + 1) * LANES]
        v4_ref[rows, :] = vh
        vb_ref[:, h * LANES:(h + 1) * LANES] = vh.astype(vb_ref.dtype)
    for p in range(IDX_HEADS // 2):
        r = rope64(iq_ref[:, p * LANES:(p + 1) * LANES]).astype(iqhm_ref.dtype)
        iqhm_ref[0, 2 * p] = r[:, :IDX_DIM]
        iqhm_ref[0, 2 * p + 1] = r[:, IDX_DIM:]
    ikr = rope64(ikw_ref[...])[:, :IDX_DIM]
    iko_ref[...] = ikr
    ikb_ref[...] = ikr.astype(ikb_ref.dtype)


def rope_all(z, col, tabs):
    nt = z.shape[0]
    nb = nt // Q_BLK
    c128, s128, c64, s64 = tabs
    tab_spec = pl.BlockSpec((Q_BLK, LANES), lambda i: (i, 0))
    kv4_spec = pl.BlockSpec((Q_BLK * A_KV_HEADS, A_HEAD_DIM), lambda i: (i, 0))
    kvb_spec = pl.BlockSpec((Q_BLK, A_KV_W), lambda i: (i, 0))
    ik_spec = pl.BlockSpec((Q_BLK, IDX_DIM), lambda i: (i, 0))
    kv4 = jax.ShapeDtypeStruct((nt * A_KV_HEADS, A_HEAD_DIM), F32)
    kvb = jax.ShapeDtypeStruct((nt, A_KV_W), BF16)
    return pl.pallas_call(
        _rope_kernel,
        grid=(nb,),
        in_specs=[pl.BlockSpec((Q_BLK, A_Q_W), lambda i: (i, col["aq"] // A_Q_W)),
                  pl.BlockSpec((Q_BLK, IDX_Q_W), lambda i: (i, col["iq"] // IDX_Q_W)),
                  pl.BlockSpec((Q_BLK, A_KV_W), lambda i: (i, col["ak"] // A_KV_W)),
                  pl.BlockSpec((Q_BLK, A_KV_W), lambda i: (i, col["av"] // A_KV_W)),
                  pl.BlockSpec((Q_BLK, LANES), lambda i: (i, col["ikw"] // LANES)),
                  tab_spec, tab_spec, tab_spec, tab_spec],
        out_specs=[pl.BlockSpec((1, A_HEADS, Q_BLK, A_HEAD_DIM), lambda i: (i, 0, 0, 0)),
                   pl.BlockSpec((1, IDX_HEADS, Q_BLK, IDX_DIM), lambda i: (i, 0, 0, 0)),
                   kv4_spec, kvb_spec, kv4_spec, kvb_spec, ik_spec, ik_spec],
        out_shape=[jax.ShapeDtypeStruct((nb, A_HEADS, Q_BLK, A_HEAD_DIM), BF16),
                   jax.ShapeDtypeStruct((nb, IDX_HEADS, Q_BLK, IDX_DIM), BF16),
                   kv4, kvb, kv4, kvb,
                   jax.ShapeDtypeStruct((nt, IDX_DIM), F32), jax.ShapeDtypeStruct((nt, IDX_DIM), BF16)],
        compiler_params=_cp(("parallel",), 32),
        name="rope",
    )(z, z, z, z, z, c128, s128, c64, s64)


def rope_tables(pos):
    def tab(half, reps):
        inv_freq = ROPE_THETA ** (-jnp.arange(half, dtype=F32) / half)
        ang = pos.astype(F32)[:, None] * inv_freq[None, :]
        c, s = jnp.cos(ang), jnp.sin(ang)
        return jnp.tile(jnp.concatenate([c, c], 1), (1, reps)), jnp.tile(jnp.concatenate([-s, s], 1), (1, reps))
    c128, s128 = tab(A_HEAD_DIM // 2, 1)
    c64, s64 = tab(IDX_DIM // 2, 2)
    return c128, s128, c64, s64


def _sort_key(x):
    i = pltpu.bitcast(x, I32)
    return i ^ ((i >> 31) & 0x7FFFFFFF)


def _kth_largest_key(count_ge, k, rows):
    def body(it, cur):
        bit = jnp.left_shift(jnp.int32(1), 31 - it)
        cand = cur | bit
        ok = count_ge(cand ^ INT_MIN) >= k
        return jnp.where(ok, cand, cur)
    cur = lax.fori_loop(0, 32, body, jnp.zeros((rows, 1), I32))
    return jnp.maximum(cur ^ INT_MIN, INT_MIN + 1)


def _flash_step(s, mask, v, m_ref, l_ref, acc_ref, g):
    hh, rr, ss = s.shape
    s = jnp.where(mask[None], s, -jnp.inf)
    m_old = m_ref[g]
    m_new = jnp.maximum(m_old, jnp.max(s, axis=-1, keepdims=True))
    p = jnp.exp(s - m_new)
    alpha = jnp.exp(m_old - m_new)
    l_ref[g] = alpha * l_ref[g] + jnp.sum(p, axis=-1, keepdims=True)
    pv = jnp.dot(p.reshape(hh * rr, ss).astype(BF16), v, preferred_element_type=F32)
    acc_ref[g] = alpha * acc_ref[g] + pv.reshape(hh, rr, -1)
    m_ref[g] = m_new


WORD = 32
GROUP_CHUNKS = WORD * LANES // KEY_CHUNK
_BIT_MASKS = ((16, 0x0000FFFF), (8, 0x00FF00FF), (4, 0x0F0F0F0F), (2, 0x33333333), (1, 0x55555555))


def _transpose_bits(w):
    w = list(w)
    for d, m in _BIT_MASKS:
        mask = jnp.int32(m if m < 2 ** 31 else m - 2 ** 32)
        for j in range(WORD):
            if j & d:
                continue
            lo, hi = w[j], w[j + d]
            t = (lax.shift_right_logical(lo, jnp.int32(d)) ^ hi) & mask
            w[j + d] = hi ^ t
            w[j] = lo ^ lax.shift_left(t, jnp.int32(d))
    return w


def _kth_largest_bitsliced(key_ref, plane_ref, cand_ref, nch, k):
    rows = key_ref.shape[1]
    tiles_per_chunk = KEY_CHUNK // LANES
    ngroups = (nch + GROUP_CHUNKS - 1) // GROUP_CHUNKS
    slab = 8

    def build_group(g, carry):
        def build_slab(s, c2):
            r0 = pl.multiple_of(s * slab, slab)
            words = []
            for j in range(WORD):
                c = g * GROUP_CHUNKS + j // tiles_per_chunk
                lt = j % tiles_per_chunk
                cc = jnp.minimum(c, key_ref.shape[0] - 1)
                w = key_ref[cc, pl.ds(r0, slab), lt * LANES:(lt + 1) * LANES] ^ INT_MIN
                words.append(jnp.where(c < nch, w, 0))
            planes = _transpose_bits(words)
            for b in range(WORD):
                plane_ref[g, b, pl.ds(r0, slab), :] = planes[b]
            return c2
        lax.fori_loop(0, rows // slab, build_slab, 0)
        cand_ref[g] = jnp.full((rows, LANES), -1, I32)
        return carry

    lax.fori_loop(0, ngroups, build_group, 0)

    def bit_step(it, carry):
        t_u, k_rem = carry
        b = WORD - 1 - it

        def count(g, cnt):
            return cnt + lax.population_count(cand_ref[g] & plane_ref[g, b])
        cnt = lax.fori_loop(0, ngroups, count, jnp.zeros((rows, LANES), I32))
        c1 = jnp.sum(cnt, axis=1, keepdims=True)
        take = c1 >= k_rem
        take_b = jnp.broadcast_to(take, (rows, LANES))

        def update(g, c2):
            e = cand_ref[g]
            a = e & plane_ref[g, b]
            cand_ref[g] = jnp.where(take_b, a, e ^ a)
            return c2
        lax.fori_loop(0, ngroups, update, 0)
        t_u = jnp.where(take, t_u | jnp.left_shift(jnp.int32(1), b), t_u)
        return t_u, jnp.where(take, k_rem, k_rem - c1)

    t_u, _ = lax.fori_loop(0, WORD, bit_step, (jnp.zeros((rows, 1), I32), jnp.full((rows, 1), k, I32)))
    return jnp.maximum(t_u ^ INT_MIN, INT_MIN + 1)


def _flash_step2(s, mask, vext, m_ref, l_ref, acc_ref, g):
    hh, rr, ss = s.shape
    d = vext.shape[1] // 2
    s = jnp.where(mask[None], s, -jnp.inf).reshape(hh * rr, ss)
    m_old = m_ref[g]
    m_new = jnp.maximum(m_old, jnp.max(s, axis=-1, keepdims=True))
    p = jnp.exp2(s - jnp.concatenate([m_new] * (ss // LANES), axis=1))
    alpha = jnp.exp2(m_old - m_new)
    pv = jnp.dot(p.astype(BF16), vext, preferred_element_type=F32)
    acc_ref[g] = alpha * acc_ref[g] + pv[:, :d]
    l_ref[g] = alpha * l_ref[g] + pv[:, d:]
    m_ref[g] = m_new


def _dsa_prompt_kernel(iq_ref, w_ref, q_ref, ik_ref, k_ref, v_ref, o_ref,
                       key_ref, plane_ref, cand_ref, wb_ref, m_ref, l_ref, acc_ref, *, topk, w_lane0):
    i = pl.program_id(0)
    nch = (i * Q_BLK + Q_BLK + KEY_CHUNK - 1) // KEY_CHUNK
    hpg = A_HEADS // A_KV_HEADS
    heads_per_dot = 8

    wscale = (IDX_DIM ** -0.5) * (IDX_HEADS ** -0.5)
    wt = w_ref[...] * wscale
    for h in range(IDX_HEADS):
        wb_ref[h] = jnp.broadcast_to(wt[:, w_lane0 + h:w_lane0 + h + 1], (Q_BLK, LANES))

    q_pos = i * Q_BLK + lax.broadcasted_iota(I32, (Q_BLK, KEY_CHUNK), 0)
    k_off = lax.broadcasted_iota(I32, (Q_BLK, KEY_CHUNK), 1)

    def score_chunk(c, carry):
        k0 = pl.multiple_of(c * KEY_CHUNK, KEY_CHUNK)
        ikc = ik_ref[pl.ds(k0, KEY_CHUNK), :]
        ntile = KEY_CHUNK // LANES
        acc = [jnp.zeros((Q_BLK, LANES), F32)] * ntile
        for hg in range(IDX_HEADS // heads_per_dot):
            lhs = iq_ref[0, hg * heads_per_dot * Q_BLK:(hg + 1) * heads_per_dot * Q_BLK, :]
            d = lax.dot_general(lhs, ikc, (((1,), (1,)), ((), ())), preferred_element_type=F32)
            for hl in range(heads_per_dot):
                wb = wb_ref[hg * heads_per_dot + hl]
                r = jnp.maximum(d[hl * Q_BLK:(hl + 1) * Q_BLK], 0.0)
                acc = [acc[j] + wb * r[:, j * LANES:(j + 1) * LANES] for j in range(ntile)]
        sc = jnp.concatenate(acc, axis=1)
        keys = jnp.where(k0 + k_off <= q_pos, _sort_key(sc), INT_MIN)
        key_ref[c] = keys
        return carry

    lax.fori_loop(0, nch, score_chunk, 0)

    thr = _kth_largest_bitsliced(key_ref, plane_ref, cand_ref, nch, topk)

    m_ref[...] = jnp.full(m_ref.shape, NEG_BIG, F32)
    l_ref[...] = jnp.zeros(l_ref.shape, F32)
    acc_ref[...] = jnp.zeros(acc_ref.shape, F32)
    ones = jnp.ones((KEY_CHUNK, A_HEAD_DIM), BF16)

    def attend_chunk(c, carry):
        k0 = pl.multiple_of(c * KEY_CHUNK, KEY_CHUNK)
        mask = key_ref[c] >= thr
        for g in range(A_KV_HEADS):
            qg = q_ref[0, g * hpg * Q_BLK:(g + 1) * hpg * Q_BLK, :]
            kc = k_ref[pl.ds(k0, KEY_CHUNK), g * LANES:(g + 1) * LANES]
            vc = v_ref[pl.ds(k0, KEY_CHUNK), g * LANES:(g + 1) * LANES]
            s = lax.dot_general(qg, kc, (((1,), (1,)), ((), ())), preferred_element_type=F32)
            _flash_step2(s.reshape(hpg, Q_BLK, KEY_CHUNK), mask, jnp.concatenate([vc, ones], axis=1),
                         m_ref, l_ref, acc_ref, g)
        return carry

    lax.fori_loop(0, nch, attend_chunk, 0)
    for g in range(A_KV_HEADS):
        o = (acc_ref[g] / l_ref[g]).astype(o_ref.dtype)
        for hl in range(hpg):
            h = g * hpg + hl
            o_ref[:, h * A_HEAD_DIM:(h + 1) * A_HEAD_DIM] = o[hl * Q_BLK:(hl + 1) * Q_BLK]


def dsa_prompt(iq_hm, z, col, q_hm, ik, k, v, t):
    nb = t // Q_BLK
    topk = min(TOPK_MAX, t // 4)
    nch_max = (t + KEY_CHUNK - 1) // KEY_CHUNK
    ngroups_max = (nch_max + GROUP_CHUNKS - 1) // GROUP_CHUNKS
    hpg = A_HEADS // A_KV_HEADS
    kern = functools.partial(_dsa_prompt_kernel, topk=topk, w_lane0=IDX_DIM)
    return pl.pallas_call(
        kern,
        grid=(nb,),
        in_specs=[pl.BlockSpec((1, IDX_HEADS * Q_BLK, IDX_DIM), lambda i: (i, 0, 0)),
                  pl.BlockSpec((Q_BLK, LANES), lambda i: (i, col["ikw"] // LANES)),
                  pl.BlockSpec((1, A_HEADS * Q_BLK, A_HEAD_DIM), lambda i: (i, 0, 0)),
                  _whole_vmem(), _whole_vmem(), _whole_vmem()],
        out_specs=pl.BlockSpec((Q_BLK, A_Q_W), lambda i: (i, 0)),
        out_shape=jax.ShapeDtypeStruct((t, A_Q_W), BF16),
        scratch_shapes=[pltpu.VMEM((nch_max, Q_BLK, KEY_CHUNK), I32),
                        pltpu.VMEM((ngroups_max, WORD, Q_BLK, LANES), I32),
                        pltpu.VMEM((ngroups_max, Q_BLK, LANES), I32),
                        pltpu.VMEM((IDX_HEADS, Q_BLK, LANES), F32),
                        pltpu.VMEM((A_KV_HEADS, hpg * Q_BLK, LANES), F32),
                        pltpu.VMEM((A_KV_HEADS, hpg * Q_BLK, LANES), F32),
                        pltpu.VMEM((A_KV_HEADS, hpg * Q_BLK, A_HEAD_DIM), F32)],
        compiler_params=_cp(("arbitrary",), 56),
        name="dsa_prompt",
    )(iq_hm, z, q_hm, ik, k, v)


S_ROWS = 8
IDX_PAGES = 8
ATT_PAGES = 4


def _dsa_sample_index_kernel(pt_ref, iq_ref, w_ref, ikn_ref, *rest, past_len, t_new, topk):
    pages = rest[:IDX_PAGES]
    key_ref, thr_ref = rest[IDX_PAGES], rest[IDX_PAGES + 1]
    j = pl.program_id(1)
    n_past_tiles = past_len // PAGE_SIZE
    wscale = (IDX_DIM ** -0.5) * (IDX_HEADS ** -0.5)
    wt = w_ref[0] * wscale
    wb = [jnp.broadcast_to(wt[:, h:h + 1], (S_ROWS, LANES)) for h in range(IDX_HEADS)]
    iq = iq_ref[0]

    def tile_scores(ik_tile):
        d = lax.dot_general(iq, ik_tile, (((1,), (1,)), ((), ())), preferred_element_type=F32)
        acc = jnp.zeros((S_ROWS, LANES), F32)
        for h in range(IDX_HEADS):
            acc = acc + wb[h] * jnp.maximum(d[h * S_ROWS:(h + 1) * S_ROWS], 0.0)
        return acc

    for r in range(IDX_PAGES):
        key_ref[0, j * IDX_PAGES + r] = _sort_key(tile_scores(pages[r][0].astype(BF16)))

    @pl.when(j == pl.num_programs(1) - 1)
    def _():
        row = lax.broadcasted_iota(I32, (S_ROWS, LANES), 0)
        lane = lax.broadcasted_iota(I32, (S_ROWS, LANES), 1)
        valid = (lane < t_new) & (lane <= row)
        key_ref[0, n_past_tiles] = jnp.where(valid, _sort_key(tile_scores(ikn_ref[0])), INT_MIN)

        def count_ge(t):
            def body(c, cnt):
                return cnt + jnp.where(key_ref[0, c] >= t, 1, 0)
            cnt = lax.fori_loop(0, n_past_tiles + 1, body, jnp.zeros((S_ROWS, LANES), I32))
            return jnp.sum(cnt, axis=1, keepdims=True)

        thr = _kth_largest_key(count_ge, topk, S_ROWS)
        thr_ref[0] = jnp.broadcast_to(thr, (S_ROWS, LANES))


def dsa_sample_index(page_table, iq_s, iw_s, ik_new, cache_ik, t_new):
    b, n_pages = page_table.shape
    past_len = n_pages * PAGE_SIZE
    topk = min(TOPK_MAX, (past_len + t_new) // 4)
    steps = n_pages // IDX_PAGES
    kern = functools.partial(_dsa_sample_index_kernel, past_len=past_len, t_new=t_new, topk=topk)

    def page_spec(r):
        return pl.BlockSpec((1, PAGE_SIZE, IDX_DIM), lambda bi, j, pt: (pt[bi, j * IDX_PAGES + r], 0, 0))

    grid_spec = pltpu.PrefetchScalarGridSpec(
        num_scalar_prefetch=1,
        grid=(b, steps),
        in_specs=[pl.BlockSpec((1, IDX_HEADS * S_ROWS, IDX_DIM), lambda bi, j, pt: (bi, 0, 0)),
                  pl.BlockSpec((1, S_ROWS, IDX_HEADS), lambda bi, j, pt: (bi, 0, 0)),
                  pl.BlockSpec((1, PAGE_SIZE, IDX_DIM), lambda bi, j, pt: (bi, 0, 0))]
                 + [page_spec(r) for r in range(IDX_PAGES)],
        out_specs=[pl.BlockSpec((1, n_pages + 1, S_ROWS, LANES), lambda bi, j, pt: (bi, 0, 0, 0)),
                   pl.BlockSpec((1, S_ROWS, LANES), lambda bi, j, pt: (bi, 0, 0))],
    )
    return pl.pallas_call(
        kern,
        grid_spec=grid_spec,
        out_shape=[jax.ShapeDtypeStruct((b, n_pages + 1, S_ROWS, LANES), I32),
                   jax.ShapeDtypeStruct((b, S_ROWS, LANES), I32)],
        compiler_params=_cp(("parallel", "arbitrary"), 32),
        name="dsa_sample_index",
    )(page_table, iq_s, iw_s, ik_new, *([cache_ik] * IDX_PAGES))


def _dsa_sample_attend_kernel(pt_ref, key_ref, thr_ref, q_ref, kn_ref, vn_ref, *rest, n_pages):
    kpages = rest[:ATT_PAGES]
    vpages = rest[ATT_PAGES:2 * ATT_PAGES]
    o_ref, m_ref, l_ref, acc_ref = rest[2 * ATT_PAGES:]
    j = pl.program_id(1)
    hpg = A_HEADS // A_KV_HEADS
    scale = A_HEAD_DIM ** -0.5
    thr = thr_ref[0]

    @pl.when(j == 0)
    def _():
        m_ref[...] = jnp.full(m_ref.shape, NEG_BIG, F32)
        l_ref[...] = jnp.zeros(l_ref.shape, F32)
        acc_ref[...] = jnp.zeros(acc_ref.shape, F32)

    def attend(mask, kc, vc):
        for g in range(A_KV_HEADS):
            qg = q_ref[0, g * hpg * S_ROWS:(g + 1) * hpg * S_ROWS, :]
            kg = kc[:, g * LANES:(g + 1) * LANES]
            vg = vc[:, g * LANES:(g + 1) * LANES]
            s = lax.dot_general(qg, kg, (((1,), (1,)), ((), ())), preferred_element_type=F32) * scale
            _flash_step(s.reshape(hpg, S_ROWS, s.shape[-1]), mask, vg, m_ref, l_ref, acc_ref, g)

    mask = jnp.concatenate([key_ref[0, j * ATT_PAGES + r] >= thr for r in range(ATT_PAGES)], axis=1)
    kc = jnp.concatenate([kp[0].astype(BF16) for kp in kpages], axis=0)
    vc = jnp.concatenate([vp[0].astype(BF16) for vp in vpages], axis=0)
    attend(mask, kc, vc)

    @pl.when(j == pl.num_programs(1) - 1)
    def _():
        attend(key_ref[0, n_pages] >= thr, kn_ref[0], vn_ref[0])
        for g in range(A_KV_HEADS):
            o = acc_ref[g] / l_ref[g]
            o_ref[0, g * hpg * S_ROWS:(g + 1) * hpg * S_ROWS, :] = (
                o.reshape(hpg * S_ROWS, A_HEAD_DIM).astype(o_ref.dtype))


def dsa_sample_attend(page_table, keys, thr, q_s, k_new, v_new, cache_k, cache_v):
    b, n_pages = page_table.shape
    steps = n_pages // ATT_PAGES
    hpg = A_HEADS // A_KV_HEADS
    kern = functools.partial(_dsa_sample_attend_kernel, n_pages=n_pages)

    def page_spec(r):
        return pl.BlockSpec((1, PAGE_SIZE, A_KV_W), lambda bi, j, pt: (pt[bi, j * ATT_PAGES + r], 0, 0))

    grid_spec = pltpu.PrefetchScalarGridSpec(
        num_scalar_prefetch=1,
        grid=(b, steps),
        in_specs=[pl.BlockSpec((1, n_pages + 1, S_ROWS, LANES), lambda bi, j, pt: (bi, 0, 0, 0)),
                  pl.BlockSpec((1, S_ROWS, LANES), lambda bi, j, pt: (bi, 0, 0)),
                  pl.BlockSpec((1, A_HEADS * S_ROWS, A_HEAD_DIM), lambda bi, j, pt: (bi, 0, 0)),
                  pl.BlockSpec((1, PAGE_SIZE, A_KV_W), lambda bi, j, pt: (bi, 0, 0)),
                  pl.BlockSpec((1, PAGE_SIZE, A_KV_W), lambda bi, j, pt: (bi, 0, 0))]
                 + [page_spec(r) for r in range(ATT_PAGES)] * 2,
        out_specs=pl.BlockSpec((1, A_HEADS * S_ROWS, A_HEAD_DIM), lambda bi, j, pt: (bi, 0, 0)),
        scratch_shapes=[pltpu.VMEM((A_KV_HEADS, hpg, S_ROWS, 1), F32),
                        pltpu.VMEM((A_KV_HEADS, hpg, S_ROWS, 1), F32),
                        pltpu.VMEM((A_KV_HEADS, hpg, S_ROWS, A_HEAD_DIM), F32)],
    )
    return pl.pallas_call(
        kern,
        grid_spec=grid_spec,
        out_shape=jax.ShapeDtypeStruct((b, A_HEADS * S_ROWS, A_HEAD_DIM), BF16),
        compiler_params=_cp(("parallel", "arbitrary"), 32),
        name="dsa_sample_attend",
    )(page_table, keys, thr, q_s, k_new, v_new, *([cache_k] * ATT_PAGES), *([cache_v] * ATT_PAGES))


ATT_PAGES2 = 16


def _dsa_sample_scores_kernel(pt_ref, iq_ref, w_ref, ikn_ref, *rest, n_pages, t_new):
    pages = rest[:n_pages]
    key_ref = rest[n_pages]
    wscale = (IDX_DIM ** -0.5) * (IDX_HEADS ** -0.5)
    wt = w_ref[0] * wscale
    wb = [jnp.broadcast_to(wt[:, h:h + 1], (S_ROWS, LANES)) for h in range(IDX_HEADS)]
    iq = iq_ref[0]

    def tile_scores(ik_tile):
        d = lax.dot_general(iq, ik_tile, (((1,), (1,)), ((), ())), preferred_element_type=F32)
        acc = jnp.zeros((S_ROWS, LANES), F32)
        for h in range(IDX_HEADS):
            acc = acc + wb[h] * jnp.maximum(d[h * S_ROWS:(h + 1) * S_ROWS], 0.0)
        return acc

    for r in range(n_pages):
        key_ref[0, r] = _sort_key(tile_scores(pages[r][0].astype(BF16)))
    row = lax.broadcasted_iota(I32, (S_ROWS, LANES), 0)
    lane = lax.broadcasted_iota(I32, (S_ROWS, LANES), 1)
    valid = jnp.where(lane < t_new, lane, S_ROWS) <= row
    key_ref[0, n_pages] = jnp.where(valid, _sort_key(tile_scores(ikn_ref[0])), INT_MIN)


def _dsa_sample_thr_kernel(key_ref, thr_ref, *, topk):
    b, n_tiles = key_ref.shape[0], key_ref.shape[1]

    def count_ge(t):
        def body(c, cnt):
            return cnt + jnp.where(key_ref[:, c] >= t, 1, 0)
        cnt = lax.fori_loop(0, n_tiles, body, jnp.zeros((b, S_ROWS, LANES), I32))
        return jnp.sum(cnt, axis=2, keepdims=True)

    def body(it, cur):
        cand = cur | jnp.left_shift(jnp.int32(1), 31 - it)
        return jnp.where(count_ge(cand ^ INT_MIN) >= topk, cand, cur)

    cur = lax.fori_loop(0, 32, body, jnp.zeros((b, S_ROWS, 1), I32))
    thr_ref[...] = jnp.broadcast_to(jnp.maximum(cur ^ INT_MIN, INT_MIN + 1), thr_ref.shape)


def dsa_sample_index2(page_table, iq_s, iw_s, ik_new, cache_ik, t_new):
    b, n_pages = page_table.shape
    topk = min(TOPK_MAX, (n_pages * PAGE_SIZE + t_new) // 4)

    def page_spec(r):
        return pl.BlockSpec((1, PAGE_SIZE, IDX_DIM), lambda bi, pt: (pt[bi, r], 0, 0))

    grid_spec = pltpu.PrefetchScalarGridSpec(
        num_scalar_prefetch=1,
        grid=(b,),
        in_specs=[pl.BlockSpec((1, IDX_HEADS * S_ROWS, IDX_DIM), lambda bi, pt: (bi, 0, 0)),
                  pl.BlockSpec((1, S_ROWS, IDX_HEADS), lambda bi, pt: (bi, 0, 0)),
                  pl.BlockSpec((1, PAGE_SIZE, IDX_DIM), lambda bi, pt: (bi, 0, 0))]
                 + [page_spec(r) for r in range(n_pages)],
        out_specs=pl.BlockSpec((1, n_pages + 1, S_ROWS, LANES), lambda bi, pt: (bi, 0, 0, 0)),
    )
    keys = pl.pallas_call(
        functools.partial(_dsa_sample_scores_kernel, n_pages=n_pages, t_new=t_new),
        grid_spec=grid_spec,
        out_shape=jax.ShapeDtypeStruct((b, n_pages + 1, S_ROWS, LANES), I32),
        compiler_params=_cp(("parallel",), 32),
        name="dsa_sample_scores",
    )(page_table, iq_s, iw_s, ik_new, *([cache_ik] * n_pages))
    thr = pl.pallas_call(
        functools.partial(_dsa_sample_thr_kernel, topk=topk),
        in_specs=[_whole_vmem()],
        out_specs=_whole_vmem(),
        out_shape=jax.ShapeDtypeStruct((b, S_ROWS, LANES), I32),
        compiler_params=pltpu.CompilerParams(vmem_limit_bytes=32 * 2 ** 20),
        name="dsa_sample_threshold",
    )(keys)
    return keys, thr


def _dsa_sample_attend2_kernel(pt_ref, key_ref, thr_ref, q_ref, kn_ref, vn_ref, *rest, n_pages):
    kpages = rest[:ATT_PAGES2]
    vpages = rest[ATT_PAGES2:2 * ATT_PAGES2]
    o_ref, m_ref, l_ref, acc_ref = rest[2 * ATT_PAGES2:]
    j = pl.program_id(1)
    hpg = A_HEADS // A_KV_HEADS
    thr = thr_ref[0]

    @pl.when(j == 0)
    def _():
        m_ref[...] = jnp.full(m_ref.shape, NEG_BIG, F32)
        l_ref[...] = jnp.zeros(l_ref.shape, F32)
        acc_ref[...] = jnp.zeros(acc_ref.shape, F32)

    def attend(mask, g, kg, vg):
        qg = q_ref[0, g * hpg * S_ROWS:(g + 1) * hpg * S_ROWS, :]
        s = lax.dot_general(qg, kg, (((1,), (1,)), ((), ())), preferred_element_type=F32)
        vext = jnp.concatenate([vg, jnp.ones(vg.shape, BF16)], axis=1)
        _flash_step2(s.reshape(hpg, S_ROWS, s.shape[-1]), mask, vext, m_ref, l_ref, acc_ref, g)

    mask = jnp.concatenate([key_ref[0, j * ATT_PAGES2 + r] >= thr for r in range(ATT_PAGES2)], axis=1)
    for g in range(A_KV_HEADS):
        rows = pl.ds(g, PAGE_SIZE, stride=A_KV_HEADS)
        kg = jnp.concatenate([kp[rows, :].astype(BF16) for kp in kpages], axis=0)
        vg = jnp.concatenate([vp[rows, :].astype(BF16) for vp in vpages], axis=0)
        attend(mask, g, kg, vg)

    @pl.when(j == pl.num_programs(1) - 1)
    def _():
        mask_new = key_ref[0, n_pages] >= thr
        for g in range(A_KV_HEADS):
            attend(mask_new, g, kn_ref[0, :, g * LANES:(g + 1) * LANES], vn_ref[0, :, g * LANES:(g + 1) * LANES])
        for g in range(A_KV_HEADS):
            o_ref[0, g * hpg * S_ROWS:(g + 1) * hpg * S_ROWS, :] = (acc_ref[g] / l_ref[g]).astype(o_ref.dtype)


def dsa_sample_attend2(page_table, keys, thr, q_s, k_new, v_new, cache_k, cache_v):
    b, n_pages = page_table.shape
    steps = n_pages // ATT_PAGES2
    hpg = A_HEADS // A_KV_HEADS
    page_rows = PAGE_SIZE * A_KV_HEADS

    def page_spec(r):
        return pl.BlockSpec((page_rows, A_HEAD_DIM), lambda bi, j, pt: (pt[bi, j * ATT_PAGES2 + r], 0))

    grid_spec = pltpu.PrefetchScalarGridSpec(
        num_scalar_prefetch=1,
        grid=(b, steps),
        in_specs=[pl.BlockSpec((1, n_pages + 1, S_ROWS, LANES), lambda bi, j, pt: (bi, 0, 0, 0)),
                  pl.BlockSpec((1, S_ROWS, LANES), lambda bi, j, pt: (bi, 0, 0)),
                  pl.BlockSpec((1, A_HEADS * S_ROWS, A_HEAD_DIM), lambda bi, j, pt: (bi, 0, 0)),
                  pl.BlockSpec((1, PAGE_SIZE, A_KV_W), lambda bi, j, pt: (bi, 0, 0)),
                  pl.BlockSpec((1, PAGE_SIZE, A_KV_W), lambda bi, j, pt: (bi, 0, 0))]
                 + [page_spec(r) for r in range(ATT_PAGES2)] * 2,
        out_specs=pl.BlockSpec((1, A_HEADS * S_ROWS, A_HEAD_DIM), lambda bi, j, pt: (bi, 0, 0)),
        scratch_shapes=[pltpu.VMEM((A_KV_HEADS, hpg * S_ROWS, LANES), F32),
                        pltpu.VMEM((A_KV_HEADS, hpg * S_ROWS, LANES), F32),
                        pltpu.VMEM((A_KV_HEADS, hpg * S_ROWS, A_HEAD_DIM), F32)],
    )
    return pl.pallas_call(
        functools.partial(_dsa_sample_attend2_kernel, n_pages=n_pages),
        grid_spec=grid_spec,
        out_shape=jax.ShapeDtypeStruct((b, A_HEADS * S_ROWS, A_HEAD_DIM), BF16),
        compiler_params=_cp(("parallel", "arbitrary"), 48),
        name="dsa_sample_attend",
    )(page_table, keys, thr, q_s, k_new, v_new, *([cache_k] * ATT_PAGES2), *([cache_v] * ATT_PAGES2))


def _cumsum_rows(x):
    n = x.shape[0]
    row = lax.broadcasted_iota(I32, x.shape, 0)
    d = 1
    while d < n:
        x = x + jnp.where(row >= d, pltpu.roll(x, d, axis=0), 0.0)
        d *= 2
    return x


def _hgrn_kernel(q_ref, f_ref, i_ref, g_ref, lb_ref, ng_ref, s0_ref, o_ref, so_ref, st_ref,
                 *, chunk, sub, t_valid):
    n = pl.program_id(1)
    nsub = chunk // sub

    @pl.when(n == 0)
    def _():
        for h in range(B_HEADS):
            st_ref[h] = s0_ref[0, h].T

    lbx = lb_ref[...]
    lbe = jnp.exp(lbx - jnp.max(lbx, axis=0, keepdims=True))
    lb_all = lbe[0:1] / jnp.sum(lbe, axis=0, keepdims=True)
    row = lax.broadcasted_iota(I32, (chunk, B_HEAD_DIM), 0)
    valid = row < t_valid
    tt = lax.broadcasted_iota(I32, (chunk, nsub * chunk), 0)
    cc = lax.broadcasted_iota(I32, (chunk, nsub * chunk), 1)
    pair_ok = ((cc // chunk) == (tt // sub)) & ((cc % chunk) <= tt)
    ng = ng_ref[...]

    for h in range(B_HEADS):
        sl = slice(h * B_HEAD_DIM, (h + 1) * B_HEAD_DIM)
        lb = lb_all[:, sl]
        f = lb + (1.0 - lb) * jax.nn.sigmoid(f_ref[:, sl])
        logf = jnp.where(valid, jnp.log(f), 0.0)
        kk = jnp.where(valid, 1.0 - f, 0.0)
        q = q_ref[:, sl]
        iv = i_ref[:, sl]
        cum = _cumsum_rows(logf)
        last = cum[chunk - 1:chunk]
        st = st_ref[h]

        o = lax.dot_general((q * jnp.exp(cum)).astype(BF16), st.astype(BF16),
                            (((1,), (1,)), ((), ())), preferred_element_type=F32)

        refs = [jnp.zeros((1, B_HEAD_DIM), F32)] + [cum[s * sub - 1:s * sub] for s in range(1, nsub)]
        ref_row = refs[0]
        for s in range(1, nsub):
            ref_row = jnp.where(row >= s * sub, refs[s], ref_row)
        qt = (q * jnp.exp(cum - ref_row)).astype(BF16)
        kcat = jnp.concatenate(
            [(kk * jnp.exp(jnp.minimum(r - cum, EXP_CLAMP))).astype(BF16) for r in refs], axis=0)
        a = lax.dot_general(qt, kcat, (((1,), (1,)), ((), ())), preferred_element_type=F32)
        p = jnp.where(pair_ok, a, 0.0).astype(BF16)
        icat = jnp.concatenate([iv.astype(BF16)] * nsub, axis=0)
        o = o + jnp.dot(p, icat, preferred_element_type=F32)

        kd = (kk * jnp.exp(last - cum)).astype(BF16)
        upd = lax.dot_general(iv.astype(BF16), kd, (((0,), (0,)), ((), ())), preferred_element_type=F32)
        st_ref[h] = st * jnp.exp(last) + upd

        on = o * lax.rsqrt(jnp.mean(o * o, axis=-1, keepdims=True) + RMS_EPS) * ng
        gate = g_ref[:, sl]
        o_ref[:, sl] = (on * (gate * jax.nn.sigmoid(gate))).astype(o_ref.dtype)

    @pl.when(n == pl.num_programs(1) - 1)
    def _():
        for h in range(B_HEADS):
            so_ref[0, h] = st_ref[h].T


def hgrn(zsrc, colblk, lb, ng, s0, batch, t_pad, chunk, sub, t_valid):
    nch = t_pad // chunk
    kern = functools.partial(_hgrn_kernel, chunk=chunk, sub=sub, t_valid=t_valid)

    def zspec(cb):
        return pl.BlockSpec((chunk, B_W), lambda b, n: (b * nch + n, cb))

    return pl.pallas_call(
        kern,
        grid=(batch, nch),
        in_specs=[zspec(colblk[0]), zspec(colblk[1]), zspec(colblk[2]), zspec(colblk[3]),
                  pl.BlockSpec(lb.shape, lambda b, n: (0, 0)),
                  pl.BlockSpec((1, B_HEAD_DIM), lambda b, n: (0, 0)),
                  pl.BlockSpec((1, B_HEADS, B_HEAD_DIM, B_HEAD_DIM), lambda b, n: (b, 0, 0, 0))],
        out_specs=[pl.BlockSpec((chunk, B_W), lambda b, n: (b * nch + n, 0)),
                   pl.BlockSpec((1, B_HEADS, B_HEAD_DIM, B_HEAD_DIM), lambda b, n: (b, 0, 0, 0))],
        out_shape=[jax.ShapeDtypeStruct((batch * t_pad, B_W), BF16),
                   jax.ShapeDtypeStruct((batch, B_HEADS, B_HEAD_DIM, B_HEAD_DIM), F32)],
        scratch_shapes=[pltpu.VMEM((B_HEADS, B_HEAD_DIM, B_HEAD_DIM), F32)],
        compiler_params=_cp(("parallel", "arbitrary"), 32),
        name="hgrn",
    )(zsrc, zsrc, zsrc, zsrc, lb, ng, s0)


def _mem_attn_kernel(q_ref, mk_ref, mv_ref, o_ref):
    scale = M_HEAD_DIM ** -0.5
    for h in range(M_HEADS):
        sl = slice(h * M_HEAD_DIM, (h + 1) * M_HEAD_DIM)
        q = q_ref[0, :, sl].astype(BF16)
        mk = mk_ref[0, :, sl].astype(BF16)
        mv = mv_ref[0, :, sl].astype(BF16)
        s = lax.dot_general(q, mk, (((1,), (1,)), ((), ())), preferred_element_type=F32) * scale
        p = jnp.exp(s - jnp.max(s, axis=-1, keepdims=True))
        p = p / jnp.sum(p, axis=-1, keepdims=True)
        o_ref[0, :, sl] = jnp.dot(p.astype(BF16), mv, preferred_element_type=F32).astype(o_ref.dtype)


def mem_attention(q3, qcolblk, mk3, mkcolblk, mv3, mvcolblk, batch, t, tq):
    m = mk3.shape[1]
    return pl.pallas_call(
        _mem_attn_kernel,
        grid=(batch, t // tq),
        in_specs=[pl.BlockSpec((1, tq, M_W), lambda b, i: (b, i, qcolblk)),
                  pl.BlockSpec((1, m, M_W), lambda b, i: (b, 0, mkcolblk)),
                  pl.BlockSpec((1, m, M_W), lambda b, i: (b, 0, mvcolblk))],
        out_specs=pl.BlockSpec((1, tq, M_W), lambda b, i: (b, i, 0)),
        out_shape=jax.ShapeDtypeStruct((batch, t, M_W), BF16),
        compiler_params=_cp(("parallel", "parallel"), 32),
        name="mem_attention",
    )(q3, mk3, mv3)


def _merge_kernel(oa_ref, ob_ref, om_ref, pa_ref, pb_ref, pm_ref, ga_ref, gb_ref, gm_ref, o_ref):
    acc = jax.nn.sigmoid(ga_ref[...]) * jnp.dot(oa_ref[...], pa_ref[...], preferred_element_type=F32)
    acc += jax.nn.sigmoid(gb_ref[...]) * jnp.dot(ob_ref[...], pb_ref[...], preferred_element_type=F32)
    acc += jax.nn.sigmoid(gm_ref[...]) * jnp.dot(om_ref[...], pm_ref[...], preferred_element_type=F32)
    o_ref[...] = acc.astype(o_ref.dtype)


def merge(oa, ob, om, pa, pb, pm, z, col, tm, tn):
    nt = oa.shape[0]
    d = pa.shape[1]

    def act(w):
        return pl.BlockSpec((tm, w), lambda i, j: (i, 0))

    def wgt(w):
        return pl.BlockSpec((w, tn), lambda i, j: (0, j))

    def gate(name):
        return pl.BlockSpec((tm, tn), lambda i, j: (i, col[name] // tn + j))

    return pl.pallas_call(
        _merge_kernel,
        grid=(nt // tm, d // tn),
        in_specs=[act(oa.shape[1]), act(ob.shape[1]), act(om.shape[1]),
                  wgt(pa.shape[0]), wgt(pb.shape[0]), wgt(pm.shape[0]),
                  gate("ga"), gate("gb"), gate("gm")],
        out_specs=pl.BlockSpec((tm, tn), lambda i, j: (i, j)),
        out_shape=jax.ShapeDtypeStruct((nt, d), BF16),
        compiler_params=_cp(("parallel", "parallel"), 48),
        name="merge",
    )(oa, ob, om, pa, pb, pm, z, z, z)


def _resid_mm_kernel(a_ref, b_ref, x_ref, o_ref, *, alpha):
    o_ref[...] = alpha * x_ref[...] + jnp.dot(a_ref[...], b_ref[...], preferred_element_type=F32)


def resid_matmul(a, b, x, alpha, tm, tn):
    m, k = a.shape
    n = b.shape[1]
    return pl.pallas_call(
        functools.partial(_resid_mm_kernel, alpha=alpha),
        grid=(m // tm, n // tn),
        in_specs=[pl.BlockSpec((tm, k), lambda i, j: (i, 0)),
                  pl.BlockSpec((k, tn), lambda i, j: (0, j)),
                  pl.BlockSpec((tm, tn), lambda i, j: (i, j))],
        out_specs=pl.BlockSpec((tm, tn), lambda i, j: (i, j)),
        out_shape=jax.ShapeDtypeStruct((m, n), F32),
        compiler_params=_cp(("parallel", "parallel"), 48),
        name="resid_matmul",
    )(a, b, x)


def _ln(x, g, b):
    mu = jnp.mean(x, axis=-1, keepdims=True)
    xc = x - mu
    var = jnp.mean(xc * xc, axis=-1, keepdims=True)
    return xc * lax.rsqrt(var + LN_EPS) * g + b


def _ln_kernel(x_ref, g_ref, b_ref, o_ref, ob_ref):
    y = _ln(x_ref[...], g_ref[...], b_ref[...])
    o_ref[...] = y
    ob_ref[...] = y.astype(ob_ref.dtype)


def layer_norm_dual(x, g, b, tm):
    n, d = x.shape
    row = pl.BlockSpec((tm, d), lambda i: (i, 0))
    vec = pl.BlockSpec((1, d), lambda i: (0, 0))
    return pl.pallas_call(
        _ln_kernel,
        grid=(n // tm,),
        in_specs=[row, vec, vec],
        out_specs=[row, row],
        out_shape=[jax.ShapeDtypeStruct((n, d), F32), jax.ShapeDtypeStruct((n, d), BF16)],
        compiler_params=_cp(("parallel",), 48),
        name="layer_norm1",
    )(x, g, b)


def _ln_resid_kernel(h_ref, p_ref, g_ref, b_ref, o_ref, *, alpha):
    o_ref[...] = _ln(alpha * h_ref[...] + p_ref[...], g_ref[...], b_ref[...])


def layer_norm_resid(h, p, g, b, alpha, tm):
    n, d = h.shape
    row = pl.BlockSpec((tm, d), lambda i: (i, 0))
    vec = pl.BlockSpec((1, d), lambda i: (0, 0))
    return pl.pallas_call(
        functools.partial(_ln_resid_kernel, alpha=alpha),
        grid=(n // tm,),
        in_specs=[row, row, vec, vec],
        out_specs=row,
        out_shape=jax.ShapeDtypeStruct((n, d), F32),
        compiler_params=_cp(("parallel",), 48),
        name="layer_norm2",
    )(h, p, g, b)


def _top_ranked(x, n_top):
    kdim = x.shape[0]
    idx = lax.broadcasted_iota(I32, x.shape, 0).astype(F32)
    rank = jnp.full(x.shape, float(n_top), F32)
    vals = []
    for a in range(n_top):
        m = jnp.max(x, axis=0, keepdims=True)
        first = jnp.min(jnp.where(x == m, idx, float(kdim)), axis=0, keepdims=True)
        sel = idx == first
        rank = jnp.where(sel, float(a), rank)
        x = jnp.where(sel, -jnp.inf, x)
        vals.append(m)
    return rank, vals


def _peer_route_kernel(pq_ref, k1_ref, k2_ref, rank2_ref, cnt1_ref, e1_ref, e2_ref):
    half = PEER_DKEY // 2
    k1 = k1_ref[...].astype(BF16)
    k2 = k2_ref[...].astype(BF16)
    nt = (((1,), (1,)), ((), ()))
    for h in range(PEER_HEADS):
        q1 = pq_ref[:, h * PEER_DKEY:h * PEER_DKEY + half].astype(BF16)
        q2 = pq_ref[:, h * PEER_DKEY + half:(h + 1) * PEER_DKEY].astype(BF16)
        s1 = lax.dot_general(k1, q1, nt, preferred_element_type=F32)
        s2 = lax.dot_general(k2, q2, nt, preferred_element_type=F32)
        rank1, v1 = _top_ranked(s1, PEER_TOPK)
        rank2, v2 = _top_ranked(s2, PEER_TOPK)
        v2m = jnp.concatenate(v2, axis=0)
        cand = jnp.concatenate([v1[a] + v2m for a in range(PEER_TOPK)], axis=0)
        crank, _ = _top_ranked(cand, PEER_TOPK)
        chosen = jnp.where(crank < float(PEER_TOPK), 1.0, 0.0)
        cmax = v1[0] + v2[0]
        zsum = jnp.sum(chosen * jnp.exp(cand - cmax), axis=0, keepdims=True)
        cnt1 = jnp.zeros_like(s1)
        for a in range(PEER_TOPK):
            m_a = jnp.sum(chosen[a * PEER_TOPK:(a + 1) * PEER_TOPK], axis=0, keepdims=True)
            cnt1 = jnp.where(rank1 == float(a), m_a, cnt1)
        rank2_ref[h] = rank2.astype(rank2_ref.dtype)
        cnt1_ref[h] = cnt1
        e1_ref[h] = jnp.exp(s1 - v1[0])
        e2_ref[h] = (jnp.exp(s2 - v2[0]) / zsum).astype(e2_ref.dtype)


def peer_route(pq, k1, k2, tn):
    n = pq.shape[0]
    shape = (PEER_HEADS, PEER_NKEYS, n)
    ospec = pl.BlockSpec((PEER_HEADS, PEER_NKEYS, tn), lambda i: (0, 0, i))
    kspec = pl.BlockSpec(k1.shape, lambda i: (0, 0))
    return pl.pallas_call(
        _peer_route_kernel,
        grid=(n // tn,),
        in_specs=[pl.BlockSpec((tn, pq.shape[1]), lambda i: (i, 0)), kspec, kspec],
        out_specs=[ospec] * 4,
        out_shape=[jax.ShapeDtypeStruct(shape, BF16), jax.ShapeDtypeStruct(shape, F32),
                   jax.ShapeDtypeStruct(shape, F32), jax.ShapeDtypeStruct(shape, BF16)],
        compiler_params=_cp(("parallel",), 32),
        name="peer_route",
    )(pq, k1, k2)


PEER_EROWS = 4


def _gelu_exact(x):
    return 0.5 * x * (1.0 + lax.erf(x * (2.0 ** -0.5)))


def _peer_dense_kernel(x_ref, u_ref, v_ref, rank2_ref, cnt1_ref, e1_ref, e2_ref, o_ref, z_ref, *, n_tiles):
    r = pl.program_id(1)
    rd, wr = (r + 1) % 2, r % 2

    @pl.when(r == 0)
    def _():
        o_ref[...] = jnp.zeros(o_ref.shape, o_ref.dtype)
        z_ref[...] = jnp.zeros(z_ref.shape, z_ref.dtype)

    o_ref[...] += lax.dot_general(z_ref[rd], v_ref[...], (((0,), (0,)), ((), ())), preferred_element_type=F32)
    at = lax.dot_general(u_ref[...], x_ref[...], (((1,), (1,)), ((), ())), preferred_element_type=F32)
    act = _gelu_exact(at).astype(BF16)
    tile = jnp.minimum(r, n_tiles - 1)
    for rr in range(PEER_EROWS):
        i1 = tile * PEER_EROWS + rr
        g = jnp.zeros((PEER_NKEYS, x_ref.shape[0]), BF16)
        for h in range(PEER_HEADS):
            cnt = cnt1_ref[h, pl.ds(i1, 1), :].astype(BF16)
            e1 = e1_ref[h, pl.ds(i1, 1), :].astype(BF16)
            g = g + jnp.where(rank2_ref[h] < cnt, e2_ref[h] * e1, jnp.zeros((), BF16))
        z_ref[wr, rr * PEER_NKEYS:(rr + 1) * PEER_NKEYS, :] = g * act[rr * PEER_NKEYS:(rr + 1) * PEER_NKEYS]


def peer_dense(xb, u, v, rank2, cnt1, e1, e2, tn):
    n, d = xb.shape
    ne = u.shape[0]
    te = PEER_EROWS * PEER_NKEYS
    n_tiles = ne // te
    aux = pl.BlockSpec((PEER_HEADS, PEER_NKEYS, tn), lambda j, r: (0, 0, j))
    return pl.pallas_call(
        functools.partial(_peer_dense_kernel, n_tiles=n_tiles),
        grid=(n // tn, n_tiles + 1),
        in_specs=[pl.BlockSpec((tn, d), lambda j, r: (j, 0)),
                  pl.BlockSpec((te, d), lambda j, r: (jnp.minimum(r, n_tiles - 1), 0)),
                  pl.BlockSpec((te, d), lambda j, r: (jnp.maximum(r - 1, 0), 0)),
                  aux, aux, aux, aux],
        out_specs=pl.BlockSpec((tn, d), lambda j, r: (j, 0)),
        out_shape=jax.ShapeDtypeStruct((n, d), F32),
        scratch_shapes=[pltpu.VMEM((2, te, tn), BF16)],
        compiler_params=_cp(("parallel", "arbitrary"), 60),
        name="peer_dense",
    )(xb, u, v, rank2, cnt1, e1, e2)


TOK_TILE = 512


def _pad_rows(a, n, axis=0):
    pad = [(0, 0)] * a.ndim
    pad[axis] = (0, n - a.shape[axis])
    return jnp.pad(a, pad)


def _pack_plan():
    splits = (A_Q_W, A_KV_W, A_KV_W, IDX_Q_W, IDX_DIM, IDX_HEADS, 4 * B_W + M_W + 3 * D_MODEL)
    src = dict(zip(("aq", "ak", "av", "iq", "ik", "iw", "rest"), np.concatenate([[0], np.cumsum(splits)[:-1]])))
    assert src["rest"] % LANES == PACK_SHIFT and src["ik"] % LANES == 0
    plan = []
    for name in ("aq", "iq", "ak", "av"):
        width = A_Q_W if name in ("aq", "iq") else A_KV_W
        plan += [(0, (src[name] + o) // Z_TILE, 0) for o in range(0, width, Z_TILE)]
    plan += [(1, 0, (src["rest"] + o) // LANES) for o in range(0, splits[-1], Z_TILE)]
    plan += [(2, 0, src["ik"] // LANES)]
    assert len(plan) * Z_TILE == Z_WIDTH and COLS["ikw"] == (len(plan) - 1) * Z_TILE
    return np.array(plan, np.int32)


PACK_SHIFT = 96
PACK_ROWS = 512


def _pack_kernel(plan_ref, a_ref, b0_ref, b1_ref, b2_ref, b3_ref, b4_ref, o_ref):
    j = pl.program_id(1)
    mode = plan_ref[j, 0]
    lane = lax.broadcasted_iota(I32, (PACK_ROWS, LANES), 1)

    @pl.when(mode == 0)
    def _():
        o_ref[...] = a_ref[...].astype(o_ref.dtype)

    @pl.when(mode == 1)
    def _():
        b = [r[...] for r in (b0_ref, b1_ref, b2_ref, b3_ref, b4_ref)]
        rolled = [pltpu.roll(x, LANES - PACK_SHIFT, axis=1) for x in b]
        for t in range(Z_TILE // LANES):
            tile = jnp.where(lane < LANES - PACK_SHIFT, rolled[t], rolled[t + 1])
            o_ref[:, t * LANES:(t + 1) * LANES] = tile.astype(o_ref.dtype)

    @pl.when(mode == 2)
    def _():
        o_ref[...] = jnp.zeros(o_ref.shape, o_ref.dtype)
        o_ref[:, :LANES] = jnp.where(lane < IDX_DIM + IDX_HEADS, b0_ref[...], 0.0).astype(o_ref.dtype)


def pack_w_in(w):
    dm = w.shape[0]
    plan = _pack_plan()
    n_src_tiles = -(-w.shape[1] // LANES)

    def bspec(r):
        return pl.BlockSpec((PACK_ROWS, LANES), lambda i, j, p: (i, jnp.minimum(p[j, 2] + r, n_src_tiles - 1)))

    grid_spec = pltpu.PrefetchScalarGridSpec(
        num_scalar_prefetch=1,
        grid=(dm // PACK_ROWS, len(plan)),
        in_specs=[pl.BlockSpec((PACK_ROWS, Z_TILE), lambda i, j, p: (i, p[j, 1]))] + [bspec(r) for r in range(5)],
        out_specs=pl.BlockSpec((PACK_ROWS, Z_TILE), lambda i, j, p: (i, j)),
    )
    return pl.pallas_call(
        _pack_kernel,
        grid_spec=grid_spec,
        out_shape=jax.ShapeDtypeStruct((dm, Z_WIDTH), BF16),
        compiler_params=_cp(("parallel", "arbitrary"), 32),
        name="pack_w_in",
    )(jnp.asarray(plan), w, w, w, w, w, w)


def _pack_w_in(w):
    splits = (A_Q_W, A_KV_W, A_KV_W, IDX_Q_W, IDX_DIM, IDX_HEADS, B_W, B_W, B_W, B_W, M_W,
              D_MODEL, D_MODEL, D_MODEL)
    offs = np.concatenate([[0], np.cumsum(splits)])
    aq, ak, av, iq, ik, iw, bq, bf, bi, bg, mq, ga, gb, gm = [w[:, offs[i]:offs[i + 1]] for i in range(len(splits))]
    used = COLS["ikw"] + IDX_DIM + IDX_HEADS
    tail = jnp.zeros((w.shape[0], Z_WIDTH - used), w.dtype)
    return jnp.concatenate([aq, iq, ak, av, bq, bf, bi, bg, mq, ga, gb, gm, ik, iw, tail], axis=1).astype(BF16)


def kernel(x_prompt, x_sample, cache_k, cache_v, cache_idx_k, cache_mem_k, cache_mem_v, state_hgrn, page_table,
           mem_prompt, w_in, w_mem_kv, hgrn_lb, hgrn_norm_g, p_a, p_b, p_m, w_out, ln1_g, ln1_b, w_pq,
           peer_sub_k1, peer_sub_k2, peer_u, peer_v, ln2_g, ln2_b):
    depth = w_in.shape[0]
    assert depth == 1, "single trunk layer"
    bp, t, d = x_prompt.shape
    bs, ts, _ = x_sample.shape
    assert bp == 1 and d == D_MODEL and t % KEY_CHUNK == 0 and bs * ts == Q_BLK and ts <= S_ROWS
    n_p, n_s = bp * t, bs * ts
    n_real = n_p + n_s
    nt = -(-n_real // TOK_TILE) * TOK_TILE
    n_pool = cache_k.shape[1]
    past_len = page_table.shape[1] * PAGE_SIZE
    alpha = (2 * depth) ** 0.25
    nb_p = n_p // Q_BLK

    x = _pad_rows(jnp.concatenate([x_prompt.reshape(n_p, d), x_sample.reshape(n_s, d)], axis=0), nt)
    z = matmul(x.astype(BF16), pack_w_in(w_in[0]), TOK_TILE, Z_TILE_WIDE, name="proj_in", weights_outer=True)

    def zcols(name, width, lo, hi):
        return z[lo:hi, COLS[name]:COLS[name] + width]

    pos = jnp.concatenate([jnp.arange(t, dtype=I32), past_len + jnp.tile(jnp.arange(ts, dtype=I32), bs),
                           jnp.zeros((nt - n_real,), I32)])
    q_hm, iq_hm, k4, kb, v4, vb, ik_rope, ikb = rope_all(z, COLS, rope_tables(pos))
    nb = q_hm.shape[0]

    oa_p = dsa_prompt(iq_hm.reshape(nb, IDX_HEADS * Q_BLK, IDX_DIM), z, COLS,
                      q_hm.reshape(nb, A_HEADS * Q_BLK, A_HEAD_DIM), ikb[:n_p], kb[:n_p], vb[:n_p], t)

    def per_seq(a, rows):
        return _pad_rows(a.reshape((bs, ts) + a.shape[1:]), rows, axis=1)

    q_s = per_seq(q_hm[nb_p].transpose(1, 0, 2), S_ROWS).transpose(0, 2, 1, 3).reshape(bs, A_HEADS * S_ROWS, A_HEAD_DIM)
    iq_s = per_seq(iq_hm[nb_p].transpose(1, 0, 2), S_ROWS).transpose(0, 2, 1, 3).reshape(bs, IDX_HEADS * S_ROWS, IDX_DIM)
    iw_s = per_seq(z[n_p:n_real, COLS["ikw"] + IDX_DIM:COLS["ikw"] + IDX_DIM + IDX_HEADS], S_ROWS)
    ik_new = per_seq(ikb[n_p:n_real], PAGE_SIZE)
    k_new = per_seq(kb[n_p:n_real], PAGE_SIZE)
    v_new = per_seq(vb[n_p:n_real], PAGE_SIZE)
    keys, thr = dsa_sample_index2(page_table, iq_s, iw_s, ik_new, cache_idx_k[0], ts)
    cache_rows = n_pool * PAGE_SIZE * A_KV_HEADS
    os_hm = dsa_sample_attend2(page_table, keys, thr, q_s, k_new, v_new,
                               cache_k[0].reshape(cache_rows, A_HEAD_DIM), cache_v[0].reshape(cache_rows, A_HEAD_DIM))
    oa_s = os_hm.reshape(bs, A_HEADS, S_ROWS, A_HEAD_DIM)[:, :, :ts].transpose(0, 2, 1, 3).reshape(n_s, A_Q_W)
    o_a = _pad_rows(jnp.concatenate([oa_p, oa_s], axis=0), nt)

    b_blk = [COLS[c] // B_W for c in ("bq", "bf", "bi", "bg")]
    ng = hgrn_norm_g[0][None]
    ob_p, st_p = hgrn(z, b_blk, hgrn_lb, ng, jnp.zeros((bp, B_HEADS, B_HEAD_DIM, B_HEAD_DIM), F32),
                      bp, t, 64, HGRN_SUB, 64)
    zs_b = per_seq(z[n_p:n_real, COLS["bq"]:COLS["bq"] + 4 * B_W], S_ROWS).reshape(bs * S_ROWS, 4 * B_W)
    ob_s, st_s = hgrn(zs_b, [0, 1, 2, 3], hgrn_lb, ng, state_hgrn[0], bs, S_ROWS, S_ROWS, S_ROWS, ts)
    ob_s = ob_s.reshape(bs, S_ROWS, B_W)[:, :ts].reshape(n_s, B_W)
    o_b = _pad_rows(jnp.concatenate([ob_p[:n_p], ob_s], axis=0), nt)

    mem_kv = matmul(mem_prompt[0].astype(BF16), w_mem_kv[0].astype(BF16), mem_prompt.shape[1], 512, name="mem_kv")
    om_p = mem_attention(z[None], COLS["mq"] // M_W, mem_kv[None], 0, mem_kv[None], 1, bp, t, 512)[0]
    zs_m = per_seq(zcols("mq", M_W, n_p, n_real), S_ROWS)
    n_mem = cache_mem_k.shape[2]
    om_s = mem_attention(zs_m, 0, cache_mem_k[0].reshape(bs, n_mem, M_W), 0,
                         cache_mem_v[0].reshape(bs, n_mem, M_W), 0, bs, S_ROWS, S_ROWS)
    o_m = _pad_rows(jnp.concatenate([om_p, om_s[:, :ts].reshape(n_s, M_W)], axis=0), nt)

    merged = merge(o_a, o_b, o_m, p_a[0].astype(BF16), p_b[0].astype(BF16), p_m[0].astype(BF16), z, COLS,
                   TOK_TILE, 512)
    h_pre = resid_matmul(merged, w_out[0].astype(BF16), x, alpha, TOK_TILE, 512)
    h, hb = layer_norm_dual(h_pre, ln1_g[0][None], ln1_b[0][None], 256)

    pq = matmul(hb, w_pq[0].astype(BF16), TOK_TILE, 512, name="peer_query")
    rank2, cnt1, e1, e2 = peer_route(pq, peer_sub_k1[0], peer_sub_k2[0], 256)
    p_out = peer_dense(hb, peer_u[0].astype(BF16), peer_v[0].astype(BF16), rank2, cnt1, e1, e2, TOK_TILE)
    y = layer_norm_resid(h, p_out, ln2_g[0][None], ln2_b[0][None], alpha, 256)

    kv_p = (depth, bp, t, A_KV_HEADS, A_HEAD_DIM)
    kv_s = (depth, bs, ts, A_KV_HEADS, A_HEAD_DIM)
    mem_shape = (depth, bp, mem_prompt.shape[1], M_HEADS, M_HEAD_DIM)
    g4 = A_KV_HEADS
    return (y[:n_p].reshape(bp, t, d), y[n_p:n_real].reshape(bs, ts, d),
            k4[:n_p * g4].reshape(kv_p), v4[:n_p * g4].reshape(kv_p), ik_rope[:n_p].reshape(depth, bp, t, IDX_DIM),
            mem_kv[:, :M_W].reshape(mem_shape), mem_kv[:, M_W:].reshape(mem_shape), st_p[None],
            k4[n_p * g4:n_real * g4].reshape(kv_s), v4[n_p * g4:n_real * g4].reshape(kv_s),
            ik_rope[n_p:n_real].reshape(depth, bs, ts, IDX_DIM), st_s[None])
```

```python
import functools
import math

import jax
import jax.numpy as jnp
import numpy as np
from jax import lax
from jax.experimental import pallas as pl
from jax.experimental.pallas import tpu as pltpu

F32 = jnp.float32
BF16 = jnp.bfloat16
I32 = jnp.int32

A_HEADS, A_KV_HEADS, A_HEAD_DIM = 16, 4, 128
IDX_HEADS, IDX_DIM = 32, 64
TOPK_MAX = 256
B_HEADS, B_HEAD_DIM = 8, 128
M_HEADS, M_HEAD_DIM = 4, 256
PEER_HEADS, PEER_NKEYS, PEER_DKEY, PEER_TOPK = 8, 128, 256, 16
PAGE_SIZE = 128
ROPE_THETA = 10000.0
LN_EPS = 1e-5
RMS_EPS = 1e-6

A_Q_W = A_HEADS * A_HEAD_DIM
A_KV_W = A_KV_HEADS * A_HEAD_DIM
IDX_Q_W = IDX_HEADS * IDX_DIM
B_W = B_HEADS * B_HEAD_DIM
M_W = M_HEADS * M_HEAD_DIM

D_MODEL = 4096

_SEGS = (("aq", A_Q_W), ("iq", IDX_Q_W), ("ak", A_KV_W), ("av", A_KV_W), ("bq", B_W), ("bf", B_W),
         ("bi", B_W), ("bg", B_W), ("mq", M_W), ("ga", D_MODEL), ("gb", D_MODEL), ("gm", D_MODEL),
         ("ikw", 128))
COLS = {}
_off = 0
for _name, _w in _SEGS:
    COLS[_name] = _off
    _off += _w
Z_TILE = 512
Z_WIDTH = -(-_off // Z_TILE) * Z_TILE
Z_TILE_WIDE = 1536
assert Z_WIDTH % Z_TILE_WIDE == 0

LANES = 128
Q_BLK = 128
KEY_CHUNK = 512
HGRN_SUB = 16
EXP_CLAMP = 80.0
Q_LOG2_SCALE = (A_HEAD_DIM ** -0.5) * math.log2(math.e)
INT_MIN = -2 ** 31
NEG_BIG = -1e30


def _cp(sem, vmem_mb):
    return pltpu.CompilerParams(dimension_semantics=sem, vmem_limit_bytes=vmem_mb * 2 ** 20)


def _whole_vmem():
    return pl.BlockSpec(memory_space=pltpu.VMEM)


def _mm_kernel(a_ref, b_ref, o_ref):
    o_ref[...] = jnp.dot(a_ref[...], b_ref[...], preferred_element_type=F32).astype(o_ref.dtype)


def _mm_nt_kernel(a_ref, bt_ref, o_ref):
    o_ref[...] = lax.dot_general(a_ref[...], bt_ref[...], (((1,), (1,)), ((), ())),
                                 preferred_element_type=F32).astype(o_ref.dtype)


def matmul_nt(a, bt, tm, tn, out_dtype=F32, name="matmul_nt"):
    m, k = a.shape
    n = bt.shape[0]
    return pl.pallas_call(
        _mm_nt_kernel,
        grid=(n // tn, m // tm),
        in_specs=[pl.BlockSpec((tm, k), lambda j, i: (i, 0)), pl.BlockSpec((tn, k), lambda j, i: (j, 0))],
        out_specs=pl.BlockSpec((tm, tn), lambda j, i: (i, j)),
        out_shape=jax.ShapeDtypeStruct((m, n), out_dtype),
        compiler_params=_cp(("parallel", "parallel"), 48),
        name=name,
    )(a, bt)


def matmul(a, b, tm, tn, out_dtype=F32, name="matmul", weights_outer=False):
    m, k = a.shape
    n = b.shape[1]
    if weights_outer:
        grid = (n // tn, m // tm)
        row, colt = (lambda j, i: (i, 0)), (lambda j, i: (0, j))
        out = lambda j, i: (i, j)
    else:
        grid = (m // tm, n // tn)
        row, colt = (lambda i, j: (i, 0)), (lambda i, j: (0, j))
        out = lambda i, j: (i, j)
    return pl.pallas_call(
        _mm_kernel,
        grid=grid,
        in_specs=[pl.BlockSpec((tm, k), row), pl.BlockSpec((k, tn), colt)],
        out_specs=pl.BlockSpec((tm, tn), out),
        out_shape=jax.ShapeDtypeStruct((m, n), out_dtype),
        compiler_params=_cp(("parallel", "parallel"), 48),
        name=name,
    )(a, b)


def _rope_kernel(q_ref, iq_ref, k_ref, v_ref, ikw_ref, c128_ref, s128_ref, c64_ref, s64_ref,
                 qhm_ref, iqhm_ref, k4_ref, kb_ref, v4_ref, vb_ref, iko_ref, ikb_ref):
    c128, s128 = c128_ref[...], s128_ref[...]
    c64, s64 = c64_ref[...], s64_ref[...]
    lane = lax.broadcasted_iota(I32, (Q_BLK, LANES), 1)
    first_half = (lane % IDX_DIM) < (IDX_DIM // 2)

    def rope128(x):
        return x * c128 + pltpu.roll(x, A_HEAD_DIM // 2, axis=1) * s128

    def rope64(x):
        partner = jnp.where(first_half, pltpu.roll(x, LANES - IDX_DIM // 2, axis=1),
                            pltpu.roll(x, IDX_DIM // 2, axis=1))
        return x * c64 + partner * s64

    for h in range(A_HEADS):
        qhm_ref[0, h] = (rope128(q_ref[:, h * LANES:(h + 1) * LANES]) * Q_LOG2_SCALE).astype(qhm_ref.dtype)
    for h in range(A_KV_HEADS):
        rows = pl.ds(h, Q_BLK, stride=A_KV_HEADS)
        kr = rope128(k_ref[:, h * LANES:(h + 1) * LANES])
        k4_ref[rows, :] = kr
        kb_ref[:, h * LANES:(h + 1) * LANES] = kr.astype(kb_ref.dtype)
        vh = v_ref[:, h * LANES:(h + 1) * LANES]
        v4_ref[rows, :] = vh
        vb_ref[:, h * LANES:(h + 1) * LANES] = vh.astype(vb_ref.dtype)
    for p in range(IDX_HEADS // 2):
        r = rope64(iq_ref[:, p * LANES:(p + 1) * LANES]).astype(iqhm_ref.dtype)
        iqhm_ref[0, 2 * p] = r[:, :IDX_DIM]
        iqhm_ref[0, 2 * p + 1] = r[:, IDX_DIM:]
    ikr = rope64(ikw_ref[...])[:, :IDX_DIM]
    iko_ref[...] = ikr
    ikb_ref[...] = ikr.astype(ikb_ref.dtype)


def rope_all(z, col, tabs):
    nt = z.shape[0]
    nb = nt // Q_BLK
    c128, s128, c64, s64 = tabs
    tab_spec = pl.BlockSpec((Q_BLK, LANES), lambda i: (i, 0))
    kv4_spec = pl.BlockSpec((Q_BLK * A_KV_HEADS, A_HEAD_DIM), lambda i: (i, 0))
    kvb_spec = pl.BlockSpec((Q_BLK, A_KV_W), lambda i: (i, 0))
    ik_spec = pl.BlockSpec((Q_BLK, IDX_DIM), lambda i: (i, 0))
    kv4 = jax.ShapeDtypeStruct((nt * A_KV_HEADS, A_HEAD_DIM), F32)
    kvb = jax.ShapeDtypeStruct((nt, A_KV_W), BF16)
    return pl.pallas_call(
        _rope_kernel,
        grid=(nb,),
        in_specs=[pl.BlockSpec((Q_BLK, A_Q_W), lambda i: (i, col["aq"] // A_Q_W)),
                  pl.BlockSpec((Q_BLK, IDX_Q_W), lambda i: (i, col["iq"] // IDX_Q_W)),
                  pl.BlockSpec((Q_BLK, A_KV_W), lambda i: (i, col["ak"] // A_KV_W)),
                  pl.BlockSpec((Q_BLK, A_KV_W), lambda i: (i, col["av"] // A_KV_W)),
                  pl.BlockSpec((Q_BLK, LANES), lambda i: (i, col["ikw"] // LANES)),
                  tab_spec, tab_spec, tab_spec, tab_spec],
        out_specs=[pl.BlockSpec((1, A_HEADS, Q_BLK, A_HEAD_DIM), lambda i: (i, 0, 0, 0)),
                   pl.BlockSpec((1, IDX_HEADS, Q_BLK, IDX_DIM), lambda i: (i, 0, 0, 0)),
                   kv4_spec, kvb_spec, kv4_spec, kvb_spec, ik_spec, ik_spec],
        out_shape=[jax.ShapeDtypeStruct((nb, A_HEADS, Q_BLK, A_HEAD_DIM), BF16),
                   jax.ShapeDtypeStruct((nb, IDX_HEADS, Q_BLK, IDX_DIM), BF16),
                   kv4, kvb, kv4, kvb,
                   jax.ShapeDtypeStruct((nt, IDX_DIM), F32), jax.ShapeDtypeStruct((nt, IDX_DIM), BF16)],
        compiler_params=_cp(("parallel",), 32),
        name="rope",
    )(z, z, z, z, z, c128, s128, c64, s64)


def rope_tables(pos):
    def tab(half, reps):
        inv_freq = ROPE_THETA ** (-jnp.arange(half, dtype=F32) / half)
        ang = pos.astype(F32)[:, None] * inv_freq[None, :]
        c, s = jnp.cos(ang), jnp.sin(ang)
        return jnp.tile(jnp.concatenate([c, c], 1), (1, reps)), jnp.tile(jnp.concatenate([-s, s], 1), (1, reps))
    c128, s128 = tab(A_HEAD_DIM // 2, 1)
    c64, s64 = tab(IDX_DIM // 2, 2)
    return c128, s128, c64, s64


def _sort_key(x):
    i = pltpu.bitcast(x, I32)
    return i ^ ((i >> 31) & 0x7FFFFFFF)


def _kth_largest_key(count_ge, k, rows):
    def body(it, cur):
        bit = jnp.left_shift(jnp.int32(1), 31 - it)
        cand = cur | bit
        ok = count_ge(cand ^ INT_MIN) >= k
        return jnp.where(ok, cand, cur)
    cur = lax.fori_loop(0, 32, body, jnp.zeros((rows, 1), I32))
    return jnp.maximum(cur ^ INT_MIN, INT_MIN + 1)


def _flash_step(s, mask, v, m_ref, l_ref, acc_ref, g):
    hh, rr, ss = s.shape
    s = jnp.where(mask[None], s, -jnp.inf)
    m_old = m_ref[g]
    m_new = jnp.maximum(m_old, jnp.max(s, axis=-1, keepdims=True))
    p = jnp.exp(s - m_new)
    alpha = jnp.exp(m_old - m_new)
    l_ref[g] = alpha * l_ref[g] + jnp.sum(p, axis=-1, keepdims=True)
    pv = jnp.dot(p.reshape(hh * rr, ss).astype(BF16), v, preferred_element_type=F32)
    acc_ref[g] = alpha * acc_ref[g] + pv.reshape(hh, rr, -1)
    m_ref[g] = m_new


WORD = 32
GROUP_CHUNKS = WORD * LANES // KEY_CHUNK
_BIT_MASKS = ((16, 0x0000FFFF), (8, 0x00FF00FF), (4, 0x0F0F0F0F), (2, 0x33333333), (1, 0x55555555))


def _transpose_bits(w):
    w = list(w)
    for d, m in _BIT_MASKS:
        mask = jnp.int32(m if m < 2 ** 31 else m - 2 ** 32)
        for j in range(WORD):
            if j & d:
                continue
            lo, hi = w[j], w[j + d]
            t = (lax.shift_right_logical(lo, jnp.int32(d)) ^ hi) & mask
            w[j + d] = hi ^ t
            w[j] = lo ^ lax.shift_left(t, jnp.int32(d))
    return w


def _kth_largest_bitsliced(key_ref, plane_ref, cand_ref, nch, k):
    rows = key_ref.shape[1]
    tiles_per_chunk = KEY_CHUNK // LANES
    ngroups = (nch + GROUP_CHUNKS - 1) // GROUP_CHUNKS
    slab = 8

    def build_group(g, carry):
        def build_slab(s, c2):
            r0 = pl.multiple_of(s * slab, slab)
            words = []
            for j in range(WORD):
                c = g * GROUP_CHUNKS + j // tiles_per_chunk
                lt = j % tiles_per_chunk
                cc = jnp.minimum(c, key_ref.shape[0] - 1)
                w = key_ref[cc, pl.ds(r0, slab), lt * LANES:(lt + 1) * LANES] ^ INT_MIN
                words.append(jnp.where(c < nch, w, 0))
            planes = _transpose_bits(words)
            for b in range(WORD):
                plane_ref[g, b, pl.ds(r0, slab), :] = planes[b]
            return c2
        lax.fori_loop(0, rows // slab, build_slab, 0)
        cand_ref[g] = jnp.full((rows, LANES), -1, I32)
        return carry

    lax.fori_loop(0, ngroups, build_group, 0)

    def bit_step(it, carry):
        t_u, k_rem = carry
        b = WORD - 1 - it

        def count(g, cnt):
            return cnt + lax.population_count(cand_ref[g] & plane_ref[g, b])
        cnt = lax.fori_loop(0, ngroups, count, jnp.zeros((rows, LANES), I32))
        c1 = jnp.sum(cnt, axis=1, keepdims=True)
        take = c1 >= k_rem
        take_b = jnp.broadcast_to(take, (rows, LANES))

        def update(g, c2):
            e = cand_ref[g]
            a = e & plane_ref[g, b]
            cand_ref[g] = jnp.where(take_b, a, e ^ a)
            return c2
        lax.fori_loop(0, ngroups, update, 0)
        t_u = jnp.where(take, t_u | jnp.left_shift(jnp.int32(1), b), t_u)
        return t_u, jnp.where(take, k_rem, k_rem - c1)

    t_u, _ = lax.fori_loop(0, WORD, bit_step, (jnp.zeros((rows, 1), I32), jnp.full((rows, 1), k, I32)))
    return jnp.maximum(t_u ^ INT_MIN, INT_MIN + 1)


def _flash_step2(s, mask, vext, m_ref, l_ref, acc_ref, g):
    hh, rr, ss = s.shape
    d = vext.shape[1] // 2
    s = jnp.where(mask[None], s, -jnp.inf).reshape(hh * rr, ss)
    m_old = m_ref[g]
    m_new = jnp.maximum(m_old, jnp.max(s, axis=-1, keepdims=True))
    p = jnp.exp2(s - jnp.concatenate([m_new] * (ss // LANES), axis=1))
    alpha = jnp.exp2(m_old - m_new)
    pv = jnp.dot(p.astype(BF16), vext, preferred_element_type=F32)
    acc_ref[g] = alpha * acc_ref[g] + pv[:, :d]
    l_ref[g] = alpha * l_ref[g] + pv[:, d:]
    m_ref[g] = m_new


def _dsa_prompt_kernel(iq_ref, w_ref, q_ref, ik_ref, k_ref, v_ref, o_ref,
                       key_ref, plane_ref, cand_ref, wb_ref, m_ref, l_ref, acc_ref, *, topk, w_lane0):
    i = pl.program_id(0)
    nch = (i * Q_BLK + Q_BLK + KEY_CHUNK - 1) // KEY_CHUNK
    hpg = A_HEADS // A_KV_HEADS
    heads_per_dot = 8

    wscale = (IDX_DIM ** -0.5) * (IDX_HEADS ** -0.5)
    wt = w_ref[...] * wscale
    for h in range(IDX_HEADS):
        wb_ref[h] = jnp.broadcast_to(wt[:, w_lane0 + h:w_lane0 + h + 1], (Q_BLK, LANES))

    q_pos = i * Q_BLK + lax.broadcasted_iota(I32, (Q_BLK, KEY_CHUNK), 0)
    k_off = lax.broadcasted_iota(I32, (Q_BLK, KEY_CHUNK), 1)

    def score_chunk(c, carry):
        k0 = pl.multiple_of(c * KEY_CHUNK, KEY_CHUNK)
        ikc = ik_ref[pl.ds(k0, KEY_CHUNK), :]
        ntile = KEY_CHUNK // LANES
        acc = [jnp.zeros((Q_BLK, LANES), F32)] * ntile
        for hg in range(IDX_HEADS // heads_per_dot):
            lhs = iq_ref[0, hg * heads_per_dot * Q_BLK:(hg + 1) * heads_per_dot * Q_BLK, :]
            d = lax.dot_general(lhs, ikc, (((1,), (1,)), ((), ())), preferred_element_type=F32)
            for hl in range(heads_per_dot):
                wb = wb_ref[hg * heads_per_dot + hl]
                r = jnp.maximum(d[hl * Q_BLK:(hl + 1) * Q_BLK], 0.0)
                acc = [acc[j] + wb * r[:, j * LANES:(j + 1) * LANES] for j in range(ntile)]
        sc = jnp.concatenate(acc, axis=1)
        keys = jnp.where(k0 + k_off <= q_pos, _sort_key(sc), INT_MIN)
        key_ref[c] = keys
        return carry

    lax.fori_loop(0, nch, score_chunk, 0)

    thr = _kth_largest_bitsliced(key_ref, plane_ref, cand_ref, nch, topk)

    m_ref[...] = jnp.full(m_ref.shape, NEG_BIG, F32)
    l_ref[...] = jnp.zeros(l_ref.shape, F32)
    acc_ref[...] = jnp.zeros(acc_ref.shape, F32)
    ones = jnp.ones((KEY_CHUNK, A_HEAD_DIM), BF16)

    def attend_chunk(c, carry):
        k0 = pl.multiple_of(c * KEY_CHUNK, KEY_CHUNK)
        mask = key_ref[c] >= thr
        for g in range(A_KV_HEADS):
            qg = q_ref[0, g * hpg * Q_BLK:(g + 1) * hpg * Q_BLK, :]
            kc = k_ref[pl.ds(k0, KEY_CHUNK), g * LANES:(g + 1) * LANES]
            vc = v_ref[pl.ds(k0, KEY_CHUNK), g * LANES:(g + 1) * LANES]
            s = lax.dot_general(qg, kc, (((1,), (1,)), ((), ())), preferred_element_type=F32)
            _flash_step2(s.reshape(hpg, Q_BLK, KEY_CHUNK), mask, jnp.concatenate([vc, ones], axis=1),
                         m_ref, l_ref, acc_ref, g)
        return carry

    lax.fori_loop(0, nch, attend_chunk, 0)
    for g in range(A_KV_HEADS):
        o = (acc_ref[g] / l_ref[g]).astype(o_ref.dtype)
        for hl in range(hpg):
            h = g * hpg + hl
            o_ref[:, h * A_HEAD_DIM:(h + 1) * A_HEAD_DIM] = o[hl * Q_BLK:(hl + 1) * Q_BLK]


def dsa_prompt(iq_hm, z, col, q_hm, ik, k, v, t):
    nb = t // Q_BLK
    topk = min(TOPK_MAX, t // 4)
    nch_max = (t + KEY_CHUNK - 1) // KEY_CHUNK
    ngroups_max = (nch_max + GROUP_CHUNKS - 1) // GROUP_CHUNKS
    hpg = A_HEADS // A_KV_HEADS
    kern = functools.partial(_dsa_prompt_kernel, topk=topk, w_lane0=IDX_DIM)
    return pl.pallas_call(
        kern,
        grid=(nb,),
        in_specs=[pl.BlockSpec((1, IDX_HEADS * Q_BLK, IDX_DIM), lambda i: (i, 0, 0)),
                  pl.BlockSpec((Q_BLK, LANES), lambda i: (i, col["ikw"] // LANES)),
                  pl.BlockSpec((1, A_HEADS * Q_BLK, A_HEAD_DIM), lambda i: (i, 0, 0)),
                  _whole_vmem(), _whole_vmem(), _whole_vmem()],
        out_specs=pl.BlockSpec((Q_BLK, A_Q_W), lambda i: (i, 0)),
        out_shape=jax.ShapeDtypeStruct((t, A_Q_W), BF16),
        scratch_shapes=[pltpu.VMEM((nch_max, Q_BLK, KEY_CHUNK), I32),
                        pltpu.VMEM((ngroups_max, WORD, Q_BLK, LANES), I32),
                        pltpu.VMEM((ngroups_max, Q_BLK, LANES), I32),
                        pltpu.VMEM((IDX_HEADS, Q_BLK, LANES), F32),
                        pltpu.VMEM((A_KV_HEADS, hpg * Q_BLK, LANES), F32),
                        pltpu.VMEM((A_KV_HEADS, hpg * Q_BLK, LANES), F32),
                        pltpu.VMEM((A_KV_HEADS, hpg * Q_BLK, A_HEAD_DIM), F32)],
        compiler_params=_cp(("arbitrary",), 56),
        name="dsa_prompt",
    )(iq_hm, z, q_hm, ik, k, v)


S_ROWS = 8
IDX_PAGES = 8
ATT_PAGES = 4


def _dsa_sample_index_kernel(pt_ref, iq_ref, w_ref, ikn_ref, *rest, past_len, t_new, topk):
    pages = rest[:IDX_PAGES]
    key_ref, thr_ref = rest[IDX_PAGES], rest[IDX_PAGES + 1]
    j = pl.program_id(1)
    n_past_tiles = past_len // PAGE_SIZE
    wscale = (IDX_DIM ** -0.5) * (IDX_HEADS ** -0.5)
    wt = w_ref[0] * wscale
    wb = [jnp.broadcast_to(wt[:, h:h + 1], (S_ROWS, LANES)) for h in range(IDX_HEADS)]
    iq = iq_ref[0]

    def tile_scores(ik_tile):
        d = lax.dot_general(iq, ik_tile, (((1,), (1,)), ((), ())), preferred_element_type=F32)
        acc = jnp.zeros((S_ROWS, LANES), F32)
        for h in range(IDX_HEADS):
            acc = acc + wb[h] * jnp.maximum(d[h * S_ROWS:(h + 1) * S_ROWS], 0.0)
        return acc

    for r in range(IDX_PAGES):
        key_ref[0, j * IDX_PAGES + r] = _sort_key(tile_scores(pages[r][0].astype(BF16)))

    @pl.when(j == pl.num_programs(1) - 1)
    def _():
        row = lax.broadcasted_iota(I32, (S_ROWS, LANES), 0)
        lane = lax.broadcasted_iota(I32, (S_ROWS, LANES), 1)
        valid = (lane < t_new) & (lane <= row)
        key_ref[0, n_past_tiles] = jnp.where(valid, _sort_key(tile_scores(ikn_ref[0])), INT_MIN)

        def count_ge(t):
            def body(c, cnt):
                return cnt + jnp.where(key_ref[0, c] >= t, 1, 0)
            cnt = lax.fori_loop(0, n_past_tiles + 1, body, jnp.zeros((S_ROWS, LANES), I32))
            return jnp.sum(cnt, axis=1, keepdims=True)

        thr = _kth_largest_key(count_ge, topk, S_ROWS)
        thr_ref[0] = jnp.broadcast_to(thr, (S_ROWS, LANES))


def dsa_sample_index(page_table, iq_s, iw_s, ik_new, cache_ik, t_new):
    b, n_pages = page_table.shape
    past_len = n_pages * PAGE_SIZE
    topk = min(TOPK_MAX, (past_len + t_new) // 4)
    steps = n_pages // IDX_PAGES
    kern = functools.partial(_dsa_sample_index_kernel, past_len=past_len, t_new=t_new, topk=topk)

    def page_spec(r):
        return pl.BlockSpec((1, PAGE_SIZE, IDX_DIM), lambda bi, j, pt: (pt[bi, j * IDX_PAGES + r], 0, 0))

    grid_spec = pltpu.PrefetchScalarGridSpec(
        num_scalar_prefetch=1,
        grid=(b, steps),
        in_specs=[pl.BlockSpec((1, IDX_HEADS * S_ROWS, IDX_DIM), lambda bi, j, pt: (bi, 0, 0)),
                  pl.BlockSpec((1, S_ROWS, IDX_HEADS), lambda bi, j, pt: (bi, 0, 0)),
                  pl.BlockSpec((1, PAGE_SIZE, IDX_DIM), lambda bi, j, pt: (bi, 0, 0))]
                 + [page_spec(r) for r in range(IDX_PAGES)],
        out_specs=[pl.BlockSpec((1, n_pages + 1, S_ROWS, LANES), lambda bi, j, pt: (bi, 0, 0, 0)),
                   pl.BlockSpec((1, S_ROWS, LANES), lambda bi, j, pt: (bi, 0, 0))],
    )
    return pl.pallas_call(
        kern,
        grid_spec=grid_spec,
        out_shape=[jax.ShapeDtypeStruct((b, n_pages + 1, S_ROWS, LANES), I32),
                   jax.ShapeDtypeStruct((b, S_ROWS, LANES), I32)],
        compiler_params=_cp(("parallel", "arbitrary"), 32),
        name="dsa_sample_index",
    )(page_table, iq_s, iw_s, ik_new, *([cache_ik] * IDX_PAGES))


def _dsa_sample_attend_kernel(pt_ref, key_ref, thr_ref, q_ref, kn_ref, vn_ref, *rest, n_pages):
    kpages = rest[:ATT_PAGES]
    vpages = rest[ATT_PAGES:2 * ATT_PAGES]
    o_ref, m_ref, l_ref, acc_ref = rest[2 * ATT_PAGES:]
    j = pl.program_id(1)
    hpg = A_HEADS // A_KV_HEADS
    scale = A_HEAD_DIM ** -0.5
    thr = thr_ref[0]

    @pl.when(j == 0)
    def _():
        m_ref[...] = jnp.full(m_ref.shape, NEG_BIG, F32)
        l_ref[...] = jnp.zeros(l_ref.shape, F32)
        acc_ref[...] = jnp.zeros(acc_ref.shape, F32)

    def attend(mask, kc, vc):
        for g in range(A_KV_HEADS):
            qg = q_ref[0, g * hpg * S_ROWS:(g + 1) * hpg * S_ROWS, :]
            kg = kc[:, g * LANES:(g + 1) * LANES]
            vg = vc[:, g * LANES:(g + 1) * LANES]
            s = lax.dot_general(qg, kg, (((1,), (1,)), ((), ())), preferred_element_type=F32) * scale
            _flash_step(s.reshape(hpg, S_ROWS, s.shape[-1]), mask, vg, m_ref, l_ref, acc_ref, g)

    mask = jnp.concatenate([key_ref[0, j * ATT_PAGES + r] >= thr for r in range(ATT_PAGES)], axis=1)
    kc = jnp.concatenate([kp[0].astype(BF16) for kp in kpages], axis=0)
    vc = jnp.concatenate([vp[0].astype(BF16) for vp in vpages], axis=0)
    attend(mask, kc, vc)

    @pl.when(j == pl.num_programs(1) - 1)
    def _():
        attend(key_ref[0, n_pages] >= thr, kn_ref[0], vn_ref[0])
        for g in range(A_KV_HEADS):
            o = acc_ref[g] / l_ref[g]
            o_ref[0, g * hpg * S_ROWS:(g + 1) * hpg * S_ROWS, :] = (
                o.reshape(hpg * S_ROWS, A_HEAD_DIM).astype(o_ref.dtype))


def dsa_sample_attend(page_table, keys, thr, q_s, k_new, v_new, cache_k, cache_v):
    b, n_pages = page_table.shape
    steps = n_pages // ATT_PAGES
    hpg = A_HEADS // A_KV_HEADS
    kern = functools.partial(_dsa_sample_attend_kernel, n_pages=n_pages)

    def page_spec(r):
        return pl.BlockSpec((1, PAGE_SIZE, A_KV_W), lambda bi, j, pt: (pt[bi, j * ATT_PAGES + r], 0, 0))

    grid_spec = pltpu.PrefetchScalarGridSpec(
        num_scalar_prefetch=1,
        grid=(b, steps),
        in_specs=[pl.BlockSpec((1, n_pages + 1, S_ROWS, LANES), lambda bi, j, pt: (bi, 0, 0, 0)),
                  pl.BlockSpec((1, S_ROWS, LANES), lambda bi, j, pt: (bi, 0, 0)),
                  pl.BlockSpec((1, A_HEADS * S_ROWS, A_HEAD_DIM), lambda bi, j, pt: (bi, 0, 0)),
                  pl.BlockSpec((1, PAGE_SIZE, A_KV_W), lambda bi, j, pt: (bi, 0, 0)),
                  pl.BlockSpec((1, PAGE_SIZE, A_KV_W), lambda bi, j, pt: (bi, 0, 0))]
                 + [page_spec(r) for r in range(ATT_PAGES)] * 2,
        out_specs=pl.BlockSpec((1, A_HEADS * S_ROWS, A_HEAD_DIM), lambda bi, j, pt: (bi, 0, 0)),
        scratch_shapes=[pltpu.VMEM((A_KV_HEADS, hpg, S_ROWS, 1), F32),
                        pltpu.VMEM((A_KV_HEADS, hpg, S_ROWS, 1), F32),
                        pltpu.VMEM((A_KV_HEADS, hpg, S_ROWS, A_HEAD_DIM), F32)],
    )
    return pl.pallas_call(
        kern,
        grid_spec=grid_spec,
        out_shape=jax.ShapeDtypeStruct((b, A_HEADS * S_ROWS, A_HEAD_DIM), BF16),
        compiler_params=_cp(("parallel", "arbitrary"), 32),
        name="dsa_sample_attend",
    )(page_table, keys, thr, q_s, k_new, v_new, *([cache_k] * ATT_PAGES), *([cache_v] * ATT_PAGES))


ATT_PAGES2 = 16


def _dsa_sample_scores_kernel(pt_ref, iq_ref, w_ref, ikn_ref, *rest, n_pages, t_new):
    pages = rest[:n_pages]
    key_ref = rest[n_pages]
    wscale = (IDX_DIM ** -0.5) * (IDX_HEADS ** -0.5)
    wt = w_ref[0] * wscale
    wb = [jnp.broadcast_to(wt[:, h:h + 1], (S_ROWS, LANES)) for h in range(IDX_HEADS)]
    iq = iq_ref[0]

    def tile_scores(ik_tile, transposed):
        dims = (((1,), (0,)), ((), ())) if transposed else (((1,), (1,)), ((), ()))
        d = lax.dot_general(iq, ik_tile, dims, preferred_element_type=F32)
        acc = jnp.zeros((S_ROWS, LANES), F32)
        for h in range(IDX_HEADS):
            acc = acc + wb[h] * jnp.maximum(d[h * S_ROWS:(h + 1) * S_ROWS], 0.0)
        return acc

    for r in range(n_pages):
        key_ref[0, r] = _sort_key(tile_scores(pages[r][0].astype(BF16), True))
    row = lax.broadcasted_iota(I32, (S_ROWS, LANES), 0)
    lane = lax.broadcasted_iota(I32, (S_ROWS, LANES), 1)
    valid = jnp.where(lane < t_new, lane, S_ROWS) <= row
    key_ref[0, n_pages] = jnp.where(valid, _sort_key(tile_scores(ikn_ref[0], False)), INT_MIN)


def _dsa_sample_thr_kernel(key_ref, thr_ref, *, topk):
    b, n_tiles = key_ref.shape[0], key_ref.shape[1]

    def count_ge(t):
        def body(c, cnt):
            return cnt + jnp.where(key_ref[:, c] >= t, 1, 0)
        cnt = lax.fori_loop(0, n_tiles, body, jnp.zeros((b, S_ROWS, LANES), I32))
        return jnp.sum(cnt, axis=2, keepdims=True)

    def body(it, cur):
        cand = cur | jnp.left_shift(jnp.int32(1), 31 - it)
        return jnp.where(count_ge(cand ^ INT_MIN) >= topk, cand, cur)

    cur = lax.fori_loop(0, 32, body, jnp.zeros((b, S_ROWS, 1), I32))
    thr_ref[...] = jnp.broadcast_to(jnp.maximum(cur ^ INT_MIN, INT_MIN + 1), thr_ref.shape)


def dsa_sample_index2(page_table, iq_s, iw_s, ik_new, cache_ik, t_new):
    b, n_pages = page_table.shape
    topk = min(TOPK_MAX, (n_pages * PAGE_SIZE + t_new) // 4)

    def page_spec(r):
        return pl.BlockSpec((1, IDX_DIM, PAGE_SIZE), lambda bi, pt: (pt[bi, r], 0, 0))

    grid_spec = pltpu.PrefetchScalarGridSpec(
        num_scalar_prefetch=1,
        grid=(b,),
        in_specs=[pl.BlockSpec((1, IDX_HEADS * S_ROWS, IDX_DIM), lambda bi, pt: (bi, 0, 0)),
                  pl.BlockSpec((1, S_ROWS, IDX_HEADS), lambda bi, pt: (bi, 0, 0)),
                  pl.BlockSpec((1, PAGE_SIZE, IDX_DIM), lambda bi, pt: (bi, 0, 0))]
                 + [page_spec(r) for r in range(n_pages)],
        out_specs=pl.BlockSpec((1, n_pages + 1, S_ROWS, LANES), lambda bi, pt: (bi, 0, 0, 0)),
    )
    keys = pl.pallas_call(
        functools.partial(_dsa_sample_scores_kernel, n_pages=n_pages, t_new=t_new),
        grid_spec=grid_spec,
        out_shape=jax.ShapeDtypeStruct((b, n_pages + 1, S_ROWS, LANES), I32),
        compiler_params=_cp(("parallel",), 32),
        name="dsa_sample_scores",
    )(page_table, iq_s, iw_s, ik_new, *([cache_ik] * n_pages))
    thr = pl.pallas_call(
        functools.partial(_dsa_sample_thr_kernel, topk=topk),
        in_specs=[_whole_vmem()],
        out_specs=_whole_vmem(),
        out_shape=jax.ShapeDtypeStruct((b, S_ROWS, LANES), I32),
        compiler_params=pltpu.CompilerParams(vmem_limit_bytes=32 * 2 ** 20),
        name="dsa_sample_threshold",
    )(keys)
    return keys, thr


def _dsa_sample_attend2_kernel(pt_ref, key_ref, thr_ref, q_ref, kn_ref, vn_ref, *rest, n_pages):
    kpages = rest[:ATT_PAGES2]
    vpages = rest[ATT_PAGES2:2 * ATT_PAGES2]
    o_ref, m_ref, l_ref, acc_ref = rest[2 * ATT_PAGES2:]
    j = pl.program_id(1)
    hpg = A_HEADS // A_KV_HEADS
    thr = thr_ref[0]

    @pl.when(j == 0)
    def _():
        m_ref[...] = jnp.full(m_ref.shape, NEG_BIG, F32)
        l_ref[...] = jnp.zeros(l_ref.shape, F32)
        acc_ref[...] = jnp.zeros(acc_ref.shape, F32)

    def attend(mask, g, kg, vg):
        qg = q_ref[0, g * hpg * S_ROWS:(g + 1) * hpg * S_ROWS, :]
        s = lax.dot_general(qg, kg, (((1,), (1,)), ((), ())), preferred_element_type=F32)
        vext = jnp.concatenate([vg, jnp.ones(vg.shape, BF16)], axis=1)
        _flash_step2(s.reshape(hpg, S_ROWS, s.shape[-1]), mask, vext, m_ref, l_ref, acc_ref, g)

    mask = jnp.concatenate([key_ref[0, j * ATT_PAGES2 + r] >= thr for r in range(ATT_PAGES2)], axis=1)
    for g in range(A_KV_HEADS):
        rows = pl.ds(g, PAGE_SIZE, stride=A_KV_HEADS)
        kg = jnp.concatenate([kp[rows, :].astype(BF16) for kp in kpages], axis=0)
        vg = jnp.concatenate([vp[rows, :].astype(BF16) for vp in vpages], axis=0)
        attend(mask, g, kg, vg)

    @pl.when(j == pl.num_programs(1) - 1)
    def _():
        mask_new = key_ref[0, n_pages] >= thr
        for g in range(A_KV_HEADS):
            attend(mask_new, g, kn_ref[0, :, g * LANES:(g + 1) * LANES], vn_ref[0, :, g * LANES:(g + 1) * LANES])
        for g in range(A_KV_HEADS):
            o_ref[0, g * hpg * S_ROWS:(g + 1) * hpg * S_ROWS, :] = (acc_ref[g] / l_ref[g]).astype(o_ref.dtype)


def dsa_sample_attend2(page_table, keys, thr, q_s, k_new, v_new, cache_k, cache_v):
    b, n_pages = page_table.shape
    steps = n_pages // ATT_PAGES2
    hpg = A_HEADS // A_KV_HEADS
    page_rows = PAGE_SIZE * A_KV_HEADS

    def page_spec(r):
        return pl.BlockSpec((page_rows, A_HEAD_DIM), lambda bi, j, pt: (pt[bi, j * ATT_PAGES2 + r], 0))

    grid_spec = pltpu.PrefetchScalarGridSpec(
        num_scalar_prefetch=1,
        grid=(b, steps),
        in_specs=[pl.BlockSpec((1, n_pages + 1, S_ROWS, LANES), lambda bi, j, pt: (bi, 0, 0, 0)),
                  pl.BlockSpec((1, S_ROWS, LANES), lambda bi, j, pt: (bi, 0, 0)),
                  pl.BlockSpec((1, A_HEADS * S_ROWS, A_HEAD_DIM), lambda bi, j, pt: (bi, 0, 0)),
                  pl.BlockSpec((1, PAGE_SIZE, A_KV_W), lambda bi, j, pt: (bi, 0, 0)),
                  pl.BlockSpec((1, PAGE_SIZE, A_KV_W), lambda bi, j, pt: (bi, 0, 0))]
                 + [page_spec(r) for r in range(ATT_PAGES2)] * 2,
        out_specs=pl.BlockSpec((1, A_HEADS * S_ROWS, A_HEAD_DIM), lambda bi, j, pt: (bi, 0, 0)),
        scratch_shapes=[pltpu.VMEM((A_KV_HEADS, hpg * S_ROWS, LANES), F32),
                        pltpu.VMEM((A_KV_HEADS, hpg * S_ROWS, LANES), F32),
                        pltpu.VMEM((A_KV_HEADS, hpg * S_ROWS, A_HEAD_DIM), F32)],
    )
    return pl.pallas_call(
        functools.partial(_dsa_sample_attend2_kernel, n_pages=n_pages),
        grid_spec=grid_spec,
        out_shape=jax.ShapeDtypeStruct((b, A_HEADS * S_ROWS, A_HEAD_DIM), BF16),
        compiler_params=_cp(("parallel", "arbitrary"), 48),
        name="dsa_sample_attend",
    )(page_table, keys, thr, q_s, k_new, v_new, *([cache_k] * ATT_PAGES2), *([cache_v] * ATT_PAGES2))


def _cumsum_rows(x):
    n = x.shape[0]
    row = lax.broadcasted_iota(I32, x.shape, 0)
    d = 1
    while d < n:
        x = x + jnp.where(row >= d, pltpu.roll(x, d, axis=0), 0.0)
        d *= 2
    return x


def _hgrn_kernel(q_ref, f_ref, i_ref, g_ref, lb_ref, ng_ref, s0_ref, o_ref, so_ref, st_ref,
                 *, chunk, sub, t_valid):
    n = pl.program_id(1)
    nsub = chunk // sub

    @pl.when(n == 0)
    def _():
        for h in range(B_HEADS):
            st_ref[h] = s0_ref[0, h].T

    lbx = lb_ref[...]
    lbe = jnp.exp(lbx - jnp.max(lbx, axis=0, keepdims=True))
    lb_all = lbe[0:1] / jnp.sum(lbe, axis=0, keepdims=True)
    row = lax.broadcasted_iota(I32, (chunk, B_HEAD_DIM), 0)
    valid = row < t_valid
    tt = lax.broadcasted_iota(I32, (chunk, nsub * chunk), 0)
    cc = lax.broadcasted_iota(I32, (chunk, nsub * chunk), 1)
    pair_ok = ((cc // chunk) == (tt // sub)) & ((cc % chunk) <= tt)
    ng = ng_ref[...]

    for h in range(B_HEADS):
        sl = slice(h * B_HEAD_DIM, (h + 1) * B_HEAD_DIM)
        lb = lb_all[:, sl]
        f = lb + (1.0 - lb) * jax.nn.sigmoid(f_ref[:, sl])
        logf = jnp.where(valid, jnp.log(f), 0.0)
        kk = jnp.where(valid, 1.0 - f, 0.0)
        q = q_ref[:, sl]
        iv = i_ref[:, sl]
        cum = _cumsum_rows(logf)
        last = cum[chunk - 1:chunk]
        st = st_ref[h]

        o = lax.dot_general((q * jnp.exp(cum)).astype(BF16), st.astype(BF16),
                            (((1,), (1,)), ((), ())), preferred_element_type=F32)

        refs = [jnp.zeros((1, B_HEAD_DIM), F32)] + [cum[s * sub - 1:s * sub] for s in range(1, nsub)]
        ref_row = refs[0]
        for s in range(1, nsub):
            ref_row = jnp.where(row >= s * sub, refs[s], ref_row)
        qt = (q * jnp.exp(cum - ref_row)).astype(BF16)
        kcat = jnp.concatenate(
            [(kk * jnp.exp(jnp.minimum(r - cum, EXP_CLAMP))).astype(BF16) for r in refs], axis=0)
        a = lax.dot_general(qt, kcat, (((1,), (1,)), ((), ())), preferred_element_type=F32)
        p = jnp.where(pair_ok, a, 0.0).astype(BF16)
        icat = jnp.concatenate([iv.astype(BF16)] * nsub, axis=0)
        o = o + jnp.dot(p, icat, preferred_element_type=F32)

        kd = (kk * jnp.exp(last - cum)).astype(BF16)
        upd = lax.dot_general(iv.astype(BF16), kd, (((0,), (0,)), ((), ())), preferred_element_type=F32)
        st_ref[h] = st * jnp.exp(last) + upd

        on = o * lax.rsqrt(jnp.mean(o * o, axis=-1, keepdims=True) + RMS_EPS) * ng
        gate = g_ref[:, sl]
        o_ref[:, sl] = (on * (gate * jax.nn.sigmoid(gate))).astype(o_ref.dtype)

    @pl.when(n == pl.num_programs(1) - 1)
    def _():
        for h in range(B_HEADS):
            so_ref[0, h] = st_ref[h].T


def hgrn(zsrc, colblk, lb, ng, s0, batch, t_pad, chunk, sub, t_valid):
    nch = t_pad // chunk
    kern = functools.partial(_hgrn_kernel, chunk=chunk, sub=sub, t_valid=t_valid)

    def zspec(cb):
        return pl.BlockSpec((chunk, B_W), lambda b, n: (b * nch + n, cb))

    return pl.pallas_call(
        kern,
        grid=(batch, nch),
        in_specs=[zspec(colblk[0]), zspec(colblk[1]), zspec(colblk[2]), zspec(colblk[3]),
                  pl.BlockSpec(lb.shape, lambda b, n: (0, 0)),
                  pl.BlockSpec((1, B_HEAD_DIM), lambda b, n: (0, 0)),
                  pl.BlockSpec((1, B_HEADS, B_HEAD_DIM, B_HEAD_DIM), lambda b, n: (b, 0, 0, 0))],
        out_specs=[pl.BlockSpec((chunk, B_W), lambda b, n: (b * nch + n, 0)),
                   pl.BlockSpec((1, B_HEADS, B_HEAD_DIM, B_HEAD_DIM), lambda b, n: (b, 0, 0, 0))],
        out_shape=[jax.ShapeDtypeStruct((batch * t_pad, B_W), BF16),
                   jax.ShapeDtypeStruct((batch, B_HEADS, B_HEAD_DIM, B_HEAD_DIM), F32)],
        scratch_shapes=[pltpu.VMEM((B_HEADS, B_HEAD_DIM, B_HEAD_DIM), F32)],
        compiler_params=_cp(("parallel", "arbitrary"), 32),
        name="hgrn",
    )(zsrc, zsrc, zsrc, zsrc, lb, ng, s0)


def _mem_attn_kernel(q_ref, mk_ref, mv_ref, o_ref):
    scale = M_HEAD_DIM ** -0.5
    for h in range(M_HEADS):
        sl = slice(h * M_HEAD_DIM, (h + 1) * M_HEAD_DIM)
        q = q_ref[0, :, sl].astype(BF16)
        mk = mk_ref[0, :, sl].astype(BF16)
        mv = mv_ref[0, :, sl].astype(BF16)
        s = lax.dot_general(q, mk, (((1,), (1,)), ((), ())), preferred_element_type=F32) * scale
        p = jnp.exp(s - jnp.max(s, axis=-1, keepdims=True))
        p = p / jnp.sum(p, axis=-1, keepdims=True)
        o_ref[0, :, sl] = jnp.dot(p.astype(BF16), mv, preferred_element_type=F32).astype(o_ref.dtype)


def mem_attention(q3, qcolblk, mk3, mkcolblk, mv3, mvcolblk, batch, t, tq):
    m = mk3.shape[1]
    return pl.pallas_call(
        _mem_attn_kernel,
        grid=(batch, t // tq),
        in_specs=[pl.BlockSpec((1, tq, M_W), lambda b, i: (b, i, qcolblk)),
                  pl.BlockSpec((1, m, M_W), lambda b, i: (b, 0, mkcolblk)),
                  pl.BlockSpec((1, m, M_W), lambda b, i: (b, 0, mvcolblk))],
        out_specs=pl.BlockSpec((1, tq, M_W), lambda b, i: (b, i, 0)),
        out_shape=jax.ShapeDtypeStruct((batch, t, M_W), BF16),
        compiler_params=_cp(("parallel", "parallel"), 32),
        name="mem_attention",
    )(q3, mk3, mv3)


def _merge_kernel(oa_ref, ob_ref, om_ref, pa_ref, pb_ref, pm_ref, ga_ref, gb_ref, gm_ref, o_ref):
    acc = jax.nn.sigmoid(ga_ref[...]) * jnp.dot(oa_ref[...], pa_ref[...], preferred_element_type=F32)
    acc += jax.nn.sigmoid(gb_ref[...]) * jnp.dot(ob_ref[...], pb_ref[...], preferred_element_type=F32)
    acc += jax.nn.sigmoid(gm_ref[...]) * jnp.dot(om_ref[...], pm_ref[...], preferred_element_type=F32)
    o_ref[...] = acc.astype(o_ref.dtype)


def merge(oa, ob, om, pa, pb, pm, z, col, tm, tn):
    nt = oa.shape[0]
    d = pa.shape[1]

    def act(w):
        return pl.BlockSpec((tm, w), lambda j, i: (i, 0))

    def wgt(w):
        return pl.BlockSpec((w, tn), lambda j, i: (0, j))

    def gate(name):
        return pl.BlockSpec((tm, tn), lambda j, i: (i, col[name] // tn + j))

    return pl.pallas_call(
        _merge_kernel,
        grid=(d // tn, nt // tm),
        in_specs=[act(oa.shape[1]), act(ob.shape[1]), act(om.shape[1]),
                  wgt(pa.shape[0]), wgt(pb.shape[0]), wgt(pm.shape[0]),
                  gate("ga"), gate("gb"), gate("gm")],
        out_specs=pl.BlockSpec((tm, tn), lambda j, i: (i, j)),
        out_shape=jax.ShapeDtypeStruct((nt, d), BF16),
        compiler_params=_cp(("parallel", "parallel"), 48),
        name="merge",
    )(oa, ob, om, pa, pb, pm, z, z, z)


def _resid_mm_kernel(a_ref, b_ref, x_ref, o_ref, *, alpha):
    o_ref[...] = alpha * x_ref[...] + jnp.dot(a_ref[...], b_ref[...], preferred_element_type=F32)


def resid_matmul(a, b, x, alpha, tm, tn):
    m, k = a.shape
    n = b.shape[1]
    return pl.pallas_call(
        functools.partial(_resid_mm_kernel, alpha=alpha),
        grid=(n // tn, m // tm),
        in_specs=[pl.BlockSpec((tm, k), lambda j, i: (i, 0)),
                  pl.BlockSpec((k, tn), lambda j, i: (0, j)),
                  pl.BlockSpec((tm, tn), lambda j, i: (i, j))],
        out_specs=pl.BlockSpec((tm, tn), lambda j, i: (i, j)),
        out_shape=jax.ShapeDtypeStruct((m, n), F32),
        compiler_params=_cp(("parallel", "parallel"), 48),
        name="resid_matmul",
    )(a, b, x)


def _ln(x, g, b):
    mu = jnp.mean(x, axis=-1, keepdims=True)
    xc = x - mu
    var = jnp.mean(xc * xc, axis=-1, keepdims=True)
    return xc * lax.rsqrt(var + LN_EPS) * g + b


def _ln_kernel(x_ref, g_ref, b_ref, o_ref, ob_ref):
    y = _ln(x_ref[...], g_ref[...], b_ref[...])
    o_ref[...] = y
    ob_ref[...] = y.astype(ob_ref.dtype)


def layer_norm_dual(x, g, b, tm):
    n, d = x.shape
    row = pl.BlockSpec((tm, d), lambda i: (i, 0))
    vec = pl.BlockSpec((1, d), lambda i: (0, 0))
    return pl.pallas_call(
        _ln_kernel,
        grid=(n // tm,),
        in_specs=[row, vec, vec],
        out_specs=[row, row],
        out_shape=[jax.ShapeDtypeStruct((n, d), F32), jax.ShapeDtypeStruct((n, d), BF16)],
        compiler_params=_cp(("parallel",), 48),
        name="layer_norm1",
    )(x, g, b)


def _ln_resid_kernel(h_ref, p_ref, g_ref, b_ref, o_ref, *, alpha):
    o_ref[...] = _ln(alpha * h_ref[...] + p_ref[...], g_ref[...], b_ref[...])


def layer_norm_resid(h, p, g, b, alpha, tm):
    n, d = h.shape
    row = pl.BlockSpec((tm, d), lambda i: (i, 0))
    vec = pl.BlockSpec((1, d), lambda i: (0, 0))
    return pl.pallas_call(
        functools.partial(_ln_resid_kernel, alpha=alpha),
        grid=(n // tm,),
        in_specs=[row, row, vec, vec],
        out_specs=row,
        out_shape=jax.ShapeDtypeStruct((n, d), F32),
        compiler_params=_cp(("parallel",), 48),
        name="layer_norm2",
    )(h, p, g, b)


def _top_ranked(x, n_top):
    kdim = x.shape[0]
    idx = lax.broadcasted_iota(I32, x.shape, 0).astype(F32)
    rank = jnp.full(x.shape, float(n_top), F32)
    vals = []
    for a in range(n_top):
        m = jnp.max(x, axis=0, keepdims=True)
        first = jnp.min(jnp.where(x == m, idx, float(kdim)), axis=0, keepdims=True)
        sel = idx == first
        rank = jnp.where(sel, float(a), rank)
        x = jnp.where(sel, -jnp.inf, x)
        vals.append(m)
    return rank, vals


def _top_ranked_distinct(x, n_top):
    rank = jnp.full(x.shape, float(n_top), F32)
    vals = []
    for a in range(n_top):
        m = jnp.max(x, axis=0, keepdims=True)
        sel = x == m
        rank = jnp.where(sel, float(a), rank)
        x = jnp.where(sel, -jnp.inf, x)
        vals.append(m)
    n_ranked = jnp.sum(jnp.where(rank < float(n_top), 1.0, 0.0), axis=0, keepdims=True)
    return rank, vals, n_ranked


def _peer_route_kernel(pq_ref, k1_ref, k2_ref, rank2_ref, cnt1_ref, e1_ref, e2_ref):
    tn = pq_ref.shape[0]
    refs = (pq_ref, k1_ref, k2_ref, rank2_ref, cnt1_ref, e1_ref, e2_ref)
    for h in range(PEER_HEADS):
        irregular = _peer_route_head(h, *refs, distinct=True)
        n_irregular = jnp.sum(irregular.reshape(tn // LANES, LANES), axis=0, keepdims=True)

        @pl.when(jnp.max(n_irregular) > 0.0)
        def _():
            _peer_route_head(h, *refs, distinct=False)


def _peer_route_head(h, pq_ref, k1_ref, k2_ref, rank2_ref, cnt1_ref, e1_ref, e2_ref, *, distinct):
    half = PEER_DKEY // 2
    k1 = k1_ref[...].astype(BF16)
    k2 = k2_ref[...].astype(BF16)
    nt = (((1,), (1,)), ((), ()))
    irregular = jnp.zeros((1, pq_ref.shape[0]), F32)

    def top(x):
        if not distinct:
            return _top_ranked(x, PEER_TOPK) + (None,)
        return _top_ranked_distinct(x, PEER_TOPK)

    if True:
        q1 = pq_ref[:, h * PEER_DKEY:h * PEER_DKEY + half].astype(BF16)
        q2 = pq_ref[:, h * PEER_DKEY + half:(h + 1) * PEER_DKEY].astype(BF16)
        s1 = lax.dot_general(k1, q1, nt, preferred_element_type=F32)
        s2 = lax.dot_general(k2, q2, nt, preferred_element_type=F32)
        rank1, v1, n1 = top(s1)
        rank2, v2, n2 = top(s2)
        v2m = jnp.concatenate(v2, axis=0)
        cand = jnp.concatenate([v1[a] + v2m for a in range(PEER_TOPK)], axis=0)
        crank, _, nc = top(cand)
        if distinct:
            for cnt in (n1, n2, nc):
                irregular = irregular + jnp.where(cnt == float(PEER_TOPK), 0.0, 1.0)
        chosen = jnp.where(crank < float(PEER_TOPK), 1.0, 0.0)
        cmax = v1[0] + v2[0]
        zsum = jnp.sum(chosen * jnp.exp(cand - cmax), axis=0, keepdims=True)
        cnt1 = jnp.zeros_like(s1)
        for a in range(PEER_TOPK):
            m_a = jnp.sum(chosen[a * PEER_TOPK:(a + 1) * PEER_TOPK], axis=0, keepdims=True)
            cnt1 = jnp.where(rank1 == float(a), m_a, cnt1)
        rank2_ref[h] = rank2.astype(rank2_ref.dtype)
        cnt1_ref[h] = cnt1
        e1_ref[h] = jnp.exp(s1 - v1[0])
        e2_ref[h] = (jnp.exp(s2 - v2[0]) / zsum).astype(e2_ref.dtype)
    return irregular


def peer_route(pq, k1, k2, tn):
    n = pq.shape[0]
    shape = (PEER_HEADS, PEER_NKEYS, n)
    ospec = pl.BlockSpec((PEER_HEADS, PEER_NKEYS, tn), lambda i: (0, 0, i))
    kspec = pl.BlockSpec(k1.shape, lambda i: (0, 0))
    return pl.pallas_call(
        _peer_route_kernel,
        grid=(n // tn,),
        in_specs=[pl.BlockSpec((tn, pq.shape[1]), lambda i: (i, 0)), kspec, kspec],
        out_specs=[ospec] * 4,
        out_shape=[jax.ShapeDtypeStruct(shape, BF16), jax.ShapeDtypeStruct(shape, F32),
                   jax.ShapeDtypeStruct(shape, F32), jax.ShapeDtypeStruct(shape, BF16)],
        compiler_params=_cp(("parallel",), 32),
        name="peer_route",
    )(pq, k1, k2)


PEER_EROWS = 4


def _gelu_exact(x):
    return 0.5 * x * (1.0 + lax.erf(x * (2.0 ** -0.5)))


def _peer_dense_kernel(x_ref, u_ref, v_ref, rank2_ref, cnt1_ref, e1_ref, e2_ref, o_ref, z_ref, *, n_tiles):
    r = pl.program_id(1)
    rd, wr = (r + 1) % 2, r % 2

    @pl.when(r == 0)
    def _():
        o_ref[...] = jnp.zeros(o_ref.shape, o_ref.dtype)
        z_ref[...] = jnp.zeros(z_ref.shape, z_ref.dtype)

    o_ref[...] += lax.dot_general(z_ref[rd], v_ref[...], (((0,), (0,)), ((), ())), preferred_element_type=F32)
    at = lax.dot_general(u_ref[...], x_ref[...], (((1,), (1,)), ((), ())), preferred_element_type=F32)
    act = _gelu_exact(at).astype(BF16)
    tile = jnp.minimum(r, n_tiles - 1)
    for rr in range(PEER_EROWS):
        i1 = tile * PEER_EROWS + rr
        g = jnp.zeros((PEER_NKEYS, x_ref.shape[0]), BF16)
        for h in range(PEER_HEADS):
            cnt = cnt1_ref[h, pl.ds(i1, 1), :].astype(BF16)
            e1 = e1_ref[h, pl.ds(i1, 1), :].astype(BF16)
            g = g + jnp.where(rank2_ref[h] < cnt, e2_ref[h] * e1, jnp.zeros((), BF16))
        z_ref[wr, rr * PEER_NKEYS:(rr + 1) * PEER_NKEYS, :] = g * act[rr * PEER_NKEYS:(rr + 1) * PEER_NKEYS]


def peer_dense(xb, u, v, rank2, cnt1, e1, e2, tn):
    n, d = xb.shape
    ne = u.shape[0]
    te = PEER_EROWS * PEER_NKEYS
    n_tiles = ne // te
    aux = pl.BlockSpec((PEER_HEADS, PEER_NKEYS, tn), lambda j, r: (0, 0, j))
    return pl.pallas_call(
        functools.partial(_peer_dense_kernel, n_tiles=n_tiles),
        grid=(n // tn, n_tiles + 1),
        in_specs=[pl.BlockSpec((tn, d), lambda j, r: (j, 0)),
                  pl.BlockSpec((te, d), lambda j, r: (jnp.minimum(r, n_tiles - 1), 0)),
                  pl.BlockSpec((te, d), lambda j, r: (jnp.maximum(r - 1, 0), 0)),
                  aux, aux, aux, aux],
        out_specs=pl.BlockSpec((tn, d), lambda j, r: (j, 0)),
        out_shape=jax.ShapeDtypeStruct((n, d), F32),
        scratch_shapes=[pltpu.VMEM((2, te, tn), BF16)],
        compiler_params=_cp(("parallel", "arbitrary"), 60),
        name="peer_dense",
    )(xb, u, v, rank2, cnt1, e1, e2)


TOK_TILE = 512
WIDE_TILE = 1024


def _pad_rows(a, n, axis=0):
    pad = [(0, 0)] * a.ndim
    pad[axis] = (0, n - a.shape[axis])
    return jnp.pad(a, pad)


def _pack_plan():
    splits = (A_Q_W, A_KV_W, A_KV_W, IDX_Q_W, IDX_DIM, IDX_HEADS, 4 * B_W + M_W + 3 * D_MODEL)
    src = dict(zip(("aq", "ak", "av", "iq", "ik", "iw", "rest"), np.concatenate([[0], np.cumsum(splits)[:-1]])))
    assert src["rest"] % LANES == PACK_SHIFT and src["ik"] % LANES == 0
    plan = []
    for name in ("aq", "iq", "ak", "av"):
        width = A_Q_W if name in ("aq", "iq") else A_KV_W
        plan += [(0, (src[name] + o) // Z_TILE, 0) for o in range(0, width, Z_TILE)]
    plan += [(1, 0, (src["rest"] + o) // LANES) for o in range(0, splits[-1], Z_TILE)]
    plan += [(2, 0, src["ik"] // LANES)]
    assert len(plan) * Z_TILE == Z_WIDTH and COLS["ikw"] == (len(plan) - 1) * Z_TILE
    return np.array(plan, np.int32)


PACK_SHIFT = 96
PACK_ROWS = 512


def _pack_kernel(plan_ref, a_ref, b0_ref, b1_ref, b2_ref, b3_ref, b4_ref, o_ref):
    j = pl.program_id(1)
    mode = plan_ref[j, 0]
    lane = lax.broadcasted_iota(I32, (PACK_ROWS, LANES), 1)

    @pl.when(mode == 0)
    def _():
        o_ref[...] = a_ref[...].astype(o_ref.dtype)

    @pl.when(mode == 1)
    def _():
        b = [r[...] for r in (b0_ref, b1_ref, b2_ref, b3_ref, b4_ref)]
        rolled = [pltpu.roll(x, LANES - PACK_SHIFT, axis=1) for x in b]
        for t in range(Z_TILE // LANES):
            tile = jnp.where(lane < LANES - PACK_SHIFT, rolled[t], rolled[t + 1])
            o_ref[:, t * LANES:(t + 1) * LANES] = tile.astype(o_ref.dtype)

    @pl.when(mode == 2)
    def _():
        o_ref[...] = jnp.zeros(o_ref.shape, o_ref.dtype)
        o_ref[:, :LANES] = jnp.where(lane < IDX_DIM + IDX_HEADS, b0_ref[...], 0.0).astype(o_ref.dtype)


def pack_w_in(w):
    dm = w.shape[0]
    plan = _pack_plan()
    n_src_tiles = -(-w.shape[1] // LANES)

    def bspec(r):
        return pl.BlockSpec((PACK_ROWS, LANES), lambda i, j, p: (i, jnp.minimum(p[j, 2] + r, n_src_tiles - 1)))

    grid_spec = pltpu.PrefetchScalarGridSpec(
        num_scalar_prefetch=1,
        grid=(dm // PACK_ROWS, len(plan)),
        in_specs=[pl.BlockSpec((PACK_ROWS, Z_TILE), lambda i, j, p: (i, p[j, 1]))] + [bspec(r) for r in range(5)],
        out_specs=pl.BlockSpec((PACK_ROWS, Z_TILE), lambda i, j, p: (i, j)),
    )
    return pl.pallas_call(
        _pack_kernel,
        grid_spec=grid_spec,
        out_shape=jax.ShapeDtypeStruct((dm, Z_WIDTH), BF16),
        compiler_params=_cp(("parallel", "arbitrary"), 32),
        name="pack_w_in",
    )(jnp.asarray(plan), w, w, w, w, w, w)


PACK_T_ROWS = 256


def _pack_t_plan():
    r = PACK_T_ROWS
    splits = (A_Q_W, A_KV_W, A_KV_W, IDX_Q_W, IDX_DIM, IDX_HEADS, 4 * B_W + M_W + 3 * D_MODEL)
    src = dict(zip(("aq", "ak", "av", "iq", "ik", "iw", "rest"), np.concatenate([[0], np.cumsum(splits)[:-1]])))
    assert src["ik"] % r == 0 and src["rest"] == src["ik"] + PACK_SHIFT and splits[-1] % r == 0
    plan = []
    for name in ("aq", "iq", "ak", "av"):
        width = A_Q_W if name in ("aq", "iq") else A_KV_W
        plan += [(0, (src[name] + o) // r, (COLS[name] + o) // r) for o in range(0, width, r)]
    plan += [(2, src["ik"] // r, COLS["ikw"] // r)]
    plan += [(1, src["ik"] // r + 1 + t, COLS["bq"] // r + t) for t in range(splits[-1] // r)]
    plan += [(3, 0, t) for t in range(COLS["ikw"] // r + 1, Z_WIDTH // r)]
    assert sorted(p[2] for p in plan) == list(range(Z_WIDTH // r))
    return np.array(plan, np.int32)


def _pack_t_kernel(plan_ref, a_ref, o_ref, tail_ref):
    mode = plan_ref[pl.program_id(0), 0]
    keep = PACK_T_ROWS - PACK_SHIFT

    @pl.when(mode == 0)
    def _():
        o_ref[...] = a_ref[...].astype(o_ref.dtype)

    @pl.when(mode == 1)
    def _():
        o_ref[:keep, :] = tail_ref[...].astype(o_ref.dtype)
        o_ref[keep:, :] = a_ref[:PACK_SHIFT, :].astype(o_ref.dtype)
        tail_ref[...] = a_ref[PACK_SHIFT:, :]

    @pl.when(mode == 2)
    def _():
        o_ref[...] = jnp.zeros(o_ref.shape, o_ref.dtype)
        o_ref[:IDX_DIM + IDX_HEADS, :] = a_ref[:IDX_DIM + IDX_HEADS, :].astype(o_ref.dtype)
        tail_ref[...] = a_ref[PACK_SHIFT:, :]

    @pl.when(mode == 3)
    def _():
        o_ref[...] = jnp.zeros(o_ref.shape, o_ref.dtype)


def pack_w_in_t(wt):
    dm = wt.shape[1]
    plan = _pack_t_plan()
    n_src = -(-wt.shape[0] // PACK_T_ROWS)
    grid_spec = pltpu.PrefetchScalarGridSpec(
        num_scalar_prefetch=1,
        grid=(len(plan),),
        in_specs=[pl.BlockSpec((PACK_T_ROWS, dm), lambda j, p: (jnp.minimum(p[j, 1], n_src - 1), 0))],
        out_specs=pl.BlockSpec((PACK_T_ROWS, dm), lambda j, p: (p[j, 2], 0)),
        scratch_shapes=[pltpu.VMEM((PACK_T_ROWS - PACK_SHIFT, dm), F32)],
    )
    return pl.pallas_call(
        _pack_t_kernel,
        grid_spec=grid_spec,
        out_shape=jax.ShapeDtypeStruct((Z_WIDTH, dm), BF16),
        compiler_params=_cp(("arbitrary",), 32),
        name="pack_w_in",
    )(jnp.asarray(plan), wt)


def _pack_w_in(w):
    splits = (A_Q_W, A_KV_W, A_KV_W, IDX_Q_W, IDX_DIM, IDX_HEADS, B_W, B_W, B_W, B_W, M_W,
              D_MODEL, D_MODEL, D_MODEL)
    offs = np.concatenate([[0], np.cumsum(splits)])
    aq, ak, av, iq, ik, iw, bq, bf, bi, bg, mq, ga, gb, gm = [w[:, offs[i]:offs[i + 1]] for i in range(len(splits))]
    used = COLS["ikw"] + IDX_DIM + IDX_HEADS
    tail = jnp.zeros((w.shape[0], Z_WIDTH - used), w.dtype)
    return jnp.concatenate([aq, iq, ak, av, bq, bf, bi, bg, mq, ga, gb, gm, ik, iw, tail], axis=1).astype(BF16)


def kernel(x_prompt, x_sample, cache_k, cache_v, cache_idx_k, cache_mem_k, cache_mem_v, state_hgrn, page_table,
           mem_prompt, w_in, w_mem_kv, hgrn_lb, hgrn_norm_g, p_a, p_b, p_m, w_out, ln1_g, ln1_b, w_pq,
           peer_sub_k1, peer_sub_k2, peer_u, peer_v, ln2_g, ln2_b):
    depth = w_in.shape[0]
    assert depth == 1, "single trunk layer"
    bp, t, d = x_prompt.shape
    bs, ts, _ = x_sample.shape
    assert bp == 1 and d == D_MODEL and t % KEY_CHUNK == 0 and bs * ts == Q_BLK and ts <= S_ROWS
    n_p, n_s = bp * t, bs * ts
    n_real = n_p + n_s
    nt = -(-n_real // TOK_TILE) * TOK_TILE
    n_pool = cache_k.shape[1]
    past_len = page_table.shape[1] * PAGE_SIZE
    alpha = (2 * depth) ** 0.25
    nb_p = n_p // Q_BLK

    x = _pad_rows(jnp.concatenate([x_prompt.reshape(n_p, d), x_sample.reshape(n_s, d)], axis=0), nt)
    w_packed_t = pack_w_in_t(jnp.swapaxes(w_in[0], 0, 1))
    z = matmul_nt(x.astype(BF16), w_packed_t, TOK_TILE, Z_TILE_WIDE, name="proj_in")

    def zcols(name, width, lo, hi):
        return z[lo:hi, COLS[name]:COLS[name] + width]

    pos = jnp.concatenate([jnp.arange(t, dtype=I32), past_len + jnp.tile(jnp.arange(ts, dtype=I32), bs),
                           jnp.zeros((nt - n_real,), I32)])
    q_hm, iq_hm, k4, kb, v4, vb, ik_rope, ikb = rope_all(z, COLS, rope_tables(pos))
    nb = q_hm.shape[0]

    oa_p = dsa_prompt(iq_hm.reshape(nb, IDX_HEADS * Q_BLK, IDX_DIM), z, COLS,
                      q_hm.reshape(nb, A_HEADS * Q_BLK, A_HEAD_DIM), ikb[:n_p], kb[:n_p], vb[:n_p], t)

    def per_seq(a, rows):
        return _pad_rows(a.reshape((bs, ts) + a.shape[1:]), rows, axis=1)

    q_s = per_seq(q_hm[nb_p].transpose(1, 0, 2), S_ROWS).transpose(0, 2, 1, 3).reshape(bs, A_HEADS * S_ROWS, A_HEAD_DIM)
    iq_s = per_seq(iq_hm[nb_p].transpose(1, 0, 2), S_ROWS).transpose(0, 2, 1, 3).reshape(bs, IDX_HEADS * S_ROWS, IDX_DIM)
    iw_s = per_seq(z[n_p:n_real, COLS["ikw"] + IDX_DIM:COLS["ikw"] + IDX_DIM + IDX_HEADS], S_ROWS)
    ik_new = per_seq(ikb[n_p:n_real], PAGE_SIZE)
    k_new = per_seq(kb[n_p:n_real], PAGE_SIZE)
    v_new = per_seq(vb[n_p:n_real], PAGE_SIZE)
    keys, thr = dsa_sample_index2(page_table, iq_s, iw_s, ik_new, jnp.swapaxes(cache_idx_k[0], 1, 2), ts)
    cache_rows = n_pool * PAGE_SIZE * A_KV_HEADS
    os_hm = dsa_sample_attend2(page_table, keys, thr, q_s, k_new, v_new,
                               cache_k[0].reshape(cache_rows, A_HEAD_DIM), cache_v[0].reshape(cache_rows, A_HEAD_DIM))
    oa_s = os_hm.reshape(bs, A_HEADS, S_ROWS, A_HEAD_DIM)[:, :, :ts].transpose(0, 2, 1, 3).reshape(n_s, A_Q_W)
    o_a = _pad_rows(jnp.concatenate([oa_p, oa_s], axis=0), nt)

    b_blk = [COLS[c] // B_W for c in ("bq", "bf", "bi", "bg")]
    ng = hgrn_norm_g[0][None]
    ob_p, st_p = hgrn(z, b_blk, hgrn_lb, ng, jnp.zeros((bp, B_HEADS, B_HEAD_DIM, B_HEAD_DIM), F32),
                      bp, t, 64, HGRN_SUB, 64)
    zs_b = per_seq(z[n_p:n_real, COLS["bq"]:COLS["bq"] + 4 * B_W], S_ROWS).reshape(bs * S_ROWS, 4 * B_W)
    ob_s, st_s = hgrn(zs_b, [0, 1, 2, 3], hgrn_lb, ng, state_hgrn[0], bs, S_ROWS, S_ROWS, S_ROWS, ts)
    ob_s = ob_s.reshape(bs, S_ROWS, B_W)[:, :ts].reshape(n_s, B_W)
    o_b = _pad_rows(jnp.concatenate([ob_p[:n_p], ob_s], axis=0), nt)

    mem_kv = matmul(mem_prompt[0].astype(BF16), w_mem_kv[0].astype(BF16), mem_prompt.shape[1], 512, name="mem_kv")
    om_p = mem_attention(z[None], COLS["mq"] // M_W, mem_kv[None], 0, mem_kv[None], 1, bp, t, 512)[0]
    zs_m = per_seq(zcols("mq", M_W, n_p, n_real), S_ROWS)
    n_mem = cache_mem_k.shape[2]
    om_s = mem_attention(zs_m, 0, cache_mem_k[0].reshape(bs, n_mem, M_W), 0,
                         cache_mem_v[0].reshape(bs, n_mem, M_W), 0, bs, S_ROWS, S_ROWS)
    o_m = _pad_rows(jnp.concatenate([om_p, om_s[:, :ts].reshape(n_s, M_W)], axis=0), nt)

    merged = merge(o_a, o_b, o_m, p_a[0].astype(BF16), p_b[0].astype(BF16), p_m[0].astype(BF16), z, COLS,
                   TOK_TILE, WIDE_TILE)
    h_pre = resid_matmul(merged, w_out[0].astype(BF16), x, alpha, TOK_TILE, WIDE_TILE)
    h, hb = layer_norm_dual(h_pre, ln1_g[0][None], ln1_b[0][None], 256)

    pq = matmul(hb, w_pq[0].astype(BF16), TOK_TILE, 512, name="peer_query")
    rank2, cnt1, e1, e2 = peer_route(pq, peer_sub_k1[0], peer_sub_k2[0], 256)
    p_out = peer_dense(hb, peer_u[0].astype(BF16), peer_v[0].astype(BF16), rank2, cnt1, e1, e2, TOK_TILE)
    y = layer_norm_resid(h, p_out, ln2_g[0][None], ln2_b[0][None], alpha, 256)

    kv_p = (depth, bp, t, A_KV_HEADS, A_HEAD_DIM)
    kv_s = (depth, bs, ts, A_KV_HEADS, A_HEAD_DIM)
    mem_shape = (depth, bp, mem_prompt.shape[1], M_HEADS, M_HEAD_DIM)
    g4 = A_KV_HEADS
    return (y[:n_p].reshape(bp, t, d), y[n_p:n_real].reshape(bs, ts, d),
            k4[:n_p * g4].reshape(kv_p), v4[:n_p * g4].reshape(kv_p), ik_rope[:n_p].reshape(depth, bp, t, IDX_DIM),
            mem_kv[:, :M_W].reshape(mem_shape), mem_kv[:, M_W:].reshape(mem_shape), st_p[None],
            k4[n_p * g4:n_real * g4].reshape(kv_s), v4[n_p * g4:n_real * g4].reshape(kv_s),
            ik_rope[n_p:n_real].reshape(depth, bs, ts, IDX_DIM), st_s[None])
```

```python
import functools
import math

import jax
import jax.numpy as jnp
import numpy as np
from jax import lax
from jax.experimental import pallas as pl
from jax.experimental.pallas import tpu as pltpu

F32 = jnp.float32
BF16 = jnp.bfloat16
I32 = jnp.int32

A_HEADS, A_KV_HEADS, A_HEAD_DIM = 16, 4, 128
IDX_HEADS, IDX_DIM = 32, 64
TOPK_MAX = 256
B_HEADS, B_HEAD_DIM = 8, 128
M_HEADS, M_HEAD_DIM = 4, 256
PEER_HEADS, PEER_NKEYS, PEER_DKEY, PEER_TOPK = 8, 128, 256, 16
PAGE_SIZE = 128
ROPE_THETA = 10000.0
LN_EPS = 1e-5
RMS_EPS = 1e-6
D_MODEL = 4096

A_Q_W = A_HEADS * A_HEAD_DIM
A_KV_W = A_KV_HEADS * A_HEAD_DIM
IDX_Q_W = IDX_HEADS * IDX_DIM
B_W = B_HEADS * B_HEAD_DIM
M_W = M_HEADS * M_HEAD_DIM

_SEGS = (("aq", A_Q_W), ("iq", IDX_Q_W), ("ak", A_KV_W), ("av", A_KV_W), ("bq", B_W), ("bf", B_W),
         ("bi", B_W), ("bg", B_W), ("mq", M_W), ("ga", D_MODEL), ("gb", D_MODEL), ("gm", D_MODEL),
         ("ikw", 128))
COLS = {}
_off = 0
for _name, _w in _SEGS:
    COLS[_name] = _off
    _off += _w
Z_TILE = 512
Z_WIDTH = -(-_off // Z_TILE) * Z_TILE
Z_TILE_WIDE = 1536
assert Z_WIDTH % Z_TILE_WIDE == 0

LANES = 128
SUBLANES = 8
Q_BLK = 128
KEY_CHUNK = 512
HGRN_CHUNK = 64
HGRN_SUB = 16
EXP_CLAMP = 80.0
Q_LOG2_SCALE = (A_HEAD_DIM ** -0.5) * math.log2(math.e)
INT_MIN = -2 ** 31
NEG_BIG = -1e30
TOK_TILE = 512
WIDE_TILE = 1024
LN_TILE = 256


def _cp(sem, vmem_mb):
    return pltpu.CompilerParams(dimension_semantics=sem, vmem_limit_bytes=vmem_mb * 2 ** 20)


def _whole_vmem():
    return pl.BlockSpec(memory_space=pltpu.VMEM)


def _pad_rows(a, n, axis=0):
    pad = [(0, 0)] * a.ndim
    pad[axis] = (0, n - a.shape[axis])
    return jnp.pad(a, pad)


def _mm_kernel(a_ref, b_ref, o_ref):
    o_ref[...] = jnp.dot(a_ref[...], b_ref[...], preferred_element_type=F32).astype(o_ref.dtype)


def matmul(a, b, tm, tn, out_dtype=F32, name="matmul"):
    m, k = a.shape
    n = b.shape[1]
    return pl.pallas_call(
        _mm_kernel,
        grid=(m // tm, n // tn),
        in_specs=[pl.BlockSpec((tm, k), lambda i, j: (i, 0)), pl.BlockSpec((k, tn), lambda i, j: (0, j))],
        out_specs=pl.BlockSpec((tm, tn), lambda i, j: (i, j)),
        out_shape=jax.ShapeDtypeStruct((m, n), out_dtype),
        compiler_params=_cp(("parallel", "parallel"), 48),
        name=name,
    )(a, b)


def _mm_nt_kernel(a_ref, bt_ref, o_ref):
    o_ref[...] = lax.dot_general(a_ref[...], bt_ref[...], (((1,), (1,)), ((), ())),
                                 preferred_element_type=F32).astype(o_ref.dtype)


def matmul_nt(a, bt, tm, tn, out_dtype=F32, name="matmul_nt"):
    m, k = a.shape
    n = bt.shape[0]
    return pl.pallas_call(
        _mm_nt_kernel,
        grid=(n // tn, m // tm),
        in_specs=[pl.BlockSpec((tm, k), lambda j, i: (i, 0)), pl.BlockSpec((tn, k), lambda j, i: (j, 0))],
        out_specs=pl.BlockSpec((tm, tn), lambda j, i: (i, j)),
        out_shape=jax.ShapeDtypeStruct((m, n), out_dtype),
        compiler_params=_cp(("parallel", "parallel"), 48),
        name=name,
    )(a, bt)


PACK_SHIFT = IDX_DIM + IDX_HEADS
PACK_T_ROWS = 256


def _pack_t_plan():
    r = PACK_T_ROWS
    splits = (A_Q_W, A_KV_W, A_KV_W, IDX_Q_W, IDX_DIM, IDX_HEADS, 4 * B_W + M_W + 3 * D_MODEL)
    src = dict(zip(("aq", "ak", "av", "iq", "ik", "iw", "rest"), np.concatenate([[0], np.cumsum(splits)[:-1]])))
    assert src["ik"] % r == 0 and src["rest"] == src["ik"] + PACK_SHIFT and splits[-1] % r == 0
    plan = []
    for name in ("aq", "iq", "ak", "av"):
        width = A_Q_W if name in ("aq", "iq") else A_KV_W
        plan += [(0, (src[name] + o) // r, (COLS[name] + o) // r) for o in range(0, width, r)]
    plan += [(2, src["ik"] // r, COLS["ikw"] // r)]
    plan += [(1, src["ik"] // r + 1 + t, COLS["bq"] // r + t) for t in range(splits[-1] // r)]
    plan += [(3, 0, t) for t in range(COLS["ikw"] // r + 1, Z_WIDTH // r)]
    assert sorted(p[2] for p in plan) == list(range(Z_WIDTH // r))
    return np.array(plan, np.int32)


def _pack_t_kernel(plan_ref, a_ref, o_ref, tail_ref):
    mode = plan_ref[pl.program_id(0), 0]
    keep = PACK_T_ROWS - PACK_SHIFT

    @pl.when(mode == 0)
    def _():
        o_ref[...] = a_ref[...].astype(o_ref.dtype)

    @pl.when(mode == 1)
    def _():
        o_ref[:keep, :] = tail_ref[...].astype(o_ref.dtype)
        o_ref[keep:, :] = a_ref[:PACK_SHIFT, :].astype(o_ref.dtype)
        tail_ref[...] = a_ref[PACK_SHIFT:, :]

    @pl.when(mode == 2)
    def _():
        o_ref[...] = jnp.zeros(o_ref.shape, o_ref.dtype)
        o_ref[:PACK_SHIFT, :] = a_ref[:PACK_SHIFT, :].astype(o_ref.dtype)
        tail_ref[...] = a_ref[PACK_SHIFT:, :]

    @pl.when(mode == 3)
    def _():
        o_ref[...] = jnp.zeros(o_ref.shape, o_ref.dtype)


def pack_w_in_t(wt):
    dm = wt.shape[1]
    plan = _pack_t_plan()
    n_src = -(-wt.shape[0] // PACK_T_ROWS)
    grid_spec = pltpu.PrefetchScalarGridSpec(
        num_scalar_prefetch=1,
        grid=(len(plan),),
        in_specs=[pl.BlockSpec((PACK_T_ROWS, dm), lambda j, p: (jnp.minimum(p[j, 1], n_src - 1), 0))],
        out_specs=pl.BlockSpec((PACK_T_ROWS, dm), lambda j, p: (p[j, 2], 0)),
        scratch_shapes=[pltpu.VMEM((PACK_T_ROWS - PACK_SHIFT, dm), F32)],
    )
    return pl.pallas_call(
        _pack_t_kernel,
        grid_spec=grid_spec,
        out_shape=jax.ShapeDtypeStruct((Z_WIDTH, dm), BF16),
        compiler_params=_cp(("arbitrary",), 32),
        name="pack_w_in",
    )(jnp.asarray(plan), wt)


def _rope_kernel(q_ref, iq_ref, k_ref, v_ref, ikw_ref, c128_ref, s128_ref, c64_ref, s64_ref,
                 qhm_ref, iqhm_ref, k4_ref, kb_ref, v4_ref, vb_ref, iko_ref, ikb_ref):
    c128, s128 = c128_ref[...], s128_ref[...]
    c64, s64 = c64_ref[...], s64_ref[...]
    lane = lax.broadcasted_iota(I32, (Q_BLK, LANES), 1)
    first_half = (lane % IDX_DIM) < (IDX_DIM // 2)

    def rope128(x):
        return x * c128 + pltpu.roll(x, A_HEAD_DIM // 2, axis=1) * s128

    def rope64(x):
        partner = jnp.where(first_half, pltpu.roll(x, LANES - IDX_DIM // 2, axis=1),
                            pltpu.roll(x, IDX_DIM // 2, axis=1))
        return x * c64 + partner * s64

    for h in range(A_HEADS):
        qhm_ref[0, h] = (rope128(q_ref[:, h * LANES:(h + 1) * LANES]) * Q_LOG2_SCALE).astype(qhm_ref.dtype)
    for h in range(A_KV_HEADS):
        rows = pl.ds(h, Q_BLK, stride=A_KV_HEADS)
        kr = rope128(k_ref[:, h * LANES:(h + 1) * LANES])
        k4_ref[rows, :] = kr
        kb_ref[:, h * LANES:(h + 1) * LANES] = kr.astype(kb_ref.dtype)
        vh = v_ref[:, h * LANES:(h + 1) * LANES]
        v4_ref[rows, :] = vh
        vb_ref[:, h * LANES:(h + 1) * LANES] = vh.astype(vb_ref.dtype)
    for p in range(IDX_HEADS // 2):
        r = rope64(iq_ref[:, p * LANES:(p + 1) * LANES]).astype(iqhm_ref.dtype)
        iqhm_ref[0, 2 * p] = r[:, :IDX_DIM]
        iqhm_ref[0, 2 * p + 1] = r[:, IDX_DIM:]
    ikr = rope64(ikw_ref[...])[:, :IDX_DIM]
    iko_ref[...] = ikr
    ikb_ref[...] = ikr.astype(ikb_ref.dtype)


def rope_all(z, col, tabs):
    nt = z.shape[0]
    nb = nt // Q_BLK
    c128, s128, c64, s64 = tabs
    tab_spec = pl.BlockSpec((Q_BLK, LANES), lambda i: (i, 0))
    kv4_spec = pl.BlockSpec((Q_BLK * A_KV_HEADS, A_HEAD_DIM), lambda i: (i, 0))
    kvb_spec = pl.BlockSpec((Q_BLK, A_KV_W), lambda i: (i, 0))
    ik_spec = pl.BlockSpec((Q_BLK, IDX_DIM), lambda i: (i, 0))
    kv4 = jax.ShapeDtypeStruct((nt * A_KV_HEADS, A_HEAD_DIM), F32)
    kvb = jax.ShapeDtypeStruct((nt, A_KV_W), BF16)
    return pl.pallas_call(
        _rope_kernel,
        grid=(nb,),
        in_specs=[pl.BlockSpec((Q_BLK, A_Q_W), lambda i: (i, col["aq"] // A_Q_W)),
                  pl.BlockSpec((Q_BLK, IDX_Q_W), lambda i: (i, col["iq"] // IDX_Q_W)),
                  pl.BlockSpec((Q_BLK, A_KV_W), lambda i: (i, col["ak"] // A_KV_W)),
                  pl.BlockSpec((Q_BLK, A_KV_W), lambda i: (i, col["av"] // A_KV_W)),
                  pl.BlockSpec((Q_BLK, LANES), lambda i: (i, col["ikw"] // LANES)),
                  tab_spec, tab_spec, tab_spec, tab_spec],
        out_specs=[pl.BlockSpec((1, A_HEADS, Q_BLK, A_HEAD_DIM), lambda i: (i, 0, 0, 0)),
                   pl.BlockSpec((1, IDX_HEADS, Q_BLK, IDX_DIM), lambda i: (i, 0, 0, 0)),
                   kv4_spec, kvb_spec, kv4_spec, kvb_spec, ik_spec, ik_spec],
        out_shape=[jax.ShapeDtypeStruct((nb, A_HEADS, Q_BLK, A_HEAD_DIM), BF16),
                   jax.ShapeDtypeStruct((nb, IDX_HEADS, Q_BLK, IDX_DIM), BF16),
                   kv4, kvb, kv4, kvb,
                   jax.ShapeDtypeStruct((nt, IDX_DIM), F32), jax.ShapeDtypeStruct((nt, IDX_DIM), BF16)],
        compiler_params=_cp(("parallel",), 32),
        name="rope",
    )(z, z, z, z, z, c128, s128, c64, s64)


def rope_tables(pos):
    def tab(half, reps):
        inv_freq = ROPE_THETA ** (-jnp.arange(half, dtype=F32) / half)
        ang = pos.astype(F32)[:, None] * inv_freq[None, :]
        c, s = jnp.cos(ang), jnp.sin(ang)
        return jnp.tile(jnp.concatenate([c, c], 1), (1, reps)), jnp.tile(jnp.concatenate([-s, s], 1), (1, reps))
    c128, s128 = tab(A_HEAD_DIM // 2, 1)
    c64, s64 = tab(IDX_DIM // 2, 2)
    return c128, s128, c64, s64


def _sort_key(x):
    i = pltpu.bitcast(x, I32)
    return i ^ ((i >> 31) & 0x7FFFFFFF)


WORD = 32
GROUP_CHUNKS = WORD * LANES // KEY_CHUNK
_BIT_MASKS = ((16, 0x0000FFFF), (8, 0x00FF00FF), (4, 0x0F0F0F0F), (2, 0x33333333), (1, 0x55555555))


def _transpose_bits(w):
    w = list(w)
    for d, m in _BIT_MASKS:
        mask = jnp.int32(m)
        for j in range(WORD):
            if j & d:
                continue
            lo, hi = w[j], w[j + d]
            t = (lax.shift_right_logical(lo, jnp.int32(d)) ^ hi) & mask
            w[j + d] = hi ^ t
            w[j] = lo ^ lax.shift_left(t, jnp.int32(d))
    return w


def _kth_largest_bitsliced(key_ref, plane_ref, cand_ref, nch, k):
    rows = key_ref.shape[1]
    tiles_per_chunk = KEY_CHUNK // LANES
    ngroups = (nch + GROUP_CHUNKS - 1) // GROUP_CHUNKS

    def build_group(g, carry):
        def build_slab(s, c2):
            r0 = pl.multiple_of(s * SUBLANES, SUBLANES)
            words = []
            for j in range(WORD):
                c = g * GROUP_CHUNKS + j // tiles_per_chunk
                lt = j % tiles_per_chunk
                cc = jnp.minimum(c, key_ref.shape[0] - 1)
                w = key_ref[cc, pl.ds(r0, SUBLANES), lt * LANES:(lt + 1) * LANES] ^ INT_MIN
                words.append(jnp.where(c < nch, w, 0))
            planes = _transpose_bits(words)
            for b in range(WORD):
                plane_ref[g, b, pl.ds(r0, SUBLANES), :] = planes[b]
            return c2
        lax.fori_loop(0, rows // SUBLANES, build_slab, 0)
        cand_ref[g] = jnp.full((rows, LANES), -1, I32)
        return carry

    lax.fori_loop(0, ngroups, build_group, 0)

    def bit_step(it, carry):
        t_u, k_rem = carry
        b = WORD - 1 - it

        def count(g, cnt):
            return cnt + lax.population_count(cand_ref[g] & plane_ref[g, b])
        cnt = lax.fori_loop(0, ngroups, count, jnp.zeros((rows, LANES), I32))
        c1 = jnp.sum(cnt, axis=1, keepdims=True)
        take = c1 >= k_rem
        take_b = jnp.broadcast_to(take, (rows, LANES))

        def update(g, c2):
            e = cand_ref[g]
            a = e & plane_ref[g, b]
            cand_ref[g] = jnp.where(take_b, a, e ^ a)
            return c2
        lax.fori_loop(0, ngroups, update, 0)
        t_u = jnp.where(take, t_u | jnp.left_shift(jnp.int32(1), b), t_u)
        return t_u, jnp.where(take, k_rem, k_rem - c1)

    t_u, k_rem = lax.fori_loop(0, WORD, bit_step, (jnp.zeros((rows, 1), I32), jnp.full((rows, 1), k, I32)))
    return t_u ^ INT_MIN, k_rem


def _lower_surplus_ties(key_ref, cand_ref, nch, t, n_wanted):
    rows = key_ref.shape[1]
    ngroups = (nch + GROUP_CHUNKS - 1) // GROUP_CHUNKS
    group_keys = WORD * LANES
    pos_bits = (key_ref.shape[0] * KEY_CHUNK - 1).bit_length()
    live = t > INT_MIN

    def count_tied(g, cnt):
        return cnt + lax.population_count(cand_ref[g])
    n_tied = jnp.sum(lax.fori_loop(0, ngroups, count_tied, jnp.zeros((rows, LANES), I32)), axis=1, keepdims=True)
    surplus = jnp.where(live, n_tied - n_wanted, 0)

    @pl.when(jnp.max(surplus) > 0)
    def _():
        lane = lax.broadcasted_iota(I32, (rows, LANES), 1)

        def tied_before(q):
            gq = q // group_keys
            jq = jnp.broadcast_to((q // LANES) % WORD, (rows, LANES))
            lq = q % LANES
            low = lax.shift_left(jnp.ones((rows, LANES), I32), jq) - 1

            def body(g, cnt):
                w = cand_ref[g]
                at_tile = jnp.where(lane < lq, lax.shift_right_logical(w, jq) & 1, 0)
                part = lax.population_count(w & low) + at_tile
                return cnt + jnp.where(g < gq, lax.population_count(w), jnp.where(g == gq, part, 0))
            cnt = lax.fori_loop(0, ngroups, body, jnp.zeros((rows, LANES), I32))
            return jnp.sum(cnt, axis=1, keepdims=True)

        def bit_step(it, q):
            cand = q | jnp.left_shift(jnp.int32(1), pos_bits - 1 - it)
            return jnp.where(tied_before(cand) < n_wanted, cand, q)
        last = lax.fori_loop(0, pos_bits, bit_step, jnp.zeros((rows, 1), I32))
        off = lax.broadcasted_iota(I32, (rows, KEY_CHUNK), 1)

        def rewrite(c, carry):
            keys = key_ref[c]
            drop = jnp.where(keys == t, 1, 0) * jnp.where(c * KEY_CHUNK + off > last, 1, 0) * jnp.where(live, 1, 0)
            key_ref[c] = keys - drop
            return carry
        lax.fori_loop(0, nch, rewrite, 0)


def _flash_step(s, mask, vext, m_ref, l_ref, acc_ref, g):
    hh, rr, ss = s.shape
    d = vext.shape[1] // 2
    s = jnp.where(mask[None], s, -jnp.inf).reshape(hh * rr, ss)
    m_old = m_ref[g]
    m_new = jnp.maximum(m_old, jnp.max(s, axis=-1, keepdims=True))
    p = jnp.exp2(s - jnp.concatenate([m_new] * (ss // LANES), axis=1))
    alpha = jnp.exp2(m_old - m_new)
    pv = jnp.dot(p.astype(BF16), vext, preferred_element_type=F32)
    acc_ref[g] = alpha * acc_ref[g] + pv[:, :d]
    l_ref[g] = alpha * l_ref[g] + pv[:, d:]
    m_ref[g] = m_new


IDX_HEADS_PER_DOT = 8


def _dsa_prompt_kernel(iq_ref, w_ref, q_ref, ik_ref, k_ref, v_ref, o_ref,
                       key_ref, plane_ref, cand_ref, wb_ref, m_ref, l_ref, acc_ref, *, topk, w_lane0):
    i = pl.program_id(0)
    nch = (i * Q_BLK + Q_BLK + KEY_CHUNK - 1) // KEY_CHUNK
    hpg = A_HEADS // A_KV_HEADS

    wt = w_ref[...] * ((IDX_DIM ** -0.5) * (IDX_HEADS ** -0.5))
    for h in range(IDX_HEADS):
        wb_ref[h] = jnp.broadcast_to(wt[:, w_lane0 + h:w_lane0 + h + 1], (Q_BLK, LANES))

    q_pos = i * Q_BLK + lax.broadcasted_iota(I32, (Q_BLK, KEY_CHUNK), 0)
    k_off = lax.broadcasted_iota(I32, (Q_BLK, KEY_CHUNK), 1)

    def score_chunk(c, carry):
        k0 = pl.multiple_of(c * KEY_CHUNK, KEY_CHUNK)
        ikc = ik_ref[pl.ds(k0, KEY_CHUNK), :]
        ntile = KEY_CHUNK // LANES
        acc = [jnp.zeros((Q_BLK, LANES), F32)] * ntile
        for hg in range(IDX_HEADS // IDX_HEADS_PER_DOT):
            lhs = iq_ref[0, hg * IDX_HEADS_PER_DOT * Q_BLK:(hg + 1) * IDX_HEADS_PER_DOT * Q_BLK, :]
            d = lax.dot_general(lhs, ikc, (((1,), (1,)), ((), ())), preferred_element_type=F32)
            for hl in range(IDX_HEADS_PER_DOT):
                wb = wb_ref[hg * IDX_HEADS_PER_DOT + hl]
                r = jnp.maximum(d[hl * Q_BLK:(hl + 1) * Q_BLK], 0.0)
                acc = [acc[j] + wb * r[:, j * LANES:(j + 1) * LANES] for j in range(ntile)]
        sc = jnp.concatenate(acc, axis=1)
        key_ref[c] = jnp.where(k0 + k_off <= q_pos, _sort_key(sc), INT_MIN)
        return carry

    lax.fori_loop(0, nch, score_chunk, 0)

    t, n_wanted = _kth_largest_bitsliced(key_ref, plane_ref, cand_ref, nch, topk)
    _lower_surplus_ties(key_ref, cand_ref, nch, t, n_wanted)
    thr = jnp.maximum(t, INT_MIN + 1)

    m_ref[...] = jnp.full(m_ref.shape, NEG_BIG, F32)
    l_ref[...] = jnp.zeros(l_ref.shape, F32)
    acc_ref[...] = jnp.zeros(acc_ref.shape, F32)
    ones = jnp.ones((KEY_CHUNK, A_HEAD_DIM), BF16)

    def attend_chunk(c, carry):
        k0 = pl.multiple_of(c * KEY_CHUNK, KEY_CHUNK)
        mask = key_ref[c] >= thr
        for g in range(A_KV_HEADS):
            qg = q_ref[0, g * hpg * Q_BLK:(g + 1) * hpg * Q_BLK, :]
            kc = k_ref[pl.ds(k0, KEY_CHUNK), g * LANES:(g + 1) * LANES]
            vc = v_ref[pl.ds(k0, KEY_CHUNK), g * LANES:(g + 1) * LANES]
            s = lax.dot_general(qg, kc, (((1,), (1,)), ((), ())), preferred_element_type=F32)
            _flash_step(s.reshape(hpg, Q_BLK, KEY_CHUNK), mask, jnp.concatenate([vc, ones], axis=1),
                        m_ref, l_ref, acc_ref, g)
        return carry

    lax.fori_loop(0, nch, attend_chunk, 0)
    for g in range(A_KV_HEADS):
        o = (acc_ref[g] / l_ref[g]).astype(o_ref.dtype)
        for hl in range(hpg):
            h = g * hpg + hl
            o_ref[:, h * A_HEAD_DIM:(h + 1) * A_HEAD_DIM] = o[hl * Q_BLK:(hl + 1) * Q_BLK]


def dsa_prompt(iq_hm, z, col, q_hm, ik, k, v, t):
    nb = t // Q_BLK
    topk = min(TOPK_MAX, t // 4)
    nch_max = (t + KEY_CHUNK - 1) // KEY_CHUNK
    ngroups_max = (nch_max + GROUP_CHUNKS - 1) // GROUP_CHUNKS
    hpg = A_HEADS // A_KV_HEADS
    kern = functools.partial(_dsa_prompt_kernel, topk=topk, w_lane0=IDX_DIM)
    return pl.pallas_call(
        kern,
        grid=(nb,),
        in_specs=[pl.BlockSpec((1, IDX_HEADS * Q_BLK, IDX_DIM), lambda i: (i, 0, 0)),
                  pl.BlockSpec((Q_BLK, LANES), lambda i: (i, col["ikw"] // LANES)),
                  pl.BlockSpec((1, A_HEADS * Q_BLK, A_HEAD_DIM), lambda i: (i, 0, 0)),
                  _whole_vmem(), _whole_vmem(), _whole_vmem()],
        out_specs=pl.BlockSpec((Q_BLK, A_Q_W), lambda i: (i, 0)),
        out_shape=jax.ShapeDtypeStruct((t, A_Q_W), BF16),
        scratch_shapes=[pltpu.VMEM((nch_max, Q_BLK, KEY_CHUNK), I32),
                        pltpu.VMEM((ngroups_max, WORD, Q_BLK, LANES), I32),
                        pltpu.VMEM((ngroups_max, Q_BLK, LANES), I32),
                        pltpu.VMEM((IDX_HEADS, Q_BLK, LANES), F32),
                        pltpu.VMEM((A_KV_HEADS, hpg * Q_BLK, LANES), F32),
                        pltpu.VMEM((A_KV_HEADS, hpg * Q_BLK, LANES), F32),
                        pltpu.VMEM((A_KV_HEADS, hpg * Q_BLK, A_HEAD_DIM), F32)],
        compiler_params=_cp(("arbitrary",), 56),
        name="dsa_prompt",
    )(iq_hm, z, q_hm, ik, k, v)


S_ROWS = SUBLANES
ATT_PAGES = 16


def _dsa_sample_scores_kernel(pt_ref, iq_ref, w_ref, ikn_ref, *rest, n_pages, t_new):
    pages = rest[:n_pages]
    key_ref = rest[n_pages]
    wt = w_ref[0] * ((IDX_DIM ** -0.5) * (IDX_HEADS ** -0.5))
    wb = [jnp.broadcast_to(wt[:, h:h + 1], (S_ROWS, LANES)) for h in range(IDX_HEADS)]
    iq = iq_ref[0]

    def tile_scores(ik_tile, transposed):
        dims = (((1,), (0,)), ((), ())) if transposed else (((1,), (1,)), ((), ()))
        d = lax.dot_general(iq, ik_tile, dims, preferred_element_type=F32)
        acc = jnp.zeros((S_ROWS, LANES), F32)
        for h in range(IDX_HEADS):
            acc = acc + wb[h] * jnp.maximum(d[h * S_ROWS:(h + 1) * S_ROWS], 0.0)
        return acc

    for r in range(n_pages):
        key_ref[0, r] = _sort_key(tile_scores(pages[r][0].astype(BF16), True))
    row = lax.broadcasted_iota(I32, (S_ROWS, LANES), 0)
    lane = lax.broadcasted_iota(I32, (S_ROWS, LANES), 1)
    valid = jnp.where(lane < t_new, lane, S_ROWS) <= row
    key_ref[0, n_pages] = jnp.where(valid, _sort_key(tile_scores(ikn_ref[0], False)), INT_MIN)


def _dsa_sample_thr_kernel(key_ref, adj_ref, thr_ref, last_ref, *, topk):
    b, n_tiles = key_ref.shape[0], key_ref.shape[1]
    pos_bits = (n_tiles * LANES - 1).bit_length()
    lane = lax.broadcasted_iota(I32, (b, S_ROWS, LANES), 2)

    def count(pred):
        def body(c, cnt):
            return cnt + jnp.where(pred(key_ref[:, c], c), 1, 0)
        cnt = lax.fori_loop(0, n_tiles, body, jnp.zeros((b, S_ROWS, LANES), I32))
        return jnp.sum(cnt, axis=2, keepdims=True)

    def bit_step(it, cur):
        cand = cur | jnp.left_shift(jnp.int32(1), 31 - it)
        return jnp.where(count(lambda x, c: x >= (cand ^ INT_MIN)) >= topk, cand, cur)

    t = lax.fori_loop(0, 32, bit_step, jnp.zeros((b, S_ROWS, 1), I32)) ^ INT_MIN
    live = t > INT_MIN
    n_above = count(lambda x, c: x > t)
    n_tied = count(lambda x, c: x == t)
    n_wanted = topk - n_above
    last_ref[...] = jnp.full(last_ref.shape, n_tiles * LANES, I32)

    @pl.when(jnp.max(jnp.where(live, n_tied - n_wanted, 0)) > 0)
    def _():
        def pos_step(it, q):
            cand = q | jnp.left_shift(jnp.int32(1), pos_bits - 1 - it)
            before = count(lambda x, c: jnp.where(x == t, 1, 0) * jnp.where(c * LANES + lane < cand, 1, 0) > 0)
            return jnp.where(before < n_wanted, cand, q)
        last = lax.fori_loop(0, pos_bits, pos_step, jnp.zeros((b, S_ROWS, 1), I32))
        last_ref[...] = jnp.broadcast_to(last, last_ref.shape)

    last = last_ref[...]
    live_i = jnp.where(live, 1, 0)

    def rewrite(c, carry):
        keys = key_ref[:, c]
        drop = jnp.where(keys == t, 1, 0) * jnp.where(c * LANES + lane > last, 1, 0) * live_i
        adj_ref[:, c] = keys - drop
        return carry
    lax.fori_loop(0, n_tiles, rewrite, 0)
    thr_ref[...] = jnp.broadcast_to(jnp.maximum(t, INT_MIN + 1), thr_ref.shape)


def dsa_sample_index(page_table, iq_s, iw_s, ik_new, cache_ik, t_new):
    b, n_pages = page_table.shape
    topk = min(TOPK_MAX, (n_pages * PAGE_SIZE + t_new) // 4)

    def page_spec(r):
        return pl.BlockSpec((1, IDX_DIM, PAGE_SIZE), lambda bi, pt: (pt[bi, r], 0, 0))

    grid_spec = pltpu.PrefetchScalarGridSpec(
        num_scalar_prefetch=1,
        grid=(b,),
        in_specs=[pl.BlockSpec((1, IDX_HEADS * S_ROWS, IDX_DIM), lambda bi, pt: (bi, 0, 0)),
                  pl.BlockSpec((1, S_ROWS, IDX_HEADS), lambda bi, pt: (bi, 0, 0)),
                  pl.BlockSpec((1, PAGE_SIZE, IDX_DIM), lambda bi, pt: (bi, 0, 0))]
                 + [page_spec(r) for r in range(n_pages)],
        out_specs=pl.BlockSpec((1, n_pages + 1, S_ROWS, LANES), lambda bi, pt: (bi, 0, 0, 0)),
    )
    keys = pl.pallas_call(
        functools.partial(_dsa_sample_scores_kernel, n_pages=n_pages, t_new=t_new),
        grid_spec=grid_spec,
        out_shape=jax.ShapeDtypeStruct((b, n_pages + 1, S_ROWS, LANES), I32),
        compiler_params=_cp(("parallel",), 32),
        name="dsa_sample_scores",
    )(page_table, iq_s, iw_s, ik_new, *([cache_ik] * n_pages))
    return pl.pallas_call(
        functools.partial(_dsa_sample_thr_kernel, topk=topk),
        in_specs=[_whole_vmem()],
        out_specs=[_whole_vmem(), _whole_vmem()],
        out_shape=[jax.ShapeDtypeStruct(keys.shape, I32), jax.ShapeDtypeStruct((b, S_ROWS, LANES), I32)],
        scratch_shapes=[pltpu.VMEM((b, S_ROWS, LANES), I32)],
        compiler_params=pltpu.CompilerParams(vmem_limit_bytes=32 * 2 ** 20),
        name="dsa_sample_threshold",
    )(keys)


def _dsa_sample_attend_kernel(pt_ref, key_ref, thr_ref, q_ref, kn_ref, vn_ref, *rest, n_pages):
    kpages = rest[:ATT_PAGES]
    vpages = rest[ATT_PAGES:2 * ATT_PAGES]
    o_ref, m_ref, l_ref, acc_ref = rest[2 * ATT_PAGES:]
    j = pl.program_id(1)
    hpg = A_HEADS // A_KV_HEADS
    thr = thr_ref[0]

    @pl.when(j == 0)
    def _():
        m_ref[...] = jnp.full(m_ref.shape, NEG_BIG, F32)
        l_ref[...] = jnp.zeros(l_ref.shape, F32)
        acc_ref[...] = jnp.zeros(acc_ref.shape, F32)

    def attend(mask, g, kg, vg):
        qg = q_ref[0, g * hpg * S_ROWS:(g + 1) * hpg * S_ROWS, :]
        s = lax.dot_general(qg, kg, (((1,), (1,)), ((), ())), preferred_element_type=F32)
        vext = jnp.concatenate([vg, jnp.ones(vg.shape, BF16)], axis=1)
        _flash_step(s.reshape(hpg, S_ROWS, s.shape[-1]), mask, vext, m_ref, l_ref, acc_ref, g)

    mask = jnp.concatenate([key_ref[0, j * ATT_PAGES + r] >= thr for r in range(ATT_PAGES)], axis=1)
    for g in range(A_KV_HEADS):
        rows = pl.ds(g, PAGE_SIZE, stride=A_KV_HEADS)
        kg = jnp.concatenate([kp[rows, :].astype(BF16) for kp in kpages], axis=0)
        vg = jnp.concatenate([vp[rows, :].astype(BF16) for vp in vpages], axis=0)
        attend(mask, g, kg, vg)

    @pl.when(j == pl.num_programs(1) - 1)
    def _():
        mask_new = key_ref[0, n_pages] >= thr
        for g in range(A_KV_HEADS):
            attend(mask_new, g, kn_ref[0, :, g * LANES:(g + 1) * LANES], vn_ref[0, :, g * LANES:(g + 1) * LANES])
        for g in range(A_KV_HEADS):
            o_ref[0, g * hpg * S_ROWS:(g + 1) * hpg * S_ROWS, :] = (acc_ref[g] / l_ref[g]).astype(o_ref.dtype)


def dsa_sample_attend(page_table, keys, thr, q_s, k_new, v_new, cache_k, cache_v):
    b, n_pages = page_table.shape
    steps = n_pages // ATT_PAGES
    hpg = A_HEADS // A_KV_HEADS
    page_rows = PAGE_SIZE * A_KV_HEADS

    def page_spec(r):
        return pl.BlockSpec((page_rows, A_HEAD_DIM), lambda bi, j, pt: (pt[bi, j * ATT_PAGES + r], 0))

    grid_spec = pltpu.PrefetchScalarGridSpec(
        num_scalar_prefetch=1,
        grid=(b, steps),
        in_specs=[pl.BlockSpec((1, n_pages + 1, S_ROWS, LANES), lambda bi, j, pt: (bi, 0, 0, 0)),
                  pl.BlockSpec((1, S_ROWS, LANES), lambda bi, j, pt: (bi, 0, 0)),
                  pl.BlockSpec((1, A_HEADS * S_ROWS, A_HEAD_DIM), lambda bi, j, pt: (bi, 0, 0)),
                  pl.BlockSpec((1, PAGE_SIZE, A_KV_W), lambda bi, j, pt: (bi, 0, 0)),
                  pl.BlockSpec((1, PAGE_SIZE, A_KV_W), lambda bi, j, pt: (bi, 0, 0))]
                 + [page_spec(r) for r in range(ATT_PAGES)] * 2,
        out_specs=pl.BlockSpec((1, A_HEADS * S_ROWS, A_HEAD_DIM), lambda bi, j, pt: (bi, 0, 0)),
        scratch_shapes=[pltpu.VMEM((A_KV_HEADS, hpg * S_ROWS, LANES), F32),
                        pltpu.VMEM((A_KV_HEADS, hpg * S_ROWS, LANES), F32),
                        pltpu.VMEM((A_KV_HEADS, hpg * S_ROWS, A_HEAD_DIM), F32)],
    )
    return pl.pallas_call(
        functools.partial(_dsa_sample_attend_kernel, n_pages=n_pages),
        grid_spec=grid_spec,
        out_shape=jax.ShapeDtypeStruct((b, A_HEADS * S_ROWS, A_HEAD_DIM), BF16),
        compiler_params=_cp(("parallel", "arbitrary"), 48),
        name="dsa_sample_attend",
    )(page_table, keys, thr, q_s, k_new, v_new, *([cache_k] * ATT_PAGES), *([cache_v] * ATT_PAGES))


def _cumsum_rows(x):
    n = x.shape[0]
    row = lax.broadcasted_iota(I32, x.shape, 0)
    d = 1
    while d < n:
        x = x + jnp.where(row >= d, pltpu.roll(x, d, axis=0), 0.0)
        d *= 2
    return x


def _hgrn_kernel(q_ref, f_ref, i_ref, g_ref, lb_ref, ng_ref, s0_ref, o_ref, so_ref, st_ref,
                 *, chunk, sub, t_valid):
    n = pl.program_id(1)
    nsub = chunk // sub

    @pl.when(n == 0)
    def _():
        for h in range(B_HEADS):
            st_ref[h] = s0_ref[0, h].T

    lbx = lb_ref[...]
    lbe = jnp.exp(lbx - jnp.max(lbx, axis=0, keepdims=True))
    lb_all = lbe[0:1] / jnp.sum(lbe, axis=0, keepdims=True)
    row = lax.broadcasted_iota(I32, (chunk, B_HEAD_DIM), 0)
    valid = row < t_valid
    tt = lax.broadcasted_iota(I32, (chunk, nsub * chunk), 0)
    cc = lax.broadcasted_iota(I32, (chunk, nsub * chunk), 1)
    pair_ok = ((cc // chunk) == (tt // sub)) & ((cc % chunk) <= tt)
    ng = ng_ref[...]

    for h in range(B_HEADS):
        sl = slice(h * B_HEAD_DIM, (h + 1) * B_HEAD_DIM)
        lb = lb_all[:, sl]
        f = lb + (1.0 - lb) * jax.nn.sigmoid(f_ref[:, sl])
        logf = jnp.where(valid, jnp.log(f), 0.0)
        kk = jnp.where(valid, 1.0 - f, 0.0)
        q = q_ref[:, sl]
        iv = i_ref[:, sl]
        cum = _cumsum_rows(logf)
        last = cum[chunk - 1:chunk]
        st = st_ref[h]

        o = lax.dot_general((q * jnp.exp(cum)).astype(BF16), st.astype(BF16),
                            (((1,), (1,)), ((), ())), preferred_element_type=F32)

        refs = [jnp.zeros((1, B_HEAD_DIM), F32)] + [cum[s * sub - 1:s * sub] for s in range(1, nsub)]
        ref_row = refs[0]
        for s in range(1, nsub):
            ref_row = jnp.where(row >= s * sub, refs[s], ref_row)
        qt = (q * jnp.exp(cum - ref_row)).astype(BF16)
        kcat = jnp.concatenate(
            [(kk * jnp.exp(jnp.minimum(r - cum, EXP_CLAMP))).astype(BF16) for r in refs], axis=0)
        a = lax.dot_general(qt, kcat, (((1,), (1,)), ((), ())), preferred_element_type=F32)
        p = jnp.where(pair_ok, a, 0.0).astype(BF16)
        icat = jnp.concatenate([iv.astype(BF16)] * nsub, axis=0)
        o = o + jnp.dot(p, icat, preferred_element_type=F32)

        kd = (kk * jnp.exp(last - cum)).astype(BF16)
        upd = lax.dot_general(iv.astype(BF16), kd, (((0,), (0,)), ((), ())), preferred_element_type=F32)
        st_ref[h] = st * jnp.exp(last) + upd

        on = o * lax.rsqrt(jnp.mean(o * o, axis=-1, keepdims=True) + RMS_EPS) * ng
        gate = g_ref[:, sl]
        o_ref[:, sl] = (on * (gate * jax.nn.sigmoid(gate))).astype(o_ref.dtype)

    @pl.when(n == pl.num_programs(1) - 1)
    def _():
        for h in range(B_HEADS):
            so_ref[0, h] = st_ref[h].T


def hgrn(zsrc, colblk, lb, ng, s0, batch, t_pad, chunk, sub, t_valid):
    nch = t_pad // chunk
    kern = functools.partial(_hgrn_kernel, chunk=chunk, sub=sub, t_valid=t_valid)

    def zspec(cb):
        return pl.BlockSpec((chunk, B_W), lambda b, n: (b * nch + n, cb))

    return pl.pallas_call(
        kern,
        grid=(batch, nch),
        in_specs=[zspec(colblk[0]), zspec(colblk[1]), zspec(colblk[2]), zspec(colblk[3]),
                  pl.BlockSpec(lb.shape, lambda b, n: (0, 0)),
                  pl.BlockSpec((1, B_HEAD_DIM), lambda b, n: (0, 0)),
                  pl.BlockSpec((1, B_HEADS, B_HEAD_DIM, B_HEAD_DIM), lambda b, n: (b, 0, 0, 0))],
        out_specs=[pl.BlockSpec((chunk, B_W), lambda b, n: (b * nch + n, 0)),
                   pl.BlockSpec((1, B_HEADS, B_HEAD_DIM, B_HEAD_DIM), lambda b, n: (b, 0, 0, 0))],
        out_shape=[jax.ShapeDtypeStruct((batch * t_pad, B_W), BF16),
                   jax.ShapeDtypeStruct((batch, B_HEADS, B_HEAD_DIM, B_HEAD_DIM), F32)],
        scratch_shapes=[pltpu.VMEM((B_HEADS, B_HEAD_DIM, B_HEAD_DIM), F32)],
        compiler_params=_cp(("parallel", "arbitrary"), 32),
        name="hgrn",
    )(zsrc, zsrc, zsrc, zsrc, lb, ng, s0)


def _mem_attn_kernel(q_ref, mk_ref, mv_ref, o_ref):
    scale = M_HEAD_DIM ** -0.5
    for h in range(M_HEADS):
        sl = slice(h * M_HEAD_DIM, (h + 1) * M_HEAD_DIM)
        q = q_ref[0, :, sl].astype(BF16)
        mk = mk_ref[0, :, sl].astype(BF16)
        mv = mv_ref[0, :, sl].astype(BF16)
        s = lax.dot_general(q, mk, (((1,), (1,)), ((), ())), preferred_element_type=F32) * scale
        p = jnp.exp(s - jnp.max(s, axis=-1, keepdims=True))
        p = p / jnp.sum(p, axis=-1, keepdims=True)
        o_ref[0, :, sl] = jnp.dot(p.astype(BF16), mv, preferred_element_type=F32).astype(o_ref.dtype)


def mem_attention(q3, qcolblk, mk3, mkcolblk, mv3, mvcolblk, batch, t, tq):
    m = mk3.shape[1]
    return pl.pallas_call(
        _mem_attn_kernel,
        grid=(batch, t // tq),
        in_specs=[pl.BlockSpec((1, tq, M_W), lambda b, i: (b, i, qcolblk)),
                  pl.BlockSpec((1, m, M_W), lambda b, i: (b, 0, mkcolblk)),
                  pl.BlockSpec((1, m, M_W), lambda b, i: (b, 0, mvcolblk))],
        out_specs=pl.BlockSpec((1, tq, M_W), lambda b, i: (b, i, 0)),
        out_shape=jax.ShapeDtypeStruct((batch, t, M_W), BF16),
        compiler_params=_cp(("parallel", "parallel"), 32),
        name="mem_attention",
    )(q3, mk3, mv3)


def _merge_kernel(oa_ref, ob_ref, om_ref, pa_ref, pb_ref, pm_ref, ga_ref, gb_ref, gm_ref, o_ref):
    acc = jax.nn.sigmoid(ga_ref[...]) * jnp.dot(oa_ref[...], pa_ref[...], preferred_element_type=F32)
    acc += jax.nn.sigmoid(gb_ref[...]) * jnp.dot(ob_ref[...], pb_ref[...], preferred_element_type=F32)
    acc += jax.nn.sigmoid(gm_ref[...]) * jnp.dot(om_ref[...], pm_ref[...], preferred_element_type=F32)
    o_ref[...] = acc.astype(o_ref.dtype)


def merge(oa, ob, om, pa, pb, pm, z, col, tm, tn):
    nt = oa.shape[0]
    d = pa.shape[1]

    def act(w):
        return pl.BlockSpec((tm, w), lambda j, i: (i, 0))

    def wgt(w):
        return pl.BlockSpec((w, tn), lambda j, i: (0, j))

    def gate(name):
        return pl.BlockSpec((tm, tn), lambda j, i: (i, col[name] // tn + j))

    return pl.pallas_call(
        _merge_kernel,
        grid=(d // tn, nt // tm),
        in_specs=[act(oa.shape[1]), act(ob.shape[1]), act(om.shape[1]),
                  wgt(pa.shape[0]), wgt(pb.shape[0]), wgt(pm.shape[0]),
                  gate("ga"), gate("gb"), gate("gm")],
        out_specs=pl.BlockSpec((tm, tn), lambda j, i: (i, j)),
        out_shape=jax.ShapeDtypeStruct((nt, d), BF16),
        compiler_params=_cp(("parallel", "parallel"), 48),
        name="merge",
    )(oa, ob, om, pa, pb, pm, z, z, z)


def _resid_mm_kernel(a_ref, b_ref, x_ref, o_ref, *, alpha):
    o_ref[...] = alpha * x_ref[...] + jnp.dot(a_ref[...], b_ref[...], preferred_element_type=F32)


def resid_matmul(a, b, x, alpha, tm, tn):
    m, k = a.shape
    n = b.shape[1]
    return pl.pallas_call(
        functools.partial(_resid_mm_kernel, alpha=alpha),
        grid=(n // tn, m // tm),
        in_specs=[pl.BlockSpec((tm, k), lambda j, i: (i, 0)),
                  pl.BlockSpec((k, tn), lambda j, i: (0, j)),
                  pl.BlockSpec((tm, tn), lambda j, i: (i, j))],
        out_specs=pl.BlockSpec((tm, tn), lambda j, i: (i, j)),
        out_shape=jax.ShapeDtypeStruct((m, n), F32),
        compiler_params=_cp(("parallel", "parallel"), 48),
        name="resid_matmul",
    )(a, b, x)


def _ln(x, g, b):
    mu = jnp.mean(x, axis=-1, keepdims=True)
    xc = x - mu
    var = jnp.mean(xc * xc, axis=-1, keepdims=True)
    return xc * lax.rsqrt(var + LN_EPS) * g + b


def _ln_kernel(x_ref, g_ref, b_ref, o_ref, ob_ref):
    y = _ln(x_ref[...], g_ref[...], b_ref[...])
    o_ref[...] = y
    ob_ref[...] = y.astype(ob_ref.dtype)


def layer_norm_dual(x, g, b, tm):
    n, d = x.shape
    row = pl.BlockSpec((tm, d), lambda i: (i, 0))
    vec = pl.BlockSpec((1, d), lambda i: (0, 0))
    return pl.pallas_call(
        _ln_kernel,
        grid=(n // tm,),
        in_specs=[row, vec, vec],
        out_specs=[row, row],
        out_shape=[jax.ShapeDtypeStruct((n, d), F32), jax.ShapeDtypeStruct((n, d), BF16)],
        compiler_params=_cp(("parallel",), 48),
        name="layer_norm1",
    )(x, g, b)


def _ln_resid_kernel(h_ref, p_ref, g_ref, b_ref, o_ref, *, alpha):
    o_ref[...] = _ln(alpha * h_ref[...] + p_ref[...], g_ref[...], b_ref[...])


def layer_norm_resid(h, p, g, b, alpha, tm):
    n, d = h.shape
    row = pl.BlockSpec((tm, d), lambda i: (i, 0))
    vec = pl.BlockSpec((1, d), lambda i: (0, 0))
    return pl.pallas_call(
        functools.partial(_ln_resid_kernel, alpha=alpha),
        grid=(n // tm,),
        in_specs=[row, row, vec, vec],
        out_specs=row,
        out_shape=jax.ShapeDtypeStruct((n, d), F32),
        compiler_params=_cp(("parallel",), 48),
        name="layer_norm2",
    )(h, p, g, b)


def _top_ranked(x, n_top, vals_ref):
    kdim = x.shape[0]
    idx = lax.broadcasted_iota(I32, x.shape, 0).astype(F32)

    def body(a, carry):
        x, rank = carry
        m = jnp.max(x, axis=0, keepdims=True)
        first = jnp.min(jnp.where(x == m, idx, float(kdim)), axis=0, keepdims=True)
        sel = idx == first
        vals_ref[pl.ds(a, 1), :] = m
        return jnp.where(sel, -jnp.inf, x), jnp.where(sel, a.astype(F32), rank)

    _, rank = lax.fori_loop(0, n_top, body, (x, jnp.full(x.shape, float(n_top), F32)))
    return rank, [vals_ref[a:a + 1, :] for a in range(n_top)]


def _top_ranked_distinct(x, n_top):
    rank = jnp.full(x.shape, float(n_top), F32)
    vals = []
    for a in range(n_top):
        m = jnp.max(x, axis=0, keepdims=True)
        sel = x == m
        rank = jnp.where(sel, float(a), rank)
        x = jnp.where(sel, -jnp.inf, x)
        vals.append(m)
    n_ranked = jnp.sum(jnp.where(rank < float(n_top), 1.0, 0.0), axis=0, keepdims=True)
    return rank, vals, n_ranked


def _peer_route_kernel(pq_ref, k1_ref, k2_ref, rank2_ref, cnt1_ref, e1_ref, e2_ref, vals_ref):
    tn = pq_ref.shape[0]
    refs = (pq_ref, k1_ref, k2_ref, rank2_ref, cnt1_ref, e1_ref, e2_ref, vals_ref)
    for h in range(PEER_HEADS):
        irregular = _peer_route_head(h, *refs, distinct=True)
        n_irregular = jnp.sum(irregular.reshape(tn // LANES, LANES), axis=0, keepdims=True)

        @pl.when(jnp.max(n_irregular) > 0.0)
        def _():
            _peer_route_head(h, *refs, distinct=False)


def _peer_route_head(h, pq_ref, k1_ref, k2_ref, rank2_ref, cnt1_ref, e1_ref, e2_ref, vals_ref, *, distinct):
    half = PEER_DKEY // 2
    nt = (((1,), (1,)), ((), ()))
    irregular = jnp.zeros((1, pq_ref.shape[0]), F32)

    def top(x, slot):
        if not distinct:
            return _top_ranked(x, PEER_TOPK, vals_ref.at[slot]) + (None,)
        return _top_ranked_distinct(x, PEER_TOPK)

    q1 = pq_ref[:, h * PEER_DKEY:h * PEER_DKEY + half].astype(BF16)
    q2 = pq_ref[:, h * PEER_DKEY + half:(h + 1) * PEER_DKEY].astype(BF16)
    s1 = lax.dot_general(k1_ref[...].astype(BF16), q1, nt, preferred_element_type=F32)
    s2 = lax.dot_general(k2_ref[...].astype(BF16), q2, nt, preferred_element_type=F32)
    rank1, v1, n1 = top(s1, 0)
    rank2, v2, n2 = top(s2, 1)
    v2m = jnp.concatenate(v2, axis=0)
    cand = jnp.concatenate([v1[a] + v2m for a in range(PEER_TOPK)], axis=0)
    crank, _, nc = top(cand, 2)
    if distinct:
        for cnt in (n1, n2, nc):
            irregular = irregular + jnp.where(cnt == float(PEER_TOPK), 0.0, 1.0)
    chosen = jnp.where(crank < float(PEER_TOPK), 1.0, 0.0)
    cmax = v1[0] + v2[0]
    zsum = jnp.sum(chosen * jnp.exp(cand - cmax), axis=0, keepdims=True)
    cnt1 = jnp.zeros_like(s1)
    for a in range(PEER_TOPK):
        m_a = jnp.sum(chosen[a * PEER_TOPK:(a + 1) * PEER_TOPK], axis=0, keepdims=True)
        cnt1 = jnp.where(rank1 == float(a), m_a, cnt1)
    rank2_ref[h] = rank2.astype(rank2_ref.dtype)
    cnt1_ref[h] = cnt1
    e1_ref[h] = jnp.exp(s1 - v1[0])
    e2_ref[h] = (jnp.exp(s2 - v2[0]) / zsum).astype(e2_ref.dtype)
    return irregular


def peer_route(pq, k1, k2, tn):
    n = pq.shape[0]
    shape = (PEER_HEADS, PEER_NKEYS, n)
    ospec = pl.BlockSpec((PEER_HEADS, PEER_NKEYS, tn), lambda i: (0, 0, i))
    kspec = pl.BlockSpec(k1.shape, lambda i: (0, 0))
    return pl.pallas_call(
        _peer_route_kernel,
        grid=(n // tn,),
        in_specs=[pl.BlockSpec((tn, pq.shape[1]), lambda i: (i, 0)), kspec, kspec],
        out_specs=[ospec] * 4,
        out_shape=[jax.ShapeDtypeStruct(shape, BF16), jax.ShapeDtypeStruct(shape, F32),
                   jax.ShapeDtypeStruct(shape, F32), jax.ShapeDtypeStruct(shape, BF16)],
        scratch_shapes=[pltpu.VMEM((3, PEER_TOPK, tn), F32)],
        compiler_params=_cp(("parallel",), 32),
        name="peer_route",
    )(pq, k1, k2)


PEER_EROWS = 4


def _gelu_exact(x):
    return 0.5 * x * (1.0 + lax.erf(x * (2.0 ** -0.5)))


def _peer_dense_kernel(x_ref, u_ref, v_ref, rank2_ref, cnt1_ref, e1_ref, e2_ref, o_ref, z_ref, *, n_tiles):
    r = pl.program_id(1)
    rd, wr = (r + 1) % 2, r % 2

    @pl.when(r == 0)
    def _():
        o_ref[...] = jnp.zeros(o_ref.shape, o_ref.dtype)
        z_ref[...] = jnp.zeros(z_ref.shape, z_ref.dtype)

    o_ref[...] += lax.dot_general(z_ref[rd], v_ref[...], (((0,), (0,)), ((), ())), preferred_element_type=F32)
    at = lax.dot_general(u_ref[...], x_ref[...], (((1,), (1,)), ((), ())), preferred_element_type=F32)
    act = _gelu_exact(at).astype(BF16)
    tile = jnp.minimum(r, n_tiles - 1)
    for rr in range(PEER_EROWS):
        i1 = tile * PEER_EROWS + rr
        g = jnp.zeros((PEER_NKEYS, x_ref.shape[0]), BF16)
        for h in range(PEER_HEADS):
            cnt = cnt1_ref[h, pl.ds(i1, 1), :].astype(BF16)
            e1 = e1_ref[h, pl.ds(i1, 1), :].astype(BF16)
            g = g + jnp.where(rank2_ref[h] < cnt, e2_ref[h] * e1, jnp.zeros((), BF16))
        z_ref[wr, rr * PEER_NKEYS:(rr + 1) * PEER_NKEYS, :] = g * act[rr * PEER_NKEYS:(rr + 1) * PEER_NKEYS]


def peer_dense(xb, u, v, rank2, cnt1, e1, e2, tn):
    n, d = xb.shape
    ne = u.shape[0]
    te = PEER_EROWS * PEER_NKEYS
    n_tiles = ne // te
    aux = pl.BlockSpec((PEER_HEADS, PEER_NKEYS, tn), lambda j, r: (0, 0, j))
    return pl.pallas_call(
        functools.partial(_peer_dense_kernel, n_tiles=n_tiles),
        grid=(n // tn, n_tiles + 1),
        in_specs=[pl.BlockSpec((tn, d), lambda j, r: (j, 0)),
                  pl.BlockSpec((te, d), lambda j, r: (jnp.minimum(r, n_tiles - 1), 0)),
                  pl.BlockSpec((te, d), lambda j, r: (jnp.maximum(r - 1, 0), 0)),
                  aux, aux, aux, aux],
        out_specs=pl.BlockSpec((tn, d), lambda j, r: (j, 0)),
        out_shape=jax.ShapeDtypeStruct((n, d), F32),
        scratch_shapes=[pltpu.VMEM((2, te, tn), BF16)],
        compiler_params=_cp(("parallel", "arbitrary"), 60),
        name="peer_dense",
    )(xb, u, v, rank2, cnt1, e1, e2)


def kernel(x_prompt, x_sample, cache_k, cache_v, cache_idx_k, cache_mem_k, cache_mem_v, state_hgrn, page_table,
           mem_prompt, w_in, w_mem_kv, hgrn_lb, hgrn_norm_g, p_a, p_b, p_m, w_out, ln1_g, ln1_b, w_pq,
           peer_sub_k1, peer_sub_k2, peer_u, peer_v, ln2_g, ln2_b):
    depth = w_in.shape[0]
    assert depth == 1, "single trunk layer"
    bp, t, d = x_prompt.shape
    bs, ts, _ = x_sample.shape
    assert bp == 1 and d == D_MODEL and t % KEY_CHUNK == 0 and bs * ts == Q_BLK and ts <= S_ROWS
    n_p, n_s = bp * t, bs * ts
    n_real = n_p + n_s
    nt = -(-n_real // TOK_TILE) * TOK_TILE
    n_pool = cache_k.shape[1]
    past_len = page_table.shape[1] * PAGE_SIZE
    alpha = (2 * depth) ** 0.25
    nb_p = n_p // Q_BLK

    x = _pad_rows(jnp.concatenate([x_prompt.reshape(n_p, d), x_sample.reshape(n_s, d)], axis=0), nt)
    w_packed_t = pack_w_in_t(jnp.swapaxes(w_in[0], 0, 1))
    z = matmul_nt(x.astype(BF16), w_packed_t, TOK_TILE, Z_TILE_WIDE, name="proj_in")

    def zcols(name, width, lo, hi):
        return z[lo:hi, COLS[name]:COLS[name] + width]

    pos = jnp.concatenate([jnp.arange(t, dtype=I32), past_len + jnp.tile(jnp.arange(ts, dtype=I32), bs),
                           jnp.zeros((nt - n_real,), I32)])
    q_hm, iq_hm, k4, kb, v4, vb, ik_rope, ikb = rope_all(z, COLS, rope_tables(pos))
    nb = q_hm.shape[0]

    oa_p = dsa_prompt(iq_hm.reshape(nb, IDX_HEADS * Q_BLK, IDX_DIM), z, COLS,
                      q_hm.reshape(nb, A_HEADS * Q_BLK, A_HEAD_DIM), ikb[:n_p], kb[:n_p], vb[:n_p], t)

    def per_seq(a, rows):
        return _pad_rows(a.reshape((bs, ts) + a.shape[1:]), rows, axis=1)

    q_s = per_seq(q_hm[nb_p].transpose(1, 0, 2), S_ROWS).transpose(0, 2, 1, 3).reshape(bs, A_HEADS * S_ROWS, A_HEAD_DIM)
    iq_s = per_seq(iq_hm[nb_p].transpose(1, 0, 2), S_ROWS).transpose(0, 2, 1, 3).reshape(bs, IDX_HEADS * S_ROWS, IDX_DIM)
    iw_s = per_seq(z[n_p:n_real, COLS["ikw"] + IDX_DIM:COLS["ikw"] + IDX_DIM + IDX_HEADS], S_ROWS)
    ik_new = per_seq(ikb[n_p:n_real], PAGE_SIZE)
    k_new = per_seq(kb[n_p:n_real], PAGE_SIZE)
    v_new = per_seq(vb[n_p:n_real], PAGE_SIZE)
    keys, thr = dsa_sample_index(page_table, iq_s, iw_s, ik_new, jnp.swapaxes(cache_idx_k[0], 1, 2), ts)
    cache_rows = n_pool * PAGE_SIZE * A_KV_HEADS
    os_hm = dsa_sample_attend(page_table, keys, thr, q_s, k_new, v_new,
                              cache_k[0].reshape(cache_rows, A_HEAD_DIM), cache_v[0].reshape(cache_rows, A_HEAD_DIM))
    oa_s = os_hm.reshape(bs, A_HEADS, S_ROWS, A_HEAD_DIM)[:, :, :ts].transpose(0, 2, 1, 3).reshape(n_s, A_Q_W)
    o_a = _pad_rows(jnp.concatenate([oa_p, oa_s], axis=0), nt)

    b_blk = [COLS[c] // B_W for c in ("bq", "bf", "bi", "bg")]
    ng = hgrn_norm_g[0][None]
    ob_p, st_p = hgrn(z, b_blk, hgrn_lb, ng, jnp.zeros((bp, B_HEADS, B_HEAD_DIM, B_HEAD_DIM), F32),
                      bp, t, HGRN_CHUNK, HGRN_SUB, HGRN_CHUNK)
    zs_b = per_seq(z[n_p:n_real, COLS["bq"]:COLS["bq"] + 4 * B_W], S_ROWS).reshape(bs * S_ROWS, 4 * B_W)
    ob_s, st_s = hgrn(zs_b, [0, 1, 2, 3], hgrn_lb, ng, state_hgrn[0], bs, S_ROWS, S_ROWS, S_ROWS, ts)
    ob_s = ob_s.reshape(bs, S_ROWS, B_W)[:, :ts].reshape(n_s, B_W)
    o_b = _pad_rows(jnp.concatenate([ob_p[:n_p], ob_s], axis=0), nt)

    mem_kv = matmul(mem_prompt[0].astype(BF16), w_mem_kv[0].astype(BF16), mem_prompt.shape[1], Z_TILE, name="mem_kv")
    om_p = mem_attention(z[None], COLS["mq"] // M_W, mem_kv[None], 0, mem_kv[None], 1, bp, t, TOK_TILE)[0]
    zs_m = per_seq(zcols("mq", M_W, n_p, n_real), S_ROWS)
    n_mem = cache_mem_k.shape[2]
    om_s = mem_attention(zs_m, 0, cache_mem_k[0].reshape(bs, n_mem, M_W), 0,
                         cache_mem_v[0].reshape(bs, n_mem, M_W), 0, bs, S_ROWS, S_ROWS)
    o_m = _pad_rows(jnp.concatenate([om_p, om_s[:, :ts].reshape(n_s, M_W)], axis=0), nt)

    merged = merge(o_a, o_b, o_m, p_a[0].astype(BF16), p_b[0].astype(BF16), p_m[0].astype(BF16), z, COLS,
                   TOK_TILE, WIDE_TILE)
    h_pre = resid_matmul(merged, w_out[0].astype(BF16), x, alpha, TOK_TILE, WIDE_TILE)
    h, hb = layer_norm_dual(h_pre, ln1_g[0][None], ln1_b[0][None], LN_TILE)

    pq = matmul(hb, w_pq[0].astype(BF16), TOK_TILE, Z_TILE, name="peer_query")
    rank2, cnt1, e1, e2 = peer_route(pq, peer_sub_k1[0], peer_sub_k2[0], LN_TILE)
    p_out = peer_dense(hb, peer_u[0].astype(BF16), peer_v[0].astype(BF16), rank2, cnt1, e1, e2, TOK_TILE)
    y = layer_norm_resid(h, p_out, ln2_g[0][None], ln2_b[0][None], alpha, LN_TILE)

    kv_p = (depth, bp, t, A_KV_HEADS, A_HEAD_DIM)
    kv_s = (depth, bs, ts, A_KV_HEADS, A_HEAD_DIM)
    mem_shape = (depth, bp, mem_prompt.shape[1], M_HEADS, M_HEAD_DIM)
    g4 = A_KV_HEADS
    return (y[:n_p].reshape(bp, t, d), y[n_p:n_real].reshape(bs, ts, d),
            k4[:n_p * g4].reshape(kv_p), v4[:n_p * g4].reshape(kv_p), ik_rope[:n_p].reshape(depth, bp, t, IDX_DIM),
            mem_kv[:, :M_W].reshape(mem_shape), mem_kv[:, M_W:].reshape(mem_shape), st_p[None],
            k4[n_p * g4:n_real * g4].reshape(kv_s), v4[n_p * g4:n_real * g4].reshape(kv_s),
            ik_rope[n_p:n_real].reshape(depth, bs, ts, IDX_DIM), st_s[None])
```

```python
import functools
import math

import jax
import jax.numpy as jnp
import numpy as np
from jax import lax
from jax.experimental import pallas as pl
from jax.experimental.pallas import tpu as pltpu

F32 = jnp.float32
BF16 = jnp.bfloat16
I32 = jnp.int32

A_HEADS, A_KV_HEADS, A_HEAD_DIM = 16, 4, 128
IDX_HEADS, IDX_DIM = 32, 64
TOPK_MAX = 256
B_HEADS, B_HEAD_DIM = 8, 128
M_HEADS, M_HEAD_DIM = 4, 256
PEER_HEADS, PEER_NKEYS, PEER_DKEY, PEER_TOPK = 8, 128, 256, 16
PAGE_SIZE = 128
ROPE_THETA = 10000.0
LN_EPS = 1e-5
RMS_EPS = 1e-6
D_MODEL = 4096

A_Q_W = A_HEADS * A_HEAD_DIM
A_KV_W = A_KV_HEADS * A_HEAD_DIM
IDX_Q_W = IDX_HEADS * IDX_DIM
B_W = B_HEADS * B_HEAD_DIM
M_W = M_HEADS * M_HEAD_DIM

_SEGS = (("aq", A_Q_W), ("iq", IDX_Q_W), ("ak", A_KV_W), ("av", A_KV_W), ("bq", B_W), ("bf", B_W),
         ("bi", B_W), ("bg", B_W), ("mq", M_W), ("ga", D_MODEL), ("gb", D_MODEL), ("gm", D_MODEL),
         ("ikw", 128))
COLS = {}
_off = 0
for _name, _w in _SEGS:
    COLS[_name] = _off
    _off += _w
Z_TILE = 512
Z_WIDTH = -(-_off // Z_TILE) * Z_TILE
Z_TILE_WIDE = 1536
assert Z_WIDTH % Z_TILE_WIDE == 0

LANES = 128
SUBLANES = 8
Q_BLK = 256
KEY_CHUNK = 512
HGRN_CHUNK = 64
HGRN_SUB = 16
EXP_CLAMP = 80.0
Q_LOG2_SCALE = (A_HEAD_DIM ** -0.5) * math.log2(math.e)
INT_MIN = -2 ** 31
NEG_BIG = -1e30
TOK_TILE = 512
WIDE_TILE = 1024
LN_TILE = 256


def _cp(sem, vmem_mb):
    return pltpu.CompilerParams(dimension_semantics=sem, vmem_limit_bytes=vmem_mb * 2 ** 20)


def _whole_vmem():
    return pl.BlockSpec(memory_space=pltpu.VMEM)


def _pad_rows(a, n, axis=0):
    pad = [(0, 0)] * a.ndim
    pad[axis] = (0, n - a.shape[axis])
    return jnp.pad(a, pad)


def _mm_kernel(a_ref, b_ref, o_ref):
    o_ref[...] = jnp.dot(a_ref[...], b_ref[...], preferred_element_type=F32).astype(o_ref.dtype)


def matmul(a, b, tm, tn, out_dtype=F32, name="matmul"):
    m, k = a.shape
    n = b.shape[1]
    return pl.pallas_call(
        _mm_kernel,
        grid=(m // tm, n // tn),
        in_specs=[pl.BlockSpec((tm, k), lambda i, j: (i, 0)), pl.BlockSpec((k, tn), lambda i, j: (0, j))],
        out_specs=pl.BlockSpec((tm, tn), lambda i, j: (i, j)),
        out_shape=jax.ShapeDtypeStruct((m, n), out_dtype),
        compiler_params=_cp(("parallel", "parallel"), 48),
        name=name,
    )(a, b)


def _mm_nt_kernel(a_ref, bt_ref, o_ref):
    o_ref[...] = lax.dot_general(a_ref[...], bt_ref[...], (((1,), (1,)), ((), ())),
                                 preferred_element_type=F32).astype(o_ref.dtype)


def matmul_nt(a, bt, tm, tn, out_dtype=F32, name="matmul_nt"):
    m, k = a.shape
    n = bt.shape[0]
    return pl.pallas_call(
        _mm_nt_kernel,
        grid=(n // tn, m // tm),
        in_specs=[pl.BlockSpec((tm, k), lambda j, i: (i, 0)), pl.BlockSpec((tn, k), lambda j, i: (j, 0))],
        out_specs=pl.BlockSpec((tm, tn), lambda j, i: (i, j)),
        out_shape=jax.ShapeDtypeStruct((m, n), out_dtype),
        compiler_params=_cp(("parallel", "parallel"), 48),
        name=name,
    )(a, bt)


PACK_SHIFT = IDX_DIM + IDX_HEADS
PACK_T_ROWS = 256


def _pack_t_plan():
    r = PACK_T_ROWS
    splits = (A_Q_W, A_KV_W, A_KV_W, IDX_Q_W, IDX_DIM, IDX_HEADS, 4 * B_W + M_W + 3 * D_MODEL)
    src = dict(zip(("aq", "ak", "av", "iq", "ik", "iw", "rest"), np.concatenate([[0], np.cumsum(splits)[:-1]])))
    assert src["ik"] % r == 0 and src["rest"] == src["ik"] + PACK_SHIFT and splits[-1] % r == 0
    plan = []
    for name in ("aq", "iq", "ak", "av"):
        width = A_Q_W if name in ("aq", "iq") else A_KV_W
        plan += [(0, (src[name] + o) // r, (COLS[name] + o) // r) for o in range(0, width, r)]
    plan += [(2, src["ik"] // r, COLS["ikw"] // r)]
    plan += [(1, src["ik"] // r + 1 + t, COLS["bq"] // r + t) for t in range(splits[-1] // r)]
    plan += [(3, 0, t) for t in range(COLS["ikw"] // r + 1, Z_WIDTH // r)]
    assert sorted(p[2] for p in plan) == list(range(Z_WIDTH // r))
    return np.array(plan, np.int32)


def _pack_t_kernel(plan_ref, a_ref, o_ref, tail_ref):
    mode = plan_ref[pl.program_id(0), 0]
    keep = PACK_T_ROWS - PACK_SHIFT

    @pl.when(mode == 0)
    def _():
        o_ref[...] = a_ref[...].astype(o_ref.dtype)

    @pl.when(mode == 1)
    def _():
        o_ref[:keep, :] = tail_ref[...].astype(o_ref.dtype)
        o_ref[keep:, :] = a_ref[:PACK_SHIFT, :].astype(o_ref.dtype)
        tail_ref[...] = a_ref[PACK_SHIFT:, :]

    @pl.when(mode == 2)
    def _():
        o_ref[...] = jnp.zeros(o_ref.shape, o_ref.dtype)
        o_ref[:PACK_SHIFT, :] = a_ref[:PACK_SHIFT, :].astype(o_ref.dtype)
        tail_ref[...] = a_ref[PACK_SHIFT:, :]

    @pl.when(mode == 3)
    def _():
        o_ref[...] = jnp.zeros(o_ref.shape, o_ref.dtype)


def pack_w_in_t(wt):
    dm = wt.shape[1]
    plan = _pack_t_plan()
    n_src = -(-wt.shape[0] // PACK_T_ROWS)
    grid_spec = pltpu.PrefetchScalarGridSpec(
        num_scalar_prefetch=1,
        grid=(len(plan),),
        in_specs=[pl.BlockSpec((PACK_T_ROWS, dm), lambda j, p: (jnp.minimum(p[j, 1], n_src - 1), 0))],
        out_specs=pl.BlockSpec((PACK_T_ROWS, dm), lambda j, p: (p[j, 2], 0)),
        scratch_shapes=[pltpu.VMEM((PACK_T_ROWS - PACK_SHIFT, dm), F32)],
    )
    return pl.pallas_call(
        _pack_t_kernel,
        grid_spec=grid_spec,
        out_shape=jax.ShapeDtypeStruct((Z_WIDTH, dm), BF16),
        compiler_params=_cp(("arbitrary",), 32),
        name="pack_w_in",
    )(jnp.asarray(plan), wt)


def _rope_kernel(q_ref, iq_ref, k_ref, v_ref, ikw_ref, c128_ref, s128_ref, c64_ref, s64_ref,
                 qhm_ref, iqhm_ref, k4_ref, kb_ref, v4_ref, vb_ref, iko_ref, ikb_ref):
    c128, s128 = c128_ref[...], s128_ref[...]
    c64, s64 = c64_ref[...], s64_ref[...]
    lane = lax.broadcasted_iota(I32, (Q_BLK, LANES), 1)
    first_half = (lane % IDX_DIM) < (IDX_DIM // 2)

    def rope128(x):
        return x * c128 + pltpu.roll(x, A_HEAD_DIM // 2, axis=1) * s128

    def rope64(x):
        partner = jnp.where(first_half, pltpu.roll(x, LANES - IDX_DIM // 2, axis=1),
                            pltpu.roll(x, IDX_DIM // 2, axis=1))
        return x * c64 + partner * s64

    for h in range(A_HEADS):
        qhm_ref[0, h] = (rope128(q_ref[:, h * LANES:(h + 1) * LANES]) * Q_LOG2_SCALE).astype(qhm_ref.dtype)
    for h in range(A_KV_HEADS):
        rows = pl.ds(h, Q_BLK, stride=A_KV_HEADS)
        kr = rope128(k_ref[:, h * LANES:(h + 1) * LANES])
        k4_ref[rows, :] = kr
        kb_ref[:, h * LANES:(h + 1) * LANES] = kr.astype(kb_ref.dtype)
        vh = v_ref[:, h * LANES:(h + 1) * LANES]
        v4_ref[rows, :] = vh
        vb_ref[:, h * LANES:(h + 1) * LANES] = vh.astype(vb_ref.dtype)
    for p in range(IDX_HEADS // 2):
        r = rope64(iq_ref[:, p * LANES:(p + 1) * LANES]).astype(iqhm_ref.dtype)
        iqhm_ref[0, 2 * p] = r[:, :IDX_DIM]
        iqhm_ref[0, 2 * p + 1] = r[:, IDX_DIM:]
    ikr = rope64(ikw_ref[...])[:, :IDX_DIM]
    iko_ref[...] = ikr
    ikb_ref[...] = ikr.astype(ikb_ref.dtype)


def rope_all(z, col, tabs):
    nt = z.shape[0]
    nb = nt // Q_BLK
    c128, s128, c64, s64 = tabs
    tab_spec = pl.BlockSpec((Q_BLK, LANES), lambda i: (i, 0))
    kv4_spec = pl.BlockSpec((Q_BLK * A_KV_HEADS, A_HEAD_DIM), lambda i: (i, 0))
    kvb_spec = pl.BlockSpec((Q_BLK, A_KV_W), lambda i: (i, 0))
    ik_spec = pl.BlockSpec((Q_BLK, IDX_DIM), lambda i: (i, 0))
    kv4 = jax.ShapeDtypeStruct((nt * A_KV_HEADS, A_HEAD_DIM), F32)
    kvb = jax.ShapeDtypeStruct((nt, A_KV_W), BF16)
    return pl.pallas_call(
        _rope_kernel,
        grid=(nb,),
        in_specs=[pl.BlockSpec((Q_BLK, A_Q_W), lambda i: (i, col["aq"] // A_Q_W)),
                  pl.BlockSpec((Q_BLK, IDX_Q_W), lambda i: (i, col["iq"] // IDX_Q_W)),
                  pl.BlockSpec((Q_BLK, A_KV_W), lambda i: (i, col["ak"] // A_KV_W)),
                  pl.BlockSpec((Q_BLK, A_KV_W), lambda i: (i, col["av"] // A_KV_W)),
                  pl.BlockSpec((Q_BLK, LANES), lambda i: (i, col["ikw"] // LANES)),
                  tab_spec, tab_spec, tab_spec, tab_spec],
        out_specs=[pl.BlockSpec((1, A_HEADS, Q_BLK, A_HEAD_DIM), lambda i: (i, 0, 0, 0)),
                   pl.BlockSpec((1, IDX_HEADS, Q_BLK, IDX_DIM), lambda i: (i, 0, 0, 0)),
                   kv4_spec, kvb_spec, kv4_spec, kvb_spec, ik_spec, ik_spec],
        out_shape=[jax.ShapeDtypeStruct((nb, A_HEADS, Q_BLK, A_HEAD_DIM), BF16),
                   jax.ShapeDtypeStruct((nb, IDX_HEADS, Q_BLK, IDX_DIM), BF16),
                   kv4, kvb, kv4, kvb,
                   jax.ShapeDtypeStruct((nt, IDX_DIM), F32), jax.ShapeDtypeStruct((nt, IDX_DIM), BF16)],
        compiler_params=_cp(("parallel",), 32),
        name="rope",
    )(z, z, z, z, z, c128, s128, c64, s64)


def rope_tables(pos):
    def tab(half, reps):
        inv_freq = ROPE_THETA ** (-jnp.arange(half, dtype=F32) / half)
        ang = pos.astype(F32)[:, None] * inv_freq[None, :]
        c, s = jnp.cos(ang), jnp.sin(ang)
        return jnp.tile(jnp.concatenate([c, c], 1), (1, reps)), jnp.tile(jnp.concatenate([-s, s], 1), (1, reps))
    c128, s128 = tab(A_HEAD_DIM // 2, 1)
    c64, s64 = tab(IDX_DIM // 2, 2)
    return c128, s128, c64, s64


def _sort_key(x):
    i = pltpu.bitcast(x, I32)
    return i ^ ((i >> 31) & 0x7FFFFFFF)


WORD = 32
GROUP_CHUNKS = WORD * LANES // KEY_CHUNK
_BIT_MASKS = ((16, 0x0000FFFF), (8, 0x00FF00FF), (4, 0x0F0F0F0F), (2, 0x33333333), (1, 0x55555555))


def _transpose_bits(w):
    w = list(w)
    for d, m in _BIT_MASKS:
        mask = jnp.int32(m)
        for j in range(WORD):
            if j & d:
                continue
            lo, hi = w[j], w[j + d]
            t = (lax.shift_right_logical(lo, jnp.int32(d)) ^ hi) & mask
            w[j + d] = hi ^ t
            w[j] = lo ^ lax.shift_left(t, jnp.int32(d))
    return w


def _kth_largest_bitsliced(key_ref, plane_ref, cand_ref, nch, k):
    rows = key_ref.shape[1]
    tiles_per_chunk = KEY_CHUNK // LANES
    ngroups = (nch + GROUP_CHUNKS - 1) // GROUP_CHUNKS

    def build_group(g, carry):
        def build_slab(s, c2):
            r0 = pl.multiple_of(s * SUBLANES, SUBLANES)
            words = []
            for j in range(WORD):
                c = g * GROUP_CHUNKS + j // tiles_per_chunk
                lt = j % tiles_per_chunk
                cc = jnp.minimum(c, key_ref.shape[0] - 1)
                w = key_ref[cc, pl.ds(r0, SUBLANES), lt * LANES:(lt + 1) * LANES] ^ INT_MIN
                words.append(jnp.where(c < nch, w, 0))
            planes = _transpose_bits(words)
            for b in range(WORD):
                plane_ref[g, b, pl.ds(r0, SUBLANES), :] = planes[b]
            return c2
        lax.fori_loop(0, rows // SUBLANES, build_slab, 0)
        cand_ref[g] = jnp.full((rows, LANES), -1, I32)
        return carry

    lax.fori_loop(0, ngroups, build_group, 0)

    def bit_step(it, carry):
        t_u, k_rem = carry
        b = WORD - 1 - it

        def count(g, cnt):
            return cnt + lax.population_count(cand_ref[g] & plane_ref[g, b])
        cnt = lax.fori_loop(0, ngroups, count, jnp.zeros((rows, LANES), I32))
        c1 = jnp.sum(cnt, axis=1, keepdims=True)
        take = c1 >= k_rem
        take_b = jnp.broadcast_to(take, (rows, LANES))

        def update(g, c2):
            e = cand_ref[g]
            a = e & plane_ref[g, b]
            cand_ref[g] = jnp.where(take_b, a, e ^ a)
            return c2
        lax.fori_loop(0, ngroups, update, 0)
        t_u = jnp.where(take, t_u | jnp.left_shift(jnp.int32(1), b), t_u)
        return t_u, jnp.where(take, k_rem, k_rem - c1)

    t_u, k_rem = lax.fori_loop(0, WORD, bit_step, (jnp.zeros((rows, 1), I32), jnp.full((rows, 1), k, I32)))
    return t_u ^ INT_MIN, k_rem


def _lower_surplus_ties(key_ref, cand_ref, nch, t, n_wanted):
    rows = key_ref.shape[1]
    ngroups = (nch + GROUP_CHUNKS - 1) // GROUP_CHUNKS
    group_keys = WORD * LANES
    pos_bits = (key_ref.shape[0] * KEY_CHUNK - 1).bit_length()
    live = t > INT_MIN

    def count_tied(g, cnt):
        return cnt + lax.population_count(cand_ref[g])
    n_tied = jnp.sum(lax.fori_loop(0, ngroups, count_tied, jnp.zeros((rows, LANES), I32)), axis=1, keepdims=True)
    surplus = jnp.where(live, n_tied - n_wanted, 0)

    @pl.when(jnp.max(surplus) > 0)
    def _():
        lane = lax.broadcasted_iota(I32, (rows, LANES), 1)

        def tied_before(q):
            gq = q // group_keys
            jq = jnp.broadcast_to((q // LANES) % WORD, (rows, LANES))
            lq = q % LANES
            low = lax.shift_left(jnp.ones((rows, LANES), I32), jq) - 1

            def body(g, cnt):
                w = cand_ref[g]
                at_tile = jnp.where(lane < lq, lax.shift_right_logical(w, jq) & 1, 0)
                part = lax.population_count(w & low) + at_tile
                return cnt + jnp.where(g < gq, lax.population_count(w), jnp.where(g == gq, part, 0))
            cnt = lax.fori_loop(0, ngroups, body, jnp.zeros((rows, LANES), I32))
            return jnp.sum(cnt, axis=1, keepdims=True)

        def bit_step(it, q):
            cand = q | jnp.left_shift(jnp.int32(1), pos_bits - 1 - it)
            return jnp.where(tied_before(cand) < n_wanted, cand, q)
        last = lax.fori_loop(0, pos_bits, bit_step, jnp.zeros((rows, 1), I32))
        off = lax.broadcasted_iota(I32, (rows, KEY_CHUNK), 1)

        def rewrite(c, carry):
            keys = key_ref[c]
            drop = jnp.where(keys == t, 1, 0) * jnp.where(c * KEY_CHUNK + off > last, 1, 0) * jnp.where(live, 1, 0)
            key_ref[c] = keys - drop
            return carry
        lax.fori_loop(0, nch, rewrite, 0)


def _flash_step(s, mask, vext, m_ref, l_ref, acc_ref, g):
    hh, rr, ss = s.shape
    d = vext.shape[1] // 2
    s = jnp.where(mask[None], s, -jnp.inf).reshape(hh * rr, ss)
    m_old = m_ref[g]
    m_new = jnp.maximum(m_old, jnp.max(s, axis=-1, keepdims=True))
    p = jnp.exp2(s - jnp.concatenate([m_new] * (ss // LANES), axis=1))
    alpha = jnp.exp2(m_old - m_new)
    pv = jnp.dot(p.astype(BF16), vext, preferred_element_type=F32)
    acc_ref[g] = alpha * acc_ref[g] + pv[:, :d]
    l_ref[g] = alpha * l_ref[g] + pv[:, d:]
    m_ref[g] = m_new


IDX_HEADS_PER_DOT = 8


def _dsa_prompt_kernel(iq_ref, w_ref, q_ref, ik_ref, k_ref, v_ref, o_ref,
                       key_ref, plane_ref, cand_ref, wb_ref, m_ref, l_ref, acc_ref, *, topk, w_lane0):
    i = pl.program_id(0)
    nch = (i * Q_BLK + Q_BLK + KEY_CHUNK - 1) // KEY_CHUNK
    hpg = A_HEADS // A_KV_HEADS

    wt = w_ref[...] * ((IDX_DIM ** -0.5) * (IDX_HEADS ** -0.5))
    for h in range(IDX_HEADS):
        wb_ref[h] = jnp.broadcast_to(wt[:, w_lane0 + h:w_lane0 + h + 1], (Q_BLK, LANES))

    q_pos = i * Q_BLK + lax.broadcasted_iota(I32, (Q_BLK, KEY_CHUNK), 0)
    k_off = lax.broadcasted_iota(I32, (Q_BLK, KEY_CHUNK), 1)

    def score_chunk(c, carry):
        k0 = pl.multiple_of(c * KEY_CHUNK, KEY_CHUNK)
        ikc = ik_ref[pl.ds(k0, KEY_CHUNK), :]
        ntile = KEY_CHUNK // LANES
        acc = [jnp.zeros((Q_BLK, LANES), F32)] * ntile
        for hg in range(IDX_HEADS // IDX_HEADS_PER_DOT):
            lhs = iq_ref[0, hg * IDX_HEADS_PER_DOT * Q_BLK:(hg + 1) * IDX_HEADS_PER_DOT * Q_BLK, :]
            d = lax.dot_general(lhs, ikc, (((1,), (1,)), ((), ())), preferred_element_type=F32)
            for hl in range(IDX_HEADS_PER_DOT):
                wb = wb_ref[hg * IDX_HEADS_PER_DOT + hl]
                r = jnp.maximum(d[hl * Q_BLK:(hl + 1) * Q_BLK], 0.0)
                acc = [acc[j] + wb * r[:, j * LANES:(j + 1) * LANES] for j in range(ntile)]
        sc = jnp.concatenate(acc, axis=1)
        key_ref[c] = jnp.where(k0 + k_off <= q_pos, _sort_key(sc), INT_MIN)
        return carry

    lax.fori_loop(0, nch, score_chunk, 0)

    t, n_wanted = _kth_largest_bitsliced(key_ref, plane_ref, cand_ref, nch, topk)
    _lower_surplus_ties(key_ref, cand_ref, nch, t, n_wanted)
    thr = jnp.maximum(t, INT_MIN + 1)

    m_ref[...] = jnp.full(m_ref.shape, NEG_BIG, F32)
    l_ref[...] = jnp.zeros(l_ref.shape, F32)
    acc_ref[...] = jnp.zeros(acc_ref.shape, F32)
    ones = jnp.ones((KEY_CHUNK, A_HEAD_DIM), BF16)

    def attend_chunk(c, carry):
        k0 = pl.multiple_of(c * KEY_CHUNK, KEY_CHUNK)
        mask = key_ref[c] >= thr
        for g in range(A_KV_HEADS):
            qg = q_ref[0, g * hpg * Q_BLK:(g + 1) * hpg * Q_BLK, :]
            kc = k_ref[pl.ds(k0, KEY_CHUNK), g * LANES:(g + 1) * LANES]
            vc = v_ref[pl.ds(k0, KEY_CHUNK), g * LANES:(g + 1) * LANES]
            s = lax.dot_general(qg, kc, (((1,), (1,)), ((), ())), preferred_element_type=F32)
            _flash_step(s.reshape(hpg, Q_BLK, KEY_CHUNK), mask, jnp.concatenate([vc, ones], axis=1),
                        m_ref, l_ref, acc_ref, g)
        return carry

    lax.fori_loop(0, nch, attend_chunk, 0)
    for g in range(A_KV_HEADS):
        o = (acc_ref[g] / l_ref[g]).astype(o_ref.dtype)
        for hl in range(hpg):
            h = g * hpg + hl
            o_ref[:, h * A_HEAD_DIM:(h + 1) * A_HEAD_DIM] = o[hl * Q_BLK:(hl + 1) * Q_BLK]


def dsa_prompt(iq_hm, z, col, q_hm, ik, k, v, t):
    nb = t // Q_BLK
    topk = min(TOPK_MAX, t // 4)
    nch_max = (t + KEY_CHUNK - 1) // KEY_CHUNK
    ngroups_max = (nch_max + GROUP_CHUNKS - 1) // GROUP_CHUNKS
    hpg = A_HEADS // A_KV_HEADS
    kern = functools.partial(_dsa_prompt_kernel, topk=topk, w_lane0=IDX_DIM)
    return pl.pallas_call(
        kern,
        grid=(nb,),
        in_specs=[pl.BlockSpec((1, IDX_HEADS * Q_BLK, IDX_DIM), lambda i: (i, 0, 0)),
                  pl.BlockSpec((Q_BLK, LANES), lambda i: (i, col["ikw"] // LANES)),
                  pl.BlockSpec((1, A_HEADS * Q_BLK, A_HEAD_DIM), lambda i: (i, 0, 0)),
                  _whole_vmem(), _whole_vmem(), _whole_vmem()],
        out_specs=pl.BlockSpec((Q_BLK, A_Q_W), lambda i: (i, 0)),
        out_shape=jax.ShapeDtypeStruct((t, A_Q_W), BF16),
        scratch_shapes=[pltpu.VMEM((nch_max, Q_BLK, KEY_CHUNK), I32),
                        pltpu.VMEM((ngroups_max, WORD, Q_BLK, LANES), I32),
                        pltpu.VMEM((ngroups_max, Q_BLK, LANES), I32),
                        pltpu.VMEM((IDX_HEADS, Q_BLK, LANES), F32),
                        pltpu.VMEM((A_KV_HEADS, hpg * Q_BLK, LANES), F32),
                        pltpu.VMEM((A_KV_HEADS, hpg * Q_BLK, LANES), F32),
                        pltpu.VMEM((A_KV_HEADS, hpg * Q_BLK, A_HEAD_DIM), F32)],
        compiler_params=_cp(("arbitrary",), 60),
        name="dsa_prompt",
    )(iq_hm, z, q_hm, ik, k, v)


S_ROWS = SUBLANES
ATT_PAGES = 16


def _dsa_sample_scores_kernel(pt_ref, iq_ref, w_ref, ikn_ref, *rest, n_pages, t_new):
    pages = rest[:n_pages]
    key_ref = rest[n_pages]
    wt = w_ref[0] * ((IDX_DIM ** -0.5) * (IDX_HEADS ** -0.5))
    wb = [jnp.broadcast_to(wt[:, h:h + 1], (S_ROWS, LANES)) for h in range(IDX_HEADS)]
    iq = iq_ref[0]

    def tile_scores(ik_tile, transposed):
        dims = (((1,), (0,)), ((), ())) if transposed else (((1,), (1,)), ((), ()))
        d = lax.dot_general(iq, ik_tile, dims, preferred_element_type=F32)
        acc = jnp.zeros((S_ROWS, LANES), F32)
        for h in range(IDX_HEADS):
            acc = acc + wb[h] * jnp.maximum(d[h * S_ROWS:(h + 1) * S_ROWS], 0.0)
        return acc

    for r in range(n_pages):
        key_ref[0, r] = _sort_key(tile_scores(pages[r][0].astype(BF16), True))
    row = lax.broadcasted_iota(I32, (S_ROWS, LANES), 0)
    lane = lax.broadcasted_iota(I32, (S_ROWS, LANES), 1)
    valid = jnp.where(lane < t_new, lane, S_ROWS) <= row
    key_ref[0, n_pages] = jnp.where(valid, _sort_key(tile_scores(ikn_ref[0], False)), INT_MIN)


def _dsa_sample_thr_kernel(key_ref, adj_ref, thr_ref, last_ref, *, topk):
    b, n_tiles = key_ref.shape[0], key_ref.shape[1]
    pos_bits = (n_tiles * LANES - 1).bit_length()
    lane = lax.broadcasted_iota(I32, (b, S_ROWS, LANES), 2)

    def count(pred):
        def body(c, cnt):
            return cnt + jnp.where(pred(key_ref[:, c], c), 1, 0)
        cnt = lax.fori_loop(0, n_tiles, body, jnp.zeros((b, S_ROWS, LANES), I32))
        return jnp.sum(cnt, axis=2, keepdims=True)

    def bit_step(it, cur):
        cand = cur | jnp.left_shift(jnp.int32(1), 31 - it)
        return jnp.where(count(lambda x, c: x >= (cand ^ INT_MIN)) >= topk, cand, cur)

    t = lax.fori_loop(0, 32, bit_step, jnp.zeros((b, S_ROWS, 1), I32)) ^ INT_MIN
    live = t > INT_MIN
    n_above = count(lambda x, c: x > t)
    n_tied = count(lambda x, c: x == t)
    n_wanted = topk - n_above
    last_ref[...] = jnp.full(last_ref.shape, n_tiles * LANES, I32)

    @pl.when(jnp.max(jnp.where(live, n_tied - n_wanted, 0)) > 0)
    def _():
        def pos_step(it, q):
            cand = q | jnp.left_shift(jnp.int32(1), pos_bits - 1 - it)
            before = count(lambda x, c: jnp.where(x == t, 1, 0) * jnp.where(c * LANES + lane < cand, 1, 0) > 0)
            return jnp.where(before < n_wanted, cand, q)
        last = lax.fori_loop(0, pos_bits, pos_step, jnp.zeros((b, S_ROWS, 1), I32))
        last_ref[...] = jnp.broadcast_to(last, last_ref.shape)

    last = last_ref[...]
    live_i = jnp.where(live, 1, 0)

    def rewrite(c, carry):
        keys = key_ref[:, c]
        drop = jnp.where(keys == t, 1, 0) * jnp.where(c * LANES + lane > last, 1, 0) * live_i
        adj_ref[:, c] = keys - drop
        return carry
    lax.fori_loop(0, n_tiles, rewrite, 0)
    thr_ref[...] = jnp.broadcast_to(jnp.maximum(t, INT_MIN + 1), thr_ref.shape)


def dsa_sample_index(page_table, iq_s, iw_s, ik_new, cache_ik, t_new):
    b, n_pages = page_table.shape
    topk = min(TOPK_MAX, (n_pages * PAGE_SIZE + t_new) // 4)

    def page_spec(r):
        return pl.BlockSpec((1, IDX_DIM, PAGE_SIZE), lambda bi, pt: (pt[bi, r], 0, 0))

    grid_spec = pltpu.PrefetchScalarGridSpec(
        num_scalar_prefetch=1,
        grid=(b,),
        in_specs=[pl.BlockSpec((1, IDX_HEADS * S_ROWS, IDX_DIM), lambda bi, pt: (bi, 0, 0)),
                  pl.BlockSpec((1, S_ROWS, IDX_HEADS), lambda bi, pt: (bi, 0, 0)),
                  pl.BlockSpec((1, PAGE_SIZE, IDX_DIM), lambda bi, pt: (bi, 0, 0))]
                 + [page_spec(r) for r in range(n_pages)],
        out_specs=pl.BlockSpec((1, n_pages + 1, S_ROWS, LANES), lambda bi, pt: (bi, 0, 0, 0)),
    )
    keys = pl.pallas_call(
        functools.partial(_dsa_sample_scores_kernel, n_pages=n_pages, t_new=t_new),
        grid_spec=grid_spec,
        out_shape=jax.ShapeDtypeStruct((b, n_pages + 1, S_ROWS, LANES), I32),
        compiler_params=_cp(("parallel",), 32),
        name="dsa_sample_scores",
    )(page_table, iq_s, iw_s, ik_new, *([cache_ik] * n_pages))
    return pl.pallas_call(
        functools.partial(_dsa_sample_thr_kernel, topk=topk),
        in_specs=[_whole_vmem()],
        out_specs=[_whole_vmem(), _whole_vmem()],
        out_shape=[jax.ShapeDtypeStruct(keys.shape, I32), jax.ShapeDtypeStruct((b, S_ROWS, LANES), I32)],
        scratch_shapes=[pltpu.VMEM((b, S_ROWS, LANES), I32)],
        compiler_params=pltpu.CompilerParams(vmem_limit_bytes=32 * 2 ** 20),
        name="dsa_sample_threshold",
    )(keys)


def _dsa_sample_attend_kernel(pt_ref, key_ref, thr_ref, q_ref, kn_ref, vn_ref, *rest, n_pages):
    kpages = rest[:ATT_PAGES]
    vpages = rest[ATT_PAGES:2 * ATT_PAGES]
    o_ref, m_ref, l_ref, acc_ref = rest[2 * ATT_PAGES:]
    j = pl.program_id(1)
    hpg = A_HEADS // A_KV_HEADS
    thr = thr_ref[0]

    @pl.when(j == 0)
    def _():
        m_ref[...] = jnp.full(m_ref.shape, NEG_BIG, F32)
        l_ref[...] = jnp.zeros(l_ref.shape, F32)
        acc_ref[...] = jnp.zeros(acc_ref.shape, F32)

    def attend(mask, g, kg, vg):
        qg = q_ref[0, g * hpg * S_ROWS:(g + 1) * hpg * S_ROWS, :]
        s = lax.dot_general(qg, kg, (((1,), (1,)), ((), ())), preferred_element_type=F32)
        vext = jnp.concatenate([vg, jnp.ones(vg.shape, BF16)], axis=1)
        _flash_step(s.reshape(hpg, S_ROWS, s.shape[-1]), mask, vext, m_ref, l_ref, acc_ref, g)

    mask = jnp.concatenate([key_ref[0, j * ATT_PAGES + r] >= thr for r in range(ATT_PAGES)], axis=1)
    for g in range(A_KV_HEADS):
        rows = pl.ds(g, PAGE_SIZE, stride=A_KV_HEADS)
        kg = jnp.concatenate([kp[rows, :].astype(BF16) for kp in kpages], axis=0)
        vg = jnp.concatenate([vp[rows, :].astype(BF16) for vp in vpages], axis=0)
        attend(mask, g, kg, vg)

    @pl.when(j == pl.num_programs(1) - 1)
    def _():
        mask_new = key_ref[0, n_pages] >= thr
        for g in range(A_KV_HEADS):
            attend(mask_new, g, kn_ref[0, :, g * LANES:(g + 1) * LANES], vn_ref[0, :, g * LANES:(g + 1) * LANES])
        for g in range(A_KV_HEADS):
            o_ref[0, g * hpg * S_ROWS:(g + 1) * hpg * S_ROWS, :] = (acc_ref[g] / l_ref[g]).astype(o_ref.dtype)


def dsa_sample_attend(page_table, keys, thr, q_s, k_new, v_new, cache_k, cache_v):
    b, n_pages = page_table.shape
    steps = n_pages // ATT_PAGES
    hpg = A_HEADS // A_KV_HEADS
    page_rows = PAGE_SIZE * A_KV_HEADS

    def page_spec(r):
        return pl.BlockSpec((page_rows, A_HEAD_DIM), lambda bi, j, pt: (pt[bi, j * ATT_PAGES + r], 0))

    grid_spec = pltpu.PrefetchScalarGridSpec(
        num_scalar_prefetch=1,
        grid=(b, steps),
        in_specs=[pl.BlockSpec((1, n_pages + 1, S_ROWS, LANES), lambda bi, j, pt: (bi, 0, 0, 0)),
                  pl.BlockSpec((1, S_ROWS, LANES), lambda bi, j, pt: (bi, 0, 0)),
                  pl.BlockSpec((1, A_HEADS * S_ROWS, A_HEAD_DIM), lambda bi, j, pt: (bi, 0, 0)),
                  pl.BlockSpec((1, PAGE_SIZE, A_KV_W), lambda bi, j, pt: (bi, 0, 0)),
                  pl.BlockSpec((1, PAGE_SIZE, A_KV_W), lambda bi, j, pt: (bi, 0, 0))]
                 + [page_spec(r) for r in range(ATT_PAGES)] * 2,
        out_specs=pl.BlockSpec((1, A_HEADS * S_ROWS, A_HEAD_DIM), lambda bi, j, pt: (bi, 0, 0)),
        scratch_shapes=[pltpu.VMEM((A_KV_HEADS, hpg * S_ROWS, LANES), F32),
                        pltpu.VMEM((A_KV_HEADS, hpg * S_ROWS, LANES), F32),
                        pltpu.VMEM((A_KV_HEADS, hpg * S_ROWS, A_HEAD_DIM), F32)],
    )
    return pl.pallas_call(
        functools.partial(_dsa_sample_attend_kernel, n_pages=n_pages),
        grid_spec=grid_spec,
        out_shape=jax.ShapeDtypeStruct((b, A_HEADS * S_ROWS, A_HEAD_DIM), BF16),
        compiler_params=_cp(("parallel", "arbitrary"), 48),
        name="dsa_sample_attend",
    )(page_table, keys, thr, q_s, k_new, v_new, *([cache_k] * ATT_PAGES), *([cache_v] * ATT_PAGES))


def _cumsum_rows(x):
    n = x.shape[0]
    row = lax.broadcasted_iota(I32, x.shape, 0)
    d = 1
    while d < n:
        x = x + jnp.where(row >= d, pltpu.roll(x, d, axis=0), 0.0)
        d *= 2
    return x


def _hgrn_kernel(q_ref, f_ref, i_ref, g_ref, lb_ref, ng_ref, s0_ref, o_ref, so_ref, st_ref,
                 *, chunk, sub, t_valid):
    n = pl.program_id(1)
    nsub = chunk // sub

    @pl.when(n == 0)
    def _():
        for h in range(B_HEADS):
            st_ref[h] = s0_ref[0, h].T

    lbx = lb_ref[...]
    lbe = jnp.exp(lbx - jnp.max(lbx, axis=0, keepdims=True))
    lb_all = lbe[0:1] / jnp.sum(lbe, axis=0, keepdims=True)
    row = lax.broadcasted_iota(I32, (chunk, B_HEAD_DIM), 0)
    valid = row < t_valid
    tt = lax.broadcasted_iota(I32, (chunk, nsub * chunk), 0)
    cc = lax.broadcasted_iota(I32, (chunk, nsub * chunk), 1)
    pair_ok = ((cc // chunk) == (tt // sub)) & ((cc % chunk) <= tt)
    ng = ng_ref[...]

    for h in range(B_HEADS):
        sl = slice(h * B_HEAD_DIM, (h + 1) * B_HEAD_DIM)
        lb = lb_all[:, sl]
        f = lb + (1.0 - lb) * jax.nn.sigmoid(f_ref[:, sl])
        logf = jnp.where(valid, jnp.log(f), 0.0)
        kk = jnp.where(valid, 1.0 - f, 0.0)
        q = q_ref[:, sl]
        iv = i_ref[:, sl]
        cum = _cumsum_rows(logf)
        last = cum[chunk - 1:chunk]
        st = st_ref[h]

        o = lax.dot_general((q * jnp.exp(cum)).astype(BF16), st.astype(BF16),
                            (((1,), (1,)), ((), ())), preferred_element_type=F32)

        refs = [jnp.zeros((1, B_HEAD_DIM), F32)] + [cum[s * sub - 1:s * sub] for s in range(1, nsub)]
        ref_row = refs[0]
        for s in range(1, nsub):
            ref_row = jnp.where(row >= s * sub, refs[s], ref_row)
        qt = (q * jnp.exp(cum - ref_row)).astype(BF16)
        kcat = jnp.concatenate(
            [(kk * jnp.exp(jnp.minimum(r - cum, EXP_CLAMP))).astype(BF16) for r in refs], axis=0)
        a = lax.dot_general(qt, kcat, (((1,), (1,)), ((), ())), preferred_element_type=F32)
        p = jnp.where(pair_ok, a, 0.0).astype(BF16)
        icat = jnp.concatenate([iv.astype(BF16)] * nsub, axis=0)
        o = o + jnp.dot(p, icat, preferred_element_type=F32)

        kd = (kk * jnp.exp(last - cum)).astype(BF16)
        upd = lax.dot_general(iv.astype(BF16), kd, (((0,), (0,)), ((), ())), preferred_element_type=F32)
        st_ref[h] = st * jnp.exp(last) + upd

        on = o * lax.rsqrt(jnp.mean(o * o, axis=-1, keepdims=True) + RMS_EPS) * ng
        gate = g_ref[:, sl]
        o_ref[:, sl] = (on * (gate * jax.nn.sigmoid(gate))).astype(o_ref.dtype)

    @pl.when(n == pl.num_programs(1) - 1)
    def _():
        for h in range(B_HEADS):
            so_ref[0, h] = st_ref[h].T


def hgrn(zsrc, colblk, lb, ng, s0, batch, t_pad, chunk, sub, t_valid):
    nch = t_pad // chunk
    kern = functools.partial(_hgrn_kernel, chunk=chunk, sub=sub, t_valid=t_valid)

    def zspec(cb):
        return pl.BlockSpec((chunk, B_W), lambda b, n: (b * nch + n, cb))

    return pl.pallas_call(
        kern,
        grid=(batch, nch),
        in_specs=[zspec(colblk[0]), zspec(colblk[1]), zspec(colblk[2]), zspec(colblk[3]),
                  pl.BlockSpec(lb.shape, lambda b, n: (0, 0)),
                  pl.BlockSpec((1, B_HEAD_DIM), lambda b, n: (0, 0)),
                  pl.BlockSpec((1, B_HEADS, B_HEAD_DIM, B_HEAD_DIM), lambda b, n: (b, 0, 0, 0))],
        out_specs=[pl.BlockSpec((chunk, B_W), lambda b, n: (b * nch + n, 0)),
                   pl.BlockSpec((1, B_HEADS, B_HEAD_DIM, B_HEAD_DIM), lambda b, n: (b, 0, 0, 0))],
        out_shape=[jax.ShapeDtypeStruct((batch * t_pad, B_W), BF16),
                   jax.ShapeDtypeStruct((batch, B_HEADS, B_HEAD_DIM, B_HEAD_DIM), F32)],
        scratch_shapes=[pltpu.VMEM((B_HEADS, B_HEAD_DIM, B_HEAD_DIM), F32)],
        compiler_params=_cp(("parallel", "arbitrary"), 32),
        name="hgrn",
    )(zsrc, zsrc, zsrc, zsrc, lb, ng, s0)


def _mem_attn_kernel(q_ref, mk_ref, mv_ref, o_ref):
    scale = M_HEAD_DIM ** -0.5
    for h in range(M_HEADS):
        sl = slice(h * M_HEAD_DIM, (h + 1) * M_HEAD_DIM)
        q = q_ref[0, :, sl].astype(BF16)
        mk = mk_ref[0, :, sl].astype(BF16)
        mv = mv_ref[0, :, sl].astype(BF16)
        s = lax.dot_general(q, mk, (((1,), (1,)), ((), ())), preferred_element_type=F32) * scale
        p = jnp.exp(s - jnp.max(s, axis=-1, keepdims=True))
        p = p / jnp.sum(p, axis=-1, keepdims=True)
        o_ref[0, :, sl] = jnp.dot(p.astype(BF16), mv, preferred_element_type=F32).astype(o_ref.dtype)


def mem_attention(q3, qcolblk, mk3, mkcolblk, mv3, mvcolblk, batch, t, tq):
    m = mk3.shape[1]
    return pl.pallas_call(
        _mem_attn_kernel,
        grid=(batch, t // tq),
        in_specs=[pl.BlockSpec((1, tq, M_W), lambda b, i: (b, i, qcolblk)),
                  pl.BlockSpec((1, m, M_W), lambda b, i: (b, 0, mkcolblk)),
                  pl.BlockSpec((1, m, M_W), lambda b, i: (b, 0, mvcolblk))],
        out_specs=pl.BlockSpec((1, tq, M_W), lambda b, i: (b, i, 0)),
        out_shape=jax.ShapeDtypeStruct((batch, t, M_W), BF16),
        compiler_params=_cp(("parallel", "parallel"), 32),
        name="mem_attention",
    )(q3, mk3, mv3)


def _merge_kernel(oa_ref, ob_ref, om_ref, pa_ref, pb_ref, pm_ref, ga_ref, gb_ref, gm_ref, o_ref):
    acc = jax.nn.sigmoid(ga_ref[...]) * jnp.dot(oa_ref[...], pa_ref[...], preferred_element_type=F32)
    acc += jax.nn.sigmoid(gb_ref[...]) * jnp.dot(ob_ref[...], pb_ref[...], preferred_element_type=F32)
    acc += jax.nn.sigmoid(gm_ref[...]) * jnp.dot(om_ref[...], pm_ref[...], preferred_element_type=F32)
    o_ref[...] = acc.astype(o_ref.dtype)


def merge(oa, ob, om, pa, pb, pm, z, col, tm, tn):
    nt = oa.shape[0]
    d = pa.shape[1]

    def act(w):
        return pl.BlockSpec((tm, w), lambda j, i: (i, 0))

    def wgt(w):
        return pl.BlockSpec((w, tn), lambda j, i: (0, j))

    def gate(name):
        return pl.BlockSpec((tm, tn), lambda j, i: (i, col[name] // tn + j))

    return pl.pallas_call(
        _merge_kernel,
        grid=(d // tn, nt // tm),
        in_specs=[act(oa.shape[1]), act(ob.shape[1]), act(om.shape[1]),
                  wgt(pa.shape[0]), wgt(pb.shape[0]), wgt(pm.shape[0]),
                  gate("ga"), gate("gb"), gate("gm")],
        out_specs=pl.BlockSpec((tm, tn), lambda j, i: (i, j)),
        out_shape=jax.ShapeDtypeStruct((nt, d), BF16),
        compiler_params=_cp(("parallel", "parallel"), 48),
        name="merge",
    )(oa, ob, om, pa, pb, pm, z, z, z)


def _resid_mm_kernel(a_ref, b_ref, x_ref, xt_ref, o_ref, *, alpha, n_xtiles):
    x = jnp.where(pl.program_id(1) < n_xtiles, x_ref[...], xt_ref[...])
    o_ref[...] = alpha * x + jnp.dot(a_ref[...], b_ref[...], preferred_element_type=F32)


def resid_matmul(a, b, x, x_tail, alpha, tm, tn):
    m, k = a.shape
    n = b.shape[1]
    n_xtiles = x.shape[0] // tm
    assert x.shape[0] % tm == 0 and m // tm == n_xtiles + 1
    return pl.pallas_call(
        functools.partial(_resid_mm_kernel, alpha=alpha, n_xtiles=n_xtiles),
        grid=(n // tn, m // tm),
        in_specs=[pl.BlockSpec((tm, k), lambda j, i: (i, 0)),
                  pl.BlockSpec((k, tn), lambda j, i: (0, j)),
                  pl.BlockSpec((tm, tn), lambda j, i: (jnp.minimum(i, n_xtiles - 1), j)),
                  pl.BlockSpec((tm, tn), lambda j, i: (0, j))],
        out_specs=pl.BlockSpec((tm, tn), lambda j, i: (i, j)),
        out_shape=jax.ShapeDtypeStruct((m, n), F32),
        compiler_params=_cp(("parallel", "parallel"), 48),
        name="resid_matmul",
    )(a, b, x, x_tail)


def _ln(x, g, b):
    mu = jnp.mean(x, axis=-1, keepdims=True)
    xc = x - mu
    var = jnp.mean(xc * xc, axis=-1, keepdims=True)
    return xc * lax.rsqrt(var + LN_EPS) * g + b


def _ln_kernel(x_ref, g_ref, b_ref, o_ref, ob_ref):
    y = _ln(x_ref[...], g_ref[...], b_ref[...])
    o_ref[...] = y
    ob_ref[...] = y.astype(ob_ref.dtype)


def layer_norm_dual(x, g, b, tm):
    n, d = x.shape
    row = pl.BlockSpec((tm, d), lambda i: (i, 0))
    vec = pl.BlockSpec((1, d), lambda i: (0, 0))
    return pl.pallas_call(
        _ln_kernel,
        grid=(n // tm,),
        in_specs=[row, vec, vec],
        out_specs=[row, row],
        out_shape=[jax.ShapeDtypeStruct((n, d), F32), jax.ShapeDtypeStruct((n, d), BF16)],
        compiler_params=_cp(("parallel",), 48),
        name="layer_norm1",
    )(x, g, b)


def _ln_resid_kernel(h_ref, p_ref, g_ref, b_ref, op_ref, os_ref, *, alpha, n_ptiles):
    i = pl.program_id(0)
    y = _ln(alpha * h_ref[...] + p_ref[...], g_ref[...], b_ref[...])

    @pl.when(i < n_ptiles)
    def _():
        op_ref[...] = y

    @pl.when(i == n_ptiles)
    def _():
        os_ref[...] = y[:os_ref.shape[0]]


def layer_norm_resid(h, p, g, b, alpha, tm, n_p, n_s):
    d = h.shape[1]
    n_ptiles = n_p // tm
    assert n_p % tm == 0 and n_s <= tm
    row = pl.BlockSpec((tm, d), lambda i: (i, 0))
    vec = pl.BlockSpec((1, d), lambda i: (0, 0))
    return pl.pallas_call(
        functools.partial(_ln_resid_kernel, alpha=alpha, n_ptiles=n_ptiles),
        grid=(n_ptiles + 1,),
        in_specs=[row, row, vec, vec],
        out_specs=[pl.BlockSpec((tm, d), lambda i: (jnp.minimum(i, n_ptiles - 1), 0)),
                   pl.BlockSpec((n_s, d), lambda i: (0, 0))],
        out_shape=[jax.ShapeDtypeStruct((n_p, d), F32), jax.ShapeDtypeStruct((n_s, d), F32)],
        compiler_params=_cp(("arbitrary",), 48),
        name="layer_norm2",
    )(h, p, g, b)


def _top_ranked(x, n_top, vals_ref):
    kdim = x.shape[0]
    idx = lax.broadcasted_iota(I32, x.shape, 0).astype(F32)

    def body(a, carry):
        x, rank = carry
        m = jnp.max(x, axis=0, keepdims=True)
        first = jnp.min(jnp.where(x == m, idx, float(kdim)), axis=0, keepdims=True)
        sel = idx == first
        vals_ref[pl.ds(a, 1), :] = m
        return jnp.where(sel, -jnp.inf, x), jnp.where(sel, lax.convert_element_type(a, F32), rank)

    _, rank = lax.fori_loop(0, n_top, body, (x, jnp.full(x.shape, float(n_top), F32)))
    return rank, [vals_ref[a:a + 1, :] for a in range(n_top)]


def _top_ranked_distinct(x, n_top):
    rank = jnp.full(x.shape, float(n_top), F32)
    vals = []
    for a in range(n_top):
        m = jnp.max(x, axis=0, keepdims=True)
        sel = x == m
        rank = jnp.where(sel, float(a), rank)
        x = jnp.where(sel, -jnp.inf, x)
        vals.append(m)
    n_ranked = jnp.sum(jnp.where(rank < float(n_top), 1.0, 0.0), axis=0, keepdims=True)
    return rank, vals, n_ranked


def _peer_route_kernel(pq_ref, k1_ref, k2_ref, rank2_ref, cnt1_ref, e1_ref, e2_ref, vals_ref):
    tn = pq_ref.shape[0]
    refs = (pq_ref, k1_ref, k2_ref, rank2_ref, cnt1_ref, e1_ref, e2_ref, vals_ref)
    for h in range(PEER_HEADS):
        irregular = _peer_route_head(h, *refs, distinct=True)
        n_irregular = jnp.sum(irregular.reshape(tn // LANES, LANES), axis=0, keepdims=True)

        @pl.when(jnp.max(n_irregular) > 0.0)
        def _():
            _peer_route_head(h, *refs, distinct=False)


def _peer_route_head(h, pq_ref, k1_ref, k2_ref, rank2_ref, cnt1_ref, e1_ref, e2_ref, vals_ref, *, distinct):
    half = PEER_DKEY // 2
    nt = (((1,), (1,)), ((), ()))
    irregular = jnp.zeros((1, pq_ref.shape[0]), F32)

    def top(x, slot):
        if not distinct:
            return _top_ranked(x, PEER_TOPK, vals_ref.at[slot]) + (None,)
        return _top_ranked_distinct(x, PEER_TOPK)

    q1 = pq_ref[:, h * PEER_DKEY:h * PEER_DKEY + half].astype(BF16)
    q2 = pq_ref[:, h * PEER_DKEY + half:(h + 1) * PEER_DKEY].astype(BF16)
    s1 = lax.dot_general(k1_ref[...].astype(BF16), q1, nt, preferred_element_type=F32)
    s2 = lax.dot_general(k2_ref[...].astype(BF16), q2, nt, preferred_element_type=F32)
    rank1, v1, n1 = top(s1, 0)
    rank2, v2, n2 = top(s2, 1)
    v2m = jnp.concatenate(v2, axis=0)
    cand = jnp.concatenate([v1[a] + v2m for a in range(PEER_TOPK)], axis=0)
    crank, _, nc = top(cand, 2)
    if distinct:
        for cnt in (n1, n2, nc):
            irregular = irregular + jnp.where(cnt == float(PEER_TOPK), 0.0, 1.0)
    chosen = jnp.where(crank < float(PEER_TOPK), 1.0, 0.0)
    cmax = v1[0] + v2[0]
    zsum = jnp.sum(chosen * jnp.exp(cand - cmax), axis=0, keepdims=True)
    cnt1 = jnp.zeros_like(s1)
    for a in range(PEER_TOPK):
        m_a = jnp.sum(chosen[a * PEER_TOPK:(a + 1) * PEER_TOPK], axis=0, keepdims=True)
        cnt1 = jnp.where(rank1 == float(a), m_a, cnt1)
    rank2_ref[h] = rank2.astype(rank2_ref.dtype)
    cnt1_ref[h] = cnt1
    e1_ref[h] = jnp.exp(s1 - v1[0])
    e2_ref[h] = (jnp.exp(s2 - v2[0]) / zsum).astype(e2_ref.dtype)
    return irregular


def peer_route(pq, k1, k2, tn):
    n = pq.shape[0]
    shape = (PEER_HEADS, PEER_NKEYS, n)
    ospec = pl.BlockSpec((PEER_HEADS, PEER_NKEYS, tn), lambda i: (0, 0, i))
    kspec = pl.BlockSpec(k1.shape, lambda i: (0, 0))
    return pl.pallas_call(
        _peer_route_kernel,
        grid=(n // tn,),
        in_specs=[pl.BlockSpec((tn, pq.shape[1]), lambda i: (i, 0)), kspec, kspec],
        out_specs=[ospec] * 4,
        out_shape=[jax.ShapeDtypeStruct(shape, BF16), jax.ShapeDtypeStruct(shape, F32),
                   jax.ShapeDtypeStruct(shape, F32), jax.ShapeDtypeStruct(shape, BF16)],
        scratch_shapes=[pltpu.VMEM((3, PEER_TOPK, tn), F32)],
        compiler_params=_cp(("parallel",), 32),
        name="peer_route",
    )(pq, k1, k2)


PEER_EROWS = 4


def _gelu_exact(x):
    return 0.5 * x * (1.0 + lax.erf(x * (2.0 ** -0.5)))


def _peer_dense_kernel(x_ref, u_ref, v_ref, rank2_ref, cnt1_ref, e1_ref, e2_ref, o_ref, z_ref, *, n_tiles):
    r = pl.program_id(1)
    rd, wr = (r + 1) % 2, r % 2

    @pl.when(r == 0)
    def _():
        o_ref[...] = jnp.zeros(o_ref.shape, o_ref.dtype)
        z_ref[...] = jnp.zeros(z_ref.shape, z_ref.dtype)

    o_ref[...] += lax.dot_general(z_ref[rd], v_ref[...], (((0,), (0,)), ((), ())), preferred_element_type=F32)
    at = lax.dot_general(u_ref[...], x_ref[...], (((1,), (1,)), ((), ())), preferred_element_type=F32)
    act = _gelu_exact(at).astype(BF16)
    tile = jnp.minimum(r, n_tiles - 1)
    for rr in range(PEER_EROWS):
        i1 = tile * PEER_EROWS + rr
        g = jnp.zeros((PEER_NKEYS, x_ref.shape[0]), BF16)
        for h in range(PEER_HEADS):
            cnt = cnt1_ref[h, pl.ds(i1, 1), :].astype(BF16)
            e1 = e1_ref[h, pl.ds(i1, 1), :].astype(BF16)
            g = g + jnp.where(rank2_ref[h] < cnt, e2_ref[h] * e1, jnp.zeros((), BF16))
        z_ref[wr, rr * PEER_NKEYS:(rr + 1) * PEER_NKEYS, :] = g * act[rr * PEER_NKEYS:(rr + 1) * PEER_NKEYS]


def peer_dense(xb, u, v, rank2, cnt1, e1, e2, tn):
    n, d = xb.shape
    ne = u.shape[0]
    te = PEER_EROWS * PEER_NKEYS
    n_tiles = ne // te
    aux = pl.BlockSpec((PEER_HEADS, PEER_NKEYS, tn), lambda j, r: (0, 0, j))
    return pl.pallas_call(
        functools.partial(_peer_dense_kernel, n_tiles=n_tiles),
        grid=(n // tn, n_tiles + 1),
        in_specs=[pl.BlockSpec((tn, d), lambda j, r: (j, 0)),
                  pl.BlockSpec((te, d), lambda j, r: (jnp.minimum(r, n_tiles - 1), 0)),
                  pl.BlockSpec((te, d), lambda j, r: (jnp.maximum(r - 1, 0), 0)),
                  aux, aux, aux, aux],
        out_specs=pl.BlockSpec((tn, d), lambda j, r: (j, 0)),
        out_shape=jax.ShapeDtypeStruct((n, d), F32),
        scratch_shapes=[pltpu.VMEM((2, te, tn), BF16)],
        compiler_params=_cp(("parallel", "arbitrary"), 60),
        name="peer_dense",
    )(xb, u, v, rank2, cnt1, e1, e2)


def kernel(x_prompt, x_sample, cache_k, cache_v, cache_idx_k, cache_mem_k, cache_mem_v, state_hgrn, page_table,
           mem_prompt, w_in, w_mem_kv, hgrn_lb, hgrn_norm_g, p_a, p_b, p_m, w_out, ln1_g, ln1_b, w_pq,
           peer_sub_k1, peer_sub_k2, peer_u, peer_v, ln2_g, ln2_b):
    depth = w_in.shape[0]
    assert depth == 1, "single trunk layer"
    bp, t, d = x_prompt.shape
    bs, ts, _ = x_sample.shape
    assert bp == 1 and d == D_MODEL and t % KEY_CHUNK == 0 and bs * ts <= Q_BLK and ts <= S_ROWS
    n_p, n_s = bp * t, bs * ts
    n_real = n_p + n_s
    nt = -(-n_real // TOK_TILE) * TOK_TILE
    n_pool = cache_k.shape[1]
    past_len = page_table.shape[1] * PAGE_SIZE
    alpha = (2 * depth) ** 0.25
    nb_p = n_p // Q_BLK

    assert n_p % TOK_TILE == 0 and nt == n_p + TOK_TILE
    xp, xs = x_prompt.reshape(n_p, d), x_sample.reshape(n_s, d)
    xb = _pad_rows(jnp.concatenate([xp.astype(BF16), xs.astype(BF16)], axis=0), nt)
    w_packed_t = pack_w_in_t(jnp.swapaxes(w_in[0], 0, 1))
    z = matmul_nt(xb, w_packed_t, TOK_TILE, Z_TILE_WIDE, name="proj_in")

    def zcols(name, width, lo, hi):
        return z[lo:hi, COLS[name]:COLS[name] + width]

    pos = jnp.concatenate([jnp.arange(t, dtype=I32), past_len + jnp.tile(jnp.arange(ts, dtype=I32), bs),
                           jnp.zeros((nt - n_real,), I32)])
    q_hm, iq_hm, k4, kb, v4, vb, ik_rope, ikb = rope_all(z, COLS, rope_tables(pos))
    nb = q_hm.shape[0]

    oa_p = dsa_prompt(iq_hm.reshape(nb, IDX_HEADS * Q_BLK, IDX_DIM), z, COLS,
                      q_hm.reshape(nb, A_HEADS * Q_BLK, A_HEAD_DIM), ikb[:n_p], kb[:n_p], vb[:n_p], t)

    def per_seq(a, rows):
        return _pad_rows(a.reshape((bs, ts) + a.shape[1:]), rows, axis=1)

    q_s = per_seq(q_hm[nb_p, :, :n_s].transpose(1, 0, 2), S_ROWS).transpose(0, 2, 1, 3)
    q_s = q_s.reshape(bs, A_HEADS * S_ROWS, A_HEAD_DIM)
    iq_s = per_seq(iq_hm[nb_p, :, :n_s].transpose(1, 0, 2), S_ROWS).transpose(0, 2, 1, 3)
    iq_s = iq_s.reshape(bs, IDX_HEADS * S_ROWS, IDX_DIM)
    iw_s = per_seq(z[n_p:n_real, COLS["ikw"] + IDX_DIM:COLS["ikw"] + IDX_DIM + IDX_HEADS], S_ROWS)
    ik_new = per_seq(ikb[n_p:n_real], PAGE_SIZE)
    k_new = per_seq(kb[n_p:n_real], PAGE_SIZE)
    v_new = per_seq(vb[n_p:n_real], PAGE_SIZE)
    keys, thr = dsa_sample_index(page_table, iq_s, iw_s, ik_new, jnp.swapaxes(cache_idx_k[0], 1, 2), ts)
    cache_rows = n_pool * PAGE_SIZE * A_KV_HEADS
    os_hm = dsa_sample_attend(page_table, keys, thr, q_s, k_new, v_new,
                              cache_k[0].reshape(cache_rows, A_HEAD_DIM), cache_v[0].reshape(cache_rows, A_HEAD_DIM))
    oa_s = os_hm.reshape(bs, A_HEADS, S_ROWS, A_HEAD_DIM)[:, :, :ts].transpose(0, 2, 1, 3).reshape(n_s, A_Q_W)
    o_a = _pad_rows(jnp.concatenate([oa_p, oa_s], axis=0), nt)

    b_blk = [COLS[c] // B_W for c in ("bq", "bf", "bi", "bg")]
    ng = hgrn_norm_g[0][None]
    ob_p, st_p = hgrn(z, b_blk, hgrn_lb, ng, jnp.zeros((bp, B_HEADS, B_HEAD_DIM, B_HEAD_DIM), F32),
                      bp, t, HGRN_CHUNK, HGRN_SUB, HGRN_CHUNK)
    zs_b = per_seq(z[n_p:n_real, COLS["bq"]:COLS["bq"] + 4 * B_W], S_ROWS).reshape(bs * S_ROWS, 4 * B_W)
    ob_s, st_s = hgrn(zs_b, [0, 1, 2, 3], hgrn_lb, ng, state_hgrn[0], bs, S_ROWS, S_ROWS, S_ROWS, ts)
    ob_s = ob_s.reshape(bs, S_ROWS, B_W)[:, :ts].reshape(n_s, B_W)
    o_b = _pad_rows(jnp.concatenate([ob_p[:n_p], ob_s], axis=0), nt)

    mem_kv = matmul(mem_prompt[0].astype(BF16), w_mem_kv[0].astype(BF16), mem_prompt.shape[1], Z_TILE, name="mem_kv")
    om_p = mem_attention(z[None], COLS["mq"] // M_W, mem_kv[None], 0, mem_kv[None], 1, bp, t, TOK_TILE)[0]
    zs_m = per_seq(zcols("mq", M_W, n_p, n_real), S_ROWS)
    n_mem = cache_mem_k.shape[2]
    om_s = mem_attention(zs_m, 0, cache_mem_k[0].reshape(bs, n_mem, M_W), 0,
                         cache_mem_v[0].reshape(bs, n_mem, M_W), 0, bs, S_ROWS, S_ROWS)
    o_m = _pad_rows(jnp.concatenate([om_p, om_s[:, :ts].reshape(n_s, M_W)], axis=0), nt)

    merged = merge(o_a, o_b, o_m, p_a[0].astype(BF16), p_b[0].astype(BF16), p_m[0].astype(BF16), z, COLS,
                   TOK_TILE, WIDE_TILE)
    h_pre = resid_matmul(merged, w_out[0].astype(BF16), xp, _pad_rows(xs, TOK_TILE), alpha, TOK_TILE, WIDE_TILE)
    h, hb = layer_norm_dual(h_pre, ln1_g[0][None], ln1_b[0][None], LN_TILE)

    pq = matmul(hb, w_pq[0].astype(BF16), TOK_TILE, Z_TILE, name="peer_query")
    rank2, cnt1, e1, e2 = peer_route(pq, peer_sub_k1[0], peer_sub_k2[0], LN_TILE)
    p_out = peer_dense(hb, peer_u[0].astype(BF16), peer_v[0].astype(BF16), rank2, cnt1, e1, e2, TOK_TILE)
    y_p, y_s = layer_norm_resid(h, p_out, ln2_g[0][None], ln2_b[0][None], alpha, LN_TILE, n_p, n_s)

    kv_p = (depth, bp, t, A_KV_HEADS, A_HEAD_DIM)
    kv_s = (depth, bs, ts, A_KV_HEADS, A_HEAD_DIM)
    mem_shape = (depth, bp, mem_prompt.shape[1], M_HEADS, M_HEAD_DIM)
    g4 = A_KV_HEADS
    return (y_p.reshape(bp, t, d), y_s.reshape(bs, ts, d),
            k4[:n_p * g4].reshape(kv_p), v4[:n_p * g4].reshape(kv_p), ik_rope[:n_p].reshape(depth, bp, t, IDX_DIM),
            mem_kv[:, :M_W].reshape(mem_shape), mem_kv[:, M_W:].reshape(mem_shape), st_p[None],
            k4[n_p * g4:n_real * g4].reshape(kv_s), v4[n_p * g4:n_real * g4].reshape(kv_s),
            ik_rope[n_p:n_real].reshape(depth, bs, ts, IDX_DIM), st_s[None])
```

```python
import functools
import math

import jax
import jax.numpy as jnp
import numpy as np
from jax import lax
from jax.experimental import pallas as pl
from jax.experimental.pallas import tpu as pltpu

F32 = jnp.float32
BF16 = jnp.bfloat16
I32 = jnp.int32

A_HEADS, A_KV_HEADS, A_HEAD_DIM = 16, 4, 128
IDX_HEADS, IDX_DIM = 32, 64
TOPK_MAX = 256
B_HEADS, B_HEAD_DIM = 8, 128
M_HEADS, M_HEAD_DIM = 4, 256
PEER_HEADS, PEER_NKEYS, PEER_DKEY, PEER_TOPK = 8, 128, 256, 16
PAGE_SIZE = 128
ROPE_THETA = 10000.0
LN_EPS = 1e-5
RMS_EPS = 1e-6
D_MODEL = 4096

A_Q_W = A_HEADS * A_HEAD_DIM
A_KV_W = A_KV_HEADS * A_HEAD_DIM
IDX_Q_W = IDX_HEADS * IDX_DIM
B_W = B_HEADS * B_HEAD_DIM
M_W = M_HEADS * M_HEAD_DIM

_SEGS = (("aq", A_Q_W), ("iq", IDX_Q_W), ("ak", A_KV_W), ("av", A_KV_W), ("bq", B_W), ("bf", B_W),
         ("bi", B_W), ("bg", B_W), ("mq", M_W), ("ga", D_MODEL), ("gb", D_MODEL), ("gm", D_MODEL),
         ("ikw", 128))
COLS = {}
_off = 0
for _name, _w in _SEGS:
    COLS[_name] = _off
    _off += _w
Z_TILE = 512
Z_WIDTH = -(-_off // Z_TILE) * Z_TILE
Z_TILE_WIDE = 1536
assert Z_WIDTH % Z_TILE_WIDE == 0

LANES = 128
SUBLANES = 8
Q_BLK = 256
KEY_CHUNK = 512
HGRN_CHUNK = 64
HGRN_SUB = 16
EXP_CLAMP = 80.0
Q_LOG2_SCALE = (A_HEAD_DIM ** -0.5) * math.log2(math.e)
INT_MIN = -2 ** 31
NEG_BIG = -1e30
TOK_TILE = 512
WIDE_TILE = 1024
LN_TILE = 256


def _cp(sem, vmem_mb):
    return pltpu.CompilerParams(dimension_semantics=sem, vmem_limit_bytes=vmem_mb * 2 ** 20)


def _whole_vmem():
    return pl.BlockSpec(memory_space=pltpu.VMEM)


def _pad_rows(a, n, axis=0):
    pad = [(0, 0)] * a.ndim
    pad[axis] = (0, n - a.shape[axis])
    return jnp.pad(a, pad)


def _mm_kernel(a_ref, b_ref, o_ref):
    o_ref[...] = jnp.dot(a_ref[...], b_ref[...], preferred_element_type=F32).astype(o_ref.dtype)


def matmul(a, b, tm, tn, out_dtype=F32, name="matmul"):
    m, k = a.shape
    n = b.shape[1]
    return pl.pallas_call(
        _mm_kernel,
        grid=(m // tm, n // tn),
        in_specs=[pl.BlockSpec((tm, k), lambda i, j: (i, 0)), pl.BlockSpec((k, tn), lambda i, j: (0, j))],
        out_specs=pl.BlockSpec((tm, tn), lambda i, j: (i, j)),
        out_shape=jax.ShapeDtypeStruct((m, n), out_dtype),
        compiler_params=_cp(("parallel", "parallel"), 48),
        name=name,
    )(a, b)


def _mm_nt_kernel(a_ref, bt_ref, o_ref):
    o_ref[...] = lax.dot_general(a_ref[...], bt_ref[...], (((1,), (1,)), ((), ())),
                                 preferred_element_type=F32).astype(o_ref.dtype)


def matmul_nt(a, bt, tm, tn, out_dtype=F32, name="matmul_nt"):
    m, k = a.shape
    n = bt.shape[0]
    return pl.pallas_call(
        _mm_nt_kernel,
        grid=(n // tn, m // tm),
        in_specs=[pl.BlockSpec((tm, k), lambda j, i: (i, 0)), pl.BlockSpec((tn, k), lambda j, i: (j, 0))],
        out_specs=pl.BlockSpec((tm, tn), lambda j, i: (i, j)),
        out_shape=jax.ShapeDtypeStruct((m, n), out_dtype),
        compiler_params=_cp(("parallel", "parallel"), 48),
        name=name,
    )(a, bt)


CAST_ROWS = 128


def _mm_nt_cast_kernel(a_ref, bt_ref, u_ref, v_ref, o_ref, ub_ref, vb_ref):
    o_ref[...] = lax.dot_general(a_ref[...], bt_ref[...], (((1,), (1,)), ((), ())),
                                 preferred_element_type=F32)
    ub_ref[...] = u_ref[...].astype(ub_ref.dtype)
    vb_ref[...] = v_ref[...].astype(vb_ref.dtype)


def matmul_nt_with_casts(a, bt, tm, tn, u, v, name):
    m, k = a.shape
    n = bt.shape[0]
    n_i = m // tm
    n_cast = u.shape[0] // CAST_ROWS
    assert u.shape == v.shape and u.shape[0] % CAST_ROWS == 0 and n_cast <= (n // tn) * n_i

    def tab(j, i):
        return (jnp.minimum(j * n_i + i, n_cast - 1), 0)

    tspec = pl.BlockSpec((CAST_ROWS, u.shape[1]), tab)
    return pl.pallas_call(
        _mm_nt_cast_kernel,
        grid=(n // tn, n_i),
        in_specs=[pl.BlockSpec((tm, k), lambda j, i: (i, 0)), pl.BlockSpec((tn, k), lambda j, i: (j, 0)),
                  tspec, tspec],
        out_specs=[pl.BlockSpec((tm, tn), lambda j, i: (i, j)), tspec, tspec],
        out_shape=[jax.ShapeDtypeStruct((m, n), F32), jax.ShapeDtypeStruct(u.shape, BF16),
                   jax.ShapeDtypeStruct(v.shape, BF16)],
        compiler_params=_cp(("arbitrary", "arbitrary"), 56),
        name=name,
    )(a, bt, u, v)


PACK_SHIFT = IDX_DIM + IDX_HEADS
PACK_T_ROWS = 256


def _pack_t_plan():
    r = PACK_T_ROWS
    splits = (A_Q_W, A_KV_W, A_KV_W, IDX_Q_W, IDX_DIM, IDX_HEADS, 4 * B_W + M_W + 3 * D_MODEL)
    src = dict(zip(("aq", "ak", "av", "iq", "ik", "iw", "rest"), np.concatenate([[0], np.cumsum(splits)[:-1]])))
    assert src["ik"] % r == 0 and src["rest"] == src["ik"] + PACK_SHIFT and splits[-1] % r == 0
    plan = []
    for name in ("aq", "iq", "ak", "av"):
        width = A_Q_W if name in ("aq", "iq") else A_KV_W
        plan += [(0, (src[name] + o) // r, (COLS[name] + o) // r) for o in range(0, width, r)]
    plan += [(2, src["ik"] // r, COLS["ikw"] // r)]
    plan += [(1, src["ik"] // r + 1 + t, COLS["bq"] // r + t) for t in range(splits[-1] // r)]
    plan += [(3, 0, t) for t in range(COLS["ikw"] // r + 1, Z_WIDTH // r)]
    assert sorted(p[2] for p in plan) == list(range(Z_WIDTH // r))
    return np.array(plan, np.int32)


def _pack_t_kernel(plan_ref, a_ref, o_ref, tail_ref):
    mode = plan_ref[pl.program_id(0), 0]
    keep = PACK_T_ROWS - PACK_SHIFT

    @pl.when(mode == 0)
    def _():
        o_ref[...] = a_ref[...].astype(o_ref.dtype)

    @pl.when(mode == 1)
    def _():
        o_ref[:keep, :] = tail_ref[...].astype(o_ref.dtype)
        o_ref[keep:, :] = a_ref[:PACK_SHIFT, :].astype(o_ref.dtype)
        tail_ref[...] = a_ref[PACK_SHIFT:, :]

    @pl.when(mode == 2)
    def _():
        o_ref[...] = jnp.zeros(o_ref.shape, o_ref.dtype)
        o_ref[:PACK_SHIFT, :] = a_ref[:PACK_SHIFT, :].astype(o_ref.dtype)
        tail_ref[...] = a_ref[PACK_SHIFT:, :]

    @pl.when(mode == 3)
    def _():
        o_ref[...] = jnp.zeros(o_ref.shape, o_ref.dtype)


def pack_w_in_t(wt):
    dm = wt.shape[1]
    plan = _pack_t_plan()
    n_src = -(-wt.shape[0] // PACK_T_ROWS)
    grid_spec = pltpu.PrefetchScalarGridSpec(
        num_scalar_prefetch=1,
        grid=(len(plan),),
        in_specs=[pl.BlockSpec((PACK_T_ROWS, dm), lambda j, p: (jnp.minimum(p[j, 1], n_src - 1), 0))],
        out_specs=pl.BlockSpec((PACK_T_ROWS, dm), lambda j, p: (p[j, 2], 0)),
        scratch_shapes=[pltpu.VMEM((PACK_T_ROWS - PACK_SHIFT, dm), F32)],
    )
    return pl.pallas_call(
        _pack_t_kernel,
        grid_spec=grid_spec,
        out_shape=jax.ShapeDtypeStruct((Z_WIDTH, dm), BF16),
        compiler_params=_cp(("arbitrary",), 32),
        name="pack_w_in",
    )(jnp.asarray(plan), wt)


def _rope_kernel(q_ref, iq_ref, k_ref, v_ref, ikw_ref, c128_ref, s128_ref, c64_ref, s64_ref,
                 qhm_ref, iqhm_ref, k4_ref, kb_ref, v4_ref, vb_ref, iko_ref, ikb_ref):
    c128, s128 = c128_ref[...], s128_ref[...]
    c64, s64 = c64_ref[...], s64_ref[...]
    lane = lax.broadcasted_iota(I32, (Q_BLK, LANES), 1)
    first_half = (lane % IDX_DIM) < (IDX_DIM // 2)

    def rope128(x):
        return x * c128 + pltpu.roll(x, A_HEAD_DIM // 2, axis=1) * s128

    def rope64(x):
        partner = jnp.where(first_half, pltpu.roll(x, LANES - IDX_DIM // 2, axis=1),
                            pltpu.roll(x, IDX_DIM // 2, axis=1))
        return x * c64 + partner * s64

    for h in range(A_HEADS):
        qhm_ref[0, h] = (rope128(q_ref[:, h * LANES:(h + 1) * LANES]) * Q_LOG2_SCALE).astype(qhm_ref.dtype)
    for h in range(A_KV_HEADS):
        rows = pl.ds(h, Q_BLK, stride=A_KV_HEADS)
        kr = rope128(k_ref[:, h * LANES:(h + 1) * LANES])
        k4_ref[rows, :] = kr
        kb_ref[:, h * LANES:(h + 1) * LANES] = kr.astype(kb_ref.dtype)
        vh = v_ref[:, h * LANES:(h + 1) * LANES]
        v4_ref[rows, :] = vh
        vb_ref[:, h * LANES:(h + 1) * LANES] = vh.astype(vb_ref.dtype)
    for p in range(IDX_HEADS // 2):
        r = rope64(iq_ref[:, p * LANES:(p + 1) * LANES]).astype(iqhm_ref.dtype)
        iqhm_ref[0, 2 * p] = r[:, :IDX_DIM]
        iqhm_ref[0, 2 * p + 1] = r[:, IDX_DIM:]
    ikr = rope64(ikw_ref[...])[:, :IDX_DIM]
    iko_ref[...] = ikr
    ikb_ref[...] = ikr.astype(ikb_ref.dtype)


def rope_all(z, col, tabs):
    nt = z.shape[0]
    nb = nt // Q_BLK
    c128, s128, c64, s64 = tabs
    tab_spec = pl.BlockSpec((Q_BLK, LANES), lambda i: (i, 0))
    kv4_spec = pl.BlockSpec((Q_BLK * A_KV_HEADS, A_HEAD_DIM), lambda i: (i, 0))
    kvb_spec = pl.BlockSpec((Q_BLK, A_KV_W), lambda i: (i, 0))
    ik_spec = pl.BlockSpec((Q_BLK, IDX_DIM), lambda i: (i, 0))
    kv4 = jax.ShapeDtypeStruct((nt * A_KV_HEADS, A_HEAD_DIM), F32)
    kvb = jax.ShapeDtypeStruct((nt, A_KV_W), BF16)
    return pl.pallas_call(
        _rope_kernel,
        grid=(nb,),
        in_specs=[pl.BlockSpec((Q_BLK, A_Q_W), lambda i: (i, col["aq"] // A_Q_W)),
                  pl.BlockSpec((Q_BLK, IDX_Q_W), lambda i: (i, col["iq"] // IDX_Q_W)),
                  pl.BlockSpec((Q_BLK, A_KV_W), lambda i: (i, col["ak"] // A_KV_W)),
                  pl.BlockSpec((Q_BLK, A_KV_W), lambda i: (i, col["av"] // A_KV_W)),
                  pl.BlockSpec((Q_BLK, LANES), lambda i: (i, col["ikw"] // LANES)),
                  tab_spec, tab_spec, tab_spec, tab_spec],
        out_specs=[pl.BlockSpec((1, A_HEADS, Q_BLK, A_HEAD_DIM), lambda i: (i, 0, 0, 0)),
                   pl.BlockSpec((1, IDX_HEADS, Q_BLK, IDX_DIM), lambda i: (i, 0, 0, 0)),
                   kv4_spec, kvb_spec, kv4_spec, kvb_spec, ik_spec, ik_spec],
        out_shape=[jax.ShapeDtypeStruct((nb, A_HEADS, Q_BLK, A_HEAD_DIM), BF16),
                   jax.ShapeDtypeStruct((nb, IDX_HEADS, Q_BLK, IDX_DIM), BF16),
                   kv4, kvb, kv4, kvb,
                   jax.ShapeDtypeStruct((nt, IDX_DIM), F32), jax.ShapeDtypeStruct((nt, IDX_DIM), BF16)],
        compiler_params=_cp(("parallel",), 32),
        name="rope",
    )(z, z, z, z, z, c128, s128, c64, s64)


def rope_tables(pos):
    def tab(half, reps):
        inv_freq = ROPE_THETA ** (-jnp.arange(half, dtype=F32) / half)
        ang = pos.astype(F32)[:, None] * inv_freq[None, :]
        c, s = jnp.cos(ang), jnp.sin(ang)
        return jnp.tile(jnp.concatenate([c, c], 1), (1, reps)), jnp.tile(jnp.concatenate([-s, s], 1), (1, reps))
    c128, s128 = tab(A_HEAD_DIM // 2, 1)
    c64, s64 = tab(IDX_DIM // 2, 2)
    return c128, s128, c64, s64


def _sort_key(x):
    i = pltpu.bitcast(x, I32)
    return i ^ ((i >> 31) & 0x7FFFFFFF)


WORD = 32
GROUP_CHUNKS = WORD * LANES // KEY_CHUNK
_BIT_MASKS = ((16, 0x0000FFFF), (8, 0x00FF00FF), (4, 0x0F0F0F0F), (2, 0x33333333), (1, 0x55555555))


def _transpose_bits(w):
    w = list(w)
    for d, m in _BIT_MASKS:
        mask = jnp.int32(m)
        for j in range(WORD):
            if j & d:
                continue
            lo, hi = w[j], w[j + d]
            t = (lax.shift_right_logical(lo, jnp.int32(d)) ^ hi) & mask
            w[j + d] = hi ^ t
            w[j] = lo ^ lax.shift_left(t, jnp.int32(d))
    return w


def _kth_largest_bitsliced(key_ref, plane_ref, cand_ref, nch, k):
    rows = key_ref.shape[1]
    tiles_per_chunk = KEY_CHUNK // LANES
    ngroups = (nch + GROUP_CHUNKS - 1) // GROUP_CHUNKS

    def build_group(g, carry):
        def build_slab(s, c2):
            r0 = pl.multiple_of(s * SUBLANES, SUBLANES)
            words = []
            for j in range(WORD):
                c = g * GROUP_CHUNKS + j // tiles_per_chunk
                lt = j % tiles_per_chunk
                cc = jnp.minimum(c, key_ref.shape[0] - 1)
                w = key_ref[cc, pl.ds(r0, SUBLANES), lt * LANES:(lt + 1) * LANES] ^ INT_MIN
                words.append(jnp.where(c < nch, w, 0))
            planes = _transpose_bits(words)
            for b in range(WORD):
                plane_ref[g, b, pl.ds(r0, SUBLANES), :] = planes[b]
            return c2
        lax.fori_loop(0, rows // SUBLANES, build_slab, 0)
        cand_ref[g] = jnp.full((rows, LANES), -1, I32)
        return carry

    lax.fori_loop(0, ngroups, build_group, 0)

    def bit_step(it, carry):
        t_u, k_rem = carry
        b = WORD - 1 - it

        def count(g, cnt):
            return cnt + lax.population_count(cand_ref[g] & plane_ref[g, b])
        cnt = lax.fori_loop(0, ngroups, count, jnp.zeros((rows, LANES), I32))
        c1 = jnp.sum(cnt, axis=1, keepdims=True)
        take = c1 >= k_rem
        take_b = jnp.broadcast_to(take, (rows, LANES))

        def update(g, c2):
            e = cand_ref[g]
            a = e & plane_ref[g, b]
            cand_ref[g] = jnp.where(take_b, a, e ^ a)
            return c2
        lax.fori_loop(0, ngroups, update, 0)
        t_u = jnp.where(take, t_u | jnp.left_shift(jnp.int32(1), b), t_u)
        return t_u, jnp.where(take, k_rem, k_rem - c1)

    t_u, k_rem = lax.fori_loop(0, WORD, bit_step, (jnp.zeros((rows, 1), I32), jnp.full((rows, 1), k, I32)))
    return t_u ^ INT_MIN, k_rem


def _lower_surplus_ties(key_ref, cand_ref, nch, t, n_wanted):
    rows = key_ref.shape[1]
    ngroups = (nch + GROUP_CHUNKS - 1) // GROUP_CHUNKS
    group_keys = WORD * LANES
    pos_bits = (key_ref.shape[0] * KEY_CHUNK - 1).bit_length()
    live = t > INT_MIN

    def count_tied(g, cnt):
        return cnt + lax.population_count(cand_ref[g])
    n_tied = jnp.sum(lax.fori_loop(0, ngroups, count_tied, jnp.zeros((rows, LANES), I32)), axis=1, keepdims=True)
    surplus = jnp.where(live, n_tied - n_wanted, 0)

    @pl.when(jnp.max(surplus) > 0)
    def _():
        lane = lax.broadcasted_iota(I32, (rows, LANES), 1)

        def tied_before(q):
            gq = q // group_keys
            jq = jnp.broadcast_to((q // LANES) % WORD, (rows, LANES))
            lq = q % LANES
            low = lax.shift_left(jnp.ones((rows, LANES), I32), jq) - 1

            def body(g, cnt):
                w = cand_ref[g]
                at_tile = jnp.where(lane < lq, lax.shift_right_logical(w, jq) & 1, 0)
                part = lax.population_count(w & low) + at_tile
                return cnt + jnp.where(g < gq, lax.population_count(w), jnp.where(g == gq, part, 0))
            cnt = lax.fori_loop(0, ngroups, body, jnp.zeros((rows, LANES), I32))
            return jnp.sum(cnt, axis=1, keepdims=True)

        def bit_step(it, q):
            cand = q | jnp.left_shift(jnp.int32(1), pos_bits - 1 - it)
            return jnp.where(tied_before(cand) < n_wanted, cand, q)
        last = lax.fori_loop(0, pos_bits, bit_step, jnp.zeros((rows, 1), I32))
        off = lax.broadcasted_iota(I32, (rows, KEY_CHUNK), 1)

        def rewrite(c, carry):
            keys = key_ref[c]
            drop = jnp.where(keys == t, 1, 0) * jnp.where(c * KEY_CHUNK + off > last, 1, 0) * jnp.where(live, 1, 0)
            key_ref[c] = keys - drop
            return carry
        lax.fori_loop(0, nch, rewrite, 0)


def _flash_step(s, mask, vext, m_ref, l_ref, acc_ref, g):
    hh, rr, ss = s.shape
    d = vext.shape[1] // 2
    s = jnp.where(mask[None], s, -jnp.inf).reshape(hh * rr, ss)
    m_old = m_ref[g]
    m_new = jnp.maximum(m_old, jnp.max(s, axis=-1, keepdims=True))
    p = jnp.exp2(s - jnp.concatenate([m_new] * (ss // LANES), axis=1))
    alpha = jnp.exp2(m_old - m_new)
    pv = jnp.dot(p.astype(BF16), vext, preferred_element_type=F32)
    acc_ref[g] = alpha * acc_ref[g] + pv[:, :d]
    l_ref[g] = alpha * l_ref[g] + pv[:, d:]
    m_ref[g] = m_new


IDX_HEADS_PER_DOT = 8


def _dsa_prompt_kernel(iq_ref, w_ref, q_ref, ik_ref, k_ref, v_ref, o_ref,
                       key_ref, plane_ref, cand_ref, wb_ref, m_ref, l_ref, acc_ref, *, topk, w_lane0):
    i = pl.program_id(0)
    nch = (i * Q_BLK + Q_BLK + KEY_CHUNK - 1) // KEY_CHUNK
    hpg = A_HEADS // A_KV_HEADS

    wt = w_ref[...] * ((IDX_DIM ** -0.5) * (IDX_HEADS ** -0.5))
    for h in range(IDX_HEADS):
        wb_ref[h] = jnp.broadcast_to(wt[:, w_lane0 + h:w_lane0 + h + 1], (Q_BLK, LANES))

    q_pos = i * Q_BLK + lax.broadcasted_iota(I32, (Q_BLK, KEY_CHUNK), 0)
    k_off = lax.broadcasted_iota(I32, (Q_BLK, KEY_CHUNK), 1)

    def score_chunk(c, carry):
        k0 = pl.multiple_of(c * KEY_CHUNK, KEY_CHUNK)
        ikc = ik_ref[pl.ds(k0, KEY_CHUNK), :]
        ntile = KEY_CHUNK // LANES
        acc = [jnp.zeros((Q_BLK, LANES), F32)] * ntile
        for hg in range(IDX_HEADS // IDX_HEADS_PER_DOT):
            lhs = iq_ref[0, hg * IDX_HEADS_PER_DOT * Q_BLK:(hg + 1) * IDX_HEADS_PER_DOT * Q_BLK, :]
            d = lax.dot_general(lhs, ikc, (((1,), (1,)), ((), ())), preferred_element_type=F32)
            for hl in range(IDX_HEADS_PER_DOT):
                wb = wb_ref[hg * IDX_HEADS_PER_DOT + hl]
                r = jnp.maximum(d[hl * Q_BLK:(hl + 1) * Q_BLK], 0.0)
                acc = [acc[j] + wb * r[:, j * LANES:(j + 1) * LANES] for j in range(ntile)]
        sc = jnp.concatenate(acc, axis=1)
        key_ref[c] = jnp.where(k0 + k_off <= q_pos, _sort_key(sc), INT_MIN)
        return carry

    lax.fori_loop(0, nch, score_chunk, 0)

    t, n_wanted = _kth_largest_bitsliced(key_ref, plane_ref, cand_ref, nch, topk)
    _lower_surplus_ties(key_ref, cand_ref, nch, t, n_wanted)
    thr = jnp.maximum(t, INT_MIN + 1)

    m_ref[...] = jnp.full(m_ref.shape, NEG_BIG, F32)
    l_ref[...] = jnp.zeros(l_ref.shape, F32)
    acc_ref[...] = jnp.zeros(acc_ref.shape, F32)
    ones = jnp.ones((KEY_CHUNK, A_HEAD_DIM), BF16)

    def attend_chunk(c, carry):
        k0 = pl.multiple_of(c * KEY_CHUNK, KEY_CHUNK)
        mask = key_ref[c] >= thr
        for g in range(A_KV_HEADS):
            qg = q_ref[0, g * hpg * Q_BLK:(g + 1) * hpg * Q_BLK, :]
            kc = k_ref[pl.ds(k0, KEY_CHUNK), g * LANES:(g + 1) * LANES]
            vc = v_ref[pl.ds(k0, KEY_CHUNK), g * LANES:(g + 1) * LANES]
            s = lax.dot_general(qg, kc, (((1,), (1,)), ((), ())), preferred_element_type=F32)
            _flash_step(s.reshape(hpg, Q_BLK, KEY_CHUNK), mask, jnp.concatenate([vc, ones], axis=1),
                        m_ref, l_ref, acc_ref, g)
        return carry

    lax.fori_loop(0, nch, attend_chunk, 0)
    for g in range(A_KV_HEADS):
        o = (acc_ref[g] / l_ref[g]).astype(o_ref.dtype)
        for hl in range(hpg):
            h = g * hpg + hl
            o_ref[:, h * A_HEAD_DIM:(h + 1) * A_HEAD_DIM] = o[hl * Q_BLK:(hl + 1) * Q_BLK]


def dsa_prompt(iq_hm, z, col, q_hm, ik, k, v, t):
    nb = t // Q_BLK
    topk = min(TOPK_MAX, t // 4)
    nch_max = (t + KEY_CHUNK - 1) // KEY_CHUNK
    ngroups_max = (nch_max + GROUP_CHUNKS - 1) // GROUP_CHUNKS
    hpg = A_HEADS // A_KV_HEADS
    kern = functools.partial(_dsa_prompt_kernel, topk=topk, w_lane0=IDX_DIM)
    return pl.pallas_call(
        kern,
        grid=(nb,),
        in_specs=[pl.BlockSpec((1, IDX_HEADS * Q_BLK, IDX_DIM), lambda i: (i, 0, 0)),
                  pl.BlockSpec((Q_BLK, LANES), lambda i: (i, col["ikw"] // LANES)),
                  pl.BlockSpec((1, A_HEADS * Q_BLK, A_HEAD_DIM), lambda i: (i, 0, 0)),
                  _whole_vmem(), _whole_vmem(), _whole_vmem()],
        out_specs=pl.BlockSpec((Q_BLK, A_Q_W), lambda i: (i, 0)),
        out_shape=jax.ShapeDtypeStruct((t, A_Q_W), BF16),
        scratch_shapes=[pltpu.VMEM((nch_max, Q_BLK, KEY_CHUNK), I32),
                        pltpu.VMEM((ngroups_max, WORD, Q_BLK, LANES), I32),
                        pltpu.VMEM((ngroups_max, Q_BLK, LANES), I32),
                        pltpu.VMEM((IDX_HEADS, Q_BLK, LANES), F32),
                        pltpu.VMEM((A_KV_HEADS, hpg * Q_BLK, LANES), F32),
                        pltpu.VMEM((A_KV_HEADS, hpg * Q_BLK, LANES), F32),
                        pltpu.VMEM((A_KV_HEADS, hpg * Q_BLK, A_HEAD_DIM), F32)],
        compiler_params=_cp(("arbitrary",), 60),
        name="dsa_prompt",
    )(iq_hm, z, q_hm, ik, k, v)


S_ROWS = SUBLANES
ATT_PAGES = 16


def _dsa_sample_scores_kernel(pt_ref, iq_ref, w_ref, ikn_ref, *rest, n_pages, t_new):
    pages = rest[:n_pages]
    key_ref = rest[n_pages]
    wt = w_ref[0] * ((IDX_DIM ** -0.5) * (IDX_HEADS ** -0.5))
    wb = [jnp.broadcast_to(wt[:, h:h + 1], (S_ROWS, LANES)) for h in range(IDX_HEADS)]
    iq = iq_ref[0]

    def tile_scores(ik_tile, transposed):
        dims = (((1,), (0,)), ((), ())) if transposed else (((1,), (1,)), ((), ()))
        d = lax.dot_general(iq, ik_tile, dims, preferred_element_type=F32)
        acc = jnp.zeros((S_ROWS, LANES), F32)
        for h in range(IDX_HEADS):
            acc = acc + wb[h] * jnp.maximum(d[h * S_ROWS:(h + 1) * S_ROWS], 0.0)
        return acc

    for r in range(n_pages):
        key_ref[0, r] = _sort_key(tile_scores(pages[r][0].astype(BF16), True))
    row = lax.broadcasted_iota(I32, (S_ROWS, LANES), 0)
    lane = lax.broadcasted_iota(I32, (S_ROWS, LANES), 1)
    valid = jnp.where(lane < t_new, lane, S_ROWS) <= row
    key_ref[0, n_pages] = jnp.where(valid, _sort_key(tile_scores(ikn_ref[0], False)), INT_MIN)


def _dsa_sample_thr_kernel(key_ref, adj_ref, thr_ref, last_ref, *, topk):
    b, n_tiles = key_ref.shape[0], key_ref.shape[1]
    pos_bits = (n_tiles * LANES - 1).bit_length()
    lane = lax.broadcasted_iota(I32, (b, S_ROWS, LANES), 2)

    def count(pred):
        def body(c, cnt):
            return cnt + jnp.where(pred(key_ref[:, c], c), 1, 0)
        cnt = lax.fori_loop(0, n_tiles, body, jnp.zeros((b, S_ROWS, LANES), I32))
        return jnp.sum(cnt, axis=2, keepdims=True)

    def bit_step(it, cur):
        cand = cur | jnp.left_shift(jnp.int32(1), 31 - it)
        return jnp.where(count(lambda x, c: x >= (cand ^ INT_MIN)) >= topk, cand, cur)

    t = lax.fori_loop(0, 32, bit_step, jnp.zeros((b, S_ROWS, 1), I32)) ^ INT_MIN
    live = t > INT_MIN
    n_above = count(lambda x, c: x > t)
    n_tied = count(lambda x, c: x == t)
    n_wanted = topk - n_above
    last_ref[...] = jnp.full(last_ref.shape, n_tiles * LANES, I32)

    @pl.when(jnp.max(jnp.where(live, n_tied - n_wanted, 0)) > 0)
    def _():
        def pos_step(it, q):
            cand = q | jnp.left_shift(jnp.int32(1), pos_bits - 1 - it)
            before = count(lambda x, c: jnp.where(x == t, 1, 0) * jnp.where(c * LANES + lane < cand, 1, 0) > 0)
            return jnp.where(before < n_wanted, cand, q)
        last = lax.fori_loop(0, pos_bits, pos_step, jnp.zeros((b, S_ROWS, 1), I32))
        last_ref[...] = jnp.broadcast_to(last, last_ref.shape)

    last = last_ref[...]
    live_i = jnp.where(live, 1, 0)

    def rewrite(c, carry):
        keys = key_ref[:, c]
        drop = jnp.where(keys == t, 1, 0) * jnp.where(c * LANES + lane > last, 1, 0) * live_i
        adj_ref[:, c] = keys - drop
        return carry
    lax.fori_loop(0, n_tiles, rewrite, 0)
    thr_ref[...] = jnp.broadcast_to(jnp.maximum(t, INT_MIN + 1), thr_ref.shape)


def dsa_sample_index(page_table, iq_s, iw_s, ik_new, cache_ik, t_new):
    b, n_pages = page_table.shape
    topk = min(TOPK_MAX, (n_pages * PAGE_SIZE + t_new) // 4)

    def page_spec(r):
        return pl.BlockSpec((1, IDX_DIM, PAGE_SIZE), lambda bi, pt: (pt[bi, r], 0, 0))

    grid_spec = pltpu.PrefetchScalarGridSpec(
        num_scalar_prefetch=1,
        grid=(b,),
        in_specs=[pl.BlockSpec((1, IDX_HEADS * S_ROWS, IDX_DIM), lambda bi, pt: (bi, 0, 0)),
                  pl.BlockSpec((1, S_ROWS, IDX_HEADS), lambda bi, pt: (bi, 0, 0)),
                  pl.BlockSpec((1, PAGE_SIZE, IDX_DIM), lambda bi, pt: (bi, 0, 0))]
                 + [page_spec(r) for r in range(n_pages)],
        out_specs=pl.BlockSpec((1, n_pages + 1, S_ROWS, LANES), lambda bi, pt: (bi, 0, 0, 0)),
    )
    keys = pl.pallas_call(
        functools.partial(_dsa_sample_scores_kernel, n_pages=n_pages, t_new=t_new),
        grid_spec=grid_spec,
        out_shape=jax.ShapeDtypeStruct((b, n_pages + 1, S_ROWS, LANES), I32),
        compiler_params=_cp(("parallel",), 32),
        name="dsa_sample_scores",
    )(page_table, iq_s, iw_s, ik_new, *([cache_ik] * n_pages))
    return pl.pallas_call(
        functools.partial(_dsa_sample_thr_kernel, topk=topk),
        in_specs=[_whole_vmem()],
        out_specs=[_whole_vmem(), _whole_vmem()],
        out_shape=[jax.ShapeDtypeStruct(keys.shape, I32), jax.ShapeDtypeStruct((b, S_ROWS, LANES), I32)],
        scratch_shapes=[pltpu.VMEM((b, S_ROWS, LANES), I32)],
        compiler_params=pltpu.CompilerParams(vmem_limit_bytes=32 * 2 ** 20),
        name="dsa_sample_threshold",
    )(keys)


def _dsa_sample_attend_kernel(pt_ref, key_ref, thr_ref, q_ref, kn_ref, vn_ref, *rest, n_pages):
    kpages = rest[:ATT_PAGES]
    vpages = rest[ATT_PAGES:2 * ATT_PAGES]
    o_ref, m_ref, l_ref, acc_ref = rest[2 * ATT_PAGES:]
    j = pl.program_id(1)
    hpg = A_HEADS // A_KV_HEADS
    thr = thr_ref[0]

    @pl.when(j == 0)
    def _():
        m_ref[...] = jnp.full(m_ref.shape, NEG_BIG, F32)
        l_ref[...] = jnp.zeros(l_ref.shape, F32)
        acc_ref[...] = jnp.zeros(acc_ref.shape, F32)

    def attend(mask, g, kg, vg):
        qg = q_ref[0, g * hpg * S_ROWS:(g + 1) * hpg * S_ROWS, :]
        s = lax.dot_general(qg, kg, (((1,), (1,)), ((), ())), preferred_element_type=F32)
        vext = jnp.concatenate([vg, jnp.ones(vg.shape, BF16)], axis=1)
        _flash_step(s.reshape(hpg, S_ROWS, s.shape[-1]), mask, vext, m_ref, l_ref, acc_ref, g)

    mask = jnp.concatenate([key_ref[0, j * ATT_PAGES + r] >= thr for r in range(ATT_PAGES)], axis=1)
    for g in range(A_KV_HEADS):
        rows = pl.ds(g, PAGE_SIZE, stride=A_KV_HEADS)
        kg = jnp.concatenate([kp[rows, :].astype(BF16) for kp in kpages], axis=0)
        vg = jnp.concatenate([vp[rows, :].astype(BF16) for vp in vpages], axis=0)
        attend(mask, g, kg, vg)

    @pl.when(j == pl.num_programs(1) - 1)
    def _():
        mask_new = key_ref[0, n_pages] >= thr
        for g in range(A_KV_HEADS):
            attend(mask_new, g, kn_ref[0, :, g * LANES:(g + 1) * LANES], vn_ref[0, :, g * LANES:(g + 1) * LANES])
        for g in range(A_KV_HEADS):
            o_ref[0, g * hpg * S_ROWS:(g + 1) * hpg * S_ROWS, :] = (acc_ref[g] / l_ref[g]).astype(o_ref.dtype)


def dsa_sample_attend(page_table, keys, thr, q_s, k_new, v_new, cache_k, cache_v):
    b, n_pages = page_table.shape
    steps = n_pages // ATT_PAGES
    hpg = A_HEADS // A_KV_HEADS
    page_rows = PAGE_SIZE * A_KV_HEADS

    def page_spec(r):
        return pl.BlockSpec((page_rows, A_HEAD_DIM), lambda bi, j, pt: (pt[bi, j * ATT_PAGES + r], 0))

    grid_spec = pltpu.PrefetchScalarGridSpec(
        num_scalar_prefetch=1,
        grid=(b, steps),
        in_specs=[pl.BlockSpec((1, n_pages + 1, S_ROWS, LANES), lambda bi, j, pt: (bi, 0, 0, 0)),
                  pl.BlockSpec((1, S_ROWS, LANES), lambda bi, j, pt: (bi, 0, 0)),
                  pl.BlockSpec((1, A_HEADS * S_ROWS, A_HEAD_DIM), lambda bi, j, pt: (bi, 0, 0)),
                  pl.BlockSpec((1, PAGE_SIZE, A_KV_W), lambda bi, j, pt: (bi, 0, 0)),
                  pl.BlockSpec((1, PAGE_SIZE, A_KV_W), lambda bi, j, pt: (bi, 0, 0))]
                 + [page_spec(r) for r in range(ATT_PAGES)] * 2,
        out_specs=pl.BlockSpec((1, A_HEADS * S_ROWS, A_HEAD_DIM), lambda bi, j, pt: (bi, 0, 0)),
        scratch_shapes=[pltpu.VMEM((A_KV_HEADS, hpg * S_ROWS, LANES), F32),
                        pltpu.VMEM((A_KV_HEADS, hpg * S_ROWS, LANES), F32),
                        pltpu.VMEM((A_KV_HEADS, hpg * S_ROWS, A_HEAD_DIM), F32)],
    )
    return pl.pallas_call(
        functools.partial(_dsa_sample_attend_kernel, n_pages=n_pages),
        grid_spec=grid_spec,
        out_shape=jax.ShapeDtypeStruct((b, A_HEADS * S_ROWS, A_HEAD_DIM), BF16),
        compiler_params=_cp(("parallel", "arbitrary"), 48),
        name="dsa_sample_attend",
    )(page_table, keys, thr, q_s, k_new, v_new, *([cache_k] * ATT_PAGES), *([cache_v] * ATT_PAGES))


def _cumsum_rows(x):
    n = x.shape[0]
    row = lax.broadcasted_iota(I32, x.shape, 0)
    d = 1
    while d < n:
        x = x + jnp.where(row >= d, pltpu.roll(x, d, axis=0), 0.0)
        d *= 2
    return x


def _hgrn_kernel(q_ref, f_ref, i_ref, g_ref, lb_ref, ng_ref, s0_ref, o_ref, so_ref, st_ref,
                 *, chunk, sub, t_valid):
    n = pl.program_id(1)
    nsub = chunk // sub

    @pl.when(n == 0)
    def _():
        for h in range(B_HEADS):
            st_ref[h] = s0_ref[0, h].T

    lbx = lb_ref[...]
    lbe = jnp.exp(lbx - jnp.max(lbx, axis=0, keepdims=True))
    lb_all = lbe[0:1] / jnp.sum(lbe, axis=0, keepdims=True)
    row = lax.broadcasted_iota(I32, (chunk, B_HEAD_DIM), 0)
    valid = row < t_valid
    tt = lax.broadcasted_iota(I32, (chunk, nsub * chunk), 0)
    cc = lax.broadcasted_iota(I32, (chunk, nsub * chunk), 1)
    pair_ok = ((cc // chunk) == (tt // sub)) & ((cc % chunk) <= tt)
    ng = ng_ref[...]

    for h in range(B_HEADS):
        sl = slice(h * B_HEAD_DIM, (h + 1) * B_HEAD_DIM)
        lb = lb_all[:, sl]
        f = lb + (1.0 - lb) * jax.nn.sigmoid(f_ref[:, sl])
        logf = jnp.where(valid, jnp.log(f), 0.0)
        kk = jnp.where(valid, 1.0 - f, 0.0)
        q = q_ref[:, sl]
        iv = i_ref[:, sl]
        cum = _cumsum_rows(logf)
        last = cum[chunk - 1:chunk]
        st = st_ref[h]

        o = lax.dot_general((q * jnp.exp(cum)).astype(BF16), st.astype(BF16),
                            (((1,), (1,)), ((), ())), preferred_element_type=F32)

        refs = [jnp.zeros((1, B_HEAD_DIM), F32)] + [cum[s * sub - 1:s * sub] for s in range(1, nsub)]
        ref_row = refs[0]
        for s in range(1, nsub):
            ref_row = jnp.where(row >= s * sub, refs[s], ref_row)
        qt = (q * jnp.exp(cum - ref_row)).astype(BF16)
        kcat = jnp.concatenate(
            [(kk * jnp.exp(jnp.minimum(r - cum, EXP_CLAMP))).astype(BF16) for r in refs], axis=0)
        a = lax.dot_general(qt, kcat, (((1,), (1,)), ((), ())), preferred_element_type=F32)
        p = jnp.where(pair_ok, a, 0.0).astype(BF16)
        icat = jnp.concatenate([iv.astype(BF16)] * nsub, axis=0)
        o = o + jnp.dot(p, icat, preferred_element_type=F32)

        kd = (kk * jnp.exp(last - cum)).astype(BF16)
        upd = lax.dot_general(iv.astype(BF16), kd, (((0,), (0,)), ((), ())), preferred_element_type=F32)
        st_ref[h] = st * jnp.exp(last) + upd

        on = o * lax.rsqrt(jnp.mean(o * o, axis=-1, keepdims=True) + RMS_EPS) * ng
        gate = g_ref[:, sl]
        o_ref[:, sl] = (on * (gate * jax.nn.sigmoid(gate))).astype(o_ref.dtype)

    @pl.when(n == pl.num_programs(1) - 1)
    def _():
        for h in range(B_HEADS):
            so_ref[0, h] = st_ref[h].T


def hgrn(zsrc, colblk, lb, ng, s0, batch, t_pad, chunk, sub, t_valid):
    nch = t_pad // chunk
    kern = functools.partial(_hgrn_kernel, chunk=chunk, sub=sub, t_valid=t_valid)

    def zspec(cb):
        return pl.BlockSpec((chunk, B_W), lambda b, n: (b * nch + n, cb))

    return pl.pallas_call(
        kern,
        grid=(batch, nch),
        in_specs=[zspec(colblk[0]), zspec(colblk[1]), zspec(colblk[2]), zspec(colblk[3]),
                  pl.BlockSpec(lb.shape, lambda b, n: (0, 0)),
                  pl.BlockSpec((1, B_HEAD_DIM), lambda b, n: (0, 0)),
                  pl.BlockSpec((1, B_HEADS, B_HEAD_DIM, B_HEAD_DIM), lambda b, n: (b, 0, 0, 0))],
        out_specs=[pl.BlockSpec((chunk, B_W), lambda b, n: (b * nch + n, 0)),
                   pl.BlockSpec((1, B_HEADS, B_HEAD_DIM, B_HEAD_DIM), lambda b, n: (b, 0, 0, 0))],
        out_shape=[jax.ShapeDtypeStruct((batch * t_pad, B_W), BF16),
                   jax.ShapeDtypeStruct((batch, B_HEADS, B_HEAD_DIM, B_HEAD_DIM), F32)],
        scratch_shapes=[pltpu.VMEM((B_HEADS, B_HEAD_DIM, B_HEAD_DIM), F32)],
        compiler_params=_cp(("parallel", "arbitrary"), 32),
        name="hgrn",
    )(zsrc, zsrc, zsrc, zsrc, lb, ng, s0)


def _mem_attn_kernel(q_ref, mk_ref, mv_ref, o_ref):
    scale = M_HEAD_DIM ** -0.5
    for h in range(M_HEADS):
        sl = slice(h * M_HEAD_DIM, (h + 1) * M_HEAD_DIM)
        q = q_ref[0, :, sl].astype(BF16)
        mk = mk_ref[0, :, sl].astype(BF16)
        mv = mv_ref[0, :, sl].astype(BF16)
        s = lax.dot_general(q, mk, (((1,), (1,)), ((), ())), preferred_element_type=F32) * scale
        p = jnp.exp(s - jnp.max(s, axis=-1, keepdims=True))
        p = p / jnp.sum(p, axis=-1, keepdims=True)
        o_ref[0, :, sl] = jnp.dot(p.astype(BF16), mv, preferred_element_type=F32).astype(o_ref.dtype)


def mem_attention(q3, qcolblk, mk3, mkcolblk, mv3, mvcolblk, batch, t, tq):
    m = mk3.shape[1]
    return pl.pallas_call(
        _mem_attn_kernel,
        grid=(batch, t // tq),
        in_specs=[pl.BlockSpec((1, tq, M_W), lambda b, i: (b, i, qcolblk)),
                  pl.BlockSpec((1, m, M_W), lambda b, i: (b, 0, mkcolblk)),
                  pl.BlockSpec((1, m, M_W), lambda b, i: (b, 0, mvcolblk))],
        out_specs=pl.BlockSpec((1, tq, M_W), lambda b, i: (b, i, 0)),
        out_shape=jax.ShapeDtypeStruct((batch, t, M_W), BF16),
        compiler_params=_cp(("parallel", "parallel"), 32),
        name="mem_attention",
    )(q3, mk3, mv3)


def _merge_kernel(oa_ref, ob_ref, om_ref, pa_ref, pb_ref, pm_ref, ga_ref, gb_ref, gm_ref, o_ref):
    acc = jax.nn.sigmoid(ga_ref[...]) * jnp.dot(oa_ref[...], pa_ref[...], preferred_element_type=F32)
    acc += jax.nn.sigmoid(gb_ref[...]) * jnp.dot(ob_ref[...], pb_ref[...], preferred_element_type=F32)
    acc += jax.nn.sigmoid(gm_ref[...]) * jnp.dot(om_ref[...], pm_ref[...], preferred_element_type=F32)
    o_ref[...] = acc.astype(o_ref.dtype)


def merge(oa, ob, om, pa, pb, pm, z, col, tm, tn):
    nt = oa.shape[0]
    d = pa.shape[1]

    def act(w):
        return pl.BlockSpec((tm, w), lambda j, i: (i, 0))

    def wgt(w):
        return pl.BlockSpec((w, tn), lambda j, i: (0, j))

    def gate(name):
        return pl.BlockSpec((tm, tn), lambda j, i: (i, col[name] // tn + j))

    return pl.pallas_call(
        _merge_kernel,
        grid=(d // tn, nt // tm),
        in_specs=[act(oa.shape[1]), act(ob.shape[1]), act(om.shape[1]),
                  wgt(pa.shape[0]), wgt(pb.shape[0]), wgt(pm.shape[0]),
                  gate("ga"), gate("gb"), gate("gm")],
        out_specs=pl.BlockSpec((tm, tn), lambda j, i: (i, j)),
        out_shape=jax.ShapeDtypeStruct((nt, d), BF16),
        compiler_params=_cp(("parallel", "parallel"), 48),
        name="merge",
    )(oa, ob, om, pa, pb, pm, z, z, z)


def _resid_mm_kernel(a_ref, b_ref, x_ref, xt_ref, o_ref, *, alpha, n_xtiles):
    x = jnp.where(pl.program_id(1) < n_xtiles, x_ref[...], xt_ref[...])
    o_ref[...] = alpha * x + jnp.dot(a_ref[...], b_ref[...], preferred_element_type=F32)


def resid_matmul(a, b, x, x_tail, alpha, tm, tn):
    m, k = a.shape
    n = b.shape[1]
    n_xtiles = x.shape[0] // tm
    assert x.shape[0] % tm == 0 and m // tm == n_xtiles + 1
    return pl.pallas_call(
        functools.partial(_resid_mm_kernel, alpha=alpha, n_xtiles=n_xtiles),
        grid=(n // tn, m // tm),
        in_specs=[pl.BlockSpec((tm, k), lambda j, i: (i, 0)),
                  pl.BlockSpec((k, tn), lambda j, i: (0, j)),
                  pl.BlockSpec((tm, tn), lambda j, i: (jnp.minimum(i, n_xtiles - 1), j)),
                  pl.BlockSpec((tm, tn), lambda j, i: (0, j))],
        out_specs=pl.BlockSpec((tm, tn), lambda j, i: (i, j)),
        out_shape=jax.ShapeDtypeStruct((m, n), F32),
        compiler_params=_cp(("parallel", "parallel"), 48),
        name="resid_matmul",
    )(a, b, x, x_tail)


def _ln(x, g, b):
    mu = jnp.mean(x, axis=-1, keepdims=True)
    xc = x - mu
    var = jnp.mean(xc * xc, axis=-1, keepdims=True)
    return xc * lax.rsqrt(var + LN_EPS) * g + b


def _ln_kernel(x_ref, g_ref, b_ref, o_ref, ob_ref):
    y = _ln(x_ref[...], g_ref[...], b_ref[...])
    o_ref[...] = y
    ob_ref[...] = y.astype(ob_ref.dtype)


def layer_norm_dual(x, g, b, tm):
    n, d = x.shape
    row = pl.BlockSpec((tm, d), lambda i: (i, 0))
    vec = pl.BlockSpec((1, d), lambda i: (0, 0))
    return pl.pallas_call(
        _ln_kernel,
        grid=(n // tm,),
        in_specs=[row, vec, vec],
        out_specs=[row, row],
        out_shape=[jax.ShapeDtypeStruct((n, d), F32), jax.ShapeDtypeStruct((n, d), BF16)],
        compiler_params=_cp(("parallel",), 48),
        name="layer_norm1",
    )(x, g, b)


def _ln_resid_kernel(h_ref, p_ref, g_ref, b_ref, op_ref, os_ref, *, alpha, n_ptiles):
    i = pl.program_id(0)
    y = _ln(alpha * h_ref[...] + p_ref[...], g_ref[...], b_ref[...])

    @pl.when(i < n_ptiles)
    def _():
        op_ref[...] = y

    @pl.when(i == n_ptiles)
    def _():
        os_ref[...] = y[:os_ref.shape[0]]


def layer_norm_resid(h, p, g, b, alpha, tm, n_p, n_s):
    d = h.shape[1]
    n_ptiles = n_p // tm
    assert n_p % tm == 0 and n_s <= tm
    row = pl.BlockSpec((tm, d), lambda i: (i, 0))
    vec = pl.BlockSpec((1, d), lambda i: (0, 0))
    return pl.pallas_call(
        functools.partial(_ln_resid_kernel, alpha=alpha, n_ptiles=n_ptiles),
        grid=(n_ptiles + 1,),
        in_specs=[row, row, vec, vec],
        out_specs=[pl.BlockSpec((tm, d), lambda i: (jnp.minimum(i, n_ptiles - 1), 0)),
                   pl.BlockSpec((n_s, d), lambda i: (0, 0))],
        out_shape=[jax.ShapeDtypeStruct((n_p, d), F32), jax.ShapeDtypeStruct((n_s, d), F32)],
        compiler_params=_cp(("arbitrary",), 48),
        name="layer_norm2",
    )(h, p, g, b)


def _top_ranked(x, n_top, vals_ref):
    kdim = x.shape[0]
    idx = lax.broadcasted_iota(I32, x.shape, 0).astype(F32)

    def body(a, carry):
        x, rank = carry
        m = jnp.max(x, axis=0, keepdims=True)
        first = jnp.min(jnp.where(x == m, idx, float(kdim)), axis=0, keepdims=True)
        sel = idx == first
        vals_ref[pl.ds(a, 1), :] = m
        return jnp.where(sel, -jnp.inf, x), jnp.where(sel, lax.convert_element_type(a, F32), rank)

    _, rank = lax.fori_loop(0, n_top, body, (x, jnp.full(x.shape, float(n_top), F32)))
    return rank, [vals_ref[a:a + 1, :] for a in range(n_top)]


def _top_ranked_distinct(x, n_top):
    rank = jnp.full(x.shape, float(n_top), F32)
    vals = []
    for a in range(n_top):
        m = jnp.max(x, axis=0, keepdims=True)
        sel = x == m
        rank = jnp.where(sel, float(a), rank)
        x = jnp.where(sel, -jnp.inf, x)
        vals.append(m)
    n_ranked = jnp.sum(jnp.where(rank < float(n_top), 1.0, 0.0), axis=0, keepdims=True)
    return rank, vals, n_ranked


def _peer_route_kernel(pq_ref, k1_ref, k2_ref, rank2_ref, cnt1_ref, e1_ref, e2_ref, vals_ref):
    tn = pq_ref.shape[0]
    refs = (pq_ref, k1_ref, k2_ref, rank2_ref, cnt1_ref, e1_ref, e2_ref, vals_ref)
    for h in range(PEER_HEADS):
        irregular = _peer_route_head(h, *refs, distinct=True)
        n_irregular = jnp.sum(irregular.reshape(tn // LANES, LANES), axis=0, keepdims=True)

        @pl.when(jnp.max(n_irregular) > 0.0)
        def _():
            _peer_route_head(h, *refs, distinct=False)


def _peer_route_head(h, pq_ref, k1_ref, k2_ref, rank2_ref, cnt1_ref, e1_ref, e2_ref, vals_ref, *, distinct):
    half = PEER_DKEY // 2
    nt = (((1,), (1,)), ((), ()))
    irregular = jnp.zeros((1, pq_ref.shape[0]), F32)

    def top(x, slot):
        if not distinct:
            return _top_ranked(x, PEER_TOPK, vals_ref.at[slot]) + (None,)
        return _top_ranked_distinct(x, PEER_TOPK)

    q1 = pq_ref[:, h * PEER_DKEY:h * PEER_DKEY + half].astype(BF16)
    q2 = pq_ref[:, h * PEER_DKEY + half:(h + 1) * PEER_DKEY].astype(BF16)
    s1 = lax.dot_general(k1_ref[...].astype(BF16), q1, nt, preferred_element_type=F32)
    s2 = lax.dot_general(k2_ref[...].astype(BF16), q2, nt, preferred_element_type=F32)
    rank1, v1, n1 = top(s1, 0)
    rank2, v2, n2 = top(s2, 1)
    v2m = jnp.concatenate(v2, axis=0)
    cand = jnp.concatenate([v1[a] + v2m for a in range(PEER_TOPK)], axis=0)
    crank, _, nc = top(cand, 2)
    if distinct:
        for cnt in (n1, n2, nc):
            irregular = irregular + jnp.where(cnt == float(PEER_TOPK), 0.0, 1.0)
    chosen = jnp.where(crank < float(PEER_TOPK), 1.0, 0.0)
    cmax = v1[0] + v2[0]
    zsum = jnp.sum(chosen * jnp.exp(cand - cmax), axis=0, keepdims=True)
    cnt1 = jnp.zeros_like(s1)
    for a in range(PEER_TOPK):
        m_a = jnp.sum(chosen[a * PEER_TOPK:(a + 1) * PEER_TOPK], axis=0, keepdims=True)
        cnt1 = jnp.where(rank1 == float(a), m_a, cnt1)
    rank2_ref[h] = rank2.astype(rank2_ref.dtype)
    cnt1_ref[h] = cnt1
    e1_ref[h] = jnp.exp(s1 - v1[0])
    e2_ref[h] = (jnp.exp(s2 - v2[0]) / zsum).astype(e2_ref.dtype)
    return irregular


def peer_route(pq, k1, k2, tn):
    n = pq.shape[0]
    shape = (PEER_HEADS, PEER_NKEYS, n)
    ospec = pl.BlockSpec((PEER_HEADS, PEER_NKEYS, tn), lambda i: (0, 0, i))
    kspec = pl.BlockSpec(k1.shape, lambda i: (0, 0))
    return pl.pallas_call(
        _peer_route_kernel,
        grid=(n // tn,),
        in_specs=[pl.BlockSpec((tn, pq.shape[1]), lambda i: (i, 0)), kspec, kspec],
        out_specs=[ospec] * 4,
        out_shape=[jax.ShapeDtypeStruct(shape, BF16), jax.ShapeDtypeStruct(shape, F32),
                   jax.ShapeDtypeStruct(shape, F32), jax.ShapeDtypeStruct(shape, BF16)],
        scratch_shapes=[pltpu.VMEM((3, PEER_TOPK, tn), F32)],
        compiler_params=_cp(("parallel",), 32),
        name="peer_route",
    )(pq, k1, k2)


PEER_EROWS = 4


def _gelu_exact(x):
    return 0.5 * x * (1.0 + lax.erf(x * (2.0 ** -0.5)))


def _peer_dense_kernel(x_ref, u_ref, v_ref, rank2_ref, cnt1_ref, e1_ref, e2_ref, o_ref, z_ref, *, n_tiles):
    r = pl.program_id(1)
    rd, wr = (r + 1) % 2, r % 2

    @pl.when(r == 0)
    def _():
        o_ref[...] = jnp.zeros(o_ref.shape, o_ref.dtype)
        z_ref[...] = jnp.zeros(z_ref.shape, z_ref.dtype)

    o_ref[...] += lax.dot_general(z_ref[rd], v_ref[...], (((0,), (0,)), ((), ())), preferred_element_type=F32)
    at = lax.dot_general(u_ref[...], x_ref[...], (((1,), (1,)), ((), ())), preferred_element_type=F32)
    act = _gelu_exact(at).astype(BF16)
    tile = jnp.minimum(r, n_tiles - 1)
    for rr in range(PEER_EROWS):
        i1 = tile * PEER_EROWS + rr
        g = jnp.zeros((PEER_NKEYS, x_ref.shape[0]), BF16)
        for h in range(PEER_HEADS):
            cnt = cnt1_ref[h, pl.ds(i1, 1), :].astype(BF16)
            e1 = e1_ref[h, pl.ds(i1, 1), :].astype(BF16)
            g = g + jnp.where(rank2_ref[h] < cnt, e2_ref[h] * e1, jnp.zeros((), BF16))
        z_ref[wr, rr * PEER_NKEYS:(rr + 1) * PEER_NKEYS, :] = g * act[rr * PEER_NKEYS:(rr + 1) * PEER_NKEYS]


def peer_dense(xb, u, v, rank2, cnt1, e1, e2, tn):
    n, d = xb.shape
    ne = u.shape[0]
    te = PEER_EROWS * PEER_NKEYS
    n_tiles = ne // te
    aux = pl.BlockSpec((PEER_HEADS, PEER_NKEYS, tn), lambda j, r: (0, 0, j))
    return pl.pallas_call(
        functools.partial(_peer_dense_kernel, n_tiles=n_tiles),
        grid=(n // tn, n_tiles + 1),
        in_specs=[pl.BlockSpec((tn, d), lambda j, r: (j, 0)),
                  pl.BlockSpec((te, d), lambda j, r: (jnp.minimum(r, n_tiles - 1), 0)),
                  pl.BlockSpec((te, d), lambda j, r: (jnp.maximum(r - 1, 0), 0)),
                  aux, aux, aux, aux],
        out_specs=pl.BlockSpec((tn, d), lambda j, r: (j, 0)),
        out_shape=jax.ShapeDtypeStruct((n, d), F32),
        scratch_shapes=[pltpu.VMEM((2, te, tn), BF16)],
        compiler_params=_cp(("parallel", "arbitrary"), 60),
        name="peer_dense",
    )(xb, u, v, rank2, cnt1, e1, e2)


def kernel(x_prompt, x_sample, cache_k, cache_v, cache_idx_k, cache_mem_k, cache_mem_v, state_hgrn, page_table,
           mem_prompt, w_in, w_mem_kv, hgrn_lb, hgrn_norm_g, p_a, p_b, p_m, w_out, ln1_g, ln1_b, w_pq,
           peer_sub_k1, peer_sub_k2, peer_u, peer_v, ln2_g, ln2_b):
    depth = w_in.shape[0]
    assert depth == 1, "single trunk layer"
    bp, t, d = x_prompt.shape
    bs, ts, _ = x_sample.shape
    assert bp == 1 and d == D_MODEL and t % KEY_CHUNK == 0 and bs * ts <= Q_BLK and ts <= S_ROWS
    n_p, n_s = bp * t, bs * ts
    n_real = n_p + n_s
    nt = -(-n_real // TOK_TILE) * TOK_TILE
    n_pool = cache_k.shape[1]
    past_len = page_table.shape[1] * PAGE_SIZE
    alpha = (2 * depth) ** 0.25
    nb_p = n_p // Q_BLK

    assert n_p % TOK_TILE == 0 and nt == n_p + TOK_TILE
    xp, xs = x_prompt.reshape(n_p, d), x_sample.reshape(n_s, d)
    xb = _pad_rows(jnp.concatenate([xp.astype(BF16), xs.astype(BF16)], axis=0), nt)
    w_packed_t = pack_w_in_t(jnp.swapaxes(w_in[0], 0, 1))
    z, peer_ub, peer_vb = matmul_nt_with_casts(xb, w_packed_t, TOK_TILE, Z_TILE_WIDE, peer_u[0], peer_v[0],
                                               name="proj_in")

    def zcols(name, width, lo, hi):
        return z[lo:hi, COLS[name]:COLS[name] + width]

    pos = jnp.concatenate([jnp.arange(t, dtype=I32), past_len + jnp.tile(jnp.arange(ts, dtype=I32), bs),
                           jnp.zeros((nt - n_real,), I32)])
    q_hm, iq_hm, k4, kb, v4, vb, ik_rope, ikb = rope_all(z, COLS, rope_tables(pos))
    nb = q_hm.shape[0]

    oa_p = dsa_prompt(iq_hm.reshape(nb, IDX_HEADS * Q_BLK, IDX_DIM), z, COLS,
                      q_hm.reshape(nb, A_HEADS * Q_BLK, A_HEAD_DIM), ikb[:n_p], kb[:n_p], vb[:n_p], t)

    def per_seq(a, rows):
        return _pad_rows(a.reshape((bs, ts) + a.shape[1:]), rows, axis=1)

    q_s = per_seq(q_hm[nb_p, :, :n_s].transpose(1, 0, 2), S_ROWS).transpose(0, 2, 1, 3)
    q_s = q_s.reshape(bs, A_HEADS * S_ROWS, A_HEAD_DIM)
    iq_s = per_seq(iq_hm[nb_p, :, :n_s].transpose(1, 0, 2), S_ROWS).transpose(0, 2, 1, 3)
    iq_s = iq_s.reshape(bs, IDX_HEADS * S_ROWS, IDX_DIM)
    iw_s = per_seq(z[n_p:n_real, COLS["ikw"] + IDX_DIM:COLS["ikw"] + IDX_DIM + IDX_HEADS], S_ROWS)
    ik_new = per_seq(ikb[n_p:n_real], PAGE_SIZE)
    k_new = per_seq(kb[n_p:n_real], PAGE_SIZE)
    v_new = per_seq(vb[n_p:n_real], PAGE_SIZE)
    keys, thr = dsa_sample_index(page_table, iq_s, iw_s, ik_new, jnp.swapaxes(cache_idx_k[0], 1, 2), ts)
    cache_rows = n_pool * PAGE_SIZE * A_KV_HEADS
    os_hm = dsa_sample_attend(page_table, keys, thr, q_s, k_new, v_new,
                              cache_k[0].reshape(cache_rows, A_HEAD_DIM), cache_v[0].reshape(cache_rows, A_HEAD_DIM))
    oa_s = os_hm.reshape(bs, A_HEADS, S_ROWS, A_HEAD_DIM)[:, :, :ts].transpose(0, 2, 1, 3).reshape(n_s, A_Q_W)
    o_a = _pad_rows(jnp.concatenate([oa_p, oa_s], axis=0), nt)

    b_blk = [COLS[c] // B_W for c in ("bq", "bf", "bi", "bg")]
    ng = hgrn_norm_g[0][None]
    ob_p, st_p = hgrn(z, b_blk, hgrn_lb, ng, jnp.zeros((bp, B_HEADS, B_HEAD_DIM, B_HEAD_DIM), F32),
                      bp, t, HGRN_CHUNK, HGRN_SUB, HGRN_CHUNK)
    zs_b = per_seq(z[n_p:n_real, COLS["bq"]:COLS["bq"] + 4 * B_W], S_ROWS).reshape(bs * S_ROWS, 4 * B_W)
    ob_s, st_s = hgrn(zs_b, [0, 1, 2, 3], hgrn_lb, ng, state_hgrn[0], bs, S_ROWS, S_ROWS, S_ROWS, ts)
    ob_s = ob_s.reshape(bs, S_ROWS, B_W)[:, :ts].reshape(n_s, B_W)
    o_b = _pad_rows(jnp.concatenate([ob_p[:n_p], ob_s], axis=0), nt)

    mem_kv = matmul(mem_prompt[0].astype(BF16), w_mem_kv[0].astype(BF16), mem_prompt.shape[1], Z_TILE, name="mem_kv")
    om_p = mem_attention(z[None], COLS["mq"] // M_W, mem_kv[None], 0, mem_kv[None], 1, bp, t, TOK_TILE)[0]
    zs_m = per_seq(zcols("mq", M_W, n_p, n_real), S_ROWS)
    n_mem = cache_mem_k.shape[2]
    om_s = mem_attention(zs_m, 0, cache_mem_k[0].reshape(bs, n_mem, M_W), 0,
                         cache_mem_v[0].reshape(bs, n_mem, M_W), 0, bs, S_ROWS, S_ROWS)
    o_m = _pad_rows(jnp.concatenate([om_p, om_s[:, :ts].reshape(n_s, M_W)], axis=0), nt)

    merged = merge(o_a, o_b, o_m, p_a[0].astype(BF16), p_b[0].astype(BF16), p_m[0].astype(BF16), z, COLS,
                   TOK_TILE, WIDE_TILE)
    h_pre = resid_matmul(merged, w_out[0].astype(BF16), xp, _pad_rows(xs, TOK_TILE), alpha, TOK_TILE, WIDE_TILE)
    h, hb = layer_norm_dual(h_pre, ln1_g[0][None], ln1_b[0][None], LN_TILE)

    pq = matmul(hb, w_pq[0].astype(BF16), TOK_TILE, Z_TILE, name="peer_query")
    rank2, cnt1, e1, e2 = peer_route(pq, peer_sub_k1[0], peer_sub_k2[0], LN_TILE)
    p_out = peer_dense(hb, peer_ub, peer_vb, rank2, cnt1, e1, e2, TOK_TILE)
    y_p, y_s = layer_norm_resid(h, p_out, ln2_g[0][None], ln2_b[0][None], alpha, LN_TILE, n_p, n_s)

    kv_p = (depth, bp, t, A_KV_HEADS, A_HEAD_DIM)
    kv_s = (depth, bs, ts, A_KV_HEADS, A_HEAD_DIM)
    mem_shape = (depth, bp, mem_prompt.shape[1], M_HEADS, M_HEAD_DIM)
    g4 = A_KV_HEADS
    return (y_p.reshape(bp, t, d), y_s.reshape(bs, ts, d),
            k4[:n_p * g4].reshape(kv_p), v4[:n_p * g4].reshape(kv_p), ik_rope[:n_p].reshape(depth, bp, t, IDX_DIM),
            mem_kv[:, :M_W].reshape(mem_shape), mem_kv[:, M_W:].reshape(mem_shape), st_p[None],
            k4[n_p * g4:n_real * g4].reshape(kv_s), v4[n_p * g4:n_real * g4].reshape(kv_s),
            ik_rope[n_p:n_real].reshape(depth, bs, ts, IDX_DIM), st_s[None])
```

```python
import functools
import math

import jax
import jax.numpy as jnp
import numpy as np
from jax import lax
from jax.experimental import pallas as pl
from jax.experimental.pallas import tpu as pltpu

F32 = jnp.float32
BF16 = jnp.bfloat16
I32 = jnp.int32

A_HEADS, A_KV_HEADS, A_HEAD_DIM = 16, 4, 128
IDX_HEADS, IDX_DIM = 32, 64
TOPK_MAX = 256
B_HEADS, B_HEAD_DIM = 8, 128
M_HEADS, M_HEAD_DIM = 4, 256
PEER_HEADS, PEER_NKEYS, PEER_DKEY, PEER_TOPK = 8, 128, 256, 16
PAGE_SIZE = 128
ROPE_THETA = 10000.0
LN_EPS = 1e-5
RMS_EPS = 1e-6
D_MODEL = 4096

A_Q_W = A_HEADS * A_HEAD_DIM
A_KV_W = A_KV_HEADS * A_HEAD_DIM
IDX_Q_W = IDX_HEADS * IDX_DIM
B_W = B_HEADS * B_HEAD_DIM
M_W = M_HEADS * M_HEAD_DIM

_SEGS = (("aq", A_Q_W), ("iq", IDX_Q_W), ("ak", A_KV_W), ("av", A_KV_W), ("bq", B_W), ("bf", B_W),
         ("bi", B_W), ("bg", B_W), ("mq", M_W), ("ga", D_MODEL), ("gb", D_MODEL), ("gm", D_MODEL),
         ("ikw", 128))
COLS = {}
_off = 0
for _name, _w in _SEGS:
    COLS[_name] = _off
    _off += _w
Z_TILE = 512
Z_WIDTH = -(-_off // Z_TILE) * Z_TILE
Z_TILE_WIDE = 1536
assert Z_WIDTH % Z_TILE_WIDE == 0

LANES = 128
SUBLANES = 8
Q_BLK = 256
KEY_CHUNK = 512
HGRN_CHUNK = 64
HGRN_SUB = 16
EXP_CLAMP = 80.0
Q_LOG2_SCALE = (A_HEAD_DIM ** -0.5) * math.log2(math.e)
INT_MIN = -2 ** 31
NEG_BIG = -1e30
TOK_TILE = 512
WIDE_TILE = 1024
LN_TILE = 256
ROUTE_TILE = LANES


def _cp(sem, vmem_mb):
    return pltpu.CompilerParams(dimension_semantics=sem, vmem_limit_bytes=vmem_mb * 2 ** 20)


def _whole_vmem():
    return pl.BlockSpec(memory_space=pltpu.VMEM)


def _pad_rows(a, n, axis=0):
    pad = [(0, 0)] * a.ndim
    pad[axis] = (0, n - a.shape[axis])
    return jnp.pad(a, pad)


def _mm_kernel(a_ref, b_ref, o_ref):
    o_ref[...] = jnp.dot(a_ref[...], b_ref[...], preferred_element_type=F32).astype(o_ref.dtype)


def matmul(a, b, tm, tn, out_dtype=F32, name="matmul"):
    m, k = a.shape
    n = b.shape[1]
    return pl.pallas_call(
        _mm_kernel,
        grid=(m // tm, n // tn),
        in_specs=[pl.BlockSpec((tm, k), lambda i, j: (i, 0)), pl.BlockSpec((k, tn), lambda i, j: (0, j))],
        out_specs=pl.BlockSpec((tm, tn), lambda i, j: (i, j)),
        out_shape=jax.ShapeDtypeStruct((m, n), out_dtype),
        compiler_params=_cp(("parallel", "parallel"), 48),
        name=name,
    )(a, b)


def _mm_nt_kernel(a_ref, bt_ref, o_ref):
    o_ref[...] = lax.dot_general(a_ref[...], bt_ref[...], (((1,), (1,)), ((), ())),
                                 preferred_element_type=F32).astype(o_ref.dtype)


def matmul_nt(a, bt, tm, tn, out_dtype=F32, name="matmul_nt"):
    m, k = a.shape
    n = bt.shape[0]
    return pl.pallas_call(
        _mm_nt_kernel,
        grid=(n // tn, m // tm),
        in_specs=[pl.BlockSpec((tm, k), lambda j, i: (i, 0)), pl.BlockSpec((tn, k), lambda j, i: (j, 0))],
        out_specs=pl.BlockSpec((tm, tn), lambda j, i: (i, j)),
        out_shape=jax.ShapeDtypeStruct((m, n), out_dtype),
        compiler_params=_cp(("parallel", "parallel"), 48),
        name=name,
    )(a, bt)


CAST_ROWS = 128


def _mm_nt_cast_kernel(a_ref, bt_ref, u_ref, v_ref, o_ref, ub_ref, vb_ref):
    o_ref[...] = lax.dot_general(a_ref[...], bt_ref[...], (((1,), (1,)), ((), ())),
                                 preferred_element_type=F32)
    ub_ref[...] = u_ref[...].astype(ub_ref.dtype)
    vb_ref[...] = v_ref[...].astype(vb_ref.dtype)


def matmul_nt_with_casts(a, bt, tm, tn, u, v, name):
    m, k = a.shape
    n = bt.shape[0]
    n_i = m // tm
    n_cast = u.shape[0] // CAST_ROWS
    assert u.shape == v.shape and u.shape[0] % CAST_ROWS == 0 and n_cast <= (n // tn) * n_i

    def tab(j, i):
        return (jnp.minimum(j * n_i + i, n_cast - 1), 0)

    tspec = pl.BlockSpec((CAST_ROWS, u.shape[1]), tab)
    return pl.pallas_call(
        _mm_nt_cast_kernel,
        grid=(n // tn, n_i),
        in_specs=[pl.BlockSpec((tm, k), lambda j, i: (i, 0)), pl.BlockSpec((tn, k), lambda j, i: (j, 0)),
                  tspec, tspec],
        out_specs=[pl.BlockSpec((tm, tn), lambda j, i: (i, j)), tspec, tspec],
        out_shape=[jax.ShapeDtypeStruct((m, n), F32), jax.ShapeDtypeStruct(u.shape, BF16),
                   jax.ShapeDtypeStruct(v.shape, BF16)],
        compiler_params=_cp(("arbitrary", "arbitrary"), 56),
        name=name,
    )(a, bt, u, v)


PACK_SHIFT = IDX_DIM + IDX_HEADS
PACK_T_ROWS = 256


def _pack_t_plan():
    r = PACK_T_ROWS
    splits = (A_Q_W, A_KV_W, A_KV_W, IDX_Q_W, IDX_DIM, IDX_HEADS, 4 * B_W + M_W + 3 * D_MODEL)
    src = dict(zip(("aq", "ak", "av", "iq", "ik", "iw", "rest"), np.concatenate([[0], np.cumsum(splits)[:-1]])))
    assert src["ik"] % r == 0 and src["rest"] == src["ik"] + PACK_SHIFT and splits[-1] % r == 0
    plan = []
    for name in ("aq", "iq", "ak", "av"):
        width = A_Q_W if name in ("aq", "iq") else A_KV_W
        plan += [(0, (src[name] + o) // r, (COLS[name] + o) // r) for o in range(0, width, r)]
    plan += [(2, src["ik"] // r, COLS["ikw"] // r)]
    plan += [(1, src["ik"] // r + 1 + t, COLS["bq"] // r + t) for t in range(splits[-1] // r)]
    plan += [(3, 0, t) for t in range(COLS["ikw"] // r + 1, Z_WIDTH // r)]
    assert sorted(p[2] for p in plan) == list(range(Z_WIDTH // r))
    return np.array(plan, np.int32)


def _pack_t_kernel(plan_ref, a_ref, o_ref, tail_ref):
    mode = plan_ref[pl.program_id(0), 0]
    keep = PACK_T_ROWS - PACK_SHIFT

    @pl.when(mode == 0)
    def _():
        o_ref[...] = a_ref[...].astype(o_ref.dtype)

    @pl.when(mode == 1)
    def _():
        o_ref[:keep, :] = tail_ref[...].astype(o_ref.dtype)
        o_ref[keep:, :] = a_ref[:PACK_SHIFT, :].astype(o_ref.dtype)
        tail_ref[...] = a_ref[PACK_SHIFT:, :]

    @pl.when(mode == 2)
    def _():
        o_ref[...] = jnp.zeros(o_ref.shape, o_ref.dtype)
        o_ref[:PACK_SHIFT, :] = a_ref[:PACK_SHIFT, :].astype(o_ref.dtype)
        tail_ref[...] = a_ref[PACK_SHIFT:, :]

    @pl.when(mode == 3)
    def _():
        o_ref[...] = jnp.zeros(o_ref.shape, o_ref.dtype)


def pack_w_in_t(wt):
    dm = wt.shape[1]
    plan = _pack_t_plan()
    n_src = -(-wt.shape[0] // PACK_T_ROWS)
    grid_spec = pltpu.PrefetchScalarGridSpec(
        num_scalar_prefetch=1,
        grid=(len(plan),),
        in_specs=[pl.BlockSpec((PACK_T_ROWS, dm), lambda j, p: (jnp.minimum(p[j, 1], n_src - 1), 0))],
        out_specs=pl.BlockSpec((PACK_T_ROWS, dm), lambda j, p: (p[j, 2], 0)),
        scratch_shapes=[pltpu.VMEM((PACK_T_ROWS - PACK_SHIFT, dm), F32)],
    )
    return pl.pallas_call(
        _pack_t_kernel,
        grid_spec=grid_spec,
        out_shape=jax.ShapeDtypeStruct((Z_WIDTH, dm), BF16),
        compiler_params=_cp(("arbitrary",), 32),
        name="pack_w_in",
    )(jnp.asarray(plan), wt)


def _rope_kernel(q_ref, iq_ref, k_ref, v_ref, ikw_ref, c128_ref, s128_ref, c64_ref, s64_ref,
                 qhm_ref, iqhm_ref, k4_ref, kb_ref, v4_ref, vb_ref, iko_ref, ikb_ref):
    c128, s128 = c128_ref[...], s128_ref[...]
    c64, s64 = c64_ref[...], s64_ref[...]
    lane = lax.broadcasted_iota(I32, (Q_BLK, LANES), 1)
    first_half = (lane % IDX_DIM) < (IDX_DIM // 2)

    def rope128(x):
        return x * c128 + pltpu.roll(x, A_HEAD_DIM // 2, axis=1) * s128

    def rope64(x):
        partner = jnp.where(first_half, pltpu.roll(x, LANES - IDX_DIM // 2, axis=1),
                            pltpu.roll(x, IDX_DIM // 2, axis=1))
        return x * c64 + partner * s64

    for h in range(A_HEADS):
        qhm_ref[0, h] = (rope128(q_ref[:, h * LANES:(h + 1) * LANES]) * Q_LOG2_SCALE).astype(qhm_ref.dtype)
    for h in range(A_KV_HEADS):
        rows = pl.ds(h, Q_BLK, stride=A_KV_HEADS)
        kr = rope128(k_ref[:, h * LANES:(h + 1) * LANES])
        k4_ref[rows, :] = kr
        kb_ref[:, h * LANES:(h + 1) * LANES] = kr.astype(kb_ref.dtype)
        vh = v_ref[:, h * LANES:(h + 1) * LANES]
        v4_ref[rows, :] = vh
        vb_ref[:, h * LANES:(h + 1) * LANES] = vh.astype(vb_ref.dtype)
    for p in range(IDX_HEADS // 2):
        r = rope64(iq_ref[:, p * LANES:(p + 1) * LANES]).astype(iqhm_ref.dtype)
        iqhm_ref[0, 2 * p] = r[:, :IDX_DIM]
        iqhm_ref[0, 2 * p + 1] = r[:, IDX_DIM:]
    ikr = rope64(ikw_ref[...])[:, :IDX_DIM]
    iko_ref[...] = ikr
    ikb_ref[...] = ikr.astype(ikb_ref.dtype)


def rope_all(z, col, tabs):
    nt = z.shape[0]
    nb = nt // Q_BLK
    c128, s128, c64, s64 = tabs
    tab_spec = pl.BlockSpec((Q_BLK, LANES), lambda i: (i, 0))
    kv4_spec = pl.BlockSpec((Q_BLK * A_KV_HEADS, A_HEAD_DIM), lambda i: (i, 0))
    kvb_spec = pl.BlockSpec((Q_BLK, A_KV_W), lambda i: (i, 0))
    ik_spec = pl.BlockSpec((Q_BLK, IDX_DIM), lambda i: (i, 0))
    kv4 = jax.ShapeDtypeStruct((nt * A_KV_HEADS, A_HEAD_DIM), F32)
    kvb = jax.ShapeDtypeStruct((nt, A_KV_W), BF16)
    return pl.pallas_call(
        _rope_kernel,
        grid=(nb,),
        in_specs=[pl.BlockSpec((Q_BLK, A_Q_W), lambda i: (i, col["aq"] // A_Q_W)),
                  pl.BlockSpec((Q_BLK, IDX_Q_W), lambda i: (i, col["iq"] // IDX_Q_W)),
                  pl.BlockSpec((Q_BLK, A_KV_W), lambda i: (i, col["ak"] // A_KV_W)),
                  pl.BlockSpec((Q_BLK, A_KV_W), lambda i: (i, col["av"] // A_KV_W)),
                  pl.BlockSpec((Q_BLK, LANES), lambda i: (i, col["ikw"] // LANES)),
                  tab_spec, tab_spec, tab_spec, tab_spec],
        out_specs=[pl.BlockSpec((1, A_HEADS, Q_BLK, A_HEAD_DIM), lambda i: (i, 0, 0, 0)),
                   pl.BlockSpec((1, IDX_HEADS, Q_BLK, IDX_DIM), lambda i: (i, 0, 0, 0)),
                   kv4_spec, kvb_spec, kv4_spec, kvb_spec, ik_spec, ik_spec],
        out_shape=[jax.ShapeDtypeStruct((nb, A_HEADS, Q_BLK, A_HEAD_DIM), BF16),
                   jax.ShapeDtypeStruct((nb, IDX_HEADS, Q_BLK, IDX_DIM), BF16),
                   kv4, kvb, kv4, kvb,
                   jax.ShapeDtypeStruct((nt, IDX_DIM), F32), jax.ShapeDtypeStruct((nt, IDX_DIM), BF16)],
        compiler_params=_cp(("parallel",), 32),
        name="rope",
    )(z, z, z, z, z, c128, s128, c64, s64)


def rope_tables(pos):
    def tab(half, reps):
        inv_freq = ROPE_THETA ** (-jnp.arange(half, dtype=F32) / half)
        ang = pos.astype(F32)[:, None] * inv_freq[None, :]
        c, s = jnp.cos(ang), jnp.sin(ang)
        return jnp.tile(jnp.concatenate([c, c], 1), (1, reps)), jnp.tile(jnp.concatenate([-s, s], 1), (1, reps))
    c128, s128 = tab(A_HEAD_DIM // 2, 1)
    c64, s64 = tab(IDX_DIM // 2, 2)
    return c128, s128, c64, s64


def _sort_key(x):
    i = pltpu.bitcast(x, I32)
    return i ^ ((i >> 31) & 0x7FFFFFFF)


WORD = 32
GROUP_CHUNKS = WORD * LANES // KEY_CHUNK
_BIT_MASKS = ((16, 0x0000FFFF), (8, 0x00FF00FF), (4, 0x0F0F0F0F), (2, 0x33333333), (1, 0x55555555))


def _transpose_bits(w):
    w = list(w)
    for d, m in _BIT_MASKS:
        mask = jnp.int32(m)
        for j in range(WORD):
            if j & d:
                continue
            lo, hi = w[j], w[j + d]
            t = (lax.shift_right_logical(lo, jnp.int32(d)) ^ hi) & mask
            w[j + d] = hi ^ t
            w[j] = lo ^ lax.shift_left(t, jnp.int32(d))
    return w


def _kth_largest_bitsliced(key_ref, plane_ref, cand_ref, nch, k):
    rows = key_ref.shape[1]
    tiles_per_chunk = KEY_CHUNK // LANES
    ngroups = (nch + GROUP_CHUNKS - 1) // GROUP_CHUNKS

    def build_group(g, carry):
        def build_slab(s, c2):
            r0 = pl.multiple_of(s * SUBLANES, SUBLANES)
            words = []
            for j in range(WORD):
                c = g * GROUP_CHUNKS + j // tiles_per_chunk
                lt = j % tiles_per_chunk
                cc = jnp.minimum(c, key_ref.shape[0] - 1)
                w = key_ref[cc, pl.ds(r0, SUBLANES), lt * LANES:(lt + 1) * LANES] ^ INT_MIN
                words.append(jnp.where(c < nch, w, 0))
            planes = _transpose_bits(words)
            for b in range(WORD):
                plane_ref[g, b, pl.ds(r0, SUBLANES), :] = planes[b]
            return c2
        lax.fori_loop(0, rows // SUBLANES, build_slab, 0)
        cand_ref[g] = jnp.full((rows, LANES), -1, I32)
        return carry

    lax.fori_loop(0, ngroups, build_group, 0)

    def bit_step(it, carry):
        t_u, k_rem = carry
        b = WORD - 1 - it

        def count(g, cnt):
            return cnt + lax.population_count(cand_ref[g] & plane_ref[g, b])
        cnt = lax.fori_loop(0, ngroups, count, jnp.zeros((rows, LANES), I32))
        c1 = jnp.sum(cnt, axis=1, keepdims=True)
        take = c1 >= k_rem
        take_b = jnp.broadcast_to(take, (rows, LANES))

        def update(g, c2):
            e = cand_ref[g]
            a = e & plane_ref[g, b]
            cand_ref[g] = jnp.where(take_b, a, e ^ a)
            return c2
        lax.fori_loop(0, ngroups, update, 0)
        t_u = jnp.where(take, t_u | jnp.left_shift(jnp.int32(1), b), t_u)
        return t_u, jnp.where(take, k_rem, k_rem - c1)

    t_u, k_rem = lax.fori_loop(0, WORD, bit_step, (jnp.zeros((rows, 1), I32), jnp.full((rows, 1), k, I32)))
    return t_u ^ INT_MIN, k_rem


def _lower_surplus_ties(key_ref, cand_ref, nch, t, n_wanted):
    rows = key_ref.shape[1]
    ngroups = (nch + GROUP_CHUNKS - 1) // GROUP_CHUNKS
    group_keys = WORD * LANES
    pos_bits = (key_ref.shape[0] * KEY_CHUNK - 1).bit_length()
    live = t > INT_MIN

    def count_tied(g, cnt):
        return cnt + lax.population_count(cand_ref[g])
    n_tied = jnp.sum(lax.fori_loop(0, ngroups, count_tied, jnp.zeros((rows, LANES), I32)), axis=1, keepdims=True)
    surplus = jnp.where(live, n_tied - n_wanted, 0)

    @pl.when(jnp.max(surplus) > 0)
    def _():
        lane = lax.broadcasted_iota(I32, (rows, LANES), 1)

        def tied_before(q):
            gq = q // group_keys
            jq = jnp.broadcast_to((q // LANES) % WORD, (rows, LANES))
            lq = q % LANES
            low = lax.shift_left(jnp.ones((rows, LANES), I32), jq) - 1

            def body(g, cnt):
                w = cand_ref[g]
                at_tile = jnp.where(lane < lq, lax.shift_right_logical(w, jq) & 1, 0)
                part = lax.population_count(w & low) + at_tile
                return cnt + jnp.where(g < gq, lax.population_count(w), jnp.where(g == gq, part, 0))
            cnt = lax.fori_loop(0, ngroups, body, jnp.zeros((rows, LANES), I32))
            return jnp.sum(cnt, axis=1, keepdims=True)

        def bit_step(it, q):
            cand = q | jnp.left_shift(jnp.int32(1), pos_bits - 1 - it)
            return jnp.where(tied_before(cand) < n_wanted, cand, q)
        last = lax.fori_loop(0, pos_bits, bit_step, jnp.zeros((rows, 1), I32))
        off = lax.broadcasted_iota(I32, (rows, KEY_CHUNK), 1)

        def rewrite(c, carry):
            keys = key_ref[c]
            drop = jnp.where(keys == t, 1, 0) * jnp.where(c * KEY_CHUNK + off > last, 1, 0) * jnp.where(live, 1, 0)
            key_ref[c] = keys - drop
            return carry
        lax.fori_loop(0, nch, rewrite, 0)


def _flash_step(s, mask, vext, m_ref, l_ref, acc_ref, g):
    hh, rr, ss = s.shape
    d = vext.shape[1] // 2
    s = jnp.where(mask[None], s, -jnp.inf).reshape(hh * rr, ss)
    m_old = m_ref[g]
    m_new = jnp.maximum(m_old, jnp.max(s, axis=-1, keepdims=True))
    p = jnp.exp2(s - jnp.concatenate([m_new] * (ss // LANES), axis=1))
    alpha = jnp.exp2(m_old - m_new)
    pv = jnp.dot(p.astype(BF16), vext, preferred_element_type=F32)
    acc_ref[g] = alpha * acc_ref[g] + pv[:, :d]
    l_ref[g] = alpha * l_ref[g] + pv[:, d:]
    m_ref[g] = m_new


IDX_HEADS_PER_DOT = 8


def _dsa_prompt_kernel(iq_ref, w_ref, q_ref, ik_ref, k_ref, v_ref, o_ref,
                       key_ref, plane_ref, cand_ref, wb_ref, m_ref, l_ref, acc_ref, *, topk, w_lane0):
    i = pl.program_id(0)
    nch = (i * Q_BLK + Q_BLK + KEY_CHUNK - 1) // KEY_CHUNK
    hpg = A_HEADS // A_KV_HEADS

    wt = w_ref[...] * ((IDX_DIM ** -0.5) * (IDX_HEADS ** -0.5))
    for h in range(IDX_HEADS):
        wb_ref[h] = jnp.broadcast_to(wt[:, w_lane0 + h:w_lane0 + h + 1], (Q_BLK, LANES))

    q_pos = i * Q_BLK + lax.broadcasted_iota(I32, (Q_BLK, KEY_CHUNK), 0)
    k_off = lax.broadcasted_iota(I32, (Q_BLK, KEY_CHUNK), 1)

    def score_chunk(c, carry):
        k0 = pl.multiple_of(c * KEY_CHUNK, KEY_CHUNK)
        ikc = ik_ref[pl.ds(k0, KEY_CHUNK), :]
        ntile = KEY_CHUNK // LANES
        acc = [jnp.zeros((Q_BLK, LANES), F32)] * ntile
        for hg in range(IDX_HEADS // IDX_HEADS_PER_DOT):
            lhs = iq_ref[0, hg * IDX_HEADS_PER_DOT * Q_BLK:(hg + 1) * IDX_HEADS_PER_DOT * Q_BLK, :]
            d = lax.dot_general(lhs, ikc, (((1,), (1,)), ((), ())), preferred_element_type=F32)
            for hl in range(IDX_HEADS_PER_DOT):
                wb = wb_ref[hg * IDX_HEADS_PER_DOT + hl]
                r = jnp.maximum(d[hl * Q_BLK:(hl + 1) * Q_BLK], 0.0)
                acc = [acc[j] + wb * r[:, j * LANES:(j + 1) * LANES] for j in range(ntile)]
        sc = jnp.concatenate(acc, axis=1)
        key_ref[c] = jnp.where(k0 + k_off <= q_pos, _sort_key(sc), INT_MIN)
        return carry

    lax.fori_loop(0, nch, score_chunk, 0)

    t, n_wanted = _kth_largest_bitsliced(key_ref, plane_ref, cand_ref, nch, topk)
    _lower_surplus_ties(key_ref, cand_ref, nch, t, n_wanted)
    thr = jnp.maximum(t, INT_MIN + 1)

    m_ref[...] = jnp.full(m_ref.shape, NEG_BIG, F32)
    l_ref[...] = jnp.zeros(l_ref.shape, F32)
    acc_ref[...] = jnp.zeros(acc_ref.shape, F32)
    ones = jnp.ones((KEY_CHUNK, A_HEAD_DIM), BF16)

    def attend_chunk(c, carry):
        k0 = pl.multiple_of(c * KEY_CHUNK, KEY_CHUNK)
        mask = key_ref[c] >= thr
        for g in range(A_KV_HEADS):
            qg = q_ref[0, g * hpg * Q_BLK:(g + 1) * hpg * Q_BLK, :]
            kc = k_ref[pl.ds(k0, KEY_CHUNK), g * LANES:(g + 1) * LANES]
            vc = v_ref[pl.ds(k0, KEY_CHUNK), g * LANES:(g + 1) * LANES]
            s = lax.dot_general(qg, kc, (((1,), (1,)), ((), ())), preferred_element_type=F32)
            _flash_step(s.reshape(hpg, Q_BLK, KEY_CHUNK), mask, jnp.concatenate([vc, ones], axis=1),
                        m_ref, l_ref, acc_ref, g)
        return carry

    lax.fori_loop(0, nch, attend_chunk, 0)
    for g in range(A_KV_HEADS):
        o = (acc_ref[g] / l_ref[g]).astype(o_ref.dtype)
        for hl in range(hpg):
            h = g * hpg + hl
            o_ref[:, h * A_HEAD_DIM:(h + 1) * A_HEAD_DIM] = o[hl * Q_BLK:(hl + 1) * Q_BLK]


def dsa_prompt(iq_hm, z, col, q_hm, ik, k, v, t):
    nb = t // Q_BLK
    topk = min(TOPK_MAX, t // 4)
    nch_max = (t + KEY_CHUNK - 1) // KEY_CHUNK
    ngroups_max = (nch_max + GROUP_CHUNKS - 1) // GROUP_CHUNKS
    hpg = A_HEADS // A_KV_HEADS
    kern = functools.partial(_dsa_prompt_kernel, topk=topk, w_lane0=IDX_DIM)
    return pl.pallas_call(
        kern,
        grid=(nb,),
        in_specs=[pl.BlockSpec((1, IDX_HEADS * Q_BLK, IDX_DIM), lambda i: (i, 0, 0)),
                  pl.BlockSpec((Q_BLK, LANES), lambda i: (i, col["ikw"] // LANES)),
                  pl.BlockSpec((1, A_HEADS * Q_BLK, A_HEAD_DIM), lambda i: (i, 0, 0)),
                  _whole_vmem(), _whole_vmem(), _whole_vmem()],
        out_specs=pl.BlockSpec((Q_BLK, A_Q_W), lambda i: (i, 0)),
        out_shape=jax.ShapeDtypeStruct((t, A_Q_W), BF16),
        scratch_shapes=[pltpu.VMEM((nch_max, Q_BLK, KEY_CHUNK), I32),
                        pltpu.VMEM((ngroups_max, WORD, Q_BLK, LANES), I32),
                        pltpu.VMEM((ngroups_max, Q_BLK, LANES), I32),
                        pltpu.VMEM((IDX_HEADS, Q_BLK, LANES), F32),
                        pltpu.VMEM((A_KV_HEADS, hpg * Q_BLK, LANES), F32),
                        pltpu.VMEM((A_KV_HEADS, hpg * Q_BLK, LANES), F32),
                        pltpu.VMEM((A_KV_HEADS, hpg * Q_BLK, A_HEAD_DIM), F32)],
        compiler_params=_cp(("arbitrary",), 60),
        name="dsa_prompt",
    )(iq_hm, z, q_hm, ik, k, v)


S_ROWS = SUBLANES
ATT_PAGES = 16


def _dsa_sample_scores_kernel(pt_ref, iq_ref, w_ref, ikn_ref, *rest, n_pages, t_new):
    pages = rest[:n_pages]
    key_ref = rest[n_pages]
    wt = w_ref[0] * ((IDX_DIM ** -0.5) * (IDX_HEADS ** -0.5))
    wb = [jnp.broadcast_to(wt[:, h:h + 1], (S_ROWS, LANES)) for h in range(IDX_HEADS)]
    iq = iq_ref[0]

    def tile_scores(ik_tile, transposed):
        dims = (((1,), (0,)), ((), ())) if transposed else (((1,), (1,)), ((), ()))
        d = lax.dot_general(iq, ik_tile, dims, preferred_element_type=F32)
        acc = jnp.zeros((S_ROWS, LANES), F32)
        for h in range(IDX_HEADS):
            acc = acc + wb[h] * jnp.maximum(d[h * S_ROWS:(h + 1) * S_ROWS], 0.0)
        return acc

    for r in range(n_pages):
        key_ref[0, r] = _sort_key(tile_scores(pages[r][0].astype(BF16), True))
    row = lax.broadcasted_iota(I32, (S_ROWS, LANES), 0)
    lane = lax.broadcasted_iota(I32, (S_ROWS, LANES), 1)
    valid = jnp.where(lane < t_new, lane, S_ROWS) <= row
    key_ref[0, n_pages] = jnp.where(valid, _sort_key(tile_scores(ikn_ref[0], False)), INT_MIN)


def _dsa_sample_thr_kernel(key_ref, adj_ref, thr_ref, last_ref, *, topk):
    b, n_tiles = key_ref.shape[0], key_ref.shape[1]
    pos_bits = (n_tiles * LANES - 1).bit_length()
    lane = lax.broadcasted_iota(I32, (b, S_ROWS, LANES), 2)

    def count(pred):
        def body(c, cnt):
            return cnt + jnp.where(pred(key_ref[:, c], c), 1, 0)
        cnt = lax.fori_loop(0, n_tiles, body, jnp.zeros((b, S_ROWS, LANES), I32))
        return jnp.sum(cnt, axis=2, keepdims=True)

    def bit_step(it, cur):
        cand = cur | jnp.left_shift(jnp.int32(1), 31 - it)
        return jnp.where(count(lambda x, c: x >= (cand ^ INT_MIN)) >= topk, cand, cur)

    t = lax.fori_loop(0, 32, bit_step, jnp.zeros((b, S_ROWS, 1), I32)) ^ INT_MIN
    live = t > INT_MIN
    n_above = count(lambda x, c: x > t)
    n_tied = count(lambda x, c: x == t)
    n_wanted = topk - n_above
    last_ref[...] = jnp.full(last_ref.shape, n_tiles * LANES, I32)

    @pl.when(jnp.max(jnp.where(live, n_tied - n_wanted, 0)) > 0)
    def _():
        def pos_step(it, q):
            cand = q | jnp.left_shift(jnp.int32(1), pos_bits - 1 - it)
            before = count(lambda x, c: jnp.where(x == t, 1, 0) * jnp.where(c * LANES + lane < cand, 1, 0) > 0)
            return jnp.where(before < n_wanted, cand, q)
        last = lax.fori_loop(0, pos_bits, pos_step, jnp.zeros((b, S_ROWS, 1), I32))
        last_ref[...] = jnp.broadcast_to(last, last_ref.shape)

    last = last_ref[...]
    live_i = jnp.where(live, 1, 0)

    def rewrite(c, carry):
        keys = key_ref[:, c]
        drop = jnp.where(keys == t, 1, 0) * jnp.where(c * LANES + lane > last, 1, 0) * live_i
        adj_ref[:, c] = keys - drop
        return carry
    lax.fori_loop(0, n_tiles, rewrite, 0)
    thr_ref[...] = jnp.broadcast_to(jnp.maximum(t, INT_MIN + 1), thr_ref.shape)


def dsa_sample_index(page_table, iq_s, iw_s, ik_new, cache_ik, t_new):
    b, n_pages = page_table.shape
    topk = min(TOPK_MAX, (n_pages * PAGE_SIZE + t_new) // 4)

    def page_spec(r):
        return pl.BlockSpec((1, IDX_DIM, PAGE_SIZE), lambda bi, pt: (pt[bi, r], 0, 0))

    grid_spec = pltpu.PrefetchScalarGridSpec(
        num_scalar_prefetch=1,
        grid=(b,),
        in_specs=[pl.BlockSpec((1, IDX_HEADS * S_ROWS, IDX_DIM), lambda bi, pt: (bi, 0, 0)),
                  pl.BlockSpec((1, S_ROWS, IDX_HEADS), lambda bi, pt: (bi, 0, 0)),
                  pl.BlockSpec((1, PAGE_SIZE, IDX_DIM), lambda bi, pt: (bi, 0, 0))]
                 + [page_spec(r) for r in range(n_pages)],
        out_specs=pl.BlockSpec((1, n_pages + 1, S_ROWS, LANES), lambda bi, pt: (bi, 0, 0, 0)),
    )
    keys = pl.pallas_call(
        functools.partial(_dsa_sample_scores_kernel, n_pages=n_pages, t_new=t_new),
        grid_spec=grid_spec,
        out_shape=jax.ShapeDtypeStruct((b, n_pages + 1, S_ROWS, LANES), I32),
        compiler_params=_cp(("parallel",), 32),
        name="dsa_sample_scores",
    )(page_table, iq_s, iw_s, ik_new, *([cache_ik] * n_pages))
    return pl.pallas_call(
        functools.partial(_dsa_sample_thr_kernel, topk=topk),
        in_specs=[_whole_vmem()],
        out_specs=[_whole_vmem(), _whole_vmem()],
        out_shape=[jax.ShapeDtypeStruct(keys.shape, I32), jax.ShapeDtypeStruct((b, S_ROWS, LANES), I32)],
        scratch_shapes=[pltpu.VMEM((b, S_ROWS, LANES), I32)],
        compiler_params=pltpu.CompilerParams(vmem_limit_bytes=32 * 2 ** 20),
        name="dsa_sample_threshold",
    )(keys)


def _dsa_sample_attend_kernel(pt_ref, key_ref, thr_ref, q_ref, kn_ref, vn_ref, *rest, n_pages):
    kpages = rest[:ATT_PAGES]
    vpages = rest[ATT_PAGES:2 * ATT_PAGES]
    o_ref, m_ref, l_ref, acc_ref = rest[2 * ATT_PAGES:]
    j = pl.program_id(1)
    hpg = A_HEADS // A_KV_HEADS
    thr = thr_ref[0]

    @pl.when(j == 0)
    def _():
        m_ref[...] = jnp.full(m_ref.shape, NEG_BIG, F32)
        l_ref[...] = jnp.zeros(l_ref.shape, F32)
        acc_ref[...] = jnp.zeros(acc_ref.shape, F32)

    def attend(mask, g, kg, vg):
        qg = q_ref[0, g * hpg * S_ROWS:(g + 1) * hpg * S_ROWS, :]
        s = lax.dot_general(qg, kg, (((1,), (1,)), ((), ())), preferred_element_type=F32)
        vext = jnp.concatenate([vg, jnp.ones(vg.shape, BF16)], axis=1)
        _flash_step(s.reshape(hpg, S_ROWS, s.shape[-1]), mask, vext, m_ref, l_ref, acc_ref, g)

    mask = jnp.concatenate([key_ref[0, j * ATT_PAGES + r] >= thr for r in range(ATT_PAGES)], axis=1)
    for g in range(A_KV_HEADS):
        rows = pl.ds(g, PAGE_SIZE, stride=A_KV_HEADS)
        kg = jnp.concatenate([kp[rows, :].astype(BF16) for kp in kpages], axis=0)
        vg = jnp.concatenate([vp[rows, :].astype(BF16) for vp in vpages], axis=0)
        attend(mask, g, kg, vg)

    @pl.when(j == pl.num_programs(1) - 1)
    def _():
        mask_new = key_ref[0, n_pages] >= thr
        for g in range(A_KV_HEADS):
            attend(mask_new, g, kn_ref[0, :, g * LANES:(g + 1) * LANES], vn_ref[0, :, g * LANES:(g + 1) * LANES])
        for g in range(A_KV_HEADS):
            o_ref[0, g * hpg * S_ROWS:(g + 1) * hpg * S_ROWS, :] = (acc_ref[g] / l_ref[g]).astype(o_ref.dtype)


def dsa_sample_attend(page_table, keys, thr, q_s, k_new, v_new, cache_k, cache_v):
    b, n_pages = page_table.shape
    steps = n_pages // ATT_PAGES
    hpg = A_HEADS // A_KV_HEADS
    page_rows = PAGE_SIZE * A_KV_HEADS

    def page_spec(r):
        return pl.BlockSpec((page_rows, A_HEAD_DIM), lambda bi, j, pt: (pt[bi, j * ATT_PAGES + r], 0))

    grid_spec = pltpu.PrefetchScalarGridSpec(
        num_scalar_prefetch=1,
        grid=(b, steps),
        in_specs=[pl.BlockSpec((1, n_pages + 1, S_ROWS, LANES), lambda bi, j, pt: (bi, 0, 0, 0)),
                  pl.BlockSpec((1, S_ROWS, LANES), lambda bi, j, pt: (bi, 0, 0)),
                  pl.BlockSpec((1, A_HEADS * S_ROWS, A_HEAD_DIM), lambda bi, j, pt: (bi, 0, 0)),
                  pl.BlockSpec((1, PAGE_SIZE, A_KV_W), lambda bi, j, pt: (bi, 0, 0)),
                  pl.BlockSpec((1, PAGE_SIZE, A_KV_W), lambda bi, j, pt: (bi, 0, 0))]
                 + [page_spec(r) for r in range(ATT_PAGES)] * 2,
        out_specs=pl.BlockSpec((1, A_HEADS * S_ROWS, A_HEAD_DIM), lambda bi, j, pt: (bi, 0, 0)),
        scratch_shapes=[pltpu.VMEM((A_KV_HEADS, hpg * S_ROWS, LANES), F32),
                        pltpu.VMEM((A_KV_HEADS, hpg * S_ROWS, LANES), F32),
                        pltpu.VMEM((A_KV_HEADS, hpg * S_ROWS, A_HEAD_DIM), F32)],
    )
    return pl.pallas_call(
        functools.partial(_dsa_sample_attend_kernel, n_pages=n_pages),
        grid_spec=grid_spec,
        out_shape=jax.ShapeDtypeStruct((b, A_HEADS * S_ROWS, A_HEAD_DIM), BF16),
        compiler_params=_cp(("parallel", "arbitrary"), 48),
        name="dsa_sample_attend",
    )(page_table, keys, thr, q_s, k_new, v_new, *([cache_k] * ATT_PAGES), *([cache_v] * ATT_PAGES))


def _cumsum_rows(x):
    n = x.shape[0]
    row = lax.broadcasted_iota(I32, x.shape, 0)
    d = 1
    while d < n:
        x = x + jnp.where(row >= d, pltpu.roll(x, d, axis=0), 0.0)
        d *= 2
    return x


def _hgrn_kernel(q_ref, f_ref, i_ref, g_ref, lb_ref, ng_ref, s0_ref, o_ref, so_ref, st_ref,
                 *, chunk, sub, t_valid):
    n = pl.program_id(1)
    nsub = chunk // sub

    @pl.when(n == 0)
    def _():
        for h in range(B_HEADS):
            st_ref[h] = s0_ref[0, h].T

    lbx = lb_ref[...]
    lbe = jnp.exp(lbx - jnp.max(lbx, axis=0, keepdims=True))
    lb_all = lbe[0:1] / jnp.sum(lbe, axis=0, keepdims=True)
    row = lax.broadcasted_iota(I32, (chunk, B_HEAD_DIM), 0)
    valid = row < t_valid
    tt = lax.broadcasted_iota(I32, (chunk, nsub * chunk), 0)
    cc = lax.broadcasted_iota(I32, (chunk, nsub * chunk), 1)
    pair_ok = ((cc // chunk) == (tt // sub)) & ((cc % chunk) <= tt)
    ng = ng_ref[...]

    for h in range(B_HEADS):
        sl = slice(h * B_HEAD_DIM, (h + 1) * B_HEAD_DIM)
        lb = lb_all[:, sl]
        f = lb + (1.0 - lb) * jax.nn.sigmoid(f_ref[:, sl])
        logf = jnp.where(valid, jnp.log(f), 0.0)
        kk = jnp.where(valid, 1.0 - f, 0.0)
        q = q_ref[:, sl]
        iv = i_ref[:, sl]
        cum = _cumsum_rows(logf)
        last = cum[chunk - 1:chunk]
        st = st_ref[h]

        o = lax.dot_general((q * jnp.exp(cum)).astype(BF16), st.astype(BF16),
                            (((1,), (1,)), ((), ())), preferred_element_type=F32)

        refs = [jnp.zeros((1, B_HEAD_DIM), F32)] + [cum[s * sub - 1:s * sub] for s in range(1, nsub)]
        ref_row = refs[0]
        for s in range(1, nsub):
            ref_row = jnp.where(row >= s * sub, refs[s], ref_row)
        qt = (q * jnp.exp(cum - ref_row)).astype(BF16)
        kcat = jnp.concatenate(
            [(kk * jnp.exp(jnp.minimum(r - cum, EXP_CLAMP))).astype(BF16) for r in refs], axis=0)
        a = lax.dot_general(qt, kcat, (((1,), (1,)), ((), ())), preferred_element_type=F32)
        p = jnp.where(pair_ok, a, 0.0).astype(BF16)
        icat = jnp.concatenate([iv.astype(BF16)] * nsub, axis=0)
        o = o + jnp.dot(p, icat, preferred_element_type=F32)

        kd = (kk * jnp.exp(last - cum)).astype(BF16)
        upd = lax.dot_general(iv.astype(BF16), kd, (((0,), (0,)), ((), ())), preferred_element_type=F32)
        st_ref[h] = st * jnp.exp(last) + upd

        on = o * lax.rsqrt(jnp.mean(o * o, axis=-1, keepdims=True) + RMS_EPS) * ng
        gate = g_ref[:, sl]
        o_ref[:, sl] = (on * (gate * jax.nn.sigmoid(gate))).astype(o_ref.dtype)

    @pl.when(n == pl.num_programs(1) - 1)
    def _():
        for h in range(B_HEADS):
            so_ref[0, h] = st_ref[h].T


def hgrn(zsrc, colblk, lb, ng, s0, batch, t_pad, chunk, sub, t_valid):
    nch = t_pad // chunk
    kern = functools.partial(_hgrn_kernel, chunk=chunk, sub=sub, t_valid=t_valid)

    def zspec(cb):
        return pl.BlockSpec((chunk, B_W), lambda b, n: (b * nch + n, cb))

    return pl.pallas_call(
        kern,
        grid=(batch, nch),
        in_specs=[zspec(colblk[0]), zspec(colblk[1]), zspec(colblk[2]), zspec(colblk[3]),
                  pl.BlockSpec(lb.shape, lambda b, n: (0, 0)),
                  pl.BlockSpec((1, B_HEAD_DIM), lambda b, n: (0, 0)),
                  pl.BlockSpec((1, B_HEADS, B_HEAD_DIM, B_HEAD_DIM), lambda b, n: (b, 0, 0, 0))],
        out_specs=[pl.BlockSpec((chunk, B_W), lambda b, n: (b * nch + n, 0)),
                   pl.BlockSpec((1, B_HEADS, B_HEAD_DIM, B_HEAD_DIM), lambda b, n: (b, 0, 0, 0))],
        out_shape=[jax.ShapeDtypeStruct((batch * t_pad, B_W), BF16),
                   jax.ShapeDtypeStruct((batch, B_HEADS, B_HEAD_DIM, B_HEAD_DIM), F32)],
        scratch_shapes=[pltpu.VMEM((B_HEADS, B_HEAD_DIM, B_HEAD_DIM), F32)],
        compiler_params=_cp(("parallel", "arbitrary"), 32),
        name="hgrn",
    )(zsrc, zsrc, zsrc, zsrc, lb, ng, s0)


def _mem_attn_kernel(q_ref, mk_ref, mv_ref, o_ref):
    scale = M_HEAD_DIM ** -0.5
    for h in range(M_HEADS):
        sl = slice(h * M_HEAD_DIM, (h + 1) * M_HEAD_DIM)
        q = q_ref[0, :, sl].astype(BF16)
        mk = mk_ref[0, :, sl].astype(BF16)
        mv = mv_ref[0, :, sl].astype(BF16)
        s = lax.dot_general(q, mk, (((1,), (1,)), ((), ())), preferred_element_type=F32) * scale
        p = jnp.exp(s - jnp.max(s, axis=-1, keepdims=True))
        p = p / jnp.sum(p, axis=-1, keepdims=True)
        o_ref[0, :, sl] = jnp.dot(p.astype(BF16), mv, preferred_element_type=F32).astype(o_ref.dtype)


def mem_attention(q3, qcolblk, mk3, mkcolblk, mv3, mvcolblk, batch, t, tq):
    m = mk3.shape[1]
    return pl.pallas_call(
        _mem_attn_kernel,
        grid=(batch, t // tq),
        in_specs=[pl.BlockSpec((1, tq, M_W), lambda b, i: (b, i, qcolblk)),
                  pl.BlockSpec((1, m, M_W), lambda b, i: (b, 0, mkcolblk)),
                  pl.BlockSpec((1, m, M_W), lambda b, i: (b, 0, mvcolblk))],
        out_specs=pl.BlockSpec((1, tq, M_W), lambda b, i: (b, i, 0)),
        out_shape=jax.ShapeDtypeStruct((batch, t, M_W), BF16),
        compiler_params=_cp(("parallel", "parallel"), 32),
        name="mem_attention",
    )(q3, mk3, mv3)


def _merge_kernel(oa_ref, ob_ref, om_ref, pa_ref, pb_ref, pm_ref, ga_ref, gb_ref, gm_ref, o_ref):
    acc = jax.nn.sigmoid(ga_ref[...]) * jnp.dot(oa_ref[...], pa_ref[...], preferred_element_type=F32)
    acc += jax.nn.sigmoid(gb_ref[...]) * jnp.dot(ob_ref[...], pb_ref[...], preferred_element_type=F32)
    acc += jax.nn.sigmoid(gm_ref[...]) * jnp.dot(om_ref[...], pm_ref[...], preferred_element_type=F32)
    o_ref[...] = acc.astype(o_ref.dtype)


def merge(oa, ob, om, pa, pb, pm, z, col, tm, tn):
    nt = oa.shape[0]
    d = pa.shape[1]

    def act(w):
        return pl.BlockSpec((tm, w), lambda j, i: (i, 0))

    def wgt(w):
        return pl.BlockSpec((w, tn), lambda j, i: (0, j))

    def gate(name):
        return pl.BlockSpec((tm, tn), lambda j, i: (i, col[name] // tn + j))

    return pl.pallas_call(
        _merge_kernel,
        grid=(d // tn, nt // tm),
        in_specs=[act(oa.shape[1]), act(ob.shape[1]), act(om.shape[1]),
                  wgt(pa.shape[0]), wgt(pb.shape[0]), wgt(pm.shape[0]),
                  gate("ga"), gate("gb"), gate("gm")],
        out_specs=pl.BlockSpec((tm, tn), lambda j, i: (i, j)),
        out_shape=jax.ShapeDtypeStruct((nt, d), BF16),
        compiler_params=_cp(("parallel", "parallel"), 48),
        name="merge",
    )(oa, ob, om, pa, pb, pm, z, z, z)


def _resid_mm_kernel(a_ref, b_ref, x_ref, xt_ref, o_ref, *, alpha, n_xtiles):
    x = jnp.where(pl.program_id(1) < n_xtiles, x_ref[...], xt_ref[...])
    o_ref[...] = alpha * x + jnp.dot(a_ref[...], b_ref[...], preferred_element_type=F32)


def resid_matmul(a, b, x, x_tail, alpha, tm, tn):
    m, k = a.shape
    n = b.shape[1]
    n_xtiles = x.shape[0] // tm
    assert x.shape[0] % tm == 0 and m // tm == n_xtiles + 1
    return pl.pallas_call(
        functools.partial(_resid_mm_kernel, alpha=alpha, n_xtiles=n_xtiles),
        grid=(n // tn, m // tm),
        in_specs=[pl.BlockSpec((tm, k), lambda j, i: (i, 0)),
                  pl.BlockSpec((k, tn), lambda j, i: (0, j)),
                  pl.BlockSpec((tm, tn), lambda j, i: (jnp.minimum(i, n_xtiles - 1), j)),
                  pl.BlockSpec((tm, tn), lambda j, i: (0, j))],
        out_specs=pl.BlockSpec((tm, tn), lambda j, i: (i, j)),
        out_shape=jax.ShapeDtypeStruct((m, n), F32),
        compiler_params=_cp(("parallel", "parallel"), 48),
        name="resid_matmul",
    )(a, b, x, x_tail)


def _ln(x, g, b):
    mu = jnp.mean(x, axis=-1, keepdims=True)
    xc = x - mu
    var = jnp.mean(xc * xc, axis=-1, keepdims=True)
    return xc * lax.rsqrt(var + LN_EPS) * g + b


def _ln_kernel(x_ref, g_ref, b_ref, o_ref, ob_ref):
    y = _ln(x_ref[...], g_ref[...], b_ref[...])
    o_ref[...] = y
    ob_ref[...] = y.astype(ob_ref.dtype)


def layer_norm_dual(x, g, b, tm):
    n, d = x.shape
    row = pl.BlockSpec((tm, d), lambda i: (i, 0))
    vec = pl.BlockSpec((1, d), lambda i: (0, 0))
    return pl.pallas_call(
        _ln_kernel,
        grid=(n // tm,),
        in_specs=[row, vec, vec],
        out_specs=[row, row],
        out_shape=[jax.ShapeDtypeStruct((n, d), F32), jax.ShapeDtypeStruct((n, d), BF16)],
        compiler_params=_cp(("parallel",), 48),
        name="layer_norm1",
    )(x, g, b)


def _ln_resid_kernel(h_ref, p_ref, g_ref, b_ref, op_ref, os_ref, *, alpha, n_ptiles):
    i = pl.program_id(0)
    y = _ln(alpha * h_ref[...] + p_ref[...], g_ref[...], b_ref[...])

    @pl.when(i < n_ptiles)
    def _():
        op_ref[...] = y

    @pl.when(i == n_ptiles)
    def _():
        os_ref[...] = y[:os_ref.shape[0]]


def layer_norm_resid(h, p, g, b, alpha, tm, n_p, n_s):
    d = h.shape[1]
    n_ptiles = n_p // tm
    assert n_p % tm == 0 and n_s <= tm
    row = pl.BlockSpec((tm, d), lambda i: (i, 0))
    vec = pl.BlockSpec((1, d), lambda i: (0, 0))
    return pl.pallas_call(
        functools.partial(_ln_resid_kernel, alpha=alpha, n_ptiles=n_ptiles),
        grid=(n_ptiles + 1,),
        in_specs=[row, row, vec, vec],
        out_specs=[pl.BlockSpec((tm, d), lambda i: (jnp.minimum(i, n_ptiles - 1), 0)),
                   pl.BlockSpec((n_s, d), lambda i: (0, 0))],
        out_shape=[jax.ShapeDtypeStruct((n_p, d), F32), jax.ShapeDtypeStruct((n_s, d), F32)],
        compiler_params=_cp(("arbitrary",), 48),
        name="layer_norm2",
    )(h, p, g, b)


def _top_ranked(x, n_top, vals_ref):
    kdim = x.shape[0]
    idx = lax.broadcasted_iota(I32, x.shape, 0).astype(F32)

    def body(a, carry):
        x, rank = carry
        m = jnp.max(x, axis=0, keepdims=True)
        first = jnp.min(jnp.where(x == m, idx, float(kdim)), axis=0, keepdims=True)
        sel = idx == first
        vals_ref[pl.ds(a, 1), :] = m
        return jnp.where(sel, -jnp.inf, x), jnp.where(sel, lax.convert_element_type(a, F32), rank)

    _, rank = lax.fori_loop(0, n_top, body, (x, jnp.full(x.shape, float(n_top), F32)))
    return rank, [vals_ref[a:a + 1, :] for a in range(n_top)]


def _top_ranked_distinct(x, n_top):
    rank = jnp.full(x.shape, float(n_top), F32)
    vals = []
    for a in range(n_top):
        m = jnp.max(x, axis=0, keepdims=True)
        sel = x == m
        rank = jnp.where(sel, float(a), rank)
        x = jnp.where(sel, -jnp.inf, x)
        vals.append(m)
    n_ranked = jnp.sum(jnp.where(rank < float(n_top), 1.0, 0.0), axis=0, keepdims=True)
    return rank, vals, n_ranked


def _peer_route_kernel(pq_ref, k1_ref, k2_ref, rank2_ref, cnt1_ref, e1_ref, e2_ref, vals_ref):
    tn = pq_ref.shape[0]
    refs = (pq_ref, k1_ref, k2_ref, rank2_ref, cnt1_ref, e1_ref, e2_ref, vals_ref)
    for h in range(PEER_HEADS):
        irregular = _peer_route_head(h, *refs, distinct=True)
        n_irregular = jnp.sum(irregular.reshape(tn // LANES, LANES), axis=0, keepdims=True)

        @pl.when(jnp.max(n_irregular) > 0.0)
        def _():
            _peer_route_head(h, *refs, distinct=False)


def _peer_route_head(h, pq_ref, k1_ref, k2_ref, rank2_ref, cnt1_ref, e1_ref, e2_ref, vals_ref, *, distinct):
    half = PEER_DKEY // 2
    nt = (((1,), (1,)), ((), ()))
    irregular = jnp.zeros((1, pq_ref.shape[0]), F32)

    def top(x, slot):
        if not distinct:
            return _top_ranked(x, PEER_TOPK, vals_ref.at[slot]) + (None,)
        return _top_ranked_distinct(x, PEER_TOPK)

    q1 = pq_ref[:, h * PEER_DKEY:h * PEER_DKEY + half].astype(BF16)
    q2 = pq_ref[:, h * PEER_DKEY + half:(h + 1) * PEER_DKEY].astype(BF16)
    s1 = lax.dot_general(k1_ref[...].astype(BF16), q1, nt, preferred_element_type=F32)
    s2 = lax.dot_general(k2_ref[...].astype(BF16), q2, nt, preferred_element_type=F32)
    rank1, v1, n1 = top(s1, 0)
    rank2, v2, n2 = top(s2, 1)
    v2m = jnp.concatenate(v2, axis=0)
    cand = jnp.concatenate([v1[a] + v2m for a in range(PEER_TOPK)], axis=0)
    crank, _, nc = top(cand, 2)
    if distinct:
        for cnt in (n1, n2, nc):
            irregular = irregular + jnp.where(cnt == float(PEER_TOPK), 0.0, 1.0)
    chosen = jnp.where(crank < float(PEER_TOPK), 1.0, 0.0)
    cmax = v1[0] + v2[0]
    zsum = jnp.sum(chosen * jnp.exp(cand - cmax), axis=0, keepdims=True)
    cnt1 = jnp.zeros_like(s1)
    for a in range(PEER_TOPK):
        m_a = jnp.sum(chosen[a * PEER_TOPK:(a + 1) * PEER_TOPK], axis=0, keepdims=True)
        cnt1 = jnp.where(rank1 == float(a), m_a, cnt1)
    rank2_ref[h] = rank2.astype(rank2_ref.dtype)
    cnt1_ref[h] = cnt1
    e1_ref[h] = jnp.exp(s1 - v1[0])
    e2_ref[h] = (jnp.exp(s2 - v2[0]) / zsum).astype(e2_ref.dtype)
    return irregular


def peer_route(pq, k1, k2, tn):
    n = pq.shape[0]
    shape = (PEER_HEADS, PEER_NKEYS, n)
    ospec = pl.BlockSpec((PEER_HEADS, PEER_NKEYS, tn), lambda i: (0, 0, i))
    kspec = pl.BlockSpec(k1.shape, lambda i: (0, 0))
    return pl.pallas_call(
        _peer_route_kernel,
        grid=(n // tn,),
        in_specs=[pl.BlockSpec((tn, pq.shape[1]), lambda i: (i, 0)), kspec, kspec],
        out_specs=[ospec] * 4,
        out_shape=[jax.ShapeDtypeStruct(shape, BF16), jax.ShapeDtypeStruct(shape, F32),
                   jax.ShapeDtypeStruct(shape, F32), jax.ShapeDtypeStruct(shape, BF16)],
        scratch_shapes=[pltpu.VMEM((3, PEER_TOPK, tn), F32)],
        compiler_params=_cp(("parallel",), 32),
        name="peer_route",
    )(pq, k1, k2)


PEER_EROWS = 4


def _gelu_exact(x):
    return 0.5 * x * (1.0 + lax.erf(x * (2.0 ** -0.5)))


def _peer_dense_kernel(x_ref, u_ref, v_ref, rank2_ref, cnt1_ref, e1_ref, e2_ref, o_ref, z_ref, *, n_tiles):
    r = pl.program_id(1)
    rd, wr = (r + 1) % 2, r % 2

    @pl.when(r == 0)
    def _():
        o_ref[...] = jnp.zeros(o_ref.shape, o_ref.dtype)
        z_ref[...] = jnp.zeros(z_ref.shape, z_ref.dtype)

    o_ref[...] += lax.dot_general(z_ref[rd], v_ref[...], (((0,), (0,)), ((), ())), preferred_element_type=F32)
    at = lax.dot_general(u_ref[...], x_ref[...], (((1,), (1,)), ((), ())), preferred_element_type=F32)
    act = _gelu_exact(at).astype(BF16)
    tile = jnp.minimum(r, n_tiles - 1)
    for rr in range(PEER_EROWS):
        i1 = tile * PEER_EROWS + rr
        g = jnp.zeros((PEER_NKEYS, x_ref.shape[0]), BF16)
        for h in range(PEER_HEADS):
            cnt = cnt1_ref[h, pl.ds(i1, 1), :].astype(BF16)
            e1 = e1_ref[h, pl.ds(i1, 1), :].astype(BF16)
            g = g + jnp.where(rank2_ref[h] < cnt, e2_ref[h] * e1, jnp.zeros((), BF16))
        z_ref[wr, rr * PEER_NKEYS:(rr + 1) * PEER_NKEYS, :] = g * act[rr * PEER_NKEYS:(rr + 1) * PEER_NKEYS]


def peer_dense(xb, u, v, rank2, cnt1, e1, e2, tn):
    n, d = xb.shape
    ne = u.shape[0]
    te = PEER_EROWS * PEER_NKEYS
    n_tiles = ne // te
    aux = pl.BlockSpec((PEER_HEADS, PEER_NKEYS, tn), lambda j, r: (0, 0, j))
    return pl.pallas_call(
        functools.partial(_peer_dense_kernel, n_tiles=n_tiles),
        grid=(n // tn, n_tiles + 1),
        in_specs=[pl.BlockSpec((tn, d), lambda j, r: (j, 0)),
                  pl.BlockSpec((te, d), lambda j, r: (jnp.minimum(r, n_tiles - 1), 0)),
                  pl.BlockSpec((te, d), lambda j, r: (jnp.maximum(r - 1, 0), 0)),
                  aux, aux, aux, aux],
        out_specs=pl.BlockSpec((tn, d), lambda j, r: (j, 0)),
        out_shape=jax.ShapeDtypeStruct((n, d), F32),
        scratch_shapes=[pltpu.VMEM((2, te, tn), BF16)],
        compiler_params=_cp(("parallel", "arbitrary"), 60),
        name="peer_dense",
    )(xb, u, v, rank2, cnt1, e1, e2)


def kernel(x_prompt, x_sample, cache_k, cache_v, cache_idx_k, cache_mem_k, cache_mem_v, state_hgrn, page_table,
           mem_prompt, w_in, w_mem_kv, hgrn_lb, hgrn_norm_g, p_a, p_b, p_m, w_out, ln1_g, ln1_b, w_pq,
           peer_sub_k1, peer_sub_k2, peer_u, peer_v, ln2_g, ln2_b):
    depth = w_in.shape[0]
    assert depth == 1, "single trunk layer"
    bp, t, d = x_prompt.shape
    bs, ts, _ = x_sample.shape
    assert bp == 1 and d == D_MODEL and t % KEY_CHUNK == 0 and bs * ts <= Q_BLK and ts <= S_ROWS
    n_p, n_s = bp * t, bs * ts
    n_real = n_p + n_s
    nt = -(-n_real // TOK_TILE) * TOK_TILE
    n_pool = cache_k.shape[1]
    past_len = page_table.shape[1] * PAGE_SIZE
    alpha = (2 * depth) ** 0.25
    nb_p = n_p // Q_BLK

    assert n_p % TOK_TILE == 0 and nt == n_p + TOK_TILE
    xp, xs = x_prompt.reshape(n_p, d), x_sample.reshape(n_s, d)
    xb = _pad_rows(jnp.concatenate([xp.astype(BF16), xs.astype(BF16)], axis=0), nt)
    w_packed_t = pack_w_in_t(jnp.swapaxes(w_in[0], 0, 1))
    z, peer_ub, peer_vb = matmul_nt_with_casts(xb, w_packed_t, TOK_TILE, Z_TILE_WIDE, peer_u[0], peer_v[0],
                                               name="proj_in")

    def zcols(name, width, lo, hi):
        return z[lo:hi, COLS[name]:COLS[name] + width]

    pos = jnp.concatenate([jnp.arange(t, dtype=I32), past_len + jnp.tile(jnp.arange(ts, dtype=I32), bs),
                           jnp.zeros((nt - n_real,), I32)])
    q_hm, iq_hm, k4, kb, v4, vb, ik_rope, ikb = rope_all(z, COLS, rope_tables(pos))
    nb = q_hm.shape[0]

    oa_p = dsa_prompt(iq_hm.reshape(nb, IDX_HEADS * Q_BLK, IDX_DIM), z, COLS,
                      q_hm.reshape(nb, A_HEADS * Q_BLK, A_HEAD_DIM), ikb[:n_p], kb[:n_p], vb[:n_p], t)

    def per_seq(a, rows):
        return _pad_rows(a.reshape((bs, ts) + a.shape[1:]), rows, axis=1)

    q_s = per_seq(q_hm[nb_p, :, :n_s].transpose(1, 0, 2), S_ROWS).transpose(0, 2, 1, 3)
    q_s = q_s.reshape(bs, A_HEADS * S_ROWS, A_HEAD_DIM)
    iq_s = per_seq(iq_hm[nb_p, :, :n_s].transpose(1, 0, 2), S_ROWS).transpose(0, 2, 1, 3)
    iq_s = iq_s.reshape(bs, IDX_HEADS * S_ROWS, IDX_DIM)
    iw_s = per_seq(z[n_p:n_real, COLS["ikw"] + IDX_DIM:COLS["ikw"] + IDX_DIM + IDX_HEADS], S_ROWS)
    ik_new = per_seq(ikb[n_p:n_real], PAGE_SIZE)
    k_new = per_seq(kb[n_p:n_real], PAGE_SIZE)
    v_new = per_seq(vb[n_p:n_real], PAGE_SIZE)
    keys, thr = dsa_sample_index(page_table, iq_s, iw_s, ik_new, jnp.swapaxes(cache_idx_k[0], 1, 2), ts)
    cache_rows = n_pool * PAGE_SIZE * A_KV_HEADS
    os_hm = dsa_sample_attend(page_table, keys, thr, q_s, k_new, v_new,
                              cache_k[0].reshape(cache_rows, A_HEAD_DIM), cache_v[0].reshape(cache_rows, A_HEAD_DIM))
    oa_s = os_hm.reshape(bs, A_HEADS, S_ROWS, A_HEAD_DIM)[:, :, :ts].transpose(0, 2, 1, 3).reshape(n_s, A_Q_W)
    o_a = _pad_rows(jnp.concatenate([oa_p, oa_s], axis=0), nt)

    b_blk = [COLS[c] // B_W for c in ("bq", "bf", "bi", "bg")]
    ng = hgrn_norm_g[0][None]
    ob_p, st_p = hgrn(z, b_blk, hgrn_lb, ng, jnp.zeros((bp, B_HEADS, B_HEAD_DIM, B_HEAD_DIM), F32),
                      bp, t, HGRN_CHUNK, HGRN_SUB, HGRN_CHUNK)
    zs_b = per_seq(z[n_p:n_real, COLS["bq"]:COLS["bq"] + 4 * B_W], S_ROWS).reshape(bs * S_ROWS, 4 * B_W)
    ob_s, st_s = hgrn(zs_b, [0, 1, 2, 3], hgrn_lb, ng, state_hgrn[0], bs, S_ROWS, S_ROWS, S_ROWS, ts)
    ob_s = ob_s.reshape(bs, S_ROWS, B_W)[:, :ts].reshape(n_s, B_W)
    o_b = _pad_rows(jnp.concatenate([ob_p[:n_p], ob_s], axis=0), nt)

    mem_kv = matmul(mem_prompt[0].astype(BF16), w_mem_kv[0].astype(BF16), mem_prompt.shape[1], Z_TILE, name="mem_kv")
    om_p = mem_attention(z[None], COLS["mq"] // M_W, mem_kv[None], 0, mem_kv[None], 1, bp, t, TOK_TILE)[0]
    zs_m = per_seq(zcols("mq", M_W, n_p, n_real), S_ROWS)
    n_mem = cache_mem_k.shape[2]
    om_s = mem_attention(zs_m, 0, cache_mem_k[0].reshape(bs, n_mem, M_W), 0,
                         cache_mem_v[0].reshape(bs, n_mem, M_W), 0, bs, S_ROWS, S_ROWS)
    o_m = _pad_rows(jnp.concatenate([om_p, om_s[:, :ts].reshape(n_s, M_W)], axis=0), nt)

    merged = merge(o_a, o_b, o_m, p_a[0].astype(BF16), p_b[0].astype(BF16), p_m[0].astype(BF16), z, COLS,
                   TOK_TILE, WIDE_TILE)
    h_pre = resid_matmul(merged, w_out[0].astype(BF16), xp, _pad_rows(xs, TOK_TILE), alpha, TOK_TILE, WIDE_TILE)
    h, hb = layer_norm_dual(h_pre, ln1_g[0][None], ln1_b[0][None], LN_TILE)

    pq = matmul(hb, w_pq[0].astype(BF16), TOK_TILE, Z_TILE, name="peer_query")
    rank2, cnt1, e1, e2 = peer_route(pq, peer_sub_k1[0], peer_sub_k2[0], ROUTE_TILE)
    p_out = peer_dense(hb, peer_ub, peer_vb, rank2, cnt1, e1, e2, TOK_TILE)
    y_p, y_s = layer_norm_resid(h, p_out, ln2_g[0][None], ln2_b[0][None], alpha, LN_TILE, n_p, n_s)

    kv_p = (depth, bp, t, A_KV_HEADS, A_HEAD_DIM)
    kv_s = (depth, bs, ts, A_KV_HEADS, A_HEAD_DIM)
    mem_shape = (depth, bp, mem_prompt.shape[1], M_HEADS, M_HEAD_DIM)
    g4 = A_KV_HEADS
    return (y_p.reshape(bp, t, d), y_s.reshape(bs, ts, d),
            k4[:n_p * g4].reshape(kv_p), v4[:n_p * g4].reshape(kv_p), ik_rope[:n_p].reshape(depth, bp, t, IDX_DIM),
            mem_kv[:, :M_W].reshape(mem_shape), mem_kv[:, M_W:].reshape(mem_shape), st_p[None],
            k4[n_p * g4:n_real * g4].reshape(kv_s), v4[n_p * g4:n_real * g4].reshape(kv_s),
            ik_rope[n_p:n_real].reshape(depth, bs, ts, IDX_DIM), st_s[None])
```

```python
import functools
import math

import jax
import jax.numpy as jnp
import numpy as np
from jax import lax
from jax.experimental import pallas as pl
from jax.experimental.pallas import tpu as pltpu

F32 = jnp.float32
BF16 = jnp.bfloat16
I32 = jnp.int32

A_HEADS, A_KV_HEADS, A_HEAD_DIM = 16, 4, 128
IDX_HEADS, IDX_DIM = 32, 64
TOPK_MAX = 256
B_HEADS, B_HEAD_DIM = 8, 128
M_HEADS, M_HEAD_DIM = 4, 256
PEER_HEADS, PEER_NKEYS, PEER_DKEY, PEER_TOPK = 8, 128, 256, 16
PAGE_SIZE = 128
ROPE_THETA = 10000.0
LN_EPS = 1e-5
RMS_EPS = 1e-6
D_MODEL = 4096

A_Q_W = A_HEADS * A_HEAD_DIM
A_KV_W = A_KV_HEADS * A_HEAD_DIM
IDX_Q_W = IDX_HEADS * IDX_DIM
B_W = B_HEADS * B_HEAD_DIM
M_W = M_HEADS * M_HEAD_DIM

_SEGS = (("aq", A_Q_W), ("iq", IDX_Q_W), ("ak", A_KV_W), ("av", A_KV_W), ("bq", B_W), ("bf", B_W),
         ("bi", B_W), ("bg", B_W), ("mq", M_W), ("ga", D_MODEL), ("gb", D_MODEL), ("gm", D_MODEL),
         ("ikw", 128))
COLS = {}
_off = 0
for _name, _w in _SEGS:
    COLS[_name] = _off
    _off += _w
Z_TILE = 512
Z_WIDTH = -(-_off // Z_TILE) * Z_TILE
Z_TILE_WIDE = 1536
assert Z_WIDTH % Z_TILE_WIDE == 0

LANES = 128
SUBLANES = 8
Q_BLK = 256
KEY_CHUNK = 512
HGRN_CHUNK = 64
HGRN_SUB = 16
EXP_CLAMP = 80.0
Q_LOG2_SCALE = (A_HEAD_DIM ** -0.5) * math.log2(math.e)
INT_MIN = -2 ** 31
NEG_BIG = -1e30
TOK_TILE = 512
WIDE_TILE = 1024
LN_TILE = 256


def _cp(sem, vmem_mb):
    return pltpu.CompilerParams(dimension_semantics=sem, vmem_limit_bytes=vmem_mb * 2 ** 20)


def _whole_vmem():
    return pl.BlockSpec(memory_space=pltpu.VMEM)


def _pad_rows(a, n, axis=0):
    pad = [(0, 0)] * a.ndim
    pad[axis] = (0, n - a.shape[axis])
    return jnp.pad(a, pad)


def _mm_kernel(a_ref, b_ref, o_ref):
    o_ref[...] = jnp.dot(a_ref[...], b_ref[...], preferred_element_type=F32).astype(o_ref.dtype)


def matmul(a, b, tm, tn, out_dtype=F32, name="matmul"):
    m, k = a.shape
    n = b.shape[1]
    return pl.pallas_call(
        _mm_kernel,
        grid=(m // tm, n // tn),
        in_specs=[pl.BlockSpec((tm, k), lambda i, j: (i, 0)), pl.BlockSpec((k, tn), lambda i, j: (0, j))],
        out_specs=pl.BlockSpec((tm, tn), lambda i, j: (i, j)),
        out_shape=jax.ShapeDtypeStruct((m, n), out_dtype),
        compiler_params=_cp(("parallel", "parallel"), 48),
        name=name,
    )(a, b)


def _mm_nt_kernel(a_ref, bt_ref, o_ref):
    o_ref[...] = lax.dot_general(a_ref[...], bt_ref[...], (((1,), (1,)), ((), ())),
                                 preferred_element_type=F32).astype(o_ref.dtype)


def matmul_nt(a, bt, tm, tn, out_dtype=F32, name="matmul_nt"):
    m, k = a.shape
    n = bt.shape[0]
    return pl.pallas_call(
        _mm_nt_kernel,
        grid=(n // tn, m // tm),
        in_specs=[pl.BlockSpec((tm, k), lambda j, i: (i, 0)), pl.BlockSpec((tn, k), lambda j, i: (j, 0))],
        out_specs=pl.BlockSpec((tm, tn), lambda j, i: (i, j)),
        out_shape=jax.ShapeDtypeStruct((m, n), out_dtype),
        compiler_params=_cp(("parallel", "parallel"), 48),
        name=name,
    )(a, bt)


CAST_ROWS = 128


def _mm_nt_cast_kernel(a_ref, bt_ref, u_ref, v_ref, o_ref, ub_ref, vb_ref):
    o_ref[...] = lax.dot_general(a_ref[...], bt_ref[...], (((1,), (1,)), ((), ())),
                                 preferred_element_type=F32)
    ub_ref[...] = u_ref[...].astype(ub_ref.dtype)
    vb_ref[...] = v_ref[...].astype(vb_ref.dtype)


def matmul_nt_with_casts(a, bt, tm, tn, u, v, name):
    m, k = a.shape
    n = bt.shape[0]
    n_i = m // tm
    n_cast = u.shape[0] // CAST_ROWS
    assert u.shape == v.shape and u.shape[0] % CAST_ROWS == 0 and n_cast <= (n // tn) * n_i

    def tab(j, i):
        return (jnp.minimum(j * n_i + i, n_cast - 1), 0)

    tspec = pl.BlockSpec((CAST_ROWS, u.shape[1]), tab)
    return pl.pallas_call(
        _mm_nt_cast_kernel,
        grid=(n // tn, n_i),
        in_specs=[pl.BlockSpec((tm, k), lambda j, i: (i, 0)), pl.BlockSpec((tn, k), lambda j, i: (j, 0)),
                  tspec, tspec],
        out_specs=[pl.BlockSpec((tm, tn), lambda j, i: (i, j)), tspec, tspec],
        out_shape=[jax.ShapeDtypeStruct((m, n), F32), jax.ShapeDtypeStruct(u.shape, BF16),
                   jax.ShapeDtypeStruct(v.shape, BF16)],
        compiler_params=_cp(("arbitrary", "arbitrary"), 56),
        name=name,
    )(a, bt, u, v)


PACK_SHIFT = IDX_DIM + IDX_HEADS
PACK_T_ROWS = 256


def _pack_t_plan():
    r = PACK_T_ROWS
    splits = (A_Q_W, A_KV_W, A_KV_W, IDX_Q_W, IDX_DIM, IDX_HEADS, 4 * B_W + M_W + 3 * D_MODEL)
    src = dict(zip(("aq", "ak", "av", "iq", "ik", "iw", "rest"), np.concatenate([[0], np.cumsum(splits)[:-1]])))
    assert src["ik"] % r == 0 and src["rest"] == src["ik"] + PACK_SHIFT and splits[-1] % r == 0
    plan = []
    for name in ("aq", "iq", "ak", "av"):
        width = A_Q_W if name in ("aq", "iq") else A_KV_W
        plan += [(0, (src[name] + o) // r, (COLS[name] + o) // r) for o in range(0, width, r)]
    plan += [(2, src["ik"] // r, COLS["ikw"] // r)]
    plan += [(1, src["ik"] // r + 1 + t, COLS["bq"] // r + t) for t in range(splits[-1] // r)]
    plan += [(3, 0, t) for t in range(COLS["ikw"] // r + 1, Z_WIDTH // r)]
    assert sorted(p[2] for p in plan) == list(range(Z_WIDTH // r))
    return np.array(plan, np.int32)


def _pack_t_kernel(plan_ref, a_ref, o_ref, tail_ref):
    mode = plan_ref[pl.program_id(0), 0]
    keep = PACK_T_ROWS - PACK_SHIFT

    @pl.when(mode == 0)
    def _():
        o_ref[...] = a_ref[...].astype(o_ref.dtype)

    @pl.when(mode == 1)
    def _():
        o_ref[:keep, :] = tail_ref[...].astype(o_ref.dtype)
        o_ref[keep:, :] = a_ref[:PACK_SHIFT, :].astype(o_ref.dtype)
        tail_ref[...] = a_ref[PACK_SHIFT:, :]

    @pl.when(mode == 2)
    def _():
        o_ref[...] = jnp.zeros(o_ref.shape, o_ref.dtype)
        o_ref[:PACK_SHIFT, :] = a_ref[:PACK_SHIFT, :].astype(o_ref.dtype)
        tail_ref[...] = a_ref[PACK_SHIFT:, :]

    @pl.when(mode == 3)
    def _():
        o_ref[...] = jnp.zeros(o_ref.shape, o_ref.dtype)


def pack_w_in_t(wt):
    dm = wt.shape[1]
    plan = _pack_t_plan()
    n_src = -(-wt.shape[0] // PACK_T_ROWS)
    grid_spec = pltpu.PrefetchScalarGridSpec(
        num_scalar_prefetch=1,
        grid=(len(plan),),
        in_specs=[pl.BlockSpec((PACK_T_ROWS, dm), lambda j, p: (jnp.minimum(p[j, 1], n_src - 1), 0))],
        out_specs=pl.BlockSpec((PACK_T_ROWS, dm), lambda j, p: (p[j, 2], 0)),
        scratch_shapes=[pltpu.VMEM((PACK_T_ROWS - PACK_SHIFT, dm), F32)],
    )
    return pl.pallas_call(
        _pack_t_kernel,
        grid_spec=grid_spec,
        out_shape=jax.ShapeDtypeStruct((Z_WIDTH, dm), BF16),
        compiler_params=_cp(("arbitrary",), 32),
        name="pack_w_in",
    )(jnp.asarray(plan), wt)


def _rope_kernel(q_ref, iq_ref, k_ref, v_ref, ikw_ref, c128_ref, s128_ref, c64_ref, s64_ref,
                 qhm_ref, iqhm_ref, k4_ref, kb_ref, v4_ref, vb_ref, iko_ref, ikb_ref):
    c128, s128 = c128_ref[...], s128_ref[...]
    c64, s64 = c64_ref[...], s64_ref[...]
    lane = lax.broadcasted_iota(I32, (Q_BLK, LANES), 1)
    first_half = (lane % IDX_DIM) < (IDX_DIM // 2)

    def rope128(x):
        return x * c128 + pltpu.roll(x, A_HEAD_DIM // 2, axis=1) * s128

    def rope64(x):
        partner = jnp.where(first_half, pltpu.roll(x, LANES - IDX_DIM // 2, axis=1),
                            pltpu.roll(x, IDX_DIM // 2, axis=1))
        return x * c64 + partner * s64

    for h in range(A_HEADS):
        qhm_ref[0, h] = (rope128(q_ref[:, h * LANES:(h + 1) * LANES]) * Q_LOG2_SCALE).astype(qhm_ref.dtype)
    for h in range(A_KV_HEADS):
        rows = pl.ds(h, Q_BLK, stride=A_KV_HEADS)
        kr = rope128(k_ref[:, h * LANES:(h + 1) * LANES])
        k4_ref[rows, :] = kr
        kb_ref[:, h * LANES:(h + 1) * LANES] = kr.astype(kb_ref.dtype)
        vh = v_ref[:, h * LANES:(h + 1) * LANES]
        v4_ref[rows, :] = vh
        vb_ref[:, h * LANES:(h + 1) * LANES] = vh.astype(vb_ref.dtype)
    for p in range(IDX_HEADS // 2):
        r = rope64(iq_ref[:, p * LANES:(p + 1) * LANES]).astype(iqhm_ref.dtype)
        iqhm_ref[0, 2 * p] = r[:, :IDX_DIM]
        iqhm_ref[0, 2 * p + 1] = r[:, IDX_DIM:]
    ikr = rope64(ikw_ref[...])[:, :IDX_DIM]
    iko_ref[...] = ikr
    ikb_ref[...] = ikr.astype(ikb_ref.dtype)


def rope_all(z, col, tabs):
    nt = z.shape[0]
    nb = nt // Q_BLK
    c128, s128, c64, s64 = tabs
    tab_spec = pl.BlockSpec((Q_BLK, LANES), lambda i: (i, 0))
    kv4_spec = pl.BlockSpec((Q_BLK * A_KV_HEADS, A_HEAD_DIM), lambda i: (i, 0))
    kvb_spec = pl.BlockSpec((Q_BLK, A_KV_W), lambda i: (i, 0))
    ik_spec = pl.BlockSpec((Q_BLK, IDX_DIM), lambda i: (i, 0))
    kv4 = jax.ShapeDtypeStruct((nt * A_KV_HEADS, A_HEAD_DIM), F32)
    kvb = jax.ShapeDtypeStruct((nt, A_KV_W), BF16)
    return pl.pallas_call(
        _rope_kernel,
        grid=(nb,),
        in_specs=[pl.BlockSpec((Q_BLK, A_Q_W), lambda i: (i, col["aq"] // A_Q_W)),
                  pl.BlockSpec((Q_BLK, IDX_Q_W), lambda i: (i, col["iq"] // IDX_Q_W)),
                  pl.BlockSpec((Q_BLK, A_KV_W), lambda i: (i, col["ak"] // A_KV_W)),
                  pl.BlockSpec((Q_BLK, A_KV_W), lambda i: (i, col["av"] // A_KV_W)),
                  pl.BlockSpec((Q_BLK, LANES), lambda i: (i, col["ikw"] // LANES)),
                  tab_spec, tab_spec, tab_spec, tab_spec],
        out_specs=[pl.BlockSpec((1, A_HEADS, Q_BLK, A_HEAD_DIM), lambda i: (i, 0, 0, 0)),
                   pl.BlockSpec((1, IDX_HEADS, Q_BLK, IDX_DIM), lambda i: (i, 0, 0, 0)),
                   kv4_spec, kvb_spec, kv4_spec, kvb_spec, ik_spec, ik_spec],
        out_shape=[jax.ShapeDtypeStruct((nb, A_HEADS, Q_BLK, A_HEAD_DIM), BF16),
                   jax.ShapeDtypeStruct((nb, IDX_HEADS, Q_BLK, IDX_DIM), BF16),
                   kv4, kvb, kv4, kvb,
                   jax.ShapeDtypeStruct((nt, IDX_DIM), F32), jax.ShapeDtypeStruct((nt, IDX_DIM), BF16)],
        compiler_params=_cp(("parallel",), 32),
        name="rope",
    )(z, z, z, z, z, c128, s128, c64, s64)


def rope_tables(pos):
    def tab(half, reps):
        inv_freq = ROPE_THETA ** (-jnp.arange(half, dtype=F32) / half)
        ang = pos.astype(F32)[:, None] * inv_freq[None, :]
        c, s = jnp.cos(ang), jnp.sin(ang)
        return jnp.tile(jnp.concatenate([c, c], 1), (1, reps)), jnp.tile(jnp.concatenate([-s, s], 1), (1, reps))
    c128, s128 = tab(A_HEAD_DIM // 2, 1)
    c64, s64 = tab(IDX_DIM // 2, 2)
    return c128, s128, c64, s64


def _sort_key(x):
    i = pltpu.bitcast(x, I32)
    return i ^ ((i >> 31) & 0x7FFFFFFF)


WORD = 32
GROUP_CHUNKS = WORD * LANES // KEY_CHUNK
_BIT_MASKS = ((16, 0x0000FFFF), (8, 0x00FF00FF), (4, 0x0F0F0F0F), (2, 0x33333333), (1, 0x55555555))


def _transpose_bits(w):
    w = list(w)
    for d, m in _BIT_MASKS:
        mask = jnp.int32(m)
        for j in range(WORD):
            if j & d:
                continue
            lo, hi = w[j], w[j + d]
            t = (lax.shift_right_logical(lo, jnp.int32(d)) ^ hi) & mask
            w[j + d] = hi ^ t
            w[j] = lo ^ lax.shift_left(t, jnp.int32(d))
    return w


def _kth_largest_bitsliced(key_ref, plane_ref, cand_ref, nch, k):
    rows = key_ref.shape[1]
    tiles_per_chunk = KEY_CHUNK // LANES
    ngroups = (nch + GROUP_CHUNKS - 1) // GROUP_CHUNKS

    def build_group(g, carry):
        def build_slab(s, c2):
            r0 = pl.multiple_of(s * SUBLANES, SUBLANES)
            words = []
            for j in range(WORD):
                c = g * GROUP_CHUNKS + j // tiles_per_chunk
                lt = j % tiles_per_chunk
                cc = jnp.minimum(c, key_ref.shape[0] - 1)
                w = key_ref[cc, pl.ds(r0, SUBLANES), lt * LANES:(lt + 1) * LANES] ^ INT_MIN
                words.append(jnp.where(c < nch, w, 0))
            planes = _transpose_bits(words)
            for b in range(WORD):
                plane_ref[g, b, pl.ds(r0, SUBLANES), :] = planes[b]
            return c2
        lax.fori_loop(0, rows // SUBLANES, build_slab, 0)
        cand_ref[g] = jnp.full((rows, LANES), -1, I32)
        return carry

    lax.fori_loop(0, ngroups, build_group, 0)

    def bit_step(it, carry):
        t_u, k_rem = carry
        b = WORD - 1 - it

        def count(g, cnt):
            return cnt + lax.population_count(cand_ref[g] & plane_ref[g, b])
        cnt = lax.fori_loop(0, ngroups, count, jnp.zeros((rows, LANES), I32))
        c1 = jnp.sum(cnt, axis=1, keepdims=True)
        take = c1 >= k_rem
        take_b = jnp.broadcast_to(take, (rows, LANES))

        def update(g, c2):
            e = cand_ref[g]
            a = e & plane_ref[g, b]
            cand_ref[g] = jnp.where(take_b, a, e ^ a)
            return c2
        lax.fori_loop(0, ngroups, update, 0)
        t_u = jnp.where(take, t_u | jnp.left_shift(jnp.int32(1), b), t_u)
        return t_u, jnp.where(take, k_rem, k_rem - c1)

    t_u, k_rem = lax.fori_loop(0, WORD, bit_step, (jnp.zeros((rows, 1), I32), jnp.full((rows, 1), k, I32)))
    return t_u ^ INT_MIN, k_rem


def _lower_surplus_ties(key_ref, cand_ref, nch, t, n_wanted):
    rows = key_ref.shape[1]
    ngroups = (nch + GROUP_CHUNKS - 1) // GROUP_CHUNKS
    group_keys = WORD * LANES
    pos_bits = (key_ref.shape[0] * KEY_CHUNK - 1).bit_length()
    live = t > INT_MIN

    def count_tied(g, cnt):
        return cnt + lax.population_count(cand_ref[g])
    n_tied = jnp.sum(lax.fori_loop(0, ngroups, count_tied, jnp.zeros((rows, LANES), I32)), axis=1, keepdims=True)
    surplus = jnp.where(live, n_tied - n_wanted, 0)

    @pl.when(jnp.max(surplus) > 0)
    def _():
        lane = lax.broadcasted_iota(I32, (rows, LANES), 1)

        def tied_before(q):
            gq = q // group_keys
            jq = jnp.broadcast_to((q // LANES) % WORD, (rows, LANES))
            lq = q % LANES
            low = lax.shift_left(jnp.ones((rows, LANES), I32), jq) - 1

            def body(g, cnt):
                w = cand_ref[g]
                at_tile = jnp.where(lane < lq, lax.shift_right_logical(w, jq) & 1, 0)
                part = lax.population_count(w & low) + at_tile
                return cnt + jnp.where(g < gq, lax.population_count(w), jnp.where(g == gq, part, 0))
            cnt = lax.fori_loop(0, ngroups, body, jnp.zeros((rows, LANES), I32))
            return jnp.sum(cnt, axis=1, keepdims=True)

        def bit_step(it, q):
            cand = q | jnp.left_shift(jnp.int32(1), pos_bits - 1 - it)
            return jnp.where(tied_before(cand) < n_wanted, cand, q)
        last = lax.fori_loop(0, pos_bits, bit_step, jnp.zeros((rows, 1), I32))
        off = lax.broadcasted_iota(I32, (rows, KEY_CHUNK), 1)

        def rewrite(c, carry):
            keys = key_ref[c]
            drop = jnp.where(keys == t, 1, 0) * jnp.where(c * KEY_CHUNK + off > last, 1, 0) * jnp.where(live, 1, 0)
            key_ref[c] = keys - drop
            return carry
        lax.fori_loop(0, nch, rewrite, 0)


def _flash_step(s, mask, vext, m_ref, l_ref, acc_ref, g):
    hh, rr, ss = s.shape
    d = vext.shape[1] // 2
    s = jnp.where(mask[None], s, -jnp.inf).reshape(hh * rr, ss)
    m_old = m_ref[g]
    m_new = jnp.maximum(m_old, jnp.max(s, axis=-1, keepdims=True))
    p = jnp.exp2(s - jnp.concatenate([m_new] * (ss // LANES), axis=1))
    alpha = jnp.exp2(m_old - m_new)
    pv = jnp.dot(p.astype(BF16), vext, preferred_element_type=F32)
    acc_ref[g] = alpha * acc_ref[g] + pv[:, :d]
    l_ref[g] = alpha * l_ref[g] + pv[:, d:]
    m_ref[g] = m_new


IDX_HEADS_PER_DOT = 8


def _dsa_prompt_kernel(iq_ref, w_ref, q_ref, ik_ref, k_ref, v_ref, o_ref,
                       key_ref, plane_ref, cand_ref, wb_ref, m_ref, l_ref, acc_ref, *, topk, w_lane0):
    i = pl.program_id(0)
    nch = (i * Q_BLK + Q_BLK + KEY_CHUNK - 1) // KEY_CHUNK
    hpg = A_HEADS // A_KV_HEADS

    wt = w_ref[...] * ((IDX_DIM ** -0.5) * (IDX_HEADS ** -0.5))
    for h in range(IDX_HEADS):
        wb_ref[h] = jnp.broadcast_to(wt[:, w_lane0 + h:w_lane0 + h + 1], (Q_BLK, LANES))

    q_pos = i * Q_BLK + lax.broadcasted_iota(I32, (Q_BLK, KEY_CHUNK), 0)
    k_off = lax.broadcasted_iota(I32, (Q_BLK, KEY_CHUNK), 1)

    def score_chunk(c, carry):
        k0 = pl.multiple_of(c * KEY_CHUNK, KEY_CHUNK)
        ikc = ik_ref[pl.ds(k0, KEY_CHUNK), :]
        ntile = KEY_CHUNK // LANES
        acc = [jnp.zeros((Q_BLK, LANES), F32)] * ntile
        for hg in range(IDX_HEADS // IDX_HEADS_PER_DOT):
            lhs = iq_ref[0, hg * IDX_HEADS_PER_DOT * Q_BLK:(hg + 1) * IDX_HEADS_PER_DOT * Q_BLK, :]
            d = lax.dot_general(lhs, ikc, (((1,), (1,)), ((), ())), preferred_element_type=F32)
            for hl in range(IDX_HEADS_PER_DOT):
                wb = wb_ref[hg * IDX_HEADS_PER_DOT + hl]
                r = jnp.maximum(d[hl * Q_BLK:(hl + 1) * Q_BLK], 0.0)
                acc = [acc[j] + wb * r[:, j * LANES:(j + 1) * LANES] for j in range(ntile)]
        sc = jnp.concatenate(acc, axis=1)
        key_ref[c] = jnp.where(k0 + k_off <= q_pos, _sort_key(sc), INT_MIN)
        return carry

    lax.fori_loop(0, nch, score_chunk, 0)

    t, n_wanted = _kth_largest_bitsliced(key_ref, plane_ref, cand_ref, nch, topk)
    _lower_surplus_ties(key_ref, cand_ref, nch, t, n_wanted)
    thr = jnp.maximum(t, INT_MIN + 1)

    m_ref[...] = jnp.full(m_ref.shape, NEG_BIG, F32)
    l_ref[...] = jnp.zeros(l_ref.shape, F32)
    acc_ref[...] = jnp.zeros(acc_ref.shape, F32)
    ones = jnp.ones((KEY_CHUNK, A_HEAD_DIM), BF16)

    def attend_chunk(c, carry):
        k0 = pl.multiple_of(c * KEY_CHUNK, KEY_CHUNK)
        mask = key_ref[c] >= thr
        for g in range(A_KV_HEADS):
            qg = q_ref[0, g * hpg * Q_BLK:(g + 1) * hpg * Q_BLK, :]
            kc = k_ref[pl.ds(k0, KEY_CHUNK), g * LANES:(g + 1) * LANES]
            vc = v_ref[pl.ds(k0, KEY_CHUNK), g * LANES:(g + 1) * LANES]
            s = lax.dot_general(qg, kc, (((1,), (1,)), ((), ())), preferred_element_type=F32)
            _flash_step(s.reshape(hpg, Q_BLK, KEY_CHUNK), mask, jnp.concatenate([vc, ones], axis=1),
                        m_ref, l_ref, acc_ref, g)
        return carry

    lax.fori_loop(0, nch, attend_chunk, 0)
    for g in range(A_KV_HEADS):
        o = (acc_ref[g] / l_ref[g]).astype(o_ref.dtype)
        for hl in range(hpg):
            h = g * hpg + hl
            o_ref[:, h * A_HEAD_DIM:(h + 1) * A_HEAD_DIM] = o[hl * Q_BLK:(hl + 1) * Q_BLK]


def dsa_prompt(iq_hm, z, col, q_hm, ik, k, v, t):
    nb = t // Q_BLK
    topk = min(TOPK_MAX, t // 4)
    nch_max = (t + KEY_CHUNK - 1) // KEY_CHUNK
    ngroups_max = (nch_max + GROUP_CHUNKS - 1) // GROUP_CHUNKS
    hpg = A_HEADS // A_KV_HEADS
    kern = functools.partial(_dsa_prompt_kernel, topk=topk, w_lane0=IDX_DIM)
    return pl.pallas_call(
        kern,
        grid=(nb,),
        in_specs=[pl.BlockSpec((1, IDX_HEADS * Q_BLK, IDX_DIM), lambda i: (i, 0, 0)),
                  pl.BlockSpec((Q_BLK, LANES), lambda i: (i, col["ikw"] // LANES)),
                  pl.BlockSpec((1, A_HEADS * Q_BLK, A_HEAD_DIM), lambda i: (i, 0, 0)),
                  _whole_vmem(), _whole_vmem(), _whole_vmem()],
        out_specs=pl.BlockSpec((Q_BLK, A_Q_W), lambda i: (i, 0)),
        out_shape=jax.ShapeDtypeStruct((t, A_Q_W), BF16),
        scratch_shapes=[pltpu.VMEM((nch_max, Q_BLK, KEY_CHUNK), I32),
                        pltpu.VMEM((ngroups_max, WORD, Q_BLK, LANES), I32),
                        pltpu.VMEM((ngroups_max, Q_BLK, LANES), I32),
                        pltpu.VMEM((IDX_HEADS, Q_BLK, LANES), F32),
                        pltpu.VMEM((A_KV_HEADS, hpg * Q_BLK, LANES), F32),
                        pltpu.VMEM((A_KV_HEADS, hpg * Q_BLK, LANES), F32),
                        pltpu.VMEM((A_KV_HEADS, hpg * Q_BLK, A_HEAD_DIM), F32)],
        compiler_params=_cp(("arbitrary",), 60),
        name="dsa_prompt",
    )(iq_hm, z, q_hm, ik, k, v)


S_ROWS = SUBLANES
ATT_PAGES = 16


def _dsa_sample_scores_kernel(pt_ref, iq_ref, w_ref, ikn_ref, *rest, n_pages, t_new):
    pages = rest[:n_pages]
    key_ref = rest[n_pages]
    wt = w_ref[0] * ((IDX_DIM ** -0.5) * (IDX_HEADS ** -0.5))
    wb = [jnp.broadcast_to(wt[:, h:h + 1], (S_ROWS, LANES)) for h in range(IDX_HEADS)]
    iq = iq_ref[0]

    def tile_scores(ik_tile, transposed):
        dims = (((1,), (0,)), ((), ())) if transposed else (((1,), (1,)), ((), ()))
        d = lax.dot_general(iq, ik_tile, dims, preferred_element_type=F32)
        acc = jnp.zeros((S_ROWS, LANES), F32)
        for h in range(IDX_HEADS):
            acc = acc + wb[h] * jnp.maximum(d[h * S_ROWS:(h + 1) * S_ROWS], 0.0)
        return acc

    for r in range(n_pages):
        key_ref[0, r] = _sort_key(tile_scores(pages[r][0].astype(BF16), True))
    row = lax.broadcasted_iota(I32, (S_ROWS, LANES), 0)
    lane = lax.broadcasted_iota(I32, (S_ROWS, LANES), 1)
    valid = jnp.where(lane < t_new, lane, S_ROWS) <= row
    key_ref[0, n_pages] = jnp.where(valid, _sort_key(tile_scores(ikn_ref[0], False)), INT_MIN)


def _dsa_sample_thr_kernel(key_ref, adj_ref, thr_ref, last_ref, *, topk):
    b, n_tiles = key_ref.shape[0], key_ref.shape[1]
    pos_bits = (n_tiles * LANES - 1).bit_length()
    lane = lax.broadcasted_iota(I32, (b, S_ROWS, LANES), 2)

    def count(pred):
        def body(c, cnt):
            return cnt + jnp.where(pred(key_ref[:, c], c), 1, 0)
        cnt = lax.fori_loop(0, n_tiles, body, jnp.zeros((b, S_ROWS, LANES), I32))
        return jnp.sum(cnt, axis=2, keepdims=True)

    def bit_step(it, cur):
        cand = cur | jnp.left_shift(jnp.int32(1), 31 - it)
        return jnp.where(count(lambda x, c: x >= (cand ^ INT_MIN)) >= topk, cand, cur)

    t = lax.fori_loop(0, 32, bit_step, jnp.zeros((b, S_ROWS, 1), I32)) ^ INT_MIN
    live = t > INT_MIN
    n_above = count(lambda x, c: x > t)
    n_tied = count(lambda x, c: x == t)
    n_wanted = topk - n_above
    last_ref[...] = jnp.full(last_ref.shape, n_tiles * LANES, I32)

    @pl.when(jnp.max(jnp.where(live, n_tied - n_wanted, 0)) > 0)
    def _():
        def pos_step(it, q):
            cand = q | jnp.left_shift(jnp.int32(1), pos_bits - 1 - it)
            before = count(lambda x, c: jnp.where(x == t, 1, 0) * jnp.where(c * LANES + lane < cand, 1, 0) > 0)
            return jnp.where(before < n_wanted, cand, q)
        last = lax.fori_loop(0, pos_bits, pos_step, jnp.zeros((b, S_ROWS, 1), I32))
        last_ref[...] = jnp.broadcast_to(last, last_ref.shape)

    last = last_ref[...]
    live_i = jnp.where(live, 1, 0)

    def rewrite(c, carry):
        keys = key_ref[:, c]
        drop = jnp.where(keys == t, 1, 0) * jnp.where(c * LANES + lane > last, 1, 0) * live_i
        adj_ref[:, c] = keys - drop
        return carry
    lax.fori_loop(0, n_tiles, rewrite, 0)
    thr_ref[...] = jnp.broadcast_to(jnp.maximum(t, INT_MIN + 1), thr_ref.shape)


def dsa_sample_index(page_table, iq_s, iw_s, ik_new, cache_ik, t_new):
    b, n_pages = page_table.shape
    topk = min(TOPK_MAX, (n_pages * PAGE_SIZE + t_new) // 4)

    def page_spec(r):
        return pl.BlockSpec((1, IDX_DIM, PAGE_SIZE), lambda bi, pt: (pt[bi, r], 0, 0))

    grid_spec = pltpu.PrefetchScalarGridSpec(
        num_scalar_prefetch=1,
        grid=(b,),
        in_specs=[pl.BlockSpec((1, IDX_HEADS * S_ROWS, IDX_DIM), lambda bi, pt: (bi, 0, 0)),
                  pl.BlockSpec((1, S_ROWS, IDX_HEADS), lambda bi, pt: (bi, 0, 0)),
                  pl.BlockSpec((1, PAGE_SIZE, IDX_DIM), lambda bi, pt: (bi, 0, 0))]
                 + [page_spec(r) for r in range(n_pages)],
        out_specs=pl.BlockSpec((1, n_pages + 1, S_ROWS, LANES), lambda bi, pt: (bi, 0, 0, 0)),
    )
    keys = pl.pallas_call(
        functools.partial(_dsa_sample_scores_kernel, n_pages=n_pages, t_new=t_new),
        grid_spec=grid_spec,
        out_shape=jax.ShapeDtypeStruct((b, n_pages + 1, S_ROWS, LANES), I32),
        compiler_params=_cp(("parallel",), 32),
        name="dsa_sample_scores",
    )(page_table, iq_s, iw_s, ik_new, *([cache_ik] * n_pages))
    return pl.pallas_call(
        functools.partial(_dsa_sample_thr_kernel, topk=topk),
        in_specs=[_whole_vmem()],
        out_specs=[_whole_vmem(), _whole_vmem()],
        out_shape=[jax.ShapeDtypeStruct(keys.shape, I32), jax.ShapeDtypeStruct((b, S_ROWS, LANES), I32)],
        scratch_shapes=[pltpu.VMEM((b, S_ROWS, LANES), I32)],
        compiler_params=pltpu.CompilerParams(vmem_limit_bytes=32 * 2 ** 20),
        name="dsa_sample_threshold",
    )(keys)


def _dsa_sample_attend_kernel(pt_ref, key_ref, thr_ref, q_ref, kn_ref, vn_ref, *rest, n_pages):
    kpages = rest[:ATT_PAGES]
    vpages = rest[ATT_PAGES:2 * ATT_PAGES]
    o_ref, m_ref, l_ref, acc_ref = rest[2 * ATT_PAGES:]
    j = pl.program_id(1)
    hpg = A_HEADS // A_KV_HEADS
    thr = thr_ref[0]

    @pl.when(j == 0)
    def _():
        m_ref[...] = jnp.full(m_ref.shape, NEG_BIG, F32)
        l_ref[...] = jnp.zeros(l_ref.shape, F32)
        acc_ref[...] = jnp.zeros(acc_ref.shape, F32)

    def attend(mask, g, kg, vg):
        qg = q_ref[0, g * hpg * S_ROWS:(g + 1) * hpg * S_ROWS, :]
        s = lax.dot_general(qg, kg, (((1,), (1,)), ((), ())), preferred_element_type=F32)
        vext = jnp.concatenate([vg, jnp.ones(vg.shape, BF16)], axis=1)
        _flash_step(s.reshape(hpg, S_ROWS, s.shape[-1]), mask, vext, m_ref, l_ref, acc_ref, g)

    mask = jnp.concatenate([key_ref[0, j * ATT_PAGES + r] >= thr for r in range(ATT_PAGES)], axis=1)
    for g in range(A_KV_HEADS):
        rows = pl.ds(g, PAGE_SIZE, stride=A_KV_HEADS)
        kg = jnp.concatenate([kp[rows, :].astype(BF16) for kp in kpages], axis=0)
        vg = jnp.concatenate([vp[rows, :].astype(BF16) for vp in vpages], axis=0)
        attend(mask, g, kg, vg)

    @pl.when(j == pl.num_programs(1) - 1)
    def _():
        mask_new = key_ref[0, n_pages] >= thr
        for g in range(A_KV_HEADS):
            attend(mask_new, g, kn_ref[0, :, g * LANES:(g + 1) * LANES], vn_ref[0, :, g * LANES:(g + 1) * LANES])
        for g in range(A_KV_HEADS):
            o_ref[0, g * hpg * S_ROWS:(g + 1) * hpg * S_ROWS, :] = (acc_ref[g] / l_ref[g]).astype(o_ref.dtype)


def dsa_sample_attend(page_table, keys, thr, q_s, k_new, v_new, cache_k, cache_v):
    b, n_pages = page_table.shape
    steps = n_pages // ATT_PAGES
    hpg = A_HEADS // A_KV_HEADS
    page_rows = PAGE_SIZE * A_KV_HEADS

    def page_spec(r):
        return pl.BlockSpec((page_rows, A_HEAD_DIM), lambda bi, j, pt: (pt[bi, j * ATT_PAGES + r], 0))

    grid_spec = pltpu.PrefetchScalarGridSpec(
        num_scalar_prefetch=1,
        grid=(b, steps),
        in_specs=[pl.BlockSpec((1, n_pages + 1, S_ROWS, LANES), lambda bi, j, pt: (bi, 0, 0, 0)),
                  pl.BlockSpec((1, S_ROWS, LANES), lambda bi, j, pt: (bi, 0, 0)),
                  pl.BlockSpec((1, A_HEADS * S_ROWS, A_HEAD_DIM), lambda bi, j, pt: (bi, 0, 0)),
                  pl.BlockSpec((1, PAGE_SIZE, A_KV_W), lambda bi, j, pt: (bi, 0, 0)),
                  pl.BlockSpec((1, PAGE_SIZE, A_KV_W), lambda bi, j, pt: (bi, 0, 0))]
                 + [page_spec(r) for r in range(ATT_PAGES)] * 2,
        out_specs=pl.BlockSpec((1, A_HEADS * S_ROWS, A_HEAD_DIM), lambda bi, j, pt: (bi, 0, 0)),
        scratch_shapes=[pltpu.VMEM((A_KV_HEADS, hpg * S_ROWS, LANES), F32),
                        pltpu.VMEM((A_KV_HEADS, hpg * S_ROWS, LANES), F32),
                        pltpu.VMEM((A_KV_HEADS, hpg * S_ROWS, A_HEAD_DIM), F32)],
    )
    return pl.pallas_call(
        functools.partial(_dsa_sample_attend_kernel, n_pages=n_pages),
        grid_spec=grid_spec,
        out_shape=jax.ShapeDtypeStruct((b, A_HEADS * S_ROWS, A_HEAD_DIM), BF16),
        compiler_params=_cp(("parallel", "arbitrary"), 48),
        name="dsa_sample_attend",
    )(page_table, keys, thr, q_s, k_new, v_new, *([cache_k] * ATT_PAGES), *([cache_v] * ATT_PAGES))


def _cumsum_rows(x):
    n = x.shape[0]
    row = lax.broadcasted_iota(I32, x.shape, 0)
    d = 1
    while d < n:
        x = x + jnp.where(row >= d, pltpu.roll(x, d, axis=0), 0.0)
        d *= 2
    return x


def _hgrn_kernel(q_ref, f_ref, i_ref, g_ref, lb_ref, ng_ref, s0_ref, o_ref, so_ref, st_ref,
                 *, chunk, sub, t_valid):
    n = pl.program_id(1)
    nsub = chunk // sub

    @pl.when(n == 0)
    def _():
        for h in range(B_HEADS):
            st_ref[h] = s0_ref[0, h].T

    lbx = lb_ref[...]
    lbe = jnp.exp(lbx - jnp.max(lbx, axis=0, keepdims=True))
    lb_all = lbe[0:1] / jnp.sum(lbe, axis=0, keepdims=True)
    row = lax.broadcasted_iota(I32, (chunk, B_HEAD_DIM), 0)
    valid = row < t_valid
    tt = lax.broadcasted_iota(I32, (chunk, nsub * chunk), 0)
    cc = lax.broadcasted_iota(I32, (chunk, nsub * chunk), 1)
    pair_ok = ((cc // chunk) == (tt // sub)) & ((cc % chunk) <= tt)
    ng = ng_ref[...]

    for h in range(B_HEADS):
        sl = slice(h * B_HEAD_DIM, (h + 1) * B_HEAD_DIM)
        lb = lb_all[:, sl]
        f = lb + (1.0 - lb) * jax.nn.sigmoid(f_ref[:, sl])
        logf = jnp.where(valid, jnp.log(f), 0.0)
        kk = jnp.where(valid, 1.0 - f, 0.0)
        q = q_ref[:, sl]
        iv = i_ref[:, sl]
        cum = _cumsum_rows(logf)
        last = cum[chunk - 1:chunk]
        st = st_ref[h]

        o = lax.dot_general((q * jnp.exp(cum)).astype(BF16), st.astype(BF16),
                            (((1,), (1,)), ((), ())), preferred_element_type=F32)

        refs = [jnp.zeros((1, B_HEAD_DIM), F32)] + [cum[s * sub - 1:s * sub] for s in range(1, nsub)]
        ref_row = refs[0]
        for s in range(1, nsub):
            ref_row = jnp.where(row >= s * sub, refs[s], ref_row)
        qt = (q * jnp.exp(cum - ref_row)).astype(BF16)
        kcat = jnp.concatenate(
            [(kk * jnp.exp(jnp.minimum(r - cum, EXP_CLAMP))).astype(BF16) for r in refs], axis=0)
        a = lax.dot_general(qt, kcat, (((1,), (1,)), ((), ())), preferred_element_type=F32)
        p = jnp.where(pair_ok, a, 0.0).astype(BF16)
        icat = jnp.concatenate([iv.astype(BF16)] * nsub, axis=0)
        o = o + jnp.dot(p, icat, preferred_element_type=F32)

        kd = (kk * jnp.exp(last - cum)).astype(BF16)
        upd = lax.dot_general(iv.astype(BF16), kd, (((0,), (0,)), ((), ())), preferred_element_type=F32)
        st_ref[h] = st * jnp.exp(last) + upd

        on = o * lax.rsqrt(jnp.mean(o * o, axis=-1, keepdims=True) + RMS_EPS) * ng
        gate = g_ref[:, sl]
        o_ref[:, sl] = (on * (gate * jax.nn.sigmoid(gate))).astype(o_ref.dtype)

    @pl.when(n == pl.num_programs(1) - 1)
    def _():
        for h in range(B_HEADS):
            so_ref[0, h] = st_ref[h].T


def hgrn(zsrc, colblk, lb, ng, s0, batch, t_pad, chunk, sub, t_valid):
    nch = t_pad // chunk
    kern = functools.partial(_hgrn_kernel, chunk=chunk, sub=sub, t_valid=t_valid)

    def zspec(cb):
        return pl.BlockSpec((chunk, B_W), lambda b, n: (b * nch + n, cb))

    return pl.pallas_call(
        kern,
        grid=(batch, nch),
        in_specs=[zspec(colblk[0]), zspec(colblk[1]), zspec(colblk[2]), zspec(colblk[3]),
                  pl.BlockSpec(lb.shape, lambda b, n: (0, 0)),
                  pl.BlockSpec((1, B_HEAD_DIM), lambda b, n: (0, 0)),
                  pl.BlockSpec((1, B_HEADS, B_HEAD_DIM, B_HEAD_DIM), lambda b, n: (b, 0, 0, 0))],
        out_specs=[pl.BlockSpec((chunk, B_W), lambda b, n: (b * nch + n, 0)),
                   pl.BlockSpec((1, B_HEADS, B_HEAD_DIM, B_HEAD_DIM), lambda b, n: (b, 0, 0, 0))],
        out_shape=[jax.ShapeDtypeStruct((batch * t_pad, B_W), BF16),
                   jax.ShapeDtypeStruct((batch, B_HEADS, B_HEAD_DIM, B_HEAD_DIM), F32)],
        scratch_shapes=[pltpu.VMEM((B_HEADS, B_HEAD_DIM, B_HEAD_DIM), F32)],
        compiler_params=_cp(("parallel", "arbitrary"), 32),
        name="hgrn",
    )(zsrc, zsrc, zsrc, zsrc, lb, ng, s0)


def _mem_attn_kernel(q_ref, mk_ref, mv_ref, o_ref):
    scale = M_HEAD_DIM ** -0.5
    for h in range(M_HEADS):
        sl = slice(h * M_HEAD_DIM, (h + 1) * M_HEAD_DIM)
        q = q_ref[0, :, sl].astype(BF16)
        mk = mk_ref[0, :, sl].astype(BF16)
        mv = mv_ref[0, :, sl].astype(BF16)
        s = lax.dot_general(q, mk, (((1,), (1,)), ((), ())), preferred_element_type=F32) * scale
        p = jnp.exp(s - jnp.max(s, axis=-1, keepdims=True))
        p = p / jnp.sum(p, axis=-1, keepdims=True)
        o_ref[0, :, sl] = jnp.dot(p.astype(BF16), mv, preferred_element_type=F32).astype(o_ref.dtype)


def mem_attention(q3, qcolblk, mk3, mkcolblk, mv3, mvcolblk, batch, t, tq):
    m = mk3.shape[1]
    return pl.pallas_call(
        _mem_attn_kernel,
        grid=(batch, t // tq),
        in_specs=[pl.BlockSpec((1, tq, M_W), lambda b, i: (b, i, qcolblk)),
                  pl.BlockSpec((1, m, M_W), lambda b, i: (b, 0, mkcolblk)),
                  pl.BlockSpec((1, m, M_W), lambda b, i: (b, 0, mvcolblk))],
        out_specs=pl.BlockSpec((1, tq, M_W), lambda b, i: (b, i, 0)),
        out_shape=jax.ShapeDtypeStruct((batch, t, M_W), BF16),
        compiler_params=_cp(("parallel", "parallel"), 32),
        name="mem_attention",
    )(q3, mk3, mv3)


def _merge_kernel(oa_ref, ob_ref, om_ref, pa_ref, pb_ref, pm_ref, ga_ref, gb_ref, gm_ref, o_ref):
    acc = jax.nn.sigmoid(ga_ref[...]) * jnp.dot(oa_ref[...], pa_ref[...], preferred_element_type=F32)
    acc += jax.nn.sigmoid(gb_ref[...]) * jnp.dot(ob_ref[...], pb_ref[...], preferred_element_type=F32)
    acc += jax.nn.sigmoid(gm_ref[...]) * jnp.dot(om_ref[...], pm_ref[...], preferred_element_type=F32)
    o_ref[...] = acc.astype(o_ref.dtype)


def merge(oa, ob, om, pa, pb, pm, z, col, tm, tn):
    nt = oa.shape[0]
    d = pa.shape[1]

    def act(w):
        return pl.BlockSpec((tm, w), lambda j, i: (i, 0))

    def wgt(w):
        return pl.BlockSpec((w, tn), lambda j, i: (0, j))

    def gate(name):
        return pl.BlockSpec((tm, tn), lambda j, i: (i, col[name] // tn + j))

    return pl.pallas_call(
        _merge_kernel,
        grid=(d // tn, nt // tm),
        in_specs=[act(oa.shape[1]), act(ob.shape[1]), act(om.shape[1]),
                  wgt(pa.shape[0]), wgt(pb.shape[0]), wgt(pm.shape[0]),
                  gate("ga"), gate("gb"), gate("gm")],
        out_specs=pl.BlockSpec((tm, tn), lambda j, i: (i, j)),
        out_shape=jax.ShapeDtypeStruct((nt, d), BF16),
        compiler_params=_cp(("parallel", "parallel"), 48),
        name="merge",
    )(oa, ob, om, pa, pb, pm, z, z, z)


def _resid_mm_kernel(a_ref, b_ref, x_ref, xt_ref, o_ref, *, alpha, n_xtiles):
    x = jnp.where(pl.program_id(1) < n_xtiles, x_ref[...], xt_ref[...])
    o_ref[...] = alpha * x + jnp.dot(a_ref[...], b_ref[...], preferred_element_type=F32)


def resid_matmul(a, b, x, x_tail, alpha, tm, tn):
    m, k = a.shape
    n = b.shape[1]
    n_xtiles = x.shape[0] // tm
    assert x.shape[0] % tm == 0 and m // tm == n_xtiles + 1
    return pl.pallas_call(
        functools.partial(_resid_mm_kernel, alpha=alpha, n_xtiles=n_xtiles),
        grid=(n // tn, m // tm),
        in_specs=[pl.BlockSpec((tm, k), lambda j, i: (i, 0)),
                  pl.BlockSpec((k, tn), lambda j, i: (0, j)),
                  pl.BlockSpec((tm, tn), lambda j, i: (jnp.minimum(i, n_xtiles - 1), j)),
                  pl.BlockSpec((tm, tn), lambda j, i: (0, j))],
        out_specs=pl.BlockSpec((tm, tn), lambda j, i: (i, j)),
        out_shape=jax.ShapeDtypeStruct((m, n), F32),
        compiler_params=_cp(("parallel", "parallel"), 48),
        name="resid_matmul",
    )(a, b, x, x_tail)


def _ln(x, g, b):
    mu = jnp.mean(x, axis=-1, keepdims=True)
    xc = x - mu
    var = jnp.mean(xc * xc, axis=-1, keepdims=True)
    return xc * lax.rsqrt(var + LN_EPS) * g + b


def _ln_kernel(x_ref, g_ref, b_ref, o_ref, ob_ref):
    y = _ln(x_ref[...], g_ref[...], b_ref[...])
    o_ref[...] = y
    ob_ref[...] = y.astype(ob_ref.dtype)


def layer_norm_dual(x, g, b, tm):
    n, d = x.shape
    row = pl.BlockSpec((tm, d), lambda i: (i, 0))
    vec = pl.BlockSpec((1, d), lambda i: (0, 0))
    return pl.pallas_call(
        _ln_kernel,
        grid=(n // tm,),
        in_specs=[row, vec, vec],
        out_specs=[row, row],
        out_shape=[jax.ShapeDtypeStruct((n, d), F32), jax.ShapeDtypeStruct((n, d), BF16)],
        compiler_params=_cp(("parallel",), 48),
        name="layer_norm1",
    )(x, g, b)


def _ln_resid_kernel(h_ref, p_ref, g_ref, b_ref, op_ref, os_ref, *, alpha, n_ptiles):
    i = pl.program_id(0)
    y = _ln(alpha * h_ref[...] + p_ref[...], g_ref[...], b_ref[...])

    @pl.when(i < n_ptiles)
    def _():
        op_ref[...] = y

    @pl.when(i == n_ptiles)
    def _():
        os_ref[...] = y[:os_ref.shape[0]]


def layer_norm_resid(h, p, g, b, alpha, tm, n_p, n_s):
    d = h.shape[1]
    n_ptiles = n_p // tm
    assert n_p % tm == 0 and n_s <= tm
    row = pl.BlockSpec((tm, d), lambda i: (i, 0))
    vec = pl.BlockSpec((1, d), lambda i: (0, 0))
    return pl.pallas_call(
        functools.partial(_ln_resid_kernel, alpha=alpha, n_ptiles=n_ptiles),
        grid=(n_ptiles + 1,),
        in_specs=[row, row, vec, vec],
        out_specs=[pl.BlockSpec((tm, d), lambda i: (jnp.minimum(i, n_ptiles - 1), 0)),
                   pl.BlockSpec((n_s, d), lambda i: (0, 0))],
        out_shape=[jax.ShapeDtypeStruct((n_p, d), F32), jax.ShapeDtypeStruct((n_s, d), F32)],
        compiler_params=_cp(("arbitrary",), 48),
        name="layer_norm2",
    )(h, p, g, b)


def _top_ranked(x, n_top, vals_ref):
    kdim = x.shape[0]
    idx = lax.broadcasted_iota(I32, x.shape, 0).astype(F32)

    def body(a, carry):
        x, rank = carry
        m = jnp.max(x, axis=0, keepdims=True)
        first = jnp.min(jnp.where(x == m, idx, float(kdim)), axis=0, keepdims=True)
        sel = idx == first
        vals_ref[pl.ds(a, 1), :] = m
        return jnp.where(sel, -jnp.inf, x), jnp.where(sel, lax.convert_element_type(a, F32), rank)

    _, rank = lax.fori_loop(0, n_top, body, (x, jnp.full(x.shape, float(n_top), F32)))
    return rank, [vals_ref[a:a + 1, :] for a in range(n_top)]


def _top_ranked_distinct(x, n_top):
    rank = jnp.full(x.shape, float(n_top), F32)
    vals = []
    for a in range(n_top):
        m = jnp.max(x, axis=0, keepdims=True)
        sel = x == m
        rank = jnp.where(sel, float(a), rank)
        x = jnp.where(sel, -jnp.inf, x)
        vals.append(m)
    n_ranked = jnp.sum(jnp.where(rank < float(n_top), 1.0, 0.0), axis=0, keepdims=True)
    return rank, vals, n_ranked


def _peer_route_kernel(pq_ref, k1_ref, k2_ref, rank2_ref, cnt1_ref, e1_ref, e2_ref, vals_ref):
    tn = pq_ref.shape[0]
    refs = (pq_ref, k1_ref, k2_ref, rank2_ref, cnt1_ref, e1_ref, e2_ref, vals_ref)

    def per_lane_tile(flags):
        return jnp.sum(flags.reshape(tn // LANES, LANES), axis=0, keepdims=True)

    irregular = [per_lane_tile(_peer_route_head(h, *refs, distinct=True)) for h in range(PEER_HEADS)]

    @pl.when(jnp.max(sum(irregular)) > 0.0)
    def _():
        for h in range(PEER_HEADS):
            @pl.when(jnp.max(irregular[h]) > 0.0)
            def _():
                _peer_route_head(h, *refs, distinct=False)


def _peer_route_head(h, pq_ref, k1_ref, k2_ref, rank2_ref, cnt1_ref, e1_ref, e2_ref, vals_ref, *, distinct):
    half = PEER_DKEY // 2
    nt = (((1,), (1,)), ((), ()))
    irregular = jnp.zeros((1, pq_ref.shape[0]), F32)

    def top(x, slot):
        if not distinct:
            return _top_ranked(x, PEER_TOPK, vals_ref.at[slot]) + (None,)
        return _top_ranked_distinct(x, PEER_TOPK)

    q1 = pq_ref[:, h * PEER_DKEY:h * PEER_DKEY + half].astype(BF16)
    q2 = pq_ref[:, h * PEER_DKEY + half:(h + 1) * PEER_DKEY].astype(BF16)
    s1 = lax.dot_general(k1_ref[...].astype(BF16), q1, nt, preferred_element_type=F32)
    s2 = lax.dot_general(k2_ref[...].astype(BF16), q2, nt, preferred_element_type=F32)
    rank1, v1, n1 = top(s1, 0)
    rank2, v2, n2 = top(s2, 1)
    v2m = jnp.concatenate(v2, axis=0)
    cand = jnp.concatenate([v1[a] + v2m for a in range(PEER_TOPK)], axis=0)
    crank, _, nc = top(cand, 2)
    if distinct:
        for cnt in (n1, n2, nc):
            irregular = irregular + jnp.where(cnt == float(PEER_TOPK), 0.0, 1.0)
    chosen = jnp.where(crank < float(PEER_TOPK), 1.0, 0.0)
    cmax = v1[0] + v2[0]
    zsum = jnp.sum(chosen * jnp.exp(cand - cmax), axis=0, keepdims=True)
    cnt1 = jnp.zeros_like(s1)
    for a in range(PEER_TOPK):
        m_a = jnp.sum(chosen[a * PEER_TOPK:(a + 1) * PEER_TOPK], axis=0, keepdims=True)
        cnt1 = jnp.where(rank1 == float(a), m_a, cnt1)
    rank2_ref[h] = rank2.astype(rank2_ref.dtype)
    cnt1_ref[h] = cnt1
    e1_ref[h] = jnp.exp(s1 - v1[0])
    e2_ref[h] = (jnp.exp(s2 - v2[0]) / zsum).astype(e2_ref.dtype)
    return irregular


def peer_route(pq, k1, k2, tn):
    n = pq.shape[0]
    shape = (PEER_HEADS, PEER_NKEYS, n)
    ospec = pl.BlockSpec((PEER_HEADS, PEER_NKEYS, tn), lambda i: (0, 0, i))
    kspec = pl.BlockSpec(k1.shape, lambda i: (0, 0))
    return pl.pallas_call(
        _peer_route_kernel,
        grid=(n // tn,),
        in_specs=[pl.BlockSpec((tn, pq.shape[1]), lambda i: (i, 0)), kspec, kspec],
        out_specs=[ospec] * 4,
        out_shape=[jax.ShapeDtypeStruct(shape, BF16), jax.ShapeDtypeStruct(shape, F32),
                   jax.ShapeDtypeStruct(shape, F32), jax.ShapeDtypeStruct(shape, BF16)],
        scratch_shapes=[pltpu.VMEM((3, PEER_TOPK, tn), F32)],
        compiler_params=_cp(("parallel",), 32),
        name="peer_route",
    )(pq, k1, k2)


PEER_EROWS = 4


def _gelu_exact(x):
    return 0.5 * x * (1.0 + lax.erf(x * (2.0 ** -0.5)))


def _peer_dense_kernel(x_ref, u_ref, v_ref, rank2_ref, cnt1_ref, e1_ref, e2_ref, o_ref, z_ref, *, n_tiles):
    r = pl.program_id(1)
    rd, wr = (r + 1) % 2, r % 2

    @pl.when(r == 0)
    def _():
        o_ref[...] = jnp.zeros(o_ref.shape, o_ref.dtype)
        z_ref[...] = jnp.zeros(z_ref.shape, z_ref.dtype)

    o_ref[...] += lax.dot_general(z_ref[rd], v_ref[...], (((0,), (0,)), ((), ())), preferred_element_type=F32)
    at = lax.dot_general(u_ref[...], x_ref[...], (((1,), (1,)), ((), ())), preferred_element_type=F32)
    act = _gelu_exact(at).astype(BF16)
    tile = jnp.minimum(r, n_tiles - 1)
    for rr in range(PEER_EROWS):
        i1 = tile * PEER_EROWS + rr
        g = jnp.zeros((PEER_NKEYS, x_ref.shape[0]), BF16)
        for h in range(PEER_HEADS):
            cnt = cnt1_ref[h, pl.ds(i1, 1), :].astype(BF16)
            e1 = e1_ref[h, pl.ds(i1, 1), :].astype(BF16)
            g = g + jnp.where(rank2_ref[h] < cnt, e2_ref[h] * e1, jnp.zeros((), BF16))
        z_ref[wr, rr * PEER_NKEYS:(rr + 1) * PEER_NKEYS, :] = g * act[rr * PEER_NKEYS:(rr + 1) * PEER_NKEYS]


def peer_dense(xb, u, v, rank2, cnt1, e1, e2, tn):
    n, d = xb.shape
    ne = u.shape[0]
    te = PEER_EROWS * PEER_NKEYS
    n_tiles = ne // te
    aux = pl.BlockSpec((PEER_HEADS, PEER_NKEYS, tn), lambda j, r: (0, 0, j))
    return pl.pallas_call(
        functools.partial(_peer_dense_kernel, n_tiles=n_tiles),
        grid=(n // tn, n_tiles + 1),
        in_specs=[pl.BlockSpec((tn, d), lambda j, r: (j, 0)),
                  pl.BlockSpec((te, d), lambda j, r: (jnp.minimum(r, n_tiles - 1), 0)),
                  pl.BlockSpec((te, d), lambda j, r: (jnp.maximum(r - 1, 0), 0)),
                  aux, aux, aux, aux],
        out_specs=pl.BlockSpec((tn, d), lambda j, r: (j, 0)),
        out_shape=jax.ShapeDtypeStruct((n, d), F32),
        scratch_shapes=[pltpu.VMEM((2, te, tn), BF16)],
        compiler_params=_cp(("parallel", "arbitrary"), 60),
        name="peer_dense",
    )(xb, u, v, rank2, cnt1, e1, e2)


def kernel(x_prompt, x_sample, cache_k, cache_v, cache_idx_k, cache_mem_k, cache_mem_v, state_hgrn, page_table,
           mem_prompt, w_in, w_mem_kv, hgrn_lb, hgrn_norm_g, p_a, p_b, p_m, w_out, ln1_g, ln1_b, w_pq,
           peer_sub_k1, peer_sub_k2, peer_u, peer_v, ln2_g, ln2_b):
    depth = w_in.shape[0]
    assert depth == 1, "single trunk layer"
    bp, t, d = x_prompt.shape
    bs, ts, _ = x_sample.shape
    assert bp == 1 and d == D_MODEL and t % KEY_CHUNK == 0 and bs * ts <= Q_BLK and ts <= S_ROWS
    n_p, n_s = bp * t, bs * ts
    n_real = n_p + n_s
    nt = -(-n_real // TOK_TILE) * TOK_TILE
    n_pool = cache_k.shape[1]
    past_len = page_table.shape[1] * PAGE_SIZE
    alpha = (2 * depth) ** 0.25
    nb_p = n_p // Q_BLK

    assert n_p % TOK_TILE == 0 and nt == n_p + TOK_TILE
    xp, xs = x_prompt.reshape(n_p, d), x_sample.reshape(n_s, d)
    xb = _pad_rows(jnp.concatenate([xp.astype(BF16), xs.astype(BF16)], axis=0), nt)
    w_packed_t = pack_w_in_t(jnp.swapaxes(w_in[0], 0, 1))
    z, peer_ub, peer_vb = matmul_nt_with_casts(xb, w_packed_t, TOK_TILE, Z_TILE_WIDE, peer_u[0], peer_v[0],
                                               name="proj_in")

    def zcols(name, width, lo, hi):
        return z[lo:hi, COLS[name]:COLS[name] + width]

    pos = jnp.concatenate([jnp.arange(t, dtype=I32), past_len + jnp.tile(jnp.arange(ts, dtype=I32), bs),
                           jnp.zeros((nt - n_real,), I32)])
    q_hm, iq_hm, k4, kb, v4, vb, ik_rope, ikb = rope_all(z, COLS, rope_tables(pos))
    nb = q_hm.shape[0]

    oa_p = dsa_prompt(iq_hm.reshape(nb, IDX_HEADS * Q_BLK, IDX_DIM), z, COLS,
                      q_hm.reshape(nb, A_HEADS * Q_BLK, A_HEAD_DIM), ikb[:n_p], kb[:n_p], vb[:n_p], t)

    def per_seq(a, rows):
        return _pad_rows(a.reshape((bs, ts) + a.shape[1:]), rows, axis=1)

    q_s = per_seq(q_hm[nb_p, :, :n_s].transpose(1, 0, 2), S_ROWS).transpose(0, 2, 1, 3)
    q_s = q_s.reshape(bs, A_HEADS * S_ROWS, A_HEAD_DIM)
    iq_s = per_seq(iq_hm[nb_p, :, :n_s].transpose(1, 0, 2), S_ROWS).transpose(0, 2, 1, 3)
    iq_s = iq_s.reshape(bs, IDX_HEADS * S_ROWS, IDX_DIM)
    iw_s = per_seq(z[n_p:n_real, COLS["ikw"] + IDX_DIM:COLS["ikw"] + IDX_DIM + IDX_HEADS], S_ROWS)
    ik_new = per_seq(ikb[n_p:n_real], PAGE_SIZE)
    k_new = per_seq(kb[n_p:n_real], PAGE_SIZE)
    v_new = per_seq(vb[n_p:n_real], PAGE_SIZE)
    keys, thr = dsa_sample_index(page_table, iq_s, iw_s, ik_new, jnp.swapaxes(cache_idx_k[0], 1, 2), ts)
    cache_rows = n_pool * PAGE_SIZE * A_KV_HEADS
    os_hm = dsa_sample_attend(page_table, keys, thr, q_s, k_new, v_new,
                              cache_k[0].reshape(cache_rows, A_HEAD_DIM), cache_v[0].reshape(cache_rows, A_HEAD_DIM))
    oa_s = os_hm.reshape(bs, A_HEADS, S_ROWS, A_HEAD_DIM)[:, :, :ts].transpose(0, 2, 1, 3).reshape(n_s, A_Q_W)
    o_a = _pad_rows(jnp.concatenate([oa_p, oa_s], axis=0), nt)

    b_blk = [COLS[c] // B_W for c in ("bq", "bf", "bi", "bg")]
    ng = hgrn_norm_g[0][None]
    ob_p, st_p = hgrn(z, b_blk, hgrn_lb, ng, jnp.zeros((bp, B_HEADS, B_HEAD_DIM, B_HEAD_DIM), F32),
                      bp, t, HGRN_CHUNK, HGRN_SUB, HGRN_CHUNK)
    zs_b = per_seq(z[n_p:n_real, COLS["bq"]:COLS["bq"] + 4 * B_W], S_ROWS).reshape(bs * S_ROWS, 4 * B_W)
    ob_s, st_s = hgrn(zs_b, [0, 1, 2, 3], hgrn_lb, ng, state_hgrn[0], bs, S_ROWS, S_ROWS, S_ROWS, ts)
    ob_s = ob_s.reshape(bs, S_ROWS, B_W)[:, :ts].reshape(n_s, B_W)
    o_b = _pad_rows(jnp.concatenate([ob_p[:n_p], ob_s], axis=0), nt)

    mem_kv = matmul(mem_prompt[0].astype(BF16), w_mem_kv[0].astype(BF16), mem_prompt.shape[1], Z_TILE, name="mem_kv")
    om_p = mem_attention(z[None], COLS["mq"] // M_W, mem_kv[None], 0, mem_kv[None], 1, bp, t, TOK_TILE)[0]
    zs_m = per_seq(zcols("mq", M_W, n_p, n_real), S_ROWS)
    n_mem = cache_mem_k.shape[2]
    om_s = mem_attention(zs_m, 0, cache_mem_k[0].reshape(bs, n_mem, M_W), 0,
                         cache_mem_v[0].reshape(bs, n_mem, M_W), 0, bs, S_ROWS, S_ROWS)
    o_m = _pad_rows(jnp.concatenate([om_p, om_s[:, :ts].reshape(n_s, M_W)], axis=0), nt)

    merged = merge(o_a, o_b, o_m, p_a[0].astype(BF16), p_b[0].astype(BF16), p_m[0].astype(BF16), z, COLS,
                   TOK_TILE, WIDE_TILE)
    h_pre = resid_matmul(merged, w_out[0].astype(BF16), xp, _pad_rows(xs, TOK_TILE), alpha, TOK_TILE, WIDE_TILE)
    h, hb = layer_norm_dual(h_pre, ln1_g[0][None], ln1_b[0][None], LN_TILE)

    pq = matmul(hb, w_pq[0].astype(BF16), TOK_TILE, Z_TILE, name="peer_query")
    rank2, cnt1, e1, e2 = peer_route(pq, peer_sub_k1[0], peer_sub_k2[0], LN_TILE)
    p_out = peer_dense(hb, peer_ub, peer_vb, rank2, cnt1, e1, e2, TOK_TILE)
    y_p, y_s = layer_norm_resid(h, p_out, ln2_g[0][None], ln2_b[0][None], alpha, LN_TILE, n_p, n_s)

    kv_p = (depth, bp, t, A_KV_HEADS, A_HEAD_DIM)
    kv_s = (depth, bs, ts, A_KV_HEADS, A_HEAD_DIM)
    mem_shape = (depth, bp, mem_prompt.shape[1], M_HEADS, M_HEAD_DIM)
    g4 = A_KV_HEADS
    return (y_p.reshape(bp, t, d), y_s.reshape(bs, ts, d),
            k4[:n_p * g4].reshape(kv_p), v4[:n_p * g4].reshape(kv_p), ik_rope[:n_p].reshape(depth, bp, t, IDX_DIM),
            mem_kv[:, :M_W].reshape(mem_shape), mem_kv[:, M_W:].reshape(mem_shape), st_p[None],
            k4[n_p * g4:n_real * g4].reshape(kv_s), v4[n_p * g4:n_real * g4].reshape(kv_s),
            ik_rope[n_p:n_real].reshape(depth, bs, ts, IDX_DIM), st_s[None])
```

```python
import functools
import math

import jax
import jax.numpy as jnp
import numpy as np
from jax import lax
from jax.experimental import pallas as pl
from jax.experimental.pallas import tpu as pltpu

F32 = jnp.float32
BF16 = jnp.bfloat16
I32 = jnp.int32

A_HEADS, A_KV_HEADS, A_HEAD_DIM = 16, 4, 128
IDX_HEADS, IDX_DIM = 32, 64
TOPK_MAX = 256
B_HEADS, B_HEAD_DIM = 8, 128
M_HEADS, M_HEAD_DIM = 4, 256
PEER_HEADS, PEER_NKEYS, PEER_DKEY, PEER_TOPK = 8, 128, 256, 16
PAGE_SIZE = 128
ROPE_THETA = 10000.0
LN_EPS = 1e-5
RMS_EPS = 1e-6
D_MODEL = 4096

A_Q_W = A_HEADS * A_HEAD_DIM
A_KV_W = A_KV_HEADS * A_HEAD_DIM
IDX_Q_W = IDX_HEADS * IDX_DIM
B_W = B_HEADS * B_HEAD_DIM
M_W = M_HEADS * M_HEAD_DIM

_SEGS = (("aq", A_Q_W), ("iq", IDX_Q_W), ("ak", A_KV_W), ("av", A_KV_W), ("bq", B_W), ("bf", B_W),
         ("bi", B_W), ("bg", B_W), ("mq", M_W), ("ga", D_MODEL), ("gb", D_MODEL), ("gm", D_MODEL),
         ("ikw", 128))
COLS = {}
_off = 0
for _name, _w in _SEGS:
    COLS[_name] = _off
    _off += _w
Z_TILE = 512
Z_WIDTH = -(-_off // Z_TILE) * Z_TILE
Z_TILE_WIDE = 1536
assert Z_WIDTH % Z_TILE_WIDE == 0

LANES = 128
SUBLANES = 8
Q_BLK = 256
KEY_CHUNK = 512
HGRN_CHUNK = 64
HGRN_SUB = 16
EXP_CLAMP = 80.0
Q_LOG2_SCALE = (A_HEAD_DIM ** -0.5) * math.log2(math.e)
INT_MIN = -2 ** 31
NEG_BIG = -1e30
TOK_TILE = 512
WIDE_TILE = 1024
LN_TILE = 256


def _cp(sem, vmem_mb):
    return pltpu.CompilerParams(dimension_semantics=sem, vmem_limit_bytes=vmem_mb * 2 ** 20)


def _whole_vmem():
    return pl.BlockSpec(memory_space=pltpu.VMEM)


def _pad_rows(a, n, axis=0):
    pad = [(0, 0)] * a.ndim
    pad[axis] = (0, n - a.shape[axis])
    return jnp.pad(a, pad)


def _mm_kernel(a_ref, b_ref, o_ref):
    o_ref[...] = jnp.dot(a_ref[...], b_ref[...], preferred_element_type=F32).astype(o_ref.dtype)


def matmul(a, b, tm, tn, out_dtype=F32, name="matmul"):
    m, k = a.shape
    n = b.shape[1]
    return pl.pallas_call(
        _mm_kernel,
        grid=(m // tm, n // tn),
        in_specs=[pl.BlockSpec((tm, k), lambda i, j: (i, 0)), pl.BlockSpec((k, tn), lambda i, j: (0, j))],
        out_specs=pl.BlockSpec((tm, tn), lambda i, j: (i, j)),
        out_shape=jax.ShapeDtypeStruct((m, n), out_dtype),
        compiler_params=_cp(("parallel", "parallel"), 48),
        name=name,
    )(a, b)


def _mm_nt_kernel(a_ref, bt_ref, o_ref):
    o_ref[...] = lax.dot_general(a_ref[...], bt_ref[...], (((1,), (1,)), ((), ())),
                                 preferred_element_type=F32).astype(o_ref.dtype)


def matmul_nt(a, bt, tm, tn, out_dtype=F32, name="matmul_nt"):
    m, k = a.shape
    n = bt.shape[0]
    return pl.pallas_call(
        _mm_nt_kernel,
        grid=(n // tn, m // tm),
        in_specs=[pl.BlockSpec((tm, k), lambda j, i: (i, 0)), pl.BlockSpec((tn, k), lambda j, i: (j, 0))],
        out_specs=pl.BlockSpec((tm, tn), lambda j, i: (i, j)),
        out_shape=jax.ShapeDtypeStruct((m, n), out_dtype),
        compiler_params=_cp(("parallel", "parallel"), 48),
        name=name,
    )(a, bt)


CAST_ROWS = 128


def _mm_nt_cast_kernel(a_ref, bt_ref, u_ref, v_ref, o_ref, ub_ref, vb_ref):
    o_ref[...] = lax.dot_general(a_ref[...], bt_ref[...], (((1,), (1,)), ((), ())),
                                 preferred_element_type=F32)
    ub_ref[...] = u_ref[...].astype(ub_ref.dtype)
    vb_ref[...] = v_ref[...].astype(vb_ref.dtype)


def matmul_nt_with_casts(a, bt, tm, tn, u, v, name):
    m, k = a.shape
    n = bt.shape[0]
    n_i = m // tm
    n_cast = u.shape[0] // CAST_ROWS
    assert u.shape == v.shape and u.shape[0] % CAST_ROWS == 0 and n_cast <= (n // tn) * n_i

    def tab(j, i):
        return (jnp.minimum(j * n_i + i, n_cast - 1), 0)

    tspec = pl.BlockSpec((CAST_ROWS, u.shape[1]), tab)
    return pl.pallas_call(
        _mm_nt_cast_kernel,
        grid=(n // tn, n_i),
        in_specs=[pl.BlockSpec((tm, k), lambda j, i: (i, 0)), pl.BlockSpec((tn, k), lambda j, i: (j, 0)),
                  tspec, tspec],
        out_specs=[pl.BlockSpec((tm, tn), lambda j, i: (i, j)), tspec, tspec],
        out_shape=[jax.ShapeDtypeStruct((m, n), F32), jax.ShapeDtypeStruct(u.shape, BF16),
                   jax.ShapeDtypeStruct(v.shape, BF16)],
        compiler_params=_cp(("arbitrary", "arbitrary"), 56),
        name=name,
    )(a, bt, u, v)


PACK_SHIFT = IDX_DIM + IDX_HEADS
PACK_T_ROWS = 256


def _pack_t_plan():
    r = PACK_T_ROWS
    splits = (A_Q_W, A_KV_W, A_KV_W, IDX_Q_W, IDX_DIM, IDX_HEADS, 4 * B_W + M_W + 3 * D_MODEL)
    src = dict(zip(("aq", "ak", "av", "iq", "ik", "iw", "rest"), np.concatenate([[0], np.cumsum(splits)[:-1]])))
    assert src["ik"] % r == 0 and src["rest"] == src["ik"] + PACK_SHIFT and splits[-1] % r == 0
    plan = []
    for name in ("aq", "iq", "ak", "av"):
        width = A_Q_W if name in ("aq", "iq") else A_KV_W
        plan += [(0, (src[name] + o) // r, (COLS[name] + o) // r) for o in range(0, width, r)]
    plan += [(2, src["ik"] // r, COLS["ikw"] // r)]
    plan += [(1, src["ik"] // r + 1 + t, COLS["bq"] // r + t) for t in range(splits[-1] // r)]
    plan += [(3, 0, t) for t in range(COLS["ikw"] // r + 1, Z_WIDTH // r)]
    assert sorted(p[2] for p in plan) == list(range(Z_WIDTH // r))
    return np.array(plan, np.int32)


def _pack_t_kernel(plan_ref, a_ref, o_ref, tail_ref):
    mode = plan_ref[pl.program_id(0), 0]
    keep = PACK_T_ROWS - PACK_SHIFT

    @pl.when(mode == 0)
    def _():
        o_ref[...] = a_ref[...].astype(o_ref.dtype)

    @pl.when(mode == 1)
    def _():
        o_ref[:keep, :] = tail_ref[...].astype(o_ref.dtype)
        o_ref[keep:, :] = a_ref[:PACK_SHIFT, :].astype(o_ref.dtype)
        tail_ref[...] = a_ref[PACK_SHIFT:, :]

    @pl.when(mode == 2)
    def _():
        o_ref[...] = jnp.zeros(o_ref.shape, o_ref.dtype)
        o_ref[:PACK_SHIFT, :] = a_ref[:PACK_SHIFT, :].astype(o_ref.dtype)
        tail_ref[...] = a_ref[PACK_SHIFT:, :]

    @pl.when(mode == 3)
    def _():
        o_ref[...] = jnp.zeros(o_ref.shape, o_ref.dtype)


def pack_w_in_t(wt):
    dm = wt.shape[1]
    plan = _pack_t_plan()
    n_src = -(-wt.shape[0] // PACK_T_ROWS)
    grid_spec = pltpu.PrefetchScalarGridSpec(
        num_scalar_prefetch=1,
        grid=(len(plan),),
        in_specs=[pl.BlockSpec((PACK_T_ROWS, dm), lambda j, p: (jnp.minimum(p[j, 1], n_src - 1), 0))],
        out_specs=pl.BlockSpec((PACK_T_ROWS, dm), lambda j, p: (p[j, 2], 0)),
        scratch_shapes=[pltpu.VMEM((PACK_T_ROWS - PACK_SHIFT, dm), F32)],
    )
    return pl.pallas_call(
        _pack_t_kernel,
        grid_spec=grid_spec,
        out_shape=jax.ShapeDtypeStruct((Z_WIDTH, dm), BF16),
        compiler_params=_cp(("arbitrary",), 32),
        name="pack_w_in",
    )(jnp.asarray(plan), wt)


def _rope_kernel(q_ref, iq_ref, k_ref, v_ref, ikw_ref, c128_ref, s128_ref, c64_ref, s64_ref,
                 qhm_ref, iqhm_ref, k4_ref, kb_ref, v4_ref, vb_ref, iko_ref, ikb_ref):
    c128, s128 = c128_ref[...], s128_ref[...]
    c64, s64 = c64_ref[...], s64_ref[...]
    lane = lax.broadcasted_iota(I32, (Q_BLK, LANES), 1)
    first_half = (lane % IDX_DIM) < (IDX_DIM // 2)

    def rope128(x):
        return x * c128 + pltpu.roll(x, A_HEAD_DIM // 2, axis=1) * s128

    def rope64(x):
        partner = jnp.where(first_half, pltpu.roll(x, LANES - IDX_DIM // 2, axis=1),
                            pltpu.roll(x, IDX_DIM // 2, axis=1))
        return x * c64 + partner * s64

    for h in range(A_HEADS):
        qhm_ref[0, h] = (rope128(q_ref[:, h * LANES:(h + 1) * LANES]) * Q_LOG2_SCALE).astype(qhm_ref.dtype)
    for h in range(A_KV_HEADS):
        rows = pl.ds(h, Q_BLK, stride=A_KV_HEADS)
        kr = rope128(k_ref[:, h * LANES:(h + 1) * LANES])
        k4_ref[rows, :] = kr
        kb_ref[:, h * LANES:(h + 1) * LANES] = kr.astype(kb_ref.dtype)
        vh = v_ref[:, h * LANES:(h + 1) * LANES]
        v4_ref[rows, :] = vh
        vb_ref[:, h * LANES:(h + 1) * LANES] = vh.astype(vb_ref.dtype)
    for p in range(IDX_HEADS // 2):
        r = rope64(iq_ref[:, p * LANES:(p + 1) * LANES]).astype(iqhm_ref.dtype)
        iqhm_ref[0, 2 * p] = r[:, :IDX_DIM]
        iqhm_ref[0, 2 * p + 1] = r[:, IDX_DIM:]
    ikr = rope64(ikw_ref[...])[:, :IDX_DIM]
    iko_ref[...] = ikr
    ikb_ref[...] = ikr.astype(ikb_ref.dtype)


def rope_all(z, col, tabs):
    nt = z.shape[0]
    nb = nt // Q_BLK
    c128, s128, c64, s64 = tabs
    tab_spec = pl.BlockSpec((Q_BLK, LANES), lambda i: (i, 0))
    kv4_spec = pl.BlockSpec((Q_BLK * A_KV_HEADS, A_HEAD_DIM), lambda i: (i, 0))
    kvb_spec = pl.BlockSpec((Q_BLK, A_KV_W), lambda i: (i, 0))
    ik_spec = pl.BlockSpec((Q_BLK, IDX_DIM), lambda i: (i, 0))
    kv4 = jax.ShapeDtypeStruct((nt * A_KV_HEADS, A_HEAD_DIM), F32)
    kvb = jax.ShapeDtypeStruct((nt, A_KV_W), BF16)
    return pl.pallas_call(
        _rope_kernel,
        grid=(nb,),
        in_specs=[pl.BlockSpec((Q_BLK, A_Q_W), lambda i: (i, col["aq"] // A_Q_W)),
                  pl.BlockSpec((Q_BLK, IDX_Q_W), lambda i: (i, col["iq"] // IDX_Q_W)),
                  pl.BlockSpec((Q_BLK, A_KV_W), lambda i: (i, col["ak"] // A_KV_W)),
                  pl.BlockSpec((Q_BLK, A_KV_W), lambda i: (i, col["av"] // A_KV_W)),
                  pl.BlockSpec((Q_BLK, LANES), lambda i: (i, col["ikw"] // LANES)),
                  tab_spec, tab_spec, tab_spec, tab_spec],
        out_specs=[pl.BlockSpec((1, A_HEADS, Q_BLK, A_HEAD_DIM), lambda i: (i, 0, 0, 0)),
                   pl.BlockSpec((1, IDX_HEADS, Q_BLK, IDX_DIM), lambda i: (i, 0, 0, 0)),
                   kv4_spec, kvb_spec, kv4_spec, kvb_spec, ik_spec, ik_spec],
        out_shape=[jax.ShapeDtypeStruct((nb, A_HEADS, Q_BLK, A_HEAD_DIM), BF16),
                   jax.ShapeDtypeStruct((nb, IDX_HEADS, Q_BLK, IDX_DIM), BF16),
                   kv4, kvb, kv4, kvb,
                   jax.ShapeDtypeStruct((nt, IDX_DIM), F32), jax.ShapeDtypeStruct((nt, IDX_DIM), BF16)],
        compiler_params=_cp(("parallel",), 32),
        name="rope",
    )(z, z, z, z, z, c128, s128, c64, s64)


def rope_tables(pos):
    def tab(half, reps):
        inv_freq = ROPE_THETA ** (-jnp.arange(half, dtype=F32) / half)
        ang = pos.astype(F32)[:, None] * inv_freq[None, :]
        c, s = jnp.cos(ang), jnp.sin(ang)
        return jnp.tile(jnp.concatenate([c, c], 1), (1, reps)), jnp.tile(jnp.concatenate([-s, s], 1), (1, reps))
    c128, s128 = tab(A_HEAD_DIM // 2, 1)
    c64, s64 = tab(IDX_DIM // 2, 2)
    return c128, s128, c64, s64


def _sort_key(x):
    i = pltpu.bitcast(x, I32)
    return i ^ ((i >> 31) & 0x7FFFFFFF)


WORD = 32
GROUP_CHUNKS = WORD * LANES // KEY_CHUNK
_BIT_MASKS = ((16, 0x0000FFFF), (8, 0x00FF00FF), (4, 0x0F0F0F0F), (2, 0x33333333), (1, 0x55555555))


def _transpose_bits(w):
    w = list(w)
    for d, m in _BIT_MASKS:
        mask = jnp.int32(m)
        for j in range(WORD):
            if j & d:
                continue
            lo, hi = w[j], w[j + d]
            t = (lax.shift_right_logical(lo, jnp.int32(d)) ^ hi) & mask
            w[j + d] = hi ^ t
            w[j] = lo ^ lax.shift_left(t, jnp.int32(d))
    return w


def _kth_largest_bitsliced(key_ref, plane_ref, cand_ref, nch, k):
    rows = key_ref.shape[1]
    tiles_per_chunk = KEY_CHUNK // LANES
    ngroups = (nch + GROUP_CHUNKS - 1) // GROUP_CHUNKS

    def build_group(g, carry):
        def build_slab(s, c2):
            r0 = pl.multiple_of(s * SUBLANES, SUBLANES)
            words = []
            for j in range(WORD):
                c = g * GROUP_CHUNKS + j // tiles_per_chunk
                lt = j % tiles_per_chunk
                cc = jnp.minimum(c, key_ref.shape[0] - 1)
                w = key_ref[cc, pl.ds(r0, SUBLANES), lt * LANES:(lt + 1) * LANES] ^ INT_MIN
                words.append(jnp.where(c < nch, w, 0))
            planes = _transpose_bits(words)
            for b in range(WORD):
                plane_ref[g, b, pl.ds(r0, SUBLANES), :] = planes[b]
            return c2
        lax.fori_loop(0, rows // SUBLANES, build_slab, 0)
        cand_ref[g] = jnp.full((rows, LANES), -1, I32)
        return carry

    lax.fori_loop(0, ngroups, build_group, 0)

    def bit_step(it, carry):
        t_u, k_rem = carry
        b = WORD - 1 - it

        def count(g, cnt):
            return cnt + lax.population_count(cand_ref[g] & plane_ref[g, b])
        cnt = lax.fori_loop(0, ngroups, count, jnp.zeros((rows, LANES), I32))
        c1 = jnp.sum(cnt, axis=1, keepdims=True)
        take = c1 >= k_rem
        take_b = jnp.broadcast_to(take, (rows, LANES))

        def update(g, c2):
            e = cand_ref[g]
            a = e & plane_ref[g, b]
            cand_ref[g] = jnp.where(take_b, a, e ^ a)
            return c2
        lax.fori_loop(0, ngroups, update, 0)
        t_u = jnp.where(take, t_u | jnp.left_shift(jnp.int32(1), b), t_u)
        return t_u, jnp.where(take, k_rem, k_rem - c1)

    t_u, k_rem = lax.fori_loop(0, WORD, bit_step, (jnp.zeros((rows, 1), I32), jnp.full((rows, 1), k, I32)))
    return t_u ^ INT_MIN, k_rem


def _lower_surplus_ties(key_ref, cand_ref, nch, t, n_wanted):
    rows = key_ref.shape[1]
    ngroups = (nch + GROUP_CHUNKS - 1) // GROUP_CHUNKS
    group_keys = WORD * LANES
    pos_bits = (key_ref.shape[0] * KEY_CHUNK - 1).bit_length()
    live = t > INT_MIN

    def count_tied(g, cnt):
        return cnt + lax.population_count(cand_ref[g])
    n_tied = jnp.sum(lax.fori_loop(0, ngroups, count_tied, jnp.zeros((rows, LANES), I32)), axis=1, keepdims=True)
    surplus = jnp.where(live, n_tied - n_wanted, 0)

    @pl.when(jnp.max(surplus) > 0)
    def _():
        lane = lax.broadcasted_iota(I32, (rows, LANES), 1)

        def tied_before(q):
            gq = q // group_keys
            jq = jnp.broadcast_to((q // LANES) % WORD, (rows, LANES))
            lq = q % LANES
            low = lax.shift_left(jnp.ones((rows, LANES), I32), jq) - 1

            def body(g, cnt):
                w = cand_ref[g]
                at_tile = jnp.where(lane < lq, lax.shift_right_logical(w, jq) & 1, 0)
                part = lax.population_count(w & low) + at_tile
                return cnt + jnp.where(g < gq, lax.population_count(w), jnp.where(g == gq, part, 0))
            cnt = lax.fori_loop(0, ngroups, body, jnp.zeros((rows, LANES), I32))
            return jnp.sum(cnt, axis=1, keepdims=True)

        def bit_step(it, q):
            cand = q | jnp.left_shift(jnp.int32(1), pos_bits - 1 - it)
            return jnp.where(tied_before(cand) < n_wanted, cand, q)
        last = lax.fori_loop(0, pos_bits, bit_step, jnp.zeros((rows, 1), I32))
        off = lax.broadcasted_iota(I32, (rows, KEY_CHUNK), 1)

        def rewrite(c, carry):
            keys = key_ref[c]
            drop = jnp.where(keys == t, 1, 0) * jnp.where(c * KEY_CHUNK + off > last, 1, 0) * jnp.where(live, 1, 0)
            key_ref[c] = keys - drop
            return carry
        lax.fori_loop(0, nch, rewrite, 0)


def _flash_step(s, mask, vext, m_ref, l_ref, acc_ref, g):
    hh, rr, ss = s.shape
    d = vext.shape[1] // 2
    s = jnp.where(mask[None], s, -jnp.inf).reshape(hh * rr, ss)
    m_old = m_ref[g]
    m_new = jnp.maximum(m_old, jnp.max(s, axis=-1, keepdims=True))
    p = jnp.exp2(s - jnp.concatenate([m_new] * (ss // LANES), axis=1))
    alpha = jnp.exp2(m_old - m_new)
    pv = jnp.dot(p.astype(BF16), vext, preferred_element_type=F32)
    acc_ref[g] = alpha * acc_ref[g] + pv[:, :d]
    l_ref[g] = alpha * l_ref[g] + pv[:, d:]
    m_ref[g] = m_new


IDX_HEADS_PER_DOT = 8


def _dsa_prompt_kernel(iq_ref, w_ref, q_ref, ik_ref, k_ref, v_ref, o_ref,
                       key_ref, plane_ref, cand_ref, wb_ref, m_ref, l_ref, acc_ref, *, topk, w_lane0):
    i = pl.program_id(0)
    nch = (i * Q_BLK + Q_BLK + KEY_CHUNK - 1) // KEY_CHUNK
    hpg = A_HEADS // A_KV_HEADS

    wt = w_ref[...] * ((IDX_DIM ** -0.5) * (IDX_HEADS ** -0.5))
    for h in range(IDX_HEADS):
        wb_ref[h] = jnp.broadcast_to(wt[:, w_lane0 + h:w_lane0 + h + 1], (Q_BLK, LANES))

    q_pos = i * Q_BLK + lax.broadcasted_iota(I32, (Q_BLK, KEY_CHUNK), 0)
    k_off = lax.broadcasted_iota(I32, (Q_BLK, KEY_CHUNK), 1)

    def score_chunk(c, carry):
        k0 = pl.multiple_of(c * KEY_CHUNK, KEY_CHUNK)
        ikc = ik_ref[pl.ds(k0, KEY_CHUNK), :]
        ntile = KEY_CHUNK // LANES
        acc = [jnp.zeros((Q_BLK, LANES), F32)] * ntile
        for hg in range(IDX_HEADS // IDX_HEADS_PER_DOT):
            lhs = iq_ref[0, hg * IDX_HEADS_PER_DOT * Q_BLK:(hg + 1) * IDX_HEADS_PER_DOT * Q_BLK, :]
            d = lax.dot_general(lhs, ikc, (((1,), (1,)), ((), ())), preferred_element_type=F32)
            for hl in range(IDX_HEADS_PER_DOT):
                wb = wb_ref[hg * IDX_HEADS_PER_DOT + hl]
                r = jnp.maximum(d[hl * Q_BLK:(hl + 1) * Q_BLK], 0.0)
                acc = [acc[j] + wb * r[:, j * LANES:(j + 1) * LANES] for j in range(ntile)]
        sc = jnp.concatenate(acc, axis=1)
        key_ref[c] = jnp.where(k0 + k_off <= q_pos, _sort_key(sc), INT_MIN)
        return carry

    lax.fori_loop(0, nch, score_chunk, 0)

    t, n_wanted = _kth_largest_bitsliced(key_ref, plane_ref, cand_ref, nch, topk)
    _lower_surplus_ties(key_ref, cand_ref, nch, t, n_wanted)
    thr = jnp.maximum(t, INT_MIN + 1)

    m_ref[...] = jnp.full(m_ref.shape, NEG_BIG, F32)
    l_ref[...] = jnp.zeros(l_ref.shape, F32)
    acc_ref[...] = jnp.zeros(acc_ref.shape, F32)
    ones = jnp.ones((KEY_CHUNK, A_HEAD_DIM), BF16)

    def attend_chunk(c, carry):
        k0 = pl.multiple_of(c * KEY_CHUNK, KEY_CHUNK)
        mask = key_ref[c] >= thr
        half = KEY_CHUNK // 2
        for g in range(A_KV_HEADS):
            for part in range(2):
                ks = pl.ds(k0 + part * half, half)
                kc = k_ref[ks, g * LANES:(g + 1) * LANES]
                vext = jnp.concatenate([v_ref[ks, g * LANES:(g + 1) * LANES], ones[:half]], axis=1)
                mk = mask[:, part * half:(part + 1) * half]
                for hl in range(hpg):
                    h = g * hpg + hl
                    s = lax.dot_general(q_ref[0, h * Q_BLK:(h + 1) * Q_BLK, :], kc, (((1,), (1,)), ((), ())),
                                        preferred_element_type=F32)
                    _flash_step(s.reshape(1, Q_BLK, half), mk, vext, m_ref, l_ref, acc_ref, h)
        return carry

    lax.fori_loop(0, nch, attend_chunk, 0)
    for h in range(A_HEADS):
        o_ref[:, h * A_HEAD_DIM:(h + 1) * A_HEAD_DIM] = (acc_ref[h] / l_ref[h]).astype(o_ref.dtype)


def dsa_prompt(iq_hm, z, col, q_hm, ik, k, v, t):
    nb = t // Q_BLK
    topk = min(TOPK_MAX, t // 4)
    nch_max = (t + KEY_CHUNK - 1) // KEY_CHUNK
    ngroups_max = (nch_max + GROUP_CHUNKS - 1) // GROUP_CHUNKS
    hpg = A_HEADS // A_KV_HEADS
    kern = functools.partial(_dsa_prompt_kernel, topk=topk, w_lane0=IDX_DIM)
    return pl.pallas_call(
        kern,
        grid=(nb,),
        in_specs=[pl.BlockSpec((1, IDX_HEADS * Q_BLK, IDX_DIM), lambda i: (i, 0, 0)),
                  pl.BlockSpec((Q_BLK, LANES), lambda i: (i, col["ikw"] // LANES)),
                  pl.BlockSpec((1, A_HEADS * Q_BLK, A_HEAD_DIM), lambda i: (i, 0, 0)),
                  _whole_vmem(), _whole_vmem(), _whole_vmem()],
        out_specs=pl.BlockSpec((Q_BLK, A_Q_W), lambda i: (i, 0)),
        out_shape=jax.ShapeDtypeStruct((t, A_Q_W), BF16),
        scratch_shapes=[pltpu.VMEM((nch_max, Q_BLK, KEY_CHUNK), I32),
                        pltpu.VMEM((ngroups_max, WORD, Q_BLK, LANES), I32),
                        pltpu.VMEM((ngroups_max, Q_BLK, LANES), I32),
                        pltpu.VMEM((IDX_HEADS, Q_BLK, LANES), F32),
                        pltpu.VMEM((A_HEADS, Q_BLK, LANES), F32),
                        pltpu.VMEM((A_HEADS, Q_BLK, LANES), F32),
                        pltpu.VMEM((A_HEADS, Q_BLK, A_HEAD_DIM), F32)],
        compiler_params=_cp(("arbitrary",), 60),
        name="dsa_prompt",
    )(iq_hm, z, q_hm, ik, k, v)


S_ROWS = SUBLANES
ATT_PAGES = 16


def _dsa_sample_scores_kernel(pt_ref, iq_ref, w_ref, ikn_ref, *rest, n_pages, t_new):
    pages = rest[:n_pages]
    key_ref = rest[n_pages]
    wt = w_ref[0] * ((IDX_DIM ** -0.5) * (IDX_HEADS ** -0.5))
    wb = [jnp.broadcast_to(wt[:, h:h + 1], (S_ROWS, LANES)) for h in range(IDX_HEADS)]
    iq = iq_ref[0]

    def tile_scores(ik_tile, transposed):
        dims = (((1,), (0,)), ((), ())) if transposed else (((1,), (1,)), ((), ()))
        d = lax.dot_general(iq, ik_tile, dims, preferred_element_type=F32)
        acc = jnp.zeros((S_ROWS, LANES), F32)
        for h in range(IDX_HEADS):
            acc = acc + wb[h] * jnp.maximum(d[h * S_ROWS:(h + 1) * S_ROWS], 0.0)
        return acc

    for r in range(n_pages):
        key_ref[0, r] = _sort_key(tile_scores(pages[r][0].astype(BF16), True))
    row = lax.broadcasted_iota(I32, (S_ROWS, LANES), 0)
    lane = lax.broadcasted_iota(I32, (S_ROWS, LANES), 1)
    valid = jnp.where(lane < t_new, lane, S_ROWS) <= row
    key_ref[0, n_pages] = jnp.where(valid, _sort_key(tile_scores(ikn_ref[0], False)), INT_MIN)


def _dsa_sample_thr_kernel(key_ref, adj_ref, thr_ref, last_ref, *, topk):
    b, n_tiles = key_ref.shape[0], key_ref.shape[1]
    pos_bits = (n_tiles * LANES - 1).bit_length()
    lane = lax.broadcasted_iota(I32, (b, S_ROWS, LANES), 2)

    def count(pred):
        def body(c, cnt):
            return cnt + jnp.where(pred(key_ref[:, c], c), 1, 0)
        cnt = lax.fori_loop(0, n_tiles, body, jnp.zeros((b, S_ROWS, LANES), I32))
        return jnp.sum(cnt, axis=2, keepdims=True)

    def bit_step(it, cur):
        cand = cur | jnp.left_shift(jnp.int32(1), 31 - it)
        return jnp.where(count(lambda x, c: x >= (cand ^ INT_MIN)) >= topk, cand, cur)

    t = lax.fori_loop(0, 32, bit_step, jnp.zeros((b, S_ROWS, 1), I32)) ^ INT_MIN
    live = t > INT_MIN
    n_above = count(lambda x, c: x > t)
    n_tied = count(lambda x, c: x == t)
    n_wanted = topk - n_above
    last_ref[...] = jnp.full(last_ref.shape, n_tiles * LANES, I32)

    @pl.when(jnp.max(jnp.where(live, n_tied - n_wanted, 0)) > 0)
    def _():
        def pos_step(it, q):
            cand = q | jnp.left_shift(jnp.int32(1), pos_bits - 1 - it)
            before = count(lambda x, c: jnp.where(x == t, 1, 0) * jnp.where(c * LANES + lane < cand, 1, 0) > 0)
            return jnp.where(before < n_wanted, cand, q)
        last = lax.fori_loop(0, pos_bits, pos_step, jnp.zeros((b, S_ROWS, 1), I32))
        last_ref[...] = jnp.broadcast_to(last, last_ref.shape)

    last = last_ref[...]
    live_i = jnp.where(live, 1, 0)

    def rewrite(c, carry):
        keys = key_ref[:, c]
        drop = jnp.where(keys == t, 1, 0) * jnp.where(c * LANES + lane > last, 1, 0) * live_i
        adj_ref[:, c] = keys - drop
        return carry
    lax.fori_loop(0, n_tiles, rewrite, 0)
    thr_ref[...] = jnp.broadcast_to(jnp.maximum(t, INT_MIN + 1), thr_ref.shape)


def dsa_sample_index(page_table, iq_s, iw_s, ik_new, cache_ik, t_new):
    b, n_pages = page_table.shape
    topk = min(TOPK_MAX, (n_pages * PAGE_SIZE + t_new) // 4)

    def page_spec(r):
        return pl.BlockSpec((1, IDX_DIM, PAGE_SIZE), lambda bi, pt: (pt[bi, r], 0, 0))

    grid_spec = pltpu.PrefetchScalarGridSpec(
        num_scalar_prefetch=1,
        grid=(b,),
        in_specs=[pl.BlockSpec((1, IDX_HEADS * S_ROWS, IDX_DIM), lambda bi, pt: (bi, 0, 0)),
                  pl.BlockSpec((1, S_ROWS, IDX_HEADS), lambda bi, pt: (bi, 0, 0)),
                  pl.BlockSpec((1, PAGE_SIZE, IDX_DIM), lambda bi, pt: (bi, 0, 0))]
                 + [page_spec(r) for r in range(n_pages)],
        out_specs=pl.BlockSpec((1, n_pages + 1, S_ROWS, LANES), lambda bi, pt: (bi, 0, 0, 0)),
    )
    keys = pl.pallas_call(
        functools.partial(_dsa_sample_scores_kernel, n_pages=n_pages, t_new=t_new),
        grid_spec=grid_spec,
        out_shape=jax.ShapeDtypeStruct((b, n_pages + 1, S_ROWS, LANES), I32),
        compiler_params=_cp(("parallel",), 32),
        name="dsa_sample_scores",
    )(page_table, iq_s, iw_s, ik_new, *([cache_ik] * n_pages))
    return pl.pallas_call(
        functools.partial(_dsa_sample_thr_kernel, topk=topk),
        in_specs=[_whole_vmem()],
        out_specs=[_whole_vmem(), _whole_vmem()],
        out_shape=[jax.ShapeDtypeStruct(keys.shape, I32), jax.ShapeDtypeStruct((b, S_ROWS, LANES), I32)],
        scratch_shapes=[pltpu.VMEM((b, S_ROWS, LANES), I32)],
        compiler_params=pltpu.CompilerParams(vmem_limit_bytes=32 * 2 ** 20),
        name="dsa_sample_threshold",
    )(keys)


def _dsa_sample_attend_kernel(pt_ref, key_ref, thr_ref, q_ref, kn_ref, vn_ref, *rest, n_pages):
    kpages = rest[:ATT_PAGES]
    vpages = rest[ATT_PAGES:2 * ATT_PAGES]
    o_ref, m_ref, l_ref, acc_ref = rest[2 * ATT_PAGES:]
    j = pl.program_id(1)
    hpg = A_HEADS // A_KV_HEADS
    thr = thr_ref[0]

    @pl.when(j == 0)
    def _():
        m_ref[...] = jnp.full(m_ref.shape, NEG_BIG, F32)
        l_ref[...] = jnp.zeros(l_ref.shape, F32)
        acc_ref[...] = jnp.zeros(acc_ref.shape, F32)

    def attend(mask, g, kg, vg):
        qg = q_ref[0, g * hpg * S_ROWS:(g + 1) * hpg * S_ROWS, :]
        s = lax.dot_general(qg, kg, (((1,), (1,)), ((), ())), preferred_element_type=F32)
        vext = jnp.concatenate([vg, jnp.ones(vg.shape, BF16)], axis=1)
        _flash_step(s.reshape(hpg, S_ROWS, s.shape[-1]), mask, vext, m_ref, l_ref, acc_ref, g)

    mask = jnp.concatenate([key_ref[0, j * ATT_PAGES + r] >= thr for r in range(ATT_PAGES)], axis=1)
    for g in range(A_KV_HEADS):
        rows = pl.ds(g, PAGE_SIZE, stride=A_KV_HEADS)
        kg = jnp.concatenate([kp[rows, :].astype(BF16) for kp in kpages], axis=0)
        vg = jnp.concatenate([vp[rows, :].astype(BF16) for vp in vpages], axis=0)
        attend(mask, g, kg, vg)

    @pl.when(j == pl.num_programs(1) - 1)
    def _():
        mask_new = key_ref[0, n_pages] >= thr
        for g in range(A_KV_HEADS):
            attend(mask_new, g, kn_ref[0, :, g * LANES:(g + 1) * LANES], vn_ref[0, :, g * LANES:(g + 1) * LANES])
        for g in range(A_KV_HEADS):
            o_ref[0, g * hpg * S_ROWS:(g + 1) * hpg * S_ROWS, :] = (acc_ref[g] / l_ref[g]).astype(o_ref.dtype)


def dsa_sample_attend(page_table, keys, thr, q_s, k_new, v_new, cache_k, cache_v):
    b, n_pages = page_table.shape
    steps = n_pages // ATT_PAGES
    hpg = A_HEADS // A_KV_HEADS
    page_rows = PAGE_SIZE * A_KV_HEADS

    def page_spec(r):
        return pl.BlockSpec((page_rows, A_HEAD_DIM), lambda bi, j, pt: (pt[bi, j * ATT_PAGES + r], 0))

    grid_spec = pltpu.PrefetchScalarGridSpec(
        num_scalar_prefetch=1,
        grid=(b, steps),
        in_specs=[pl.BlockSpec((1, n_pages + 1, S_ROWS, LANES), lambda bi, j, pt: (bi, 0, 0, 0)),
                  pl.BlockSpec((1, S_ROWS, LANES), lambda bi, j, pt: (bi, 0, 0)),
                  pl.BlockSpec((1, A_HEADS * S_ROWS, A_HEAD_DIM), lambda bi, j, pt: (bi, 0, 0)),
                  pl.BlockSpec((1, PAGE_SIZE, A_KV_W), lambda bi, j, pt: (bi, 0, 0)),
                  pl.BlockSpec((1, PAGE_SIZE, A_KV_W), lambda bi, j, pt: (bi, 0, 0))]
                 + [page_spec(r) for r in range(ATT_PAGES)] * 2,
        out_specs=pl.BlockSpec((1, A_HEADS * S_ROWS, A_HEAD_DIM), lambda bi, j, pt: (bi, 0, 0)),
        scratch_shapes=[pltpu.VMEM((A_KV_HEADS, hpg * S_ROWS, LANES), F32),
                        pltpu.VMEM((A_KV_HEADS, hpg * S_ROWS, LANES), F32),
                        pltpu.VMEM((A_KV_HEADS, hpg * S_ROWS, A_HEAD_DIM), F32)],
    )
    return pl.pallas_call(
        functools.partial(_dsa_sample_attend_kernel, n_pages=n_pages),
        grid_spec=grid_spec,
        out_shape=jax.ShapeDtypeStruct((b, A_HEADS * S_ROWS, A_HEAD_DIM), BF16),
        compiler_params=_cp(("parallel", "arbitrary"), 48),
        name="dsa_sample_attend",
    )(page_table, keys, thr, q_s, k_new, v_new, *([cache_k] * ATT_PAGES), *([cache_v] * ATT_PAGES))


def _cumsum_rows(x):
    n = x.shape[0]
    row = lax.broadcasted_iota(I32, x.shape, 0)
    d = 1
    while d < n:
        x = x + jnp.where(row >= d, pltpu.roll(x, d, axis=0), 0.0)
        d *= 2
    return x


def _hgrn_kernel(q_ref, f_ref, i_ref, g_ref, lb_ref, ng_ref, s0_ref, o_ref, so_ref, st_ref,
                 *, chunk, sub, t_valid):
    n = pl.program_id(1)
    nsub = chunk // sub

    @pl.when(n == 0)
    def _():
        for h in range(B_HEADS):
            st_ref[h] = s0_ref[0, h].T

    lbx = lb_ref[...]
    lbe = jnp.exp(lbx - jnp.max(lbx, axis=0, keepdims=True))
    lb_all = lbe[0:1] / jnp.sum(lbe, axis=0, keepdims=True)
    row = lax.broadcasted_iota(I32, (chunk, B_HEAD_DIM), 0)
    valid = row < t_valid
    tt = lax.broadcasted_iota(I32, (chunk, nsub * chunk), 0)
    cc = lax.broadcasted_iota(I32, (chunk, nsub * chunk), 1)
    pair_ok = ((cc // chunk) == (tt // sub)) & ((cc % chunk) <= tt)
    ng = ng_ref[...]

    for h in range(B_HEADS):
        sl = slice(h * B_HEAD_DIM, (h + 1) * B_HEAD_DIM)
        lb = lb_all[:, sl]
        f = lb + (1.0 - lb) * jax.nn.sigmoid(f_ref[:, sl])
        logf = jnp.where(valid, jnp.log(f), 0.0)
        kk = jnp.where(valid, 1.0 - f, 0.0)
        q = q_ref[:, sl]
        iv = i_ref[:, sl]
        cum = _cumsum_rows(logf)
        last = cum[chunk - 1:chunk]
        st = st_ref[h]

        o = lax.dot_general((q * jnp.exp(cum)).astype(BF16), st.astype(BF16),
                            (((1,), (1,)), ((), ())), preferred_element_type=F32)

        refs = [jnp.zeros((1, B_HEAD_DIM), F32)] + [cum[s * sub - 1:s * sub] for s in range(1, nsub)]
        ref_row = refs[0]
        for s in range(1, nsub):
            ref_row = jnp.where(row >= s * sub, refs[s], ref_row)
        qt = (q * jnp.exp(cum - ref_row)).astype(BF16)
        kcat = jnp.concatenate(
            [(kk * jnp.exp(jnp.minimum(r - cum, EXP_CLAMP))).astype(BF16) for r in refs], axis=0)
        a = lax.dot_general(qt, kcat, (((1,), (1,)), ((), ())), preferred_element_type=F32)
        p = jnp.where(pair_ok, a, 0.0).astype(BF16)
        icat = jnp.concatenate([iv.astype(BF16)] * nsub, axis=0)
        o = o + jnp.dot(p, icat, preferred_element_type=F32)

        kd = (kk * jnp.exp(last - cum)).astype(BF16)
        upd = lax.dot_general(iv.astype(BF16), kd, (((0,), (0,)), ((), ())), preferred_element_type=F32)
        st_ref[h] = st * jnp.exp(last) + upd

        on = o * lax.rsqrt(jnp.mean(o * o, axis=-1, keepdims=True) + RMS_EPS) * ng
        gate = g_ref[:, sl]
        o_ref[:, sl] = (on * (gate * jax.nn.sigmoid(gate))).astype(o_ref.dtype)

    @pl.when(n == pl.num_programs(1) - 1)
    def _():
        for h in range(B_HEADS):
            so_ref[0, h] = st_ref[h].T


def hgrn(zsrc, colblk, lb, ng, s0, batch, t_pad, chunk, sub, t_valid):
    nch = t_pad // chunk
    kern = functools.partial(_hgrn_kernel, chunk=chunk, sub=sub, t_valid=t_valid)

    def zspec(cb):
        return pl.BlockSpec((chunk, B_W), lambda b, n: (b * nch + n, cb))

    return pl.pallas_call(
        kern,
        grid=(batch, nch),
        in_specs=[zspec(colblk[0]), zspec(colblk[1]), zspec(colblk[2]), zspec(colblk[3]),
                  pl.BlockSpec(lb.shape, lambda b, n: (0, 0)),
                  pl.BlockSpec((1, B_HEAD_DIM), lambda b, n: (0, 0)),
                  pl.BlockSpec((1, B_HEADS, B_HEAD_DIM, B_HEAD_DIM), lambda b, n: (b, 0, 0, 0))],
        out_specs=[pl.BlockSpec((chunk, B_W), lambda b, n: (b * nch + n, 0)),
                   pl.BlockSpec((1, B_HEADS, B_HEAD_DIM, B_HEAD_DIM), lambda b, n: (b, 0, 0, 0))],
        out_shape=[jax.ShapeDtypeStruct((batch * t_pad, B_W), BF16),
                   jax.ShapeDtypeStruct((batch, B_HEADS, B_HEAD_DIM, B_HEAD_DIM), F32)],
        scratch_shapes=[pltpu.VMEM((B_HEADS, B_HEAD_DIM, B_HEAD_DIM), F32)],
        compiler_params=_cp(("parallel", "arbitrary"), 32),
        name="hgrn",
    )(zsrc, zsrc, zsrc, zsrc, lb, ng, s0)


def _mem_attn_kernel(q_ref, mk_ref, mv_ref, o_ref):
    scale = M_HEAD_DIM ** -0.5
    for h in range(M_HEADS):
        sl = slice(h * M_HEAD_DIM, (h + 1) * M_HEAD_DIM)
        q = q_ref[0, :, sl].astype(BF16)
        mk = mk_ref[0, :, sl].astype(BF16)
        mv = mv_ref[0, :, sl].astype(BF16)
        s = lax.dot_general(q, mk, (((1,), (1,)), ((), ())), preferred_element_type=F32) * scale
        p = jnp.exp(s - jnp.max(s, axis=-1, keepdims=True))
        p = p / jnp.sum(p, axis=-1, keepdims=True)
        o_ref[0, :, sl] = jnp.dot(p.astype(BF16), mv, preferred_element_type=F32).astype(o_ref.dtype)


def mem_attention(q3, qcolblk, mk3, mkcolblk, mv3, mvcolblk, batch, t, tq):
    m = mk3.shape[1]
    return pl.pallas_call(
        _mem_attn_kernel,
        grid=(batch, t // tq),
        in_specs=[pl.BlockSpec((1, tq, M_W), lambda b, i: (b, i, qcolblk)),
                  pl.BlockSpec((1, m, M_W), lambda b, i: (b, 0, mkcolblk)),
                  pl.BlockSpec((1, m, M_W), lambda b, i: (b, 0, mvcolblk))],
        out_specs=pl.BlockSpec((1, tq, M_W), lambda b, i: (b, i, 0)),
        out_shape=jax.ShapeDtypeStruct((batch, t, M_W), BF16),
        compiler_params=_cp(("parallel", "parallel"), 32),
        name="mem_attention",
    )(q3, mk3, mv3)


def _merge_kernel(oa_ref, ob_ref, om_ref, pa_ref, pb_ref, pm_ref, ga_ref, gb_ref, gm_ref, o_ref):
    acc = jax.nn.sigmoid(ga_ref[...]) * jnp.dot(oa_ref[...], pa_ref[...], preferred_element_type=F32)
    acc += jax.nn.sigmoid(gb_ref[...]) * jnp.dot(ob_ref[...], pb_ref[...], preferred_element_type=F32)
    acc += jax.nn.sigmoid(gm_ref[...]) * jnp.dot(om_ref[...], pm_ref[...], preferred_element_type=F32)
    o_ref[...] = acc.astype(o_ref.dtype)


def merge(oa, ob, om, pa, pb, pm, z, col, tm, tn):
    nt = oa.shape[0]
    d = pa.shape[1]

    def act(w):
        return pl.BlockSpec((tm, w), lambda j, i: (i, 0))

    def wgt(w):
        return pl.BlockSpec((w, tn), lambda j, i: (0, j))

    def gate(name):
        return pl.BlockSpec((tm, tn), lambda j, i: (i, col[name] // tn + j))

    return pl.pallas_call(
        _merge_kernel,
        grid=(d // tn, nt // tm),
        in_specs=[act(oa.shape[1]), act(ob.shape[1]), act(om.shape[1]),
                  wgt(pa.shape[0]), wgt(pb.shape[0]), wgt(pm.shape[0]),
                  gate("ga"), gate("gb"), gate("gm")],
        out_specs=pl.BlockSpec((tm, tn), lambda j, i: (i, j)),
        out_shape=jax.ShapeDtypeStruct((nt, d), BF16),
        compiler_params=_cp(("parallel", "parallel"), 48),
        name="merge",
    )(oa, ob, om, pa, pb, pm, z, z, z)


def _resid_mm_kernel(a_ref, b_ref, x_ref, xt_ref, o_ref, *, alpha, n_xtiles):
    x = jnp.where(pl.program_id(1) < n_xtiles, x_ref[...], xt_ref[...])
    o_ref[...] = alpha * x + jnp.dot(a_ref[...], b_ref[...], preferred_element_type=F32)


def resid_matmul(a, b, x, x_tail, alpha, tm, tn):
    m, k = a.shape
    n = b.shape[1]
    n_xtiles = x.shape[0] // tm
    assert x.shape[0] % tm == 0 and m // tm == n_xtiles + 1
    return pl.pallas_call(
        functools.partial(_resid_mm_kernel, alpha=alpha, n_xtiles=n_xtiles),
        grid=(n // tn, m // tm),
        in_specs=[pl.BlockSpec((tm, k), lambda j, i: (i, 0)),
                  pl.BlockSpec((k, tn), lambda j, i: (0, j)),
                  pl.BlockSpec((tm, tn), lambda j, i: (jnp.minimum(i, n_xtiles - 1), j)),
                  pl.BlockSpec((tm, tn), lambda j, i: (0, j))],
        out_specs=pl.BlockSpec((tm, tn), lambda j, i: (i, j)),
        out_shape=jax.ShapeDtypeStruct((m, n), F32),
        compiler_params=_cp(("parallel", "parallel"), 48),
        name="resid_matmul",
    )(a, b, x, x_tail)


def _ln(x, g, b):
    mu = jnp.mean(x, axis=-1, keepdims=True)
    xc = x - mu
    var = jnp.mean(xc * xc, axis=-1, keepdims=True)
    return xc * lax.rsqrt(var + LN_EPS) * g + b


def _ln_kernel(x_ref, g_ref, b_ref, o_ref, ob_ref):
    y = _ln(x_ref[...], g_ref[...], b_ref[...])
    o_ref[...] = y
    ob_ref[...] = y.astype(ob_ref.dtype)


def layer_norm_dual(x, g, b, tm):
    n, d = x.shape
    row = pl.BlockSpec((tm, d), lambda i: (i, 0))
    vec = pl.BlockSpec((1, d), lambda i: (0, 0))
    return pl.pallas_call(
        _ln_kernel,
        grid=(n // tm,),
        in_specs=[row, vec, vec],
        out_specs=[row, row],
        out_shape=[jax.ShapeDtypeStruct((n, d), F32), jax.ShapeDtypeStruct((n, d), BF16)],
        compiler_params=_cp(("parallel",), 48),
        name="layer_norm1",
    )(x, g, b)


def _ln_resid_kernel(h_ref, p_ref, g_ref, b_ref, op_ref, os_ref, *, alpha, n_ptiles):
    i = pl.program_id(0)
    y = _ln(alpha * h_ref[...] + p_ref[...], g_ref[...], b_ref[...])

    @pl.when(i < n_ptiles)
    def _():
        op_ref[...] = y

    @pl.when(i == n_ptiles)
    def _():
        os_ref[...] = y[:os_ref.shape[0]]


def layer_norm_resid(h, p, g, b, alpha, tm, n_p, n_s):
    d = h.shape[1]
    n_ptiles = n_p // tm
    assert n_p % tm == 0 and n_s <= tm
    row = pl.BlockSpec((tm, d), lambda i: (i, 0))
    vec = pl.BlockSpec((1, d), lambda i: (0, 0))
    return pl.pallas_call(
        functools.partial(_ln_resid_kernel, alpha=alpha, n_ptiles=n_ptiles),
        grid=(n_ptiles + 1,),
        in_specs=[row, row, vec, vec],
        out_specs=[pl.BlockSpec((tm, d), lambda i: (jnp.minimum(i, n_ptiles - 1), 0)),
                   pl.BlockSpec((n_s, d), lambda i: (0, 0))],
        out_shape=[jax.ShapeDtypeStruct((n_p, d), F32), jax.ShapeDtypeStruct((n_s, d), F32)],
        compiler_params=_cp(("arbitrary",), 48),
        name="layer_norm2",
    )(h, p, g, b)


def _top_ranked(x, n_top, vals_ref):
    kdim = x.shape[0]
    idx = lax.broadcasted_iota(I32, x.shape, 0).astype(F32)

    def body(a, carry):
        x, rank = carry
        m = jnp.max(x, axis=0, keepdims=True)
        first = jnp.min(jnp.where(x == m, idx, float(kdim)), axis=0, keepdims=True)
        sel = idx == first
        vals_ref[pl.ds(a, 1), :] = m
        return jnp.where(sel, -jnp.inf, x), jnp.where(sel, lax.convert_element_type(a, F32), rank)

    _, rank = lax.fori_loop(0, n_top, body, (x, jnp.full(x.shape, float(n_top), F32)))
    return rank, [vals_ref[a:a + 1, :] for a in range(n_top)]


def _top_ranked_distinct(x, n_top):
    rank = jnp.full(x.shape, float(n_top), F32)
    vals = []
    for a in range(n_top):
        m = jnp.max(x, axis=0, keepdims=True)
        sel = x == m
        rank = jnp.where(sel, float(a), rank)
        x = jnp.where(sel, -jnp.inf, x)
        vals.append(m)
    n_ranked = jnp.sum(jnp.where(rank < float(n_top), 1.0, 0.0), axis=0, keepdims=True)
    return rank, vals, n_ranked


def _peer_route_kernel(pq_ref, k1_ref, k2_ref, rank2_ref, cnt1_ref, e1_ref, e2_ref, vals_ref):
    tn = pq_ref.shape[0]
    refs = (pq_ref, k1_ref, k2_ref, rank2_ref, cnt1_ref, e1_ref, e2_ref, vals_ref)

    def per_lane_tile(flags):
        return jnp.sum(flags.reshape(tn // LANES, LANES), axis=0, keepdims=True)

    irregular = [per_lane_tile(_peer_route_head(h, *refs, distinct=True)) for h in range(PEER_HEADS)]

    @pl.when(jnp.max(sum(irregular)) > 0.0)
    def _():
        for h in range(PEER_HEADS):
            @pl.when(jnp.max(irregular[h]) > 0.0)
            def _():
                _peer_route_head(h, *refs, distinct=False)


def _peer_route_head(h, pq_ref, k1_ref, k2_ref, rank2_ref, cnt1_ref, e1_ref, e2_ref, vals_ref, *, distinct):
    half = PEER_DKEY // 2
    nt = (((1,), (1,)), ((), ()))
    irregular = jnp.zeros((1, pq_ref.shape[0]), F32)

    def top(x, slot):
        if not distinct:
            return _top_ranked(x, PEER_TOPK, vals_ref.at[slot]) + (None,)
        return _top_ranked_distinct(x, PEER_TOPK)

    q1 = pq_ref[:, h * PEER_DKEY:h * PEER_DKEY + half].astype(BF16)
    q2 = pq_ref[:, h * PEER_DKEY + half:(h + 1) * PEER_DKEY].astype(BF16)
    s1 = lax.dot_general(k1_ref[...].astype(BF16), q1, nt, preferred_element_type=F32)
    s2 = lax.dot_general(k2_ref[...].astype(BF16), q2, nt, preferred_element_type=F32)
    rank1, v1, n1 = top(s1, 0)
    rank2, v2, n2 = top(s2, 1)
    v2m = jnp.concatenate(v2, axis=0)
    cand = jnp.concatenate([v1[a] + v2m for a in range(PEER_TOPK)], axis=0)
    crank, _, nc = top(cand, 2)
    if distinct:
        for cnt in (n1, n2, nc):
            irregular = irregular + jnp.where(cnt == float(PEER_TOPK), 0.0, 1.0)
    chosen = jnp.where(crank < float(PEER_TOPK), 1.0, 0.0)
    cmax = v1[0] + v2[0]
    zsum = jnp.sum(chosen * jnp.exp(cand - cmax), axis=0, keepdims=True)
    cnt1 = jnp.zeros_like(s1)
    for a in range(PEER_TOPK):
        m_a = jnp.sum(chosen[a * PEER_TOPK:(a + 1) * PEER_TOPK], axis=0, keepdims=True)
        cnt1 = jnp.where(rank1 == float(a), m_a, cnt1)
    rank2_ref[h] = rank2.astype(rank2_ref.dtype)
    cnt1_ref[h] = cnt1
    e1_ref[h] = jnp.exp(s1 - v1[0])
    e2_ref[h] = (jnp.exp(s2 - v2[0]) / zsum).astype(e2_ref.dtype)
    return irregular


def peer_route(pq, k1, k2, tn):
    n = pq.shape[0]
    shape = (PEER_HEADS, PEER_NKEYS, n)
    ospec = pl.BlockSpec((PEER_HEADS, PEER_NKEYS, tn), lambda i: (0, 0, i))
    kspec = pl.BlockSpec(k1.shape, lambda i: (0, 0))
    return pl.pallas_call(
        _peer_route_kernel,
        grid=(n // tn,),
        in_specs=[pl.BlockSpec((tn, pq.shape[1]), lambda i: (i, 0)), kspec, kspec],
        out_specs=[ospec] * 4,
        out_shape=[jax.ShapeDtypeStruct(shape, BF16), jax.ShapeDtypeStruct(shape, F32),
                   jax.ShapeDtypeStruct(shape, F32), jax.ShapeDtypeStruct(shape, BF16)],
        scratch_shapes=[pltpu.VMEM((3, PEER_TOPK, tn), F32)],
        compiler_params=_cp(("parallel",), 32),
        name="peer_route",
    )(pq, k1, k2)


PEER_EROWS = 4


def _gelu_exact(x):
    return 0.5 * x * (1.0 + lax.erf(x * (2.0 ** -0.5)))


def _peer_dense_kernel(x_ref, u_ref, v_ref, rank2_ref, cnt1_ref, e1_ref, e2_ref, o_ref, z_ref, *, n_tiles):
    r = pl.program_id(1)
    rd, wr = (r + 1) % 2, r % 2

    @pl.when(r == 0)
    def _():
        o_ref[...] = jnp.zeros(o_ref.shape, o_ref.dtype)
        z_ref[...] = jnp.zeros(z_ref.shape, z_ref.dtype)

    o_ref[...] += lax.dot_general(z_ref[rd], v_ref[...], (((0,), (0,)), ((), ())), preferred_element_type=F32)
    at = lax.dot_general(u_ref[...], x_ref[...], (((1,), (1,)), ((), ())), preferred_element_type=F32)
    act = _gelu_exact(at).astype(BF16)
    tile = jnp.minimum(r, n_tiles - 1)
    for rr in range(PEER_EROWS):
        i1 = tile * PEER_EROWS + rr
        g = jnp.zeros((PEER_NKEYS, x_ref.shape[0]), BF16)
        for h in range(PEER_HEADS):
            cnt = cnt1_ref[h, pl.ds(i1, 1), :].astype(BF16)
            e1 = e1_ref[h, pl.ds(i1, 1), :].astype(BF16)
            g = g + jnp.where(rank2_ref[h] < cnt, e2_ref[h] * e1, jnp.zeros((), BF16))
        z_ref[wr, rr * PEER_NKEYS:(rr + 1) * PEER_NKEYS, :] = g * act[rr * PEER_NKEYS:(rr + 1) * PEER_NKEYS]


def peer_dense(xb, u, v, rank2, cnt1, e1, e2, tn):
    n, d = xb.shape
    ne = u.shape[0]
    te = PEER_EROWS * PEER_NKEYS
    n_tiles = ne // te
    aux = pl.BlockSpec((PEER_HEADS, PEER_NKEYS, tn), lambda j, r: (0, 0, j))
    return pl.pallas_call(
        functools.partial(_peer_dense_kernel, n_tiles=n_tiles),
        grid=(n // tn, n_tiles + 1),
        in_specs=[pl.BlockSpec((tn, d), lambda j, r: (j, 0)),
                  pl.BlockSpec((te, d), lambda j, r: (jnp.minimum(r, n_tiles - 1), 0)),
                  pl.BlockSpec((te, d), lambda j, r: (jnp.maximum(r - 1, 0), 0)),
                  aux, aux, aux, aux],
        out_specs=pl.BlockSpec((tn, d), lambda j, r: (j, 0)),
        out_shape=jax.ShapeDtypeStruct((n, d), F32),
        scratch_shapes=[pltpu.VMEM((2, te, tn), BF16)],
        compiler_params=_cp(("parallel", "arbitrary"), 60),
        name="peer_dense",
    )(xb, u, v, rank2, cnt1, e1, e2)


def kernel(x_prompt, x_sample, cache_k, cache_v, cache_idx_k, cache_mem_k, cache_mem_v, state_hgrn, page_table,
           mem_prompt, w_in, w_mem_kv, hgrn_lb, hgrn_norm_g, p_a, p_b, p_m, w_out, ln1_g, ln1_b, w_pq,
           peer_sub_k1, peer_sub_k2, peer_u, peer_v, ln2_g, ln2_b):
    depth = w_in.shape[0]
    assert depth == 1, "single trunk layer"
    bp, t, d = x_prompt.shape
    bs, ts, _ = x_sample.shape
    assert bp == 1 and d == D_MODEL and t % KEY_CHUNK == 0 and bs * ts <= Q_BLK and ts <= S_ROWS
    n_p, n_s = bp * t, bs * ts
    n_real = n_p + n_s
    nt = -(-n_real // TOK_TILE) * TOK_TILE
    n_pool = cache_k.shape[1]
    past_len = page_table.shape[1] * PAGE_SIZE
    alpha = (2 * depth) ** 0.25
    nb_p = n_p // Q_BLK

    assert n_p % TOK_TILE == 0 and nt == n_p + TOK_TILE
    xp, xs = x_prompt.reshape(n_p, d), x_sample.reshape(n_s, d)
    xb = _pad_rows(jnp.concatenate([xp.astype(BF16), xs.astype(BF16)], axis=0), nt)
    w_packed_t = pack_w_in_t(jnp.swapaxes(w_in[0], 0, 1))
    z, peer_ub, peer_vb = matmul_nt_with_casts(xb, w_packed_t, TOK_TILE, Z_TILE_WIDE, peer_u[0], peer_v[0],
                                               name="proj_in")

    def zcols(name, width, lo, hi):
        return z[lo:hi, COLS[name]:COLS[name] + width]

    pos = jnp.concatenate([jnp.arange(t, dtype=I32), past_len + jnp.tile(jnp.arange(ts, dtype=I32), bs),
                           jnp.zeros((nt - n_real,), I32)])
    q_hm, iq_hm, k4, kb, v4, vb, ik_rope, ikb = rope_all(z, COLS, rope_tables(pos))
    nb = q_hm.shape[0]

    oa_p = dsa_prompt(iq_hm.reshape(nb, IDX_HEADS * Q_BLK, IDX_DIM), z, COLS,
                      q_hm.reshape(nb, A_HEADS * Q_BLK, A_HEAD_DIM), ikb[:n_p], kb[:n_p], vb[:n_p], t)

    def per_seq(a, rows):
        return _pad_rows(a.reshape((bs, ts) + a.shape[1:]), rows, axis=1)

    q_s = per_seq(q_hm[nb_p, :, :n_s].transpose(1, 0, 2), S_ROWS).transpose(0, 2, 1, 3)
    q_s = q_s.reshape(bs, A_HEADS * S_ROWS, A_HEAD_DIM)
    iq_s = per_seq(iq_hm[nb_p, :, :n_s].transpose(1, 0, 2), S_ROWS).transpose(0, 2, 1, 3)
    iq_s = iq_s.reshape(bs, IDX_HEADS * S_ROWS, IDX_DIM)
    iw_s = per_seq(z[n_p:n_real, COLS["ikw"] + IDX_DIM:COLS["ikw"] + IDX_DIM + IDX_HEADS], S_ROWS)
    ik_new = per_seq(ikb[n_p:n_real], PAGE_SIZE)
    k_new = per_seq(kb[n_p:n_real], PAGE_SIZE)
    v_new = per_seq(vb[n_p:n_real], PAGE_SIZE)
    keys, thr = dsa_sample_index(page_table, iq_s, iw_s, ik_new, jnp.swapaxes(cache_idx_k[0], 1, 2), ts)
    cache_rows = n_pool * PAGE_SIZE * A_KV_HEADS
    os_hm = dsa_sample_attend(page_table, keys, thr, q_s, k_new, v_new,
                              cache_k[0].reshape(cache_rows, A_HEAD_DIM), cache_v[0].reshape(cache_rows, A_HEAD_DIM))
    oa_s = os_hm.reshape(bs, A_HEADS, S_ROWS, A_HEAD_DIM)[:, :, :ts].transpose(0, 2, 1, 3).reshape(n_s, A_Q_W)
    o_a = _pad_rows(jnp.concatenate([oa_p, oa_s], axis=0), nt)

    b_blk = [COLS[c] // B_W for c in ("bq", "bf", "bi", "bg")]
    ng = hgrn_norm_g[0][None]
    ob_p, st_p = hgrn(z, b_blk, hgrn_lb, ng, jnp.zeros((bp, B_HEADS, B_HEAD_DIM, B_HEAD_DIM), F32),
                      bp, t, HGRN_CHUNK, HGRN_SUB, HGRN_CHUNK)
    zs_b = per_seq(z[n_p:n_real, COLS["bq"]:COLS["bq"] + 4 * B_W], S_ROWS).reshape(bs * S_ROWS, 4 * B_W)
    ob_s, st_s = hgrn(zs_b, [0, 1, 2, 3], hgrn_lb, ng, state_hgrn[0], bs, S_ROWS, S_ROWS, S_ROWS, ts)
    ob_s = ob_s.reshape(bs, S_ROWS, B_W)[:, :ts].reshape(n_s, B_W)
    o_b = _pad_rows(jnp.concatenate([ob_p[:n_p], ob_s], axis=0), nt)

    mem_kv = matmul(mem_prompt[0].astype(BF16), w_mem_kv[0].astype(BF16), mem_prompt.shape[1], Z_TILE, name="mem_kv")
    om_p = mem_attention(z[None], COLS["mq"] // M_W, mem_kv[None], 0, mem_kv[None], 1, bp, t, TOK_TILE)[0]
    zs_m = per_seq(zcols("mq", M_W, n_p, n_real), S_ROWS)
    n_mem = cache_mem_k.shape[2]
    om_s = mem_attention(zs_m, 0, cache_mem_k[0].reshape(bs, n_mem, M_W), 0,
                         cache_mem_v[0].reshape(bs, n_mem, M_W), 0, bs, S_ROWS, S_ROWS)
    o_m = _pad_rows(jnp.concatenate([om_p, om_s[:, :ts].reshape(n_s, M_W)], axis=0), nt)

    merged = merge(o_a, o_b, o_m, p_a[0].astype(BF16), p_b[0].astype(BF16), p_m[0].astype(BF16), z, COLS,
                   TOK_TILE, WIDE_TILE)
    h_pre = resid_matmul(merged, w_out[0].astype(BF16), xp, _pad_rows(xs, TOK_TILE), alpha, TOK_TILE, WIDE_TILE)
    h, hb = layer_norm_dual(h_pre, ln1_g[0][None], ln1_b[0][None], LN_TILE)

    pq = matmul(hb, w_pq[0].astype(BF16), TOK_TILE, Z_TILE, name="peer_query")
    rank2, cnt1, e1, e2 = peer_route(pq, peer_sub_k1[0], peer_sub_k2[0], LN_TILE)
    p_out = peer_dense(hb, peer_ub, peer_vb, rank2, cnt1, e1, e2, TOK_TILE)
    y_p, y_s = layer_norm_resid(h, p_out, ln2_g[0][None], ln2_b[0][None], alpha, LN_TILE, n_p, n_s)

    kv_p = (depth, bp, t, A_KV_HEADS, A_HEAD_DIM)
    kv_s = (depth, bs, ts, A_KV_HEADS, A_HEAD_DIM)
    mem_shape = (depth, bp, mem_prompt.shape[1], M_HEADS, M_HEAD_DIM)
    g4 = A_KV_HEADS
    return (y_p.reshape(bp, t, d), y_s.reshape(bs, ts, d),
            k4[:n_p * g4].reshape(kv_p), v4[:n_p * g4].reshape(kv_p), ik_rope[:n_p].reshape(depth, bp, t, IDX_DIM),
            mem_kv[:, :M_W].reshape(mem_shape), mem_kv[:, M_W:].reshape(mem_shape), st_p[None],
            k4[n_p * g4:n_real * g4].reshape(kv_s), v4[n_p * g4:n_real * g4].reshape(kv_s),
            ik_rope[n_p:n_real].reshape(depth, bs, ts, IDX_DIM), st_s[None])
```
